```python
import jax, jax.numpy as jnp
from jax import lax
import numpy as np

D_MODEL = 1024
BATCH = 8
SEQ = 2048
DEPTH = 4
DEC_BATCH = 128
DEC_SEQ = 4
PAST_LEN = 8192
PAGE_SIZE = 128

D_PLE = 256
M_HEADS = 4
M_DQK = D_MODEL // 8
M_DV = D_MODEL // 4
M_CHUNK = 64
S_HEADS = 16
S_KV_HEADS = 4
S_HEAD_DIM = D_MODEL // S_HEADS
S_GROUP = S_HEADS // S_KV_HEADS
WINDOW = 128
REL_BUCKETS = 32
REL_MAX_DIST = WINDOW
N_GROUPS = 4
EXPERTS_PER_GROUP = 8
N_EXPERTS = N_GROUPS * EXPERTS_PER_GROUP
TOP_K = 2
D_EXPERT = D_MODEL // 2
MOE_BLOCK = 128
DN_ALPHA = (2 * DEPTH) ** 0.25
DN_BETA = (8 * DEPTH) ** -0.25
LN_EPS = 1e-5

_IN_SIZES = (
    M_HEADS * M_DQK,
    M_HEADS * M_DQK,
    M_HEADS * M_DV,
    M_HEADS * M_DV,
    M_HEADS,
    M_HEADS,
    S_HEADS * S_HEAD_DIM,
    S_KV_HEADS * S_HEAD_DIM,
    S_KV_HEADS * S_HEAD_DIM,
    D_MODEL,
    D_MODEL,
)
_IN_SPLITS = tuple(sum(_IN_SIZES[:j + 1]) for j in range(len(_IN_SIZES) - 1))
D_IN = sum(_IN_SIZES)
F_OFF = _IN_SPLITS[4]

kernel_name = 'hybrid_mlstm_swa_hmoe_decoder_step'


def layer_norm(x, g, b):
    xf = x.astype(jnp.float32)
    mu = xf.mean(-1, keepdims=True)
    var = jnp.square(xf - mu).mean(-1, keepdims=True)
    return ((xf - mu) * lax.rsqrt(var + LN_EPS) * g.astype(jnp.float32) + b.astype(jnp.float32)).astype(x.dtype)


def rel_bucket(dist):
    n = np.maximum(dist, 0)
    max_exact = REL_BUCKETS // 2
    large = max_exact + (np.log(np.maximum(n, 1) / max_exact) / np.log(REL_MAX_DIST / max_exact)
                         * (REL_BUCKETS - max_exact)).astype(np.int32)
    large = np.minimum(large, REL_BUCKETS - 1)
    return np.where(n < max_exact, n, large).astype(np.int32)


def rel_bias(rel_table, dist):
    bias = rel_table[rel_bucket(dist)].astype(jnp.float32)
    return jnp.transpose(bias, (2, 0, 1)).reshape((S_KV_HEADS, S_GROUP) + dist.shape)


def sink_softmax(s, sink):
    sk = sink.astype(jnp.float32)[:, :, None, None]
    m = jnp.maximum(s.max(-1, keepdims=True), sk)
    p = jnp.exp(s - m)
    return p / (p.sum(-1, keepdims=True) + jnp.exp(sk - m))


def swa_prompt(q, k, v, rel_table, sink):
    B, T = q.shape[:2]
    nb = T // WINDOW
    qb = q.reshape(B, nb, WINDOW, S_KV_HEADS, S_GROUP, S_HEAD_DIM)
    kb = k.reshape(B, nb, WINDOW, S_KV_HEADS, S_HEAD_DIM)
    vb = v.reshape(B, nb, WINDOW, S_KV_HEADS, S_HEAD_DIM)
    shift = ((0, 0), (1, 0), (0, 0), (0, 0), (0, 0))
    kk = jnp.concatenate([jnp.pad(kb, shift)[:, :-1], kb], axis=2)
    vv = jnp.concatenate([jnp.pad(vb, shift)[:, :-1], vb], axis=2)
    s = jnp.einsum('bnqhgd,bnkhd->bnhgqk', qb, kk).astype(jnp.float32) * (S_HEAD_DIM ** -0.5)
    qi = np.arange(WINDOW)[:, None]
    kj = np.arange(2 * WINDOW)[None, :]
    dist = qi + WINDOW - kj
    band = (dist >= 0) & (dist <= WINDOW)
    blk = np.arange(nb)[:, None, None]
    mask = band[None] & ((blk > 0) | (kj >= WINDOW)[None])
    s = jnp.where(mask[None, :, None, None], s + rel_bias(rel_table, dist), -jnp.inf)
    p = sink_softmax(s, sink.reshape(S_KV_HEADS, S_GROUP))
    o = jnp.einsum('bnhgqk,bnkhd->bnqhgd', p.astype(vv.dtype), vv)
    return o.reshape(B, T, S_HEADS * S_HEAD_DIM), k[:, -WINDOW:], v[:, -WINDOW:]


def swa_sample(q, k, v, buf_k, buf_v, rel_table, sink):
    B, T = q.shape[:2]
    kk = jnp.concatenate([buf_k.astype(k.dtype), k], axis=1)
    vv = jnp.concatenate([buf_v.astype(v.dtype), v], axis=1)
    qg = q.reshape(B, T, S_KV_HEADS, S_GROUP, S_HEAD_DIM)
    s = jnp.einsum('bqhgd,bkhd->bhgqk', qg, kk).astype(jnp.float32) * (S_HEAD_DIM ** -0.5)
    dist = np.arange(T)[:, None] + WINDOW - np.arange(WINDOW + T)[None, :]
    mask = (dist >= 0) & (dist <= WINDOW)
    s = jnp.where(mask, s + rel_bias(rel_table, dist), -jnp.inf)
    p = sink_softmax(s, sink.reshape(S_KV_HEADS, S_GROUP))
    o = jnp.einsum('bhgqk,bkhd->bqhgd', p.astype(vv.dtype), vv)
    return o.reshape(B, T, S_HEADS * S_HEAD_DIM), kk[:, -WINDOW:], vv[:, -WINDOW:]


def mlstm_mix(q, k, v, ig, fg, C0, n0, m0):
    B, T = q.shape[:2]
    L = M_CHUNK if T % M_CHUNK == 0 else T
    nc = T // L

    def to_chunks(a):
        a = a.reshape((B, nc, L) + a.shape[2:])
        return jnp.moveaxis(jnp.moveaxis(a, 1, 0), 2, 3)

    causal = np.tril(np.ones((L, L), dtype=bool))

    def step(carry, inp):
        C, n, m = carry
        qc, kc, vc, ic, lfc = inp
        b = jnp.cumsum(lfc, axis=-1)
        dmat = jnp.where(causal, b[..., :, None] - b[..., None, :] + ic[..., None, :], -jnp.inf)
        inter = b + m[..., None]
        mhat = jnp.maximum(inter, dmat.max(-1))
        w_intra = jnp.exp(dmat - mhat[..., None])
        w_inter = jnp.exp(inter - mhat)
        s = jnp.einsum('bhld,bhsd->bhls', qc, kc) * w_intra
        num = jnp.einsum('bhls,bhsv->bhlv', s, vc) + w_inter[..., None] * jnp.einsum('bhld,bhdv->bhlv', qc, C)
        den = s.sum(-1) + w_inter * jnp.einsum('bhld,bhd->bhl', qc, n)
        h = num / jnp.maximum(jnp.abs(den), jnp.exp(-mhat))[..., None]
        b_end = b[..., -1]
        g = ic + b_end[..., None] - b
        m_new = jnp.maximum(b_end + m, g.max(-1))
        decay = jnp.exp(b_end + m - m_new)
        kw = kc * jnp.exp(g - m_new[..., None])[..., None]
        C_new = decay[..., None, None] * C + jnp.einsum('bhld,bhlv->bhdv', kw, vc)
        n_new = decay[..., None] * n + kw.sum(2)
        return (C_new, n_new, m_new), h

    xs = (to_chunks(q), to_chunks(k), to_chunks(v), to_chunks(ig), to_chunks(jax.nn.log_sigmoid(fg)))
    (C, n, m), h = lax.scan(step, (C0, n0, m0), xs)
    h = jnp.moveaxis(jnp.moveaxis(h, 3, 2), 0, 1).reshape(B, T, M_HEADS, M_DV)
    return h, C, n, m


def routed_experts(xt, eid, tok, wt, w_gate, w_up, w_down):
    M, D = xt.shape
    A = eid.shape[0]
    nblk = -(-A // MOE_BLOCK) + N_EXPERTS
    P = nblk * MOE_BLOCK
    order = jnp.argsort(eid)
    eid_s, tok_s, wt_s = eid[order], tok[order], wt[order]
    counts = jnp.bincount(eid, length=N_EXPERTS)
    padded = (counts + MOE_BLOCK - 1) // MOE_BLOCK * MOE_BLOCK
    pad_end = jnp.cumsum(padded)
    pad_start = pad_end - padded
    start = jnp.cumsum(counts) - counts
    pos = pad_start[eid_s] + jnp.arange(A) - start[eid_s]
    row_tok = jnp.full((P,), M, jnp.int32).at[pos].set(tok_s)
    row_w = jnp.zeros((P,), jnp.float32).at[pos].set(wt_s)
    blk_e = jnp.minimum(jnp.searchsorted(pad_end, jnp.arange(nblk) * MOE_BLOCK, side='right'), N_EXPERTS - 1)
    x_pad = jnp.concatenate([xt, jnp.zeros((1, D), xt.dtype)], axis=0)
    xb = x_pad[row_tok].reshape(nblk, MOE_BLOCK, D)

    def expert_block(args):
        xblk, e = args
        h = jax.nn.silu(xblk @ w_gate[e]) * (xblk @ w_up[e])
        return h @ w_down[e]

    yb = lax.map(expert_block, (xb, blk_e))
    y = jnp.zeros((M + 1, D), jnp.float32).at[row_tok].add(yb.reshape(P, D).astype(jnp.float32) * row_w[:, None])
    return y[:M].astype(xt.dtype)


def hier_moe(x, w_rg, b_rg, w_re, b_re, w_eg, w_eu, w_ed):
    B, T, D = x.shape
    M = B * T
    xt = x.reshape(M, D)
    g_logits = (xt @ w_rg + b_rg).astype(jnp.float32)
    g_prob = jax.nn.softmax(g_logits, axis=-1)
    g_idx = jnp.argmax(g_logits, axis=-1)
    g_w = jnp.take_along_axis(g_prob, g_idx[:, None], axis=-1)
    e_logits = (xt @ w_re + b_re).astype(jnp.float32).reshape(M, N_GROUPS, EXPERTS_PER_GROUP)
    e_logits = jnp.take_along_axis(e_logits, g_idx[:, None, None], axis=1)[:, 0]
    top_p, top_i = lax.top_k(jax.nn.softmax(e_logits, axis=-1), TOP_K)
    gate = g_w * top_p / top_p.sum(-1, keepdims=True)
    eid = (g_idx[:, None] * EXPERTS_PER_GROUP + top_i).reshape(-1).astype(jnp.int32)
    tok = jnp.repeat(jnp.arange(M, dtype=jnp.int32), TOP_K)
    y = routed_experts(xt, eid, tok, gate.reshape(-1), w_eg, w_eu, w_ed)
    return y.reshape(B, T, D)


def layer(x, p_i, C0, n0, m0, buf_k, buf_v, w_in, b_in, mh_gain, w_a, w_b, w_out, rel_table, sink,
          ln_g, ln_b, w_rg, b_rg, w_re, b_re, w_eg, w_eu, w_ed, w_pg, w_pp):
    B, T, _ = x.shape
    f32 = jnp.float32
    z = x @ w_in + b_in
    qm, km, vm, og, ig, fg, qs, ks, vs, ga, gb = jnp.split(z, _IN_SPLITS, axis=-1)
    q = qm.reshape(B, T, M_HEADS, M_DQK).astype(f32) * (M_DQK ** -0.5)
    k = km.reshape(B, T, M_HEADS, M_DQK).astype(f32)
    v = vm.reshape(B, T, M_HEADS, M_DV).astype(f32)
    h, C, n, m = mlstm_mix(q, k, v, ig.astype(f32), fg.astype(f32), C0.astype(f32), n0.astype(f32), m0.astype(f32))
    mu = h.mean(-1, keepdims=True)
    var = jnp.square(h - mu).mean(-1, keepdims=True)
    h = (h - mu) * lax.rsqrt(var + LN_EPS) * mh_gain.reshape(M_HEADS, M_DV).astype(f32)
    ya = (h.reshape(B, T, M_HEADS * M_DV) * jax.nn.sigmoid(og.astype(f32))).astype(x.dtype)
    qs = qs.reshape(B, T, S_HEADS, S_HEAD_DIM)
    ks = ks.reshape(B, T, S_KV_HEADS, S_HEAD_DIM)
    vs = vs.reshape(B, T, S_KV_HEADS, S_HEAD_DIM)
    if buf_k is None:
        yb, kw, vw = swa_prompt(qs, ks, vs, rel_table, sink)
    else:
        yb, kw, vw = swa_sample(qs, ks, vs, buf_k, buf_v, rel_table, sink)
    mix = (jax.nn.sigmoid(ga) * (ya @ w_a) + jax.nn.sigmoid(gb) * (yb @ w_b)) @ w_out
    x = layer_norm(DN_ALPHA * x + mix, ln_g[0], ln_b[0])
    x = layer_norm(DN_ALPHA * x + hier_moe(x, w_rg, b_rg, w_re, b_re, w_eg, w_eu, w_ed), ln_g[1], ln_b[1])
    ple = jax.nn.sigmoid(x @ w_pg) * (p_i @ w_pp)
    x = layer_norm(DN_ALPHA * x + ple, ln_g[2], ln_b[2])
    return x, (C, n, m, kw, vw)


def setup_inputs(seed: int = 0) -> dict:
    key = jax.random.key(seed)
    ks = jax.random.split(key, 32)
    f32 = jnp.float32

    def nrm(k, shape, scale):
        return jax.random.normal(k, shape, f32) * scale

    b_in = nrm(ks[10], (DEPTH, D_IN), 0.01)
    b_in = b_in.at[:, F_OFF:F_OFF + M_HEADS].add(jnp.linspace(3.0, 6.0, M_HEADS))
    return {
        'x_prompt': nrm(ks[0], (BATCH, SEQ, D_MODEL), 1.0),
        'x_sample': nrm(ks[1], (DEC_BATCH, DEC_SEQ, D_MODEL), 1.0),
        'state_mlstm_C': nrm(ks[2], (DEPTH, DEC_BATCH, M_HEADS, M_DQK, M_DV), 0.5),
        'state_mlstm_n': nrm(ks[3], (DEPTH, DEC_BATCH, M_HEADS, M_DQK), 0.5),
        'state_mlstm_m': nrm(ks[4], (DEPTH, DEC_BATCH, M_HEADS), 1.0),
        'state_swa_k': nrm(ks[5], (DEPTH, DEC_BATCH, WINDOW, S_KV_HEADS, S_HEAD_DIM), 1.0),
        'state_swa_v': nrm(ks[6], (DEPTH, DEC_BATCH, WINDOW, S_KV_HEADS, S_HEAD_DIM), 1.0),
        'p_prompt': nrm(ks[7], (DEPTH, BATCH, SEQ, D_PLE), 1.0),
        'p_sample': nrm(ks[8], (DEPTH, DEC_BATCH, DEC_SEQ, D_PLE), 1.0),
        'w_in': nrm(ks[9], (DEPTH, D_MODEL, D_IN), D_MODEL ** -0.5),
        'b_in': b_in,
        'mh_gain': 1.0 + nrm(ks[11], (DEPTH, M_HEADS * M_DV), 0.02),
        'w_a': nrm(ks[12], (DEPTH, M_HEADS * M_DV, D_MODEL), (M_HEADS * M_DV) ** -0.5),
        'w_b': nrm(ks[13], (DEPTH, S_HEADS * S_HEAD_DIM, D_MODEL), (S_HEADS * S_HEAD_DIM) ** -0.5),
        'w_out': nrm(ks[14], (DEPTH, D_MODEL, D_MODEL), D_MODEL ** -0.5 * DN_BETA),
        'rel_table': nrm(ks[15], (REL_BUCKETS, S_HEADS), 0.5),
        'w_sink': nrm(ks[16], (DEPTH, S_HEADS), 0.5),
        'ln_g': 1.0 + nrm(ks[17], (DEPTH, 3, D_MODEL), 0.02),
        'ln_b': nrm(ks[18], (DEPTH, 3, D_MODEL), 0.01),
        'w_rg': nrm(ks[19], (DEPTH, D_MODEL, N_GROUPS), D_MODEL ** -0.5),
        'b_rg': nrm(ks[20], (DEPTH, N_GROUPS), 0.01),
        'w_re': nrm(ks[21], (DEPTH, D_MODEL, N_EXPERTS), D_MODEL ** -0.5),
        'b_re': nrm(ks[22], (DEPTH, N_EXPERTS), 0.01),
        'w_eg': nrm(ks[23], (DEPTH, N_EXPERTS, D_MODEL, D_EXPERT), D_MODEL ** -0.5 * DN_BETA),
        'w_eu': nrm(ks[24], (DEPTH, N_EXPERTS, D_MODEL, D_EXPERT), D_MODEL ** -0.5 * DN_BETA),
        'w_ed': nrm(ks[25], (DEPTH, N_EXPERTS, D_EXPERT, D_MODEL), D_EXPERT ** -0.5 * DN_BETA),
        'w_pg': nrm(ks[26], (DEPTH, D_MODEL, D_MODEL), D_MODEL ** -0.5),
        'w_pp': nrm(ks[27], (DEPTH, D_PLE, D_MODEL), D_PLE ** -0.5 * DN_BETA),
    }


def reference(x_prompt, x_sample, state_mlstm_C, state_mlstm_n, state_mlstm_m, state_swa_k, state_swa_v,
              p_prompt, p_sample, w_in, b_in, mh_gain, w_a, w_b, w_out, rel_table, w_sink, ln_g, ln_b,
              w_rg, b_rg, w_re, b_re, w_eg, w_eu, w_ed, w_pg, w_pp):
    bp = x_prompt.shape[0]
    zC = jnp.zeros((bp, M_HEADS, M_DQK, M_DV), jnp.float32)
    zn = jnp.zeros((bp, M_HEADS, M_DQK), jnp.float32)
    zm = jnp.zeros((bp, M_HEADS), jnp.float32)
    xp, xs = x_prompt, x_sample
    new_p, new_s = [], []
    for i in range(DEPTH):
        wts = (w_in[i], b_in[i], mh_gain[i], w_a[i], w_b[i], w_out[i], rel_table, w_sink[i], ln_g[i], ln_b[i],
               w_rg[i], b_rg[i], w_re[i], b_re[i], w_eg[i], w_eu[i], w_ed[i], w_pg[i], w_pp[i])
        xp, sp = layer(xp, p_prompt[i], zC, zn, zm, None, None, *wts)
        xs, ss = layer(xs, p_sample[i], state_mlstm_C[i], state_mlstm_n[i], state_mlstm_m[i],
                       state_swa_k[i], state_swa_v[i], *wts)
        new_p.append(sp)
        new_s.append(ss)

    def stk(lst, j):
        return jnp.stack([s[j] for s in lst], axis=0)

    return (xp, xs,
            stk(new_p, 0), stk(new_p, 1), stk(new_p, 2), stk(new_p, 3), stk(new_p, 4),
            stk(new_s, 0), stk(new_s, 1), stk(new_s, 2), stk(new_s, 3), stk(new_s, 4))
```

```python
import functools

import numpy as np
import jax
import jax.numpy as jnp
from jax import lax
from jax.experimental import pallas as pl
from jax.experimental.pallas import tpu as pltpu

F32 = jnp.float32
BF16 = jnp.bfloat16
HIGHEST = lax.Precision.HIGHEST

D_MODEL = 1024
DEPTH = 4
D_PLE = 256
M_HEADS = 4
M_DQK = 128
M_DV = 256
S_HEADS = 16
S_KV_HEADS = 4
S_HEAD_DIM = 64
S_GROUP = 4
WINDOW = 128
REL_BUCKETS = 32
N_GROUPS = 4
EXPERTS_PER_GROUP = 8
N_EXPERTS = 32
TOP_K = 2
D_EXPERT = 512
DN_ALPHA = (2 * DEPTH) ** 0.25
LN_EPS = 1e-5
NEG = -1e30

ZQ, ZK, ZV, ZOG = 0, 512, 1024, 2048
ZGA, ZGB, ZQS, ZKS, ZVS, ZG = 3072, 4096, 5120, 6144, 6400, 6656
NZ = 6784
LANES = 128
VMEM_LIMIT = 56 * 1024 * 1024


def _cparams(sem):
    return pltpu.CompilerParams(dimension_semantics=sem, vmem_limit_bytes=VMEM_LIMIT)


def _const_spec(shape):
    nd = len(shape)
    return pl.BlockSpec(shape, lambda *_: (0,) * nd, pipeline_mode=pl.Buffered(1))


def _layer_norm(y, g, b):
    mu = jnp.mean(y, axis=-1, keepdims=True)
    d = y - mu
    var = jnp.mean(d * d, axis=-1, keepdims=True)
    return d * lax.rsqrt(var + LN_EPS) * g + b


def _dot(a, b):
    return jnp.dot(a, b, preferred_element_type=F32)


def _dot_nt(a, b):
    return lax.dot_general(a, b, (((1,), (1,)), ((), ())), preferred_element_type=F32)


def _dot_tn(a, b):
    return lax.dot_general(a, b, (((0,), (0,)), ((), ())), preferred_element_type=F32)


def _inproj_kernel(x_ref, w_ref, b_ref, z_ref):
    xb = x_ref[...].astype(BF16)
    for j0 in range(0, NZ, 512):
        j1 = min(j0 + 512, NZ)
        z_ref[:, j0:j1] = _dot(xb, w_ref[:, j0:j1]) + b_ref[:, j0:j1]


def _inproj(x, w, b, tm):
    m = x.shape[0]
    return pl.pallas_call(
        _inproj_kernel,
        grid=(m // tm,),
        in_specs=[pl.BlockSpec((tm, D_MODEL), lambda i: (i, 0)),
                  _const_spec((D_MODEL, NZ)),
                  _const_spec((1, NZ))],
        out_specs=pl.BlockSpec((tm, NZ), lambda i: (i, 0)),
        out_shape=jax.ShapeDtypeStruct((m, NZ), F32),
        compiler_params=_cparams(("parallel",)),
        name="inproj",
    )(x, w, b)


def _log_sigmoid(x):
    return jnp.minimum(x, 0.0) - jnp.log(1.0 + jnp.exp(-jnp.abs(x)))


def _mlstm_kernel(zq, zk, zv, zog, zg, zgt, gain, c0, n0, m0, ya, co, no, mo, *, chunk, t_valid):
    L = chunk

    @pl.when(pl.program_id(1) == 0)
    def _():
        co[...] = c0[...]
        no[...] = n0[...]
        mo[...] = m0[...]

    g_c = zg[...]
    g_r = zgt[0]
    ig_c, lf_c = g_c, _log_sigmoid(g_c)
    ig_r, lf_r = g_r, _log_sigmoid(g_r)
    if t_valid < L:
        rv = lax.broadcasted_iota(jnp.int32, (L, 1), 0) < t_valid
        cv = lax.broadcasted_iota(jnp.int32, (1, L), 1) < t_valid
        ig_c, lf_c = jnp.where(rv, ig_c, NEG), jnp.where(rv, lf_c, 0.0)
        ig_r, lf_r = jnp.where(cv, ig_r, NEG), jnp.where(cv, lf_r, 0.0)
    ri = lax.broadcasted_iota(jnp.int32, (L, L), 0)
    ci = lax.broadcasted_iota(jnp.int32, (L, L), 1)
    causal = ri >= ci
    b_c = jnp.dot(causal.astype(F32), lf_c, precision=HIGHEST, preferred_element_type=F32)
    b_r = jnp.dot(lf_r, (ri <= ci).astype(F32), precision=HIGHEST, preferred_element_type=F32)

    for h in range(M_HEADS):
        icol, irow = ig_c[:, h:h + 1], ig_r[h:h + 1, :]
        bcol, brow = b_c[:, 4 + h:5 + h], b_r[4 + h:5 + h, :]
        m = mo[0, h:h + 1, 0:1]
        C = co[0, h]
        n = no[0, h:h + 1, :]
        q = zq[:, h * M_DQK:(h + 1) * M_DQK] * (M_DQK ** -0.5)
        k = zk[:, h * M_DQK:(h + 1) * M_DQK]
        v = zv[:, h * M_DV:(h + 1) * M_DV]
        qb, kb, vb = q.astype(BF16), k.astype(BF16), v.astype(BF16)

        dmat = jnp.where(causal, bcol - brow + irow, NEG)
        inter = bcol + m
        mhat = jnp.maximum(inter, jnp.max(dmat, axis=-1, keepdims=True))
        w_intra = jnp.exp(dmat - mhat)
        w_inter = jnp.exp(inter - mhat)
        s = _dot_nt(qb, kb) * w_intra
        num = _dot(s.astype(BF16), vb) + w_inter * _dot(qb, C.astype(BF16))
        den = jnp.sum(s, axis=-1, keepdims=True) + w_inter * jnp.sum(q * n, axis=-1, keepdims=True)
        hh = num / jnp.maximum(jnp.abs(den), jnp.exp(-mhat))

        b_end = bcol[L - 1:L, :]
        gcol = icol + b_end - bcol
        m_new = jnp.maximum(b_end + m, jnp.max(gcol, axis=0, keepdims=True))
        decay = jnp.exp(b_end + m - m_new)
        kw = k * jnp.exp(gcol - m_new)
        co[0, h] = decay * C + _dot_tn(kw.astype(BF16), vb)
        no[0, h:h + 1, :] = decay * n + jnp.sum(kw, axis=0, keepdims=True)
        mo[0, h:h + 1, :] = jnp.broadcast_to(m_new, (1, LANES))

        mu = jnp.mean(hh, axis=-1, keepdims=True)
        d = hh - mu
        var = jnp.mean(d * d, axis=-1, keepdims=True)
        hn = d * lax.rsqrt(var + LN_EPS) * gain[:, h * M_DV:(h + 1) * M_DV]
        ya[:, h * M_DV:(h + 1) * M_DV] = hn * jax.nn.sigmoid(zog[:, h * M_DV:(h + 1) * M_DV])


def _mlstm(z, gain, c0, n0, m0, *, batch, t_pad, t_valid, chunk):
    nc = t_pad // chunk
    zgt = z[:, ZG:ZG + LANES].reshape(batch * nc, chunk, LANES).transpose(0, 2, 1)
    row = lambda b, c: b * nc + c
    kern = functools.partial(_mlstm_kernel, chunk=chunk, t_valid=min(t_valid, chunk))
    st4 = pl.BlockSpec((1, M_HEADS, M_DQK, M_DV), lambda b, c: (b, 0, 0, 0))
    st3 = pl.BlockSpec((1, M_HEADS, LANES), lambda b, c: (b, 0, 0))
    return pl.pallas_call(
        kern,
        grid=(batch, nc),
        in_specs=[pl.BlockSpec((chunk, 512), lambda b, c: (row(b, c), ZQ // 512)),
                  pl.BlockSpec((chunk, 512), lambda b, c: (row(b, c), ZK // 512)),
                  pl.BlockSpec((chunk, 1024), lambda b, c: (row(b, c), ZV // 1024)),
                  pl.BlockSpec((chunk, 1024), lambda b, c: (row(b, c), ZOG // 1024)),
                  pl.BlockSpec((chunk, LANES), lambda b, c: (row(b, c), ZG // LANES)),
                  pl.BlockSpec((1, LANES, chunk), lambda b, c: (row(b, c), 0, 0)),
                  pl.BlockSpec((1, D_MODEL), lambda b, c: (0, 0)),
                  st4, st3, st3],
        out_specs=[pl.BlockSpec((chunk, D_MODEL), lambda b, c: (row(b, c), 0)), st4, st3, st3],
        out_shape=[jax.ShapeDtypeStruct((batch * t_pad, D_MODEL), F32),
                   jax.ShapeDtypeStruct((batch, M_HEADS, M_DQK, M_DV), F32),
                   jax.ShapeDtypeStruct((batch, M_HEADS, LANES), F32),
                   jax.ShapeDtypeStruct((batch, M_HEADS, LANES), F32)],
        compiler_params=_cparams(("parallel", "arbitrary")),
        name="mlstm",
    )(z, z, z, z, z, zgt, gain, c0, n0, m0)


def _swa_kernel(q_ref, kc_ref, vc_ref, kp_ref, vp_ref, bp_ref, bc_ref, sink_ref, o_ref, *, first_block_has_no_past):
    q = q_ref[...]
    have_past = pl.program_id(1) > 0
    outs = []
    for kvh in range(S_KV_HEADS):
        sl = slice(kvh * S_HEAD_DIM, (kvh + 1) * S_HEAD_DIM)
        qg = jnp.concatenate(
            [q[:, (kvh * S_GROUP + g) * S_HEAD_DIM:(kvh * S_GROUP + g + 1) * S_HEAD_DIM] for g in range(S_GROUP)],
            axis=0).astype(BF16)
        kp, vp = kp_ref[:, sl].astype(BF16), vp_ref[:, sl].astype(BF16)
        kc, vc = kc_ref[:, sl].astype(BF16), vc_ref[:, sl].astype(BF16)
        sp = _dot_nt(qg, kp) * (S_HEAD_DIM ** -0.5) + bp_ref[kvh]
        sc = _dot_nt(qg, kc) * (S_HEAD_DIM ** -0.5) + bc_ref[kvh]
        if first_block_has_no_past:
            sp = jnp.where(have_past, sp, NEG)
        sk = sink_ref[kvh]
        mx = jnp.maximum(jnp.maximum(jnp.max(sp, axis=-1, keepdims=True), jnp.max(sc, axis=-1, keepdims=True)), sk)
        pp, pc = jnp.exp(sp - mx), jnp.exp(sc - mx)
        den = jnp.sum(pp, axis=-1, keepdims=True) + jnp.sum(pc, axis=-1, keepdims=True) + jnp.exp(sk - mx)
        o = (_dot(pp.astype(BF16), vp) + _dot(pc.astype(BF16), vc)) / den
        tq = q.shape[0]
        outs.extend(o[g * tq:(g + 1) * tq] for g in range(S_GROUP))
    o_ref[...] = jnp.concatenate(outs, axis=1)


def _swa_bias(rel_table, sink, tq):
    t = np.arange(tq)[:, None]
    d_prev = t + WINDOW - np.arange(WINDOW)[None, :]
    d_cur = t - np.arange(tq)[None, :]

    def bucket(dist):
        nn = np.maximum(dist, 0)
        max_exact = REL_BUCKETS // 2
        large = max_exact + (np.log(np.maximum(nn, 1) / max_exact) / np.log(WINDOW / max_exact)
                             * (REL_BUCKETS - max_exact)).astype(np.int32)
        large = np.minimum(large, REL_BUCKETS - 1)
        return np.where(nn < max_exact, nn, large).astype(np.int32)

    def build(dist, ok):
        bias = jnp.transpose(rel_table[bucket(dist)].astype(F32), (2, 0, 1))
        bias = jnp.where(ok[None], bias, NEG)
        return bias.reshape(S_KV_HEADS, S_GROUP * tq, dist.shape[1])

    bp = build(d_prev, d_prev <= WINDOW)
    bc = build(d_cur, d_cur >= 0)
    sk = jnp.broadcast_to(sink.astype(F32).reshape(S_KV_HEADS, S_GROUP, 1, 1),
                          (S_KV_HEADS, S_GROUP, tq, 1)).reshape(S_KV_HEADS, S_GROUP * tq, 1)
    return bp, bc, sk


def _swa_call(kern, grid, in_specs, rows, operands, tq, name):
    bias_specs = [pl.BlockSpec((S_KV_HEADS, S_GROUP * tq, WINDOW), lambda b, n: (0, 0, 0)),
                  pl.BlockSpec((S_KV_HEADS, S_GROUP * tq, tq), lambda b, n: (0, 0, 0)),
                  pl.BlockSpec((S_KV_HEADS, S_GROUP * tq, 1), lambda b, n: (0, 0, 0))]
    return pl.pallas_call(
        kern,
        grid=grid,
        in_specs=in_specs + bias_specs,
        out_specs=pl.BlockSpec((tq, D_MODEL), lambda b, n: (b * grid[1] + n, 0)),
        out_shape=jax.ShapeDtypeStruct((rows, D_MODEL), F32),
        compiler_params=_cparams(("parallel", "arbitrary")),
        name=name,
    )(*operands)


def _swa_prompt(z, bp, bc, sk, *, batch, seq):
    nb = seq // WINDOW
    row = lambda b, n: b * nb + n
    prev = lambda b, n: jnp.maximum(b * nb + n - 1, 0)
    in_specs = [pl.BlockSpec((WINDOW, 1024), lambda b, n: (row(b, n), ZQS // 1024)),
                pl.BlockSpec((WINDOW, 256), lambda b, n: (row(b, n), ZKS // 256)),
                pl.BlockSpec((WINDOW, 256), lambda b, n: (row(b, n), ZVS // 256)),
                pl.BlockSpec((WINDOW, 256), lambda b, n: (prev(b, n), ZKS // 256)),
                pl.BlockSpec((WINDOW, 256), lambda b, n: (prev(b, n), ZVS // 256))]
    kern = functools.partial(_swa_kernel, first_block_has_no_past=True)
    return _swa_call(kern, (batch, nb), in_specs, batch * seq, (z, z, z, z, z, bp, bc, sk), WINDOW, "swa_prompt")


def _swa_sample(z, buf_k, buf_v, bp, bc, sk, *, batch, t_pad):
    in_specs = [pl.BlockSpec((t_pad, 1024), lambda b, n: (b, ZQS // 1024)),
                pl.BlockSpec((t_pad, 256), lambda b, n: (b, ZKS // 256)),
                pl.BlockSpec((t_pad, 256), lambda b, n: (b, ZVS // 256)),
                pl.BlockSpec((None, WINDOW, 256), lambda b, n: (b, 0, 0)),
                pl.BlockSpec((None, WINDOW, 256), lambda b, n: (b, 0, 0))]
    kern = functools.partial(_swa_kernel, first_block_has_no_past=False)
    return _swa_call(kern, (batch, 1), in_specs, batch * t_pad, (z, z, z, buf_k, buf_v, bp, bc, sk), t_pad,
                     "swa_sample")


def _merge_kernel(x_ref, ya_ref, yb_ref, ga_ref, gb_ref, wa_ref, wb_ref, wo_ref, lng_ref, lnb_ref, wr_ref, br_ref,
                  x1_ref, route_ref):
    a = _dot(ya_ref[...].astype(BF16), wa_ref[...])
    b = _dot(yb_ref[...].astype(BF16), wb_ref[...])
    u = jax.nn.sigmoid(ga_ref[...]) * a + jax.nn.sigmoid(gb_ref[...]) * b
    mix = _dot(u.astype(BF16), wo_ref[...])
    x1 = _layer_norm(DN_ALPHA * x_ref[...] + mix, lng_ref[0:1, :], lnb_ref[0:1, :])
    x1_ref[...] = x1

    logits = jnp.dot(x1, wr_ref[...], precision=HIGHEST, preferred_element_type=F32) + br_ref[...]
    lane = lax.broadcasted_iota(jnp.int32, logits.shape, 1)
    lanef = lane.astype(F32)
    gmask = lane < N_GROUPS
    gl = jnp.where(gmask, logits, NEG)
    gmax = jnp.max(gl, axis=-1, keepdims=True)
    gidx = jnp.min(jnp.where(gmask & (gl == gmax), lanef, 999.0), axis=-1, keepdims=True)
    g_w = 1.0 / jnp.sum(jnp.where(gmask, jnp.exp(gl - gmax), 0.0), axis=-1, keepdims=True)
    lo = N_GROUPS + EXPERTS_PER_GROUP * gidx
    emask = (lanef >= lo) & (lanef < lo + EXPERTS_PER_GROUP)
    el = jnp.where(emask, logits, NEG)
    emax = jnp.max(el, axis=-1, keepdims=True)
    ep = jnp.where(emask, jnp.exp(el - emax), 0.0)
    prob = ep / jnp.sum(ep, axis=-1, keepdims=True)
    prob = jnp.where(emask, prob, -1.0)
    p1 = jnp.max(prob, axis=-1, keepdims=True)
    i1 = jnp.min(jnp.where(prob == p1, lanef, 999.0), axis=-1, keepdims=True)
    prob2 = jnp.where(lanef == i1, -1.0, prob)
    p2 = jnp.max(prob2, axis=-1, keepdims=True)
    i2 = jnp.min(jnp.where(prob2 == p2, lanef, 999.0), axis=-1, keepdims=True)
    tot = p1 + p2
    route = jnp.where(lane == 0, i1 - N_GROUPS,
                      jnp.where(lane == 1, i2 - N_GROUPS,
                                jnp.where(lane == 2, g_w * p1 / tot,
                                          jnp.where(lane == 3, g_w * p2 / tot, 0.0))))
    route_ref[...] = route


def _merge(x, ya, yb, z, wa, wb, wo, lng, lnb, wr, br, tm):
    m = x.shape[0]
    rowblk = lambda w, j: pl.BlockSpec((tm, w), lambda i: (i, j))
    return pl.pallas_call(
        _merge_kernel,
        grid=(m // tm,),
        in_specs=[rowblk(D_MODEL, 0), rowblk(D_MODEL, 0), rowblk(D_MODEL, 0),
                  rowblk(D_MODEL, ZGA // D_MODEL), rowblk(D_MODEL, ZGB // D_MODEL),
                  _const_spec((D_MODEL, D_MODEL)), _const_spec((D_MODEL, D_MODEL)), _const_spec((D_MODEL, D_MODEL)),
                  _const_spec((3, D_MODEL)), _const_spec((3, D_MODEL)),
                  _const_spec((D_MODEL, LANES)), _const_spec((1, LANES))],
        out_specs=[rowblk(D_MODEL, 0), rowblk(LANES, 0)],
        out_shape=[jax.ShapeDtypeStruct((m, D_MODEL), F32), jax.ShapeDtypeStruct((m, LANES), F32)],
        compiler_params=_cparams(("parallel",)),
        name="merge",
    )(x, ya, yb, z, z, wa, wb, wo, lng, lnb, wr, br)


def _moe_kernel(blk_e_ref, nused_ref, src_ref, dst_ref, x_hbm, wg_ref, wu_ref, wd_ref, y_hbm,
                xbuf, ybuf, gsem, ssem, *, blk):
    b = pl.program_id(0)
    nused = nused_ref[0]

    def gather(block, slot):
        def body(r, carry):
            tok = src_ref[block * blk + r]
            pltpu.make_async_copy(x_hbm.at[pl.ds(tok, 1)], xbuf.at[slot, pl.ds(r, 1)], gsem.at[slot]).start()
            return carry
        lax.fori_loop(0, blk, body, 0, unroll=8)

    def wait_gather(slot):
        pltpu.make_async_copy(x_hbm.at[pl.ds(0, blk)], xbuf.at[slot], gsem.at[slot]).wait()

    def scatter(block, slot):
        def body(r, carry):
            row = dst_ref[block * blk + r]
            pltpu.make_async_copy(ybuf.at[slot, pl.ds(r, 1)], y_hbm.at[pl.ds(row, 1)], ssem.at[slot]).start()
            return carry
        lax.fori_loop(0, blk, body, 0, unroll=8)

    def wait_scatter(slot):
        pltpu.make_async_copy(ybuf.at[slot], y_hbm.at[pl.ds(0, blk)], ssem.at[slot]).wait()

    @pl.when(b == 0)
    def _():
        ybuf[...] = jnp.zeros_like(ybuf)
        n_real = y_hbm.shape[0] - 2 * blk
        for s in range(2):
            cp = pltpu.make_async_copy(ybuf.at[s], y_hbm.at[pl.ds(n_real + s * blk, blk)], ssem.at[s])
            cp.start()
            cp.wait()
        gather(0, 0)

    @pl.when(b + 1 < nused)
    def _():
        gather(b + 1, (b + 1) % 2)

    @pl.when(b < nused)
    def _():
        slot = b % 2
        wait_gather(slot)

        @pl.when(b >= 2)
        def _():
            wait_scatter(slot)

        xb = xbuf[slot].astype(BF16)
        g = _dot(xb, wg_ref[0])
        u = _dot(xb, wu_ref[0])
        hid = (g * jax.nn.sigmoid(g) * u).astype(BF16)
        ybuf[slot] = _dot(hid, wd_ref[0])
        scatter(b, slot)

    @pl.when(b == pl.num_programs(0) - 1)
    def _():
        wait_scatter(0)
        wait_scatter(1)


def _route_tables(eid, blk, nblk):
    m = eid.shape[0]
    a = 2 * m
    flat = eid.reshape(a)
    onehot = (flat[:, None] == jnp.arange(N_EXPERTS, dtype=jnp.int32)[None, :]).astype(jnp.int32)
    csum = jnp.cumsum(onehot, axis=0)
    counts = csum[-1]
    rank = jnp.take_along_axis(csum, flat[:, None], axis=1)[:, 0] - 1
    padded = (counts + blk - 1) // blk * blk
    pad_end = jnp.cumsum(padded)
    pos = (pad_end - padded)[flat] + rank
    p = nblk * blk
    arange_a = jnp.arange(a, dtype=jnp.int32)
    src = jnp.zeros((p,), jnp.int32).at[pos].set(arange_a // 2)
    dst = (a + jnp.arange(p, dtype=jnp.int32) % (2 * blk)).at[pos].set(arange_a)
    nused = (pad_end[-1] // blk).astype(jnp.int32)
    blk_start = jnp.minimum(jnp.arange(nblk, dtype=jnp.int32), nused - 1) * blk
    blk_e = jnp.minimum(jnp.searchsorted(pad_end, blk_start, side='right'), N_EXPERTS - 1).astype(jnp.int32)
    return blk_e, nused.reshape(1), src, dst


def _moe(x1, eid, wg, wu, wd, blk):
    m = x1.shape[0]
    nblk = (2 * m) // blk + N_EXPERTS
    blk_e, nused, src, dst = _route_tables(eid, blk, nblk)
    wspec = lambda s: pl.BlockSpec((1,) + s, lambda b, be, nu, sr, ds: (be[b], 0, 0))
    grid_spec = pltpu.PrefetchScalarGridSpec(
        num_scalar_prefetch=4,
        grid=(nblk,),
        in_specs=[pl.BlockSpec(memory_space=pl.ANY),
                  wspec((D_MODEL, D_EXPERT)), wspec((D_MODEL, D_EXPERT)), wspec((D_EXPERT, D_MODEL))],
        out_specs=pl.BlockSpec(memory_space=pl.ANY),
        scratch_shapes=[pltpu.VMEM((2, blk, D_MODEL), F32), pltpu.VMEM((2, blk, D_MODEL), F32),
                        pltpu.SemaphoreType.DMA((2,)), pltpu.SemaphoreType.DMA((2,))],
    )
    y2 = pl.pallas_call(
        functools.partial(_moe_kernel, blk=blk),
        grid_spec=grid_spec,
        out_shape=jax.ShapeDtypeStruct((2 * m + 2 * blk, D_MODEL), F32),
        compiler_params=_cparams(("arbitrary",)),
        name="moe",
    )(blk_e, nused, src, dst, x1, wg, wu, wd)
    return y2.reshape(m + blk, 2 * D_MODEL)


def _post_kernel(x1_ref, y2_ref, route_ref, p_ref, wpg_ref, wpp_ref, lng_ref, lnb_ref, o_ref):
    route = route_ref[...]
    moe = route[:, 2:3] * y2_ref[:, :D_MODEL] + route[:, 3:4] * y2_ref[:, D_MODEL:]
    x2 = _layer_norm(DN_ALPHA * x1_ref[...] + moe, lng_ref[1:2, :], lnb_ref[1:2, :])
    ple = jax.nn.sigmoid(_dot(x2.astype(BF16), wpg_ref[...])) * _dot(p_ref[...].astype(BF16), wpp_ref[...])
    o_ref[...] = _layer_norm(DN_ALPHA * x2 + ple, lng_ref[2:3, :], lnb_ref[2:3, :])


def _post(x1, y2, route, p, wpg, wpp, lng, lnb, tm):
    m = x1.shape[0]
    rowblk = lambda w: pl.BlockSpec((tm, w), lambda i: (i, 0))
    return pl.pallas_call(
        _post_kernel,
        grid=(m // tm,),
        in_specs=[rowblk(D_MODEL), rowblk(2 * D_MODEL), rowblk(LANES), rowblk(D_PLE),
                  _const_spec((D_MODEL, D_MODEL)), _const_spec((D_PLE, D_MODEL)),
                  _const_spec((3, D_MODEL)), _const_spec((3, D_MODEL))],
        out_specs=rowblk(D_MODEL),
        out_shape=jax.ShapeDtypeStruct((m, D_MODEL), F32),
        compiler_params=_cparams(("parallel",)),
        name="post",
    )(x1, y2, route, p, wpg, wpp, lng, lnb)


def _rearrange_in(w):
    pad = jnp.zeros(w.shape[:-1] + (NZ - 6664,), w.dtype)
    return jnp.concatenate([w[..., :3072], w[..., 4616:6664], w[..., 3080:4616], w[..., 3072:3080], pad], axis=-1)


def kernel(x_prompt, x_sample, state_mlstm_C, state_mlstm_n, state_mlstm_m, state_swa_k, state_swa_v, p_prompt, p_sample, w_in, b_in, mh_gain, w_a, w_b, w_out, rel_table, w_sink, ln_g, ln_b, w_rg, b_rg, w_re, b_re, w_eg, w_eu, w_ed, w_pg, w_pp):
    bp_, seq = x_prompt.shape[:2]
    bs_, tdec = x_sample.shape[:2]
    t_pad = 8
    mp = bp_ * seq
    ms = bs_ * tdec

    w_in_r = _rearrange_in(w_in).astype(BF16)
    b_in_r = _rearrange_in(b_in).reshape(DEPTH, 1, NZ)
    wa_b, wb_b, wo_b = w_a.astype(BF16), w_b.astype(BF16), w_out.astype(BF16)
    weg_b, weu_b, wed_b = w_eg.astype(BF16), w_eu.astype(BF16), w_ed.astype(BF16)
    wpg_b, wpp_b = w_pg.astype(BF16), w_pp.astype(BF16)
    w_r = jnp.concatenate([w_rg, w_re, jnp.zeros((DEPTH, D_MODEL, LANES - N_GROUPS - N_EXPERTS), F32)], axis=-1)
    b_r = jnp.concatenate([b_rg, b_re, jnp.zeros((DEPTH, LANES - N_GROUPS - N_EXPERTS), F32)], axis=-1)
    b_r = b_r.reshape(DEPTH, 1, LANES)
    gain = mh_gain.reshape(DEPTH, 1, D_MODEL)

    bias_p = _swa_bias(rel_table, w_sink[0], WINDOW)[:2]
    bias_s = _swa_bias(rel_table, w_sink[0], t_pad)[:2]

    zc = jnp.zeros((bp_, M_HEADS, M_DQK, M_DV), F32)
    zn = jnp.zeros((bp_, M_HEADS, LANES), F32)

    xp = x_prompt.reshape(mp, D_MODEL)
    xs = x_sample.reshape(ms, D_MODEL)
    st_p, st_s = [], []
    for i in range(DEPTH):
        z = _inproj(xp, w_in_r[i], b_in_r[i], 256)
        ya, c_new, n_new, m_new = _mlstm(z, gain[i], zc, zn, zn, batch=bp_, t_pad=seq, t_valid=seq, chunk=128)
        sk_p = _swa_bias(rel_table, w_sink[i], WINDOW)[2]
        yb = _swa_prompt(z, bias_p[0], bias_p[1], sk_p, batch=bp_, seq=seq)
        x1, route = _merge(xp, ya, yb, z, wa_b[i], wb_b[i], wo_b[i], ln_g[i], ln_b[i], w_r[i], b_r[i], 256)
        y2 = _moe(x1, route[:, :2].astype(jnp.int32), weg_b[i], weu_b[i], wed_b[i], 256)
        xp = _post(x1, y2, route, p_prompt[i].reshape(mp, D_PLE), wpg_b[i], wpp_b[i], ln_g[i], ln_b[i], 256)
        z3 = z.reshape(bp_, seq, NZ)
        st_p.append((c_new, n_new, m_new[:, :, 0],
                     z3[:, seq - WINDOW:, ZKS:ZKS + 256].reshape(bp_, WINDOW, S_KV_HEADS, S_HEAD_DIM),
                     z3[:, seq - WINDOW:, ZVS:ZVS + 256].reshape(bp_, WINDOW, S_KV_HEADS, S_HEAD_DIM)))

        xs_pad = jnp.pad(xs.reshape(bs_, tdec, D_MODEL), ((0, 0), (0, t_pad - tdec), (0, 0)))
        xs_pad = xs_pad.reshape(bs_ * t_pad, D_MODEL)
        z = _inproj(xs_pad, w_in_r[i], b_in_r[i], 256)
        m0 = jnp.broadcast_to(state_mlstm_m[i][:, :, None], (bs_, M_HEADS, LANES))
        ya, c_new, n_new, m_new = _mlstm(z, gain[i], state_mlstm_C[i], state_mlstm_n[i], m0,
                                         batch=bs_, t_pad=t_pad, t_valid=tdec, chunk=t_pad)
        sk_s = _swa_bias(rel_table, w_sink[i], t_pad)[2]
        buf_k = state_swa_k[i].reshape(bs_, WINDOW, 256)
        buf_v = state_swa_v[i].reshape(bs_, WINDOW, 256)
        yb = _swa_sample(z, buf_k, buf_v, bias_s[0], bias_s[1], sk_s, batch=bs_, t_pad=t_pad)
        x1, route = _merge(xs_pad, ya, yb, z, wa_b[i], wb_b[i], wo_b[i], ln_g[i], ln_b[i], w_r[i], b_r[i], 256)
        strip = lambda t: t.reshape(bs_, t_pad, t.shape[-1])[:, :tdec].reshape(ms, t.shape[-1])
        x1, route = strip(x1), strip(route)
        y2 = _moe(x1, route[:, :2].astype(jnp.int32), weg_b[i], weu_b[i], wed_b[i], 128)
        xs = _post(x1, y2, route, p_sample[i].reshape(ms, D_PLE), wpg_b[i], wpp_b[i], ln_g[i], ln_b[i], 256)
        z3 = z.reshape(bs_, t_pad, NZ)
        k_new = jnp.concatenate([buf_k[:, tdec:], z3[:, :tdec, ZKS:ZKS + 256]], axis=1)
        v_new = jnp.concatenate([buf_v[:, tdec:], z3[:, :tdec, ZVS:ZVS + 256]], axis=1)
        st_s.append((c_new, n_new, m_new[:, :, 0],
                     k_new.reshape(bs_, WINDOW, S_KV_HEADS, S_HEAD_DIM),
                     v_new.reshape(bs_, WINDOW, S_KV_HEADS, S_HEAD_DIM)))

    stk = lambda lst, j: jnp.stack([s[j] for s in lst], axis=0)
    return (xp.reshape(bp_, seq, D_MODEL), xs.reshape(bs_, tdec, D_MODEL),
            stk(st_p, 0), stk(st_p, 1), stk(st_p, 2), stk(st_p, 3), stk(st_p, 4),
            stk(st_s, 0), stk(st_s, 1), stk(st_s, 2), stk(st_s, 3), stk(st_s, 4))
```

```python
import functools

import numpy as np
import jax
import jax.numpy as jnp
from jax import lax
from jax.experimental import pallas as pl
from jax.experimental.pallas import tpu as pltpu

F32 = jnp.float32
BF16 = jnp.bfloat16
I32 = jnp.int32
HIGHEST = lax.Precision.HIGHEST

D_MODEL = 1024
DEPTH = 4
D_PLE = 256
M_HEADS = 4
M_DQK = 128
M_DV = 256
S_HEADS = 16
S_KV_HEADS = 4
S_HEAD_DIM = 64
S_GROUP = 4
WINDOW = 128
REL_BUCKETS = 32
N_GROUPS = 4
EXPERTS_PER_GROUP = 8
N_EXPERTS = 32
D_EXPERT = 512
DN_ALPHA = (2 * DEPTH) ** 0.25
LN_EPS = 1e-5
NEG = -1e30

IN_A = 3200
IN_QKV2 = (3080, 4616)
IN_GATES = (4616, 6664)
ZQ, ZK, ZV, ZOG, ZG = 0, 512, 1024, 2048, 3072
IN_B = 3584
ZGA, ZGB, ZQS, ZKS, ZVS = 0, 1024, 2048, 3072, 3328

LANES = 128
T_PAD = 8
TM = 256
BLK = 256
VMEM_LIMIT = 56 * 1024 * 1024


def _cparams(sem):
    return pltpu.CompilerParams(dimension_semantics=sem, vmem_limit_bytes=VMEM_LIMIT)


def _const_spec(shape):
    nd = len(shape)
    return pl.BlockSpec(shape, lambda *_: (0,) * nd, pipeline_mode=pl.Buffered(1))


def _layer_spec(shape, layer):
    nd = len(shape)
    return pl.BlockSpec((None,) + shape, lambda *_: (layer,) + (0,) * nd, pipeline_mode=pl.Buffered(1))


def _layer_norm(y, g, b):
    mu = jnp.mean(y, axis=-1, keepdims=True)
    d = y - mu
    var = jnp.mean(d * d, axis=-1, keepdims=True)
    return d * lax.rsqrt(var + LN_EPS) * g + b


def _dot(a, b):
    return jnp.dot(a, b, preferred_element_type=F32)


def _dot_nt(a, b):
    return lax.dot_general(a, b, (((1,), (1,)), ((), ())), preferred_element_type=F32)


def _dot_tn(a, b):
    return lax.dot_general(a, b, (((0,), (0,)), ((), ())), preferred_element_type=F32)


def _cast_into(dst_ref, src_ref, col0=0, step=512):
    n = src_ref.shape[-1]
    for j in range(0, n, step):
        w = min(step, n - j)
        dst_ref[:, col0 + j:col0 + j + w] = src_ref[:, j:j + w].astype(BF16)


def _inproj_kernel(*refs, widths):
    ng = len(widths)
    x_ref, w_refs, b_refs, z_ref, wbf = refs[0], refs[1:1 + ng], refs[1 + ng:1 + 2 * ng], refs[1 + 2 * ng], refs[2 + 2 * ng]

    @pl.when(pl.program_id(0) == 0)
    def _():
        c = 0
        for w_ref, n in zip(w_refs, widths):
            _cast_into(wbf, w_ref, c)
            c += n

    xb = x_ref[...].astype(BF16)
    c = 0
    for b_ref, n in zip(b_refs, widths):
        for j in range(0, n, 512):
            w = min(512, n - j)
            z_ref[:, c + j:c + j + w] = _dot(xb, wbf[:, c + j:c + j + w]) + b_ref[:, j:j + w]
        c += n


def _inproj(x, ws, bs, layer, name):
    m = x.shape[0]
    widths = tuple(w for _, w in ws)
    n_out = sum(widths)
    return pl.pallas_call(
        functools.partial(_inproj_kernel, widths=widths),
        grid=(m // TM,),
        in_specs=([pl.BlockSpec((TM, D_MODEL), lambda i: (i, 0))]
                  + [_layer_spec((D_MODEL, n), layer) for _, n in ws]
                  + [_layer_spec((1, n), layer) for _, n in bs]),
        out_specs=pl.BlockSpec((TM, n_out), lambda i: (i, 0)),
        out_shape=jax.ShapeDtypeStruct((m, n_out), F32),
        scratch_shapes=[pltpu.VMEM((D_MODEL, n_out), BF16)],
        compiler_params=_cparams(("arbitrary",)),
        name=name,
    )(x, *[w for w, _ in ws], *[b for b, _ in bs])


def _log_sigmoid(x):
    return jnp.minimum(x, 0.0) - jnp.log(1.0 + jnp.exp(-jnp.abs(x)))


def _mlstm_kernel(zq, zk, zv, zog, zg, zgt, gain, c0, n0, m0, ya, co, no, mo, *, chunk, t_valid):
    L = chunk

    @pl.when(pl.program_id(1) == 0)
    def _():
        co[...] = c0[...]
        no[...] = n0[...]
        mo[...] = m0[...]

    g_c = zg[...]
    g_r = zgt[0]
    ig_c, lf_c = g_c, _log_sigmoid(g_c)
    ig_r, lf_r = g_r, _log_sigmoid(g_r)
    if t_valid < L:
        rv = lax.broadcasted_iota(I32, (L, 1), 0) < t_valid
        cv = lax.broadcasted_iota(I32, (1, L), 1) < t_valid
        ig_c, lf_c = jnp.where(rv, ig_c, NEG), jnp.where(rv, lf_c, 0.0)
        ig_r, lf_r = jnp.where(cv, ig_r, NEG), jnp.where(cv, lf_r, 0.0)
    ri = lax.broadcasted_iota(I32, (L, L), 0)
    ci = lax.broadcasted_iota(I32, (L, L), 1)
    causal = ri >= ci
    b_c = jnp.dot(causal.astype(F32), lf_c, precision=HIGHEST, preferred_element_type=F32)
    b_r = jnp.dot(lf_r, (ri <= ci).astype(F32), precision=HIGHEST, preferred_element_type=F32)

    for h in range(M_HEADS):
        icol, irow = ig_c[:, h:h + 1], ig_r[h:h + 1, :]
        bcol, brow = b_c[:, 4 + h:5 + h], b_r[4 + h:5 + h, :]
        m = mo[0, h:h + 1, 0:1]
        C = co[0, h]
        n = no[0, h:h + 1, :]
        q = zq[:, h * M_DQK:(h + 1) * M_DQK] * (M_DQK ** -0.5)
        k = zk[:, h * M_DQK:(h + 1) * M_DQK]
        v = zv[:, h * M_DV:(h + 1) * M_DV]
        qb, kb, vb = q.astype(BF16), k.astype(BF16), v.astype(BF16)

        dmat = jnp.where(causal, bcol - brow + irow, NEG)
        inter = bcol + m
        mhat = jnp.maximum(inter, jnp.max(dmat, axis=-1, keepdims=True))
        w_intra = jnp.exp(dmat - mhat)
        w_inter = jnp.exp(inter - mhat)
        s = _dot_nt(qb, kb) * w_intra
        num = _dot(s.astype(BF16), vb) + w_inter * _dot(qb, C.astype(BF16))
        den = jnp.sum(s, axis=-1, keepdims=True) + w_inter * jnp.sum(q * n, axis=-1, keepdims=True)
        hh = num / jnp.maximum(jnp.abs(den), jnp.exp(-mhat))

        b_end = bcol[L - 1:L, :]
        gcol = icol + b_end - bcol
        m_new = jnp.maximum(b_end + m, jnp.max(gcol, axis=0, keepdims=True))
        decay = jnp.exp(b_end + m - m_new)
        kw = k * jnp.exp(gcol - m_new)
        co[0, h] = decay * C + _dot_tn(kw.astype(BF16), vb)
        no[0, h:h + 1, :] = decay * n + jnp.sum(kw, axis=0, keepdims=True)
        mo[0, h:h + 1, :] = jnp.broadcast_to(m_new, (1, LANES))

        mu = jnp.mean(hh, axis=-1, keepdims=True)
        d = hh - mu
        var = jnp.mean(d * d, axis=-1, keepdims=True)
        hn = d * lax.rsqrt(var + LN_EPS) * gain[:, h * M_DV:(h + 1) * M_DV]
        ya[:, h * M_DV:(h + 1) * M_DV] = hn * jax.nn.sigmoid(zog[:, h * M_DV:(h + 1) * M_DV])


def _mlstm(za, gain, layer, c0, n0, m0, st_layer, *, row0, batch, t_pad, t_valid, chunk):
    nc = t_pad // chunk
    rows = batch * t_pad
    zgt = lax.slice(za, (row0, ZG), (row0 + rows, ZG + LANES)).reshape(batch * nc, chunk, LANES).transpose(0, 2, 1)
    r0 = row0 // chunk
    row = lambda b, c: r0 + b * nc + c
    kern = functools.partial(_mlstm_kernel, chunk=chunk, t_valid=min(t_valid, chunk))
    st4 = pl.BlockSpec((1, M_HEADS, M_DQK, M_DV), lambda b, c: (b, 0, 0, 0))
    st3 = pl.BlockSpec((1, M_HEADS, LANES), lambda b, c: (b, 0, 0))
    in4 = pl.BlockSpec((None, 1, M_HEADS, M_DQK, M_DV), lambda b, c: (st_layer, b, 0, 0, 0))
    in3 = pl.BlockSpec((None, 1, M_HEADS, LANES), lambda b, c: (st_layer, b, 0, 0))
    return pl.pallas_call(
        kern,
        grid=(batch, nc),
        in_specs=[pl.BlockSpec((chunk, 512), lambda b, c: (row(b, c), ZQ // 512)),
                  pl.BlockSpec((chunk, 512), lambda b, c: (row(b, c), ZK // 512)),
                  pl.BlockSpec((chunk, 1024), lambda b, c: (row(b, c), ZV // 1024)),
                  pl.BlockSpec((chunk, 1024), lambda b, c: (row(b, c), ZOG // 1024)),
                  pl.BlockSpec((chunk, LANES), lambda b, c: (row(b, c), ZG // LANES)),
                  pl.BlockSpec((1, LANES, chunk), lambda b, c: (b * nc + c, 0, 0)),
                  pl.BlockSpec((None, 1, D_MODEL), lambda b, c: (layer, 0, 0)),
                  in4, in3, in3],
        out_specs=[pl.BlockSpec((chunk, D_MODEL), lambda b, c: (b * nc + c, 0)), st4, st3, st3],
        out_shape=[jax.ShapeDtypeStruct((rows, D_MODEL), F32),
                   jax.ShapeDtypeStruct((batch, M_HEADS, M_DQK, M_DV), F32),
                   jax.ShapeDtypeStruct((batch, M_HEADS, LANES), F32),
                   jax.ShapeDtypeStruct((batch, M_HEADS, LANES), F32)],
        compiler_params=_cparams(("parallel", "arbitrary")),
        name="mlstm",
    )(za, za, za, za, za, zgt, gain, c0, n0, m0)


def _swa_kernel(q_ref, kc_ref, vc_ref, kp_ref, vp_ref, bp_ref, bc_ref, sink_ref, o_ref, *, first_block_has_no_past):
    q = q_ref[...]
    tq = q.shape[0]
    have_past = pl.program_id(1) > 0
    outs = []
    for kvh in range(S_KV_HEADS):
        sl = slice(kvh * S_HEAD_DIM, (kvh + 1) * S_HEAD_DIM)
        qg = jnp.concatenate(
            [q[:, (kvh * S_GROUP + g) * S_HEAD_DIM:(kvh * S_GROUP + g + 1) * S_HEAD_DIM] for g in range(S_GROUP)],
            axis=0).astype(BF16)
        kp, vp = kp_ref[:, sl].astype(BF16), vp_ref[:, sl].astype(BF16)
        kc, vc = kc_ref[:, sl].astype(BF16), vc_ref[:, sl].astype(BF16)
        sp = _dot_nt(qg, kp) * (S_HEAD_DIM ** -0.5) + bp_ref[kvh]
        sc = _dot_nt(qg, kc) * (S_HEAD_DIM ** -0.5) + bc_ref[kvh]
        if first_block_has_no_past:
            sp = jnp.where(have_past, sp, NEG)
        sk = sink_ref[kvh]
        mx = jnp.maximum(jnp.maximum(jnp.max(sp, axis=-1, keepdims=True), jnp.max(sc, axis=-1, keepdims=True)), sk)
        pp, pc = jnp.exp(sp - mx), jnp.exp(sc - mx)
        den = jnp.sum(pp, axis=-1, keepdims=True) + jnp.sum(pc, axis=-1, keepdims=True) + jnp.exp(sk - mx)
        o = (_dot(pp.astype(BF16), vp) + _dot(pc.astype(BF16), vc)) / den
        outs.extend(o[g * tq:(g + 1) * tq] for g in range(S_GROUP))
    o_ref[...] = jnp.concatenate(outs, axis=1)


def _rel_bucket(dist):
    nn = np.maximum(dist, 0)
    max_exact = REL_BUCKETS // 2
    large = max_exact + (np.log(np.maximum(nn, 1) / max_exact) / np.log(WINDOW / max_exact)
                         * (REL_BUCKETS - max_exact)).astype(np.int32)
    large = np.minimum(large, REL_BUCKETS - 1)
    return np.where(nn < max_exact, nn, large).astype(np.int32)


def _swa_bias(rel_table, tq):
    t = np.arange(tq)[:, None]
    d_prev = t + WINDOW - np.arange(WINDOW)[None, :]
    d_cur = t - np.arange(tq)[None, :]

    def build(dist, ok):
        bias = jnp.transpose(rel_table[_rel_bucket(dist)].astype(F32), (2, 0, 1))
        bias = jnp.where(ok[None], bias, NEG)
        return bias.reshape(S_KV_HEADS, S_GROUP * tq, dist.shape[1])

    return build(d_prev, d_prev <= WINDOW), build(d_cur, d_cur >= 0)


def _sink_cols(w_sink, tq):
    s = w_sink.astype(F32).reshape(DEPTH, S_KV_HEADS, S_GROUP, 1, 1)
    return jnp.broadcast_to(s, (DEPTH, S_KV_HEADS, S_GROUP, tq, 1)).reshape(DEPTH, S_KV_HEADS, S_GROUP * tq, 1)


def _swa_call(kern, grid, in_specs, rows, operands, tq, layer, name):
    bias_specs = [pl.BlockSpec((S_KV_HEADS, S_GROUP * tq, WINDOW), lambda b, n: (0, 0, 0)),
                  pl.BlockSpec((S_KV_HEADS, S_GROUP * tq, tq), lambda b, n: (0, 0, 0)),
                  pl.BlockSpec((None, S_KV_HEADS, S_GROUP * tq, 1), lambda b, n: (layer, 0, 0, 0))]
    return pl.pallas_call(
        kern,
        grid=grid,
        in_specs=in_specs + bias_specs,
        out_specs=pl.BlockSpec((tq, D_MODEL), lambda b, n: (b * grid[1] + n, 0)),
        out_shape=jax.ShapeDtypeStruct((rows, D_MODEL), F32),
        compiler_params=_cparams(("parallel", "arbitrary")),
        name=name,
    )(*operands)


def _swa_prompt(zb, bp, bc, sk, layer, *, batch, seq):
    nb = seq // WINDOW
    row = lambda b, n: b * nb + n
    prev = lambda b, n: jnp.maximum(b * nb + n - 1, 0)
    in_specs = [pl.BlockSpec((WINDOW, 1024), lambda b, n: (row(b, n), ZQS // 1024)),
                pl.BlockSpec((WINDOW, 256), lambda b, n: (row(b, n), ZKS // 256)),
                pl.BlockSpec((WINDOW, 256), lambda b, n: (row(b, n), ZVS // 256)),
                pl.BlockSpec((WINDOW, 256), lambda b, n: (prev(b, n), ZKS // 256)),
                pl.BlockSpec((WINDOW, 256), lambda b, n: (prev(b, n), ZVS // 256))]
    kern = functools.partial(_swa_kernel, first_block_has_no_past=True)
    return _swa_call(kern, (batch, nb), in_specs, batch * seq, (zb, zb, zb, zb, zb, bp, bc, sk), WINDOW, layer,
                     "swa_prompt")


def _swa_sample(zb, buf_k, buf_v, bp, bc, sk, layer, *, row0, batch):
    r0 = row0 // T_PAD
    in_specs = [pl.BlockSpec((T_PAD, 1024), lambda b, n: (r0 + b, ZQS // 1024)),
                pl.BlockSpec((T_PAD, 256), lambda b, n: (r0 + b, ZKS // 256)),
                pl.BlockSpec((T_PAD, 256), lambda b, n: (r0 + b, ZVS // 256)),
                pl.BlockSpec((None, None, WINDOW, 256), lambda b, n: (layer, b, 0, 0)),
                pl.BlockSpec((None, None, WINDOW, 256), lambda b, n: (layer, b, 0, 0))]
    kern = functools.partial(_swa_kernel, first_block_has_no_past=False)
    return _swa_call(kern, (batch, 1), in_specs, batch * T_PAD, (zb, zb, zb, buf_k, buf_v, bp, bc, sk), T_PAD, layer,
                     "swa_sample")


def _merge_kernel(x_ref, yap_ref, yas_ref, ybp_ref, ybs_ref, ga_ref, gb_ref, wa_ref, wb_ref, wo_ref, lng_ref, lnb_ref,
                  wr_ref, br_ref, x1_ref, route_ref, wbf, *, prompt_tiles):
    i = pl.program_id(0)

    @pl.when(i == 0)
    def _():
        for j, w_ref in enumerate((wa_ref, wb_ref, wo_ref)):
            _cast_into(wbf.at[j], w_ref)

    is_prompt = i < prompt_tiles
    ya = jnp.where(is_prompt, yap_ref[...], yas_ref[...])
    yb = jnp.where(is_prompt, ybp_ref[...], ybs_ref[...])
    a = _dot(ya.astype(BF16), wbf[0])
    b = _dot(yb.astype(BF16), wbf[1])
    u = jax.nn.sigmoid(ga_ref[...]) * a + jax.nn.sigmoid(gb_ref[...]) * b
    mix = _dot(u.astype(BF16), wbf[2])
    x1 = _layer_norm(DN_ALPHA * x_ref[...] + mix, lng_ref[0:1, :], lnb_ref[0:1, :])
    x1_ref[...] = x1

    logits = jnp.dot(x1, wr_ref[...], precision=HIGHEST, preferred_element_type=F32) + br_ref[...]
    lane = lax.broadcasted_iota(I32, logits.shape, 1)
    lanef = lane.astype(F32)
    gmask = lane < N_GROUPS
    gl = jnp.where(gmask, logits, NEG)
    gmax = jnp.max(gl, axis=-1, keepdims=True)
    gidx = jnp.min(jnp.where(gmask & (gl == gmax), lanef, 999.0), axis=-1, keepdims=True)
    g_w = 1.0 / jnp.sum(jnp.where(gmask, jnp.exp(gl - gmax), 0.0), axis=-1, keepdims=True)
    lo = N_GROUPS + EXPERTS_PER_GROUP * gidx
    emask = (lanef >= lo) & (lanef < lo + EXPERTS_PER_GROUP)
    el = jnp.where(emask, logits, NEG)
    emax = jnp.max(el, axis=-1, keepdims=True)
    ep = jnp.where(emask, jnp.exp(el - emax), 0.0)
    prob = ep / jnp.sum(ep, axis=-1, keepdims=True)
    prob = jnp.where(emask, prob, -1.0)
    p1 = jnp.max(prob, axis=-1, keepdims=True)
    i1 = jnp.min(jnp.where(prob == p1, lanef, 999.0), axis=-1, keepdims=True)
    prob2 = jnp.where(lanef == i1, -1.0, prob)
    p2 = jnp.max(prob2, axis=-1, keepdims=True)
    i2 = jnp.min(jnp.where(prob2 == p2, lanef, 999.0), axis=-1, keepdims=True)
    tot = p1 + p2
    route_ref[...] = jnp.where(lane == 0, i1 - N_GROUPS,
                               jnp.where(lane == 1, i2 - N_GROUPS,
                                         jnp.where(lane == 2, g_w * p1 / tot,
                                                   jnp.where(lane == 3, g_w * p2 / tot, 0.0))))


def _merge(x, ya_p, ya_s, yb_p, yb_s, zb, w_a, w_b, w_out, ln_g, ln_b, w_r, b_r, layer):
    m = x.shape[0]
    pt = ya_p.shape[0] // TM
    rowblk = lambda w, j: pl.BlockSpec((TM, w), lambda i: (i, j))
    pblk = pl.BlockSpec((TM, D_MODEL), lambda i: (jnp.minimum(i, pt - 1), 0))
    sblk = pl.BlockSpec((TM, D_MODEL), lambda i: (jnp.maximum(i - pt, 0), 0))
    return pl.pallas_call(
        functools.partial(_merge_kernel, prompt_tiles=pt),
        grid=(m // TM,),
        in_specs=[rowblk(D_MODEL, 0), pblk, sblk, pblk, sblk,
                  rowblk(D_MODEL, ZGA // D_MODEL), rowblk(D_MODEL, ZGB // D_MODEL),
                  _layer_spec((D_MODEL, D_MODEL), layer), _layer_spec((D_MODEL, D_MODEL), layer),
                  _layer_spec((D_MODEL, D_MODEL), layer),
                  _layer_spec((3, D_MODEL), layer), _layer_spec((3, D_MODEL), layer),
                  _layer_spec((D_MODEL, LANES), layer), _layer_spec((1, LANES), layer)],
        out_specs=[rowblk(D_MODEL, 0), rowblk(LANES, 0)],
        out_shape=[jax.ShapeDtypeStruct((m, D_MODEL), F32), jax.ShapeDtypeStruct((m, LANES), F32)],
        scratch_shapes=[pltpu.VMEM((3, D_MODEL, D_MODEL), BF16)],
        compiler_params=_cparams(("arbitrary",)),
        name="merge",
    )(x, ya_p, ya_s, yb_p, yb_s, zb, zb, w_a, w_b, w_out, ln_g, ln_b, w_r, b_r)


def _route_tables(eid, valid, nblk):
    m = eid.shape[0]
    a = 2 * m
    flat = eid.reshape(a)
    ok = jnp.repeat(valid, 2)
    onehot = ((flat[:, None] == jnp.arange(N_EXPERTS, dtype=I32)[None, :]) & ok[:, None]).astype(I32)
    csum = jnp.cumsum(onehot, axis=0)
    counts = csum[-1]
    rank = jnp.sum(onehot * csum, axis=1) - 1
    padded = (counts + BLK - 1) // BLK * BLK
    pad_end = jnp.cumsum(padded)
    pad_start = pad_end - padded
    pos = jnp.sum(onehot * pad_start[None, :], axis=1) + rank
    pos = jnp.where(ok, pos, nblk * BLK + jnp.arange(a, dtype=I32) % (2 * TM))
    nused = pad_end[-1] // BLK
    unused = nused + jnp.arange(N_EXPERTS, dtype=I32)
    zstart = jnp.concatenate([jnp.where(padded > 0, pad_end - BLK, -1),
                              jnp.where(unused < nblk, unused * BLK, -1)]).astype(I32)
    bstart = jnp.minimum(jnp.arange(nblk + 2, dtype=I32), nused - 1) * BLK
    blk_e = jnp.minimum(jnp.sum((pad_end[None, :] <= bstart[:, None]).astype(I32), axis=1), N_EXPERTS - 1)
    return pos.astype(I32), zstart, blk_e.astype(I32), nused.astype(I32).reshape(1)


def _dispatch_kernel(pos_ref, zstart_ref, x_ref, xs_hbm, zbuf, sem, *, nblk):
    i = pl.program_id(0)
    tm = x_ref.shape[0]
    tail = [nblk * BLK + j * BLK for j in range(2 * tm // BLK)]

    @pl.when(i == 0)
    def _():
        zbuf[...] = jnp.zeros_like(zbuf)

        def clear(start):
            return pltpu.make_async_copy(zbuf, xs_hbm.at[pl.ds(start, BLK)], sem.at[1])

        for phase in ("start", "wait"):
            for e in range(zstart_ref.shape[0]):
                @pl.when(zstart_ref[e] >= 0)
                def _():
                    getattr(clear(pl.multiple_of(zstart_ref[e], BLK)), phase)()
            for start in tail:
                getattr(clear(start), phase)()

    base = 2 * i * tm
    for r in range(tm):
        for k in range(2):
            pltpu.make_async_copy(x_ref.at[pl.ds(r, 1)], xs_hbm.at[pl.ds(pos_ref[base + 2 * r + k], 1)],
                                  sem.at[0]).start()
    for k in range(2):
        pltpu.make_async_copy(x_ref, xs_hbm.at[pl.ds(0, tm)], sem.at[0]).wait()


def _dispatch(x1, pos, zstart, nblk):
    m = x1.shape[0]
    grid_spec = pltpu.PrefetchScalarGridSpec(
        num_scalar_prefetch=2,
        grid=(m // TM,),
        in_specs=[pl.BlockSpec((TM, D_MODEL), lambda i, p, z: (i, 0))],
        out_specs=pl.BlockSpec(memory_space=pl.ANY),
        scratch_shapes=[pltpu.VMEM((BLK, D_MODEL), F32), pltpu.SemaphoreType.DMA((2,))],
    )
    return pl.pallas_call(
        functools.partial(_dispatch_kernel, nblk=nblk),
        grid_spec=grid_spec,
        out_shape=jax.ShapeDtypeStruct((nblk * BLK + 2 * TM, D_MODEL), F32),
        compiler_params=_cparams(("arbitrary",)),
        name="dispatch",
    )(pos, zstart, x1)


def _experts_kernel(blk_e_ref, nused_ref, xs_ref, wg_ref, wu_ref, wd_ref, ys_ref, wgb, wub, wdb):
    b = pl.program_id(0)
    used = b < nused_ref[0]
    new_expert = (b == 0) | (blk_e_ref[b] != blk_e_ref[jnp.maximum(b - 1, 0)])

    @pl.when(used & new_expert)
    def _():
        _cast_into(wgb, wg_ref)
        _cast_into(wub, wu_ref)
        _cast_into(wdb, wd_ref)

    @pl.when(used)
    def _():
        xb = xs_ref[...].astype(BF16)
        g = _dot(xb, wgb[...])
        u = _dot(xb, wub[...])
        ys_ref[...] = _dot((g * jax.nn.sigmoid(g) * u).astype(BF16), wdb[...])

    @pl.when(jnp.logical_not(used))
    def _():
        ys_ref[...] = jnp.zeros_like(ys_ref)


def _experts(xs, blk_e, nused, w_eg, w_eu, w_ed, layer):
    nb = xs.shape[0] // BLK
    wspec = lambda s: pl.BlockSpec((None, None) + s, lambda b, be, nu: (layer, be[b], 0, 0))
    grid_spec = pltpu.PrefetchScalarGridSpec(
        num_scalar_prefetch=2,
        grid=(nb,),
        in_specs=[pl.BlockSpec((BLK, D_MODEL), lambda b, be, nu: (jnp.minimum(b, nu[0] - 1), 0)),
                  wspec((D_MODEL, D_EXPERT)), wspec((D_MODEL, D_EXPERT)), wspec((D_EXPERT, D_MODEL))],
        out_specs=pl.BlockSpec((BLK, D_MODEL), lambda b, be, nu: (b, 0)),
        scratch_shapes=[pltpu.VMEM((D_MODEL, D_EXPERT), BF16), pltpu.VMEM((D_MODEL, D_EXPERT), BF16),
                        pltpu.VMEM((D_EXPERT, D_MODEL), BF16)],
    )
    return pl.pallas_call(
        _experts_kernel,
        grid_spec=grid_spec,
        out_shape=jax.ShapeDtypeStruct(xs.shape, F32),
        compiler_params=_cparams(("arbitrary",)),
        name="experts",
    )(blk_e, nused, xs, w_eg, w_eu, w_ed)


def _post_kernel(pos_ref, x1_ref, route_ref, pp_ref, ps_ref, wpg_ref, wpp_ref, lng_ref, lnb_ref, ys_hbm, o_ref,
                 ybuf, wpg_b, wpp_b, sem, *, prompt_tiles):
    i = pl.program_id(0)
    tm = x1_ref.shape[0]

    def gather(tile, slot):
        base = 2 * tile * tm
        for r in range(tm):
            for k in range(2):
                pltpu.make_async_copy(ys_hbm.at[pl.ds(pos_ref[base + 2 * r + k], 1)],
                                      ybuf.at[slot, k, pl.ds(r, 1)], sem.at[slot]).start()

    @pl.when(i == 0)
    def _():
        _cast_into(wpg_b, wpg_ref)
        _cast_into(wpp_b, wpp_ref)
        gather(0, 0)

    @pl.when(i + 1 < pl.num_programs(0))
    def _():
        gather(i + 1, (i + 1) % 2)

    slot = i % 2
    for k in range(2):
        pltpu.make_async_copy(ys_hbm.at[pl.ds(0, tm)], ybuf.at[slot, k], sem.at[slot]).wait()

    route = route_ref[...]
    moe = route[:, 2:3] * ybuf[slot, 0] + route[:, 3:4] * ybuf[slot, 1]
    x2 = _layer_norm(DN_ALPHA * x1_ref[...] + moe, lng_ref[1:2, :], lnb_ref[1:2, :])
    p = jnp.where(i < prompt_tiles, pp_ref[...], ps_ref[...])
    ple = jax.nn.sigmoid(_dot(x2.astype(BF16), wpg_b[...])) * _dot(p.astype(BF16), wpp_b[...])
    o_ref[...] = _layer_norm(DN_ALPHA * x2 + ple, lng_ref[2:3, :], lnb_ref[2:3, :])


def _post(x1, ys, pos, route, p_prompt, p_sample, w_pg, w_pp, ln_g, ln_b, layer):
    m = x1.shape[0]
    pt = p_prompt.shape[1] // TM
    rowblk = lambda w: pl.BlockSpec((TM, w), lambda i, p: (i, 0))
    lspec = lambda s: pl.BlockSpec((None,) + s, lambda i, p: (layer,) + (0,) * len(s), pipeline_mode=pl.Buffered(1))
    grid_spec = pltpu.PrefetchScalarGridSpec(
        num_scalar_prefetch=1,
        grid=(m // TM,),
        in_specs=[rowblk(D_MODEL), rowblk(LANES),
                  pl.BlockSpec((None, TM, D_PLE), lambda i, p: (layer, jnp.minimum(i, pt - 1), 0)),
                  pl.BlockSpec((None, TM, D_PLE), lambda i, p: (layer, jnp.maximum(i - pt, 0), 0)),
                  lspec((D_MODEL, D_MODEL)), lspec((D_PLE, D_MODEL)), lspec((3, D_MODEL)), lspec((3, D_MODEL)),
                  pl.BlockSpec(memory_space=pl.ANY)],
        out_specs=rowblk(D_MODEL),
        scratch_shapes=[pltpu.VMEM((2, 2, TM, D_MODEL), F32), pltpu.VMEM((D_MODEL, D_MODEL), BF16),
                        pltpu.VMEM((D_PLE, D_MODEL), BF16), pltpu.SemaphoreType.DMA((2,))],
    )
    return pl.pallas_call(
        functools.partial(_post_kernel, prompt_tiles=pt),
        grid_spec=grid_spec,
        out_shape=jax.ShapeDtypeStruct((m, D_MODEL), F32),
        compiler_params=_cparams(("arbitrary",)),
        name="post",
    )(pos, x1, route, p_prompt, p_sample, w_pg, w_pp, ln_g, ln_b, ys)


def kernel(x_prompt, x_sample, state_mlstm_C, state_mlstm_n, state_mlstm_m, state_swa_k, state_swa_v, p_prompt, p_sample, w_in, b_in, mh_gain, w_a, w_b, w_out, rel_table, w_sink, ln_g, ln_b, w_rg, b_rg, w_re, b_re, w_eg, w_eu, w_ed, w_pg, w_pp):
    bp_, seq = x_prompt.shape[:2]
    bs_, tdec = x_sample.shape[:2]
    mp = bp_ * seq
    msp = bs_ * T_PAD
    mall = mp + msp
    nblk = 2 * (mp + bs_ * tdec) // BLK + N_EXPERTS

    pad_t = lambda t: jnp.pad(t, [(0, 0)] * (t.ndim - 2) + [(0, T_PAD - tdec), (0, 0)])
    x = jnp.concatenate([x_prompt.reshape(mp, D_MODEL), pad_t(x_sample).reshape(msp, D_MODEL)], axis=0)
    valid = jnp.concatenate([jnp.ones((mp,), bool), jnp.tile(jnp.arange(T_PAD) < tdec, bs_)])
    p_prompt = p_prompt.reshape(DEPTH, mp, D_PLE)
    p_sample = pad_t(p_sample).reshape(DEPTH, msp, D_PLE)

    b_in3 = b_in.reshape(DEPTH, 1, -1)
    w_qkv2, b_qkv2 = w_in[:, :, IN_QKV2[0]:IN_QKV2[1]], b_in3[:, :, IN_QKV2[0]:IN_QKV2[1]]
    w_gates, b_gates = w_in[:, :, IN_GATES[0]:IN_GATES[1]], b_in3[:, :, IN_GATES[0]:IN_GATES[1]]
    n_r = LANES - N_GROUPS - N_EXPERTS
    w_r = jnp.concatenate([w_rg, w_re, jnp.zeros((DEPTH, D_MODEL, n_r), F32)], axis=-1)
    b_r = jnp.concatenate([b_rg, b_re, jnp.zeros((DEPTH, n_r), F32)], axis=-1).reshape(DEPTH, 1, LANES)
    gain = mh_gain.reshape(DEPTH, 1, D_MODEL)

    bias_p = _swa_bias(rel_table, WINDOW)
    bias_s = _swa_bias(rel_table, T_PAD)
    sink_p = _sink_cols(w_sink, WINDOW)
    sink_s = _sink_cols(w_sink, T_PAD)
    buf_k = state_swa_k.reshape(DEPTH, bs_, WINDOW, S_KV_HEADS * S_HEAD_DIM)
    buf_v = state_swa_v.reshape(DEPTH, bs_, WINDOW, S_KV_HEADS * S_HEAD_DIM)
    m0_s = jnp.broadcast_to(state_mlstm_m[..., None], (DEPTH, bs_, M_HEADS, LANES))
    zc = jnp.zeros((1, bp_, M_HEADS, M_DQK, M_DV), F32)
    zn = jnp.zeros((1, bp_, M_HEADS, LANES), F32)

    st_p, st_s = [], []
    for i in range(DEPTH):
        za = _inproj(x, [(w_in, IN_A)], [(b_in3, IN_A)], i, "inproj_a")
        zb = _inproj(x, [(w_gates, 2048), (w_qkv2, 1536)], [(b_gates, 2048), (b_qkv2, 1536)], i, "inproj_b")

        ya_p, c_p, n_p, m_p = _mlstm(za, gain, i, zc, zn, zn, 0, row0=0, batch=bp_, t_pad=seq, t_valid=seq, chunk=128)
        ya_s, c_s, n_s, m_s = _mlstm(za, gain, i, state_mlstm_C, state_mlstm_n, m0_s, i,
                                     row0=mp, batch=bs_, t_pad=T_PAD, t_valid=tdec, chunk=T_PAD)
        yb_p = _swa_prompt(zb, bias_p[0], bias_p[1], sink_p, i, batch=bp_, seq=seq)
        yb_s = _swa_sample(zb, buf_k, buf_v, bias_s[0], bias_s[1], sink_s, i, row0=mp, batch=bs_)

        x1, route = _merge(x, ya_p, ya_s, yb_p, yb_s, zb, w_a, w_b, w_out, ln_g, ln_b, w_r, b_r, i)
        pos, zstart, blk_e, nused = _route_tables(route[:, :2].astype(I32), valid, nblk)
        xs = _dispatch(x1, pos, zstart, nblk)
        ys = _experts(xs, blk_e, nused, w_eg, w_eu, w_ed, i)
        x = _post(x1, ys, pos, route, p_prompt, p_sample, w_pg, w_pp, ln_g, ln_b, i)

        kv_p = zb[:mp].reshape(bp_, seq, IN_B)[:, seq - WINDOW:, ZKS:]
        kv_s = zb[mp:].reshape(bs_, T_PAD, IN_B)[:, :tdec, ZKS:]
        heads = lambda t: t.reshape(t.shape[0], WINDOW, S_KV_HEADS, S_HEAD_DIM)
        st_p.append((c_p, n_p, m_p[:, :, 0], heads(kv_p[..., :256]), heads(kv_p[..., 256:])))
        st_s.append((c_s, n_s, m_s[:, :, 0],
                     heads(jnp.concatenate([buf_k[i][:, tdec:], kv_s[..., :256]], axis=1)),
                     heads(jnp.concatenate([buf_v[i][:, tdec:], kv_s[..., 256:]], axis=1))))

    stk = lambda lst, j: jnp.stack([s[j] for s in lst], axis=0)
    y_p = x[:mp].reshape(bp_, seq, D_MODEL)
    y_s = x[mp:].reshape(bs_, T_PAD, D_MODEL)[:, :tdec]
    return (y_p, y_s,
            stk(st_p, 0), stk(st_p, 1), stk(st_p, 2), stk(st_p, 3), stk(st_p, 4),
            stk(st_s, 0), stk(st_s, 1), stk(st_s, 2), stk(st_s, 3), stk(st_s, 4))
```

```python
import functools

import numpy as np
import jax
import jax.numpy as jnp
from jax import lax
from jax.experimental import pallas as pl
from jax.experimental.pallas import tpu as pltpu

F32 = jnp.float32
BF16 = jnp.bfloat16
I32 = jnp.int32
HIGHEST = lax.Precision.HIGHEST

D_MODEL = 1024
DEPTH = 4
D_PLE = 256
M_HEADS = 4
M_DQK = 128
M_DV = 256
S_HEADS = 16
S_KV_HEADS = 4
S_HEAD_DIM = 64
S_GROUP = 4
WINDOW = 128
REL_BUCKETS = 32
N_GROUPS = 4
EXPERTS_PER_GROUP = 8
N_EXPERTS = 32
D_EXPERT = 512
DN_ALPHA = (2 * DEPTH) ** 0.25
LN_EPS = 1e-5
NEG = -1e30

IN_A = 3200
IN_QKV2 = (3080, 4616)
IN_GATES = (4616, 6664)
ZQ, ZK, ZV, ZOG, ZG = 0, 512, 1024, 2048, 3072
IN_B = 3584
ZGA, ZGB, ZQS, ZKS, ZVS = 0, 1024, 2048, 3072, 3328

LANES = 128
T_PAD = 8
TM = 256
BLK = 256
VMEM_LIMIT = 56 * 1024 * 1024


def _cparams(sem):
    return pltpu.CompilerParams(dimension_semantics=sem, vmem_limit_bytes=VMEM_LIMIT)


def _const_spec(shape):
    nd = len(shape)
    return pl.BlockSpec(shape, lambda *_: (0,) * nd, pipeline_mode=pl.Buffered(1))


def _layer_spec(shape, layer):
    nd = len(shape)
    return pl.BlockSpec((None,) + shape, lambda *_: (layer,) + (0,) * nd, pipeline_mode=pl.Buffered(1))


def _layer_norm(y, g, b):
    mu = jnp.mean(y, axis=-1, keepdims=True)
    d = y - mu
    var = jnp.mean(d * d, axis=-1, keepdims=True)
    return d * lax.rsqrt(var + LN_EPS) * g + b


def _dot(a, b):
    return jnp.dot(a, b, preferred_element_type=F32)


def _dot_nt(a, b):
    return lax.dot_general(a, b, (((1,), (1,)), ((), ())), preferred_element_type=F32)


def _dot_tn(a, b):
    return lax.dot_general(a, b, (((0,), (0,)), ((), ())), preferred_element_type=F32)


def _cast_into(dst_ref, src_ref, col0=0, step=512):
    n = src_ref.shape[-1]
    for j in range(0, n, step):
        w = min(step, n - j)
        dst_ref[:, col0 + j:col0 + j + w] = src_ref[:, j:j + w].astype(BF16)


def _inproj_kernel(*refs, widths):
    ng = len(widths)
    x_ref, w_refs, b_refs, z_ref, wbf = refs[0], refs[1:1 + ng], refs[1 + ng:1 + 2 * ng], refs[1 + 2 * ng], refs[2 + 2 * ng]

    @pl.when(pl.program_id(0) == 0)
    def _():
        c = 0
        for w_ref, n in zip(w_refs, widths):
            _cast_into(wbf, w_ref, c)
            c += n

    xb = x_ref[...].astype(BF16)
    c = 0
    for b_ref, n in zip(b_refs, widths):
        for j in range(0, n, 512):
            w = min(512, n - j)
            z_ref[:, c + j:c + j + w] = _dot(xb, wbf[:, c + j:c + j + w]) + b_ref[:, j:j + w]
        c += n


def _inproj(x, ws, bs, layer, name):
    m = x.shape[0]
    widths = tuple(w for _, w in ws)
    n_out = sum(widths)
    return pl.pallas_call(
        functools.partial(_inproj_kernel, widths=widths),
        grid=(m // TM,),
        in_specs=([pl.BlockSpec((TM, D_MODEL), lambda i: (i, 0))]
                  + [_layer_spec((D_MODEL, n), layer) for _, n in ws]
                  + [_layer_spec((1, n), layer) for _, n in bs]),
        out_specs=pl.BlockSpec((TM, n_out), lambda i: (i, 0)),
        out_shape=jax.ShapeDtypeStruct((m, n_out), F32),
        scratch_shapes=[pltpu.VMEM((D_MODEL, n_out), BF16)],
        compiler_params=_cparams(("arbitrary",)),
        name=name,
    )(x, *[w for w, _ in ws], *[b for b, _ in bs])


def _log_sigmoid(x):
    return jnp.minimum(x, 0.0) - jnp.log(1.0 + jnp.exp(-jnp.abs(x)))


def _mlstm_kernel(zq, zk, zv, zog, zg, zgt, gain, c0, n0, m0, ya, co, no, mo, *, chunk, t_valid):
    L = chunk

    @pl.when(pl.program_id(1) == 0)
    def _():
        co[...] = c0[...]
        no[...] = n0[...]
        mo[...] = m0[...]

    g_c = zg[...]
    g_r = zgt[0]
    ig_c, lf_c = g_c, _log_sigmoid(g_c)
    ig_r, lf_r = g_r, _log_sigmoid(g_r)
    if t_valid < L:
        rv = lax.broadcasted_iota(I32, (L, 1), 0) < t_valid
        cv = lax.broadcasted_iota(I32, (1, L), 1) < t_valid
        ig_c, lf_c = jnp.where(rv, ig_c, NEG), jnp.where(rv, lf_c, 0.0)
        ig_r, lf_r = jnp.where(cv, ig_r, NEG), jnp.where(cv, lf_r, 0.0)
    ri = lax.broadcasted_iota(I32, (L, L), 0)
    ci = lax.broadcasted_iota(I32, (L, L), 1)
    causal = ri >= ci
    b_c = jnp.dot(causal.astype(F32), lf_c, precision=HIGHEST, preferred_element_type=F32)
    b_r = jnp.dot(lf_r, (ri <= ci).astype(F32), precision=HIGHEST, preferred_element_type=F32)

    for h in range(M_HEADS):
        icol, irow = ig_c[:, h:h + 1], ig_r[h:h + 1, :]
        bcol, brow = b_c[:, 4 + h:5 + h], b_r[4 + h:5 + h, :]
        m = mo[0, h:h + 1, 0:1]
        C = co[0, h]
        n = no[0, h:h + 1, :]
        q = zq[:, h * M_DQK:(h + 1) * M_DQK] * (M_DQK ** -0.5)
        k = zk[:, h * M_DQK:(h + 1) * M_DQK]
        v = zv[:, h * M_DV:(h + 1) * M_DV]
        qb, kb, vb = q.astype(BF16), k.astype(BF16), v.astype(BF16)

        dmat = jnp.where(causal, bcol - brow + irow, NEG)
        inter = bcol + m
        mhat = jnp.maximum(inter, jnp.max(dmat, axis=-1, keepdims=True))
        w_intra = jnp.exp(dmat - mhat)
        w_inter = jnp.exp(inter - mhat)
        s = _dot_nt(qb, kb) * w_intra
        num = _dot(s.astype(BF16), vb) + w_inter * _dot(qb, C.astype(BF16))
        den = jnp.sum(s, axis=-1, keepdims=True) + w_inter * jnp.sum(q * n, axis=-1, keepdims=True)
        hh = num / jnp.maximum(jnp.abs(den), jnp.exp(-mhat))

        b_end = bcol[L - 1:L, :]
        gcol = icol + b_end - bcol
        m_new = jnp.maximum(b_end + m, jnp.max(gcol, axis=0, keepdims=True))
        decay = jnp.exp(b_end + m - m_new)
        kw = k * jnp.exp(gcol - m_new)
        co[0, h] = decay * C + _dot_tn(kw.astype(BF16), vb)
        no[0, h:h + 1, :] = decay * n + jnp.sum(kw, axis=0, keepdims=True)
        mo[0, h:h + 1, :] = jnp.broadcast_to(m_new, (1, LANES))

        mu = jnp.mean(hh, axis=-1, keepdims=True)
        d = hh - mu
        var = jnp.mean(d * d, axis=-1, keepdims=True)
        hn = d * lax.rsqrt(var + LN_EPS) * gain[:, h * M_DV:(h + 1) * M_DV]
        ya[:, h * M_DV:(h + 1) * M_DV] = hn * jax.nn.sigmoid(zog[:, h * M_DV:(h + 1) * M_DV])


def _mlstm(za, gain, layer, c0, n0, m0, st_layer, *, row0, batch, t_pad, t_valid, chunk):
    nc = t_pad // chunk
    rows = batch * t_pad
    zgt = lax.slice(za, (row0, ZG), (row0 + rows, ZG + LANES)).reshape(batch * nc, chunk, LANES).transpose(0, 2, 1)
    r0 = row0 // chunk
    row = lambda b, c: r0 + b * nc + c
    kern = functools.partial(_mlstm_kernel, chunk=chunk, t_valid=min(t_valid, chunk))
    st4 = pl.BlockSpec((1, M_HEADS, M_DQK, M_DV), lambda b, c: (b, 0, 0, 0))
    st3 = pl.BlockSpec((1, M_HEADS, LANES), lambda b, c: (b, 0, 0))
    in4 = pl.BlockSpec((None, 1, M_HEADS, M_DQK, M_DV), lambda b, c: (st_layer, b, 0, 0, 0))
    in3 = pl.BlockSpec((None, 1, M_HEADS, LANES), lambda b, c: (st_layer, b, 0, 0))
    return pl.pallas_call(
        kern,
        grid=(batch, nc),
        in_specs=[pl.BlockSpec((chunk, 512), lambda b, c: (row(b, c), ZQ // 512)),
                  pl.BlockSpec((chunk, 512), lambda b, c: (row(b, c), ZK // 512)),
                  pl.BlockSpec((chunk, 1024), lambda b, c: (row(b, c), ZV // 1024)),
                  pl.BlockSpec((chunk, 1024), lambda b, c: (row(b, c), ZOG // 1024)),
                  pl.BlockSpec((chunk, LANES), lambda b, c: (row(b, c), ZG // LANES)),
                  pl.BlockSpec((1, LANES, chunk), lambda b, c: (b * nc + c, 0, 0)),
                  pl.BlockSpec((None, 1, D_MODEL), lambda b, c: (layer, 0, 0)),
                  in4, in3, in3],
        out_specs=[pl.BlockSpec((chunk, D_MODEL), lambda b, c: (b * nc + c, 0)), st4, st3, st3],
        out_shape=[jax.ShapeDtypeStruct((rows, D_MODEL), F32),
                   jax.ShapeDtypeStruct((batch, M_HEADS, M_DQK, M_DV), F32),
                   jax.ShapeDtypeStruct((batch, M_HEADS, LANES), F32),
                   jax.ShapeDtypeStruct((batch, M_HEADS, LANES), F32)],
        compiler_params=_cparams(("parallel", "arbitrary")),
        name="mlstm",
    )(za, za, za, za, za, zgt, gain, c0, n0, m0)


def _swa_kernel(q_ref, kc_ref, vc_ref, kp_ref, vp_ref, bp_ref, bc_ref, sink_ref, o_ref, *, first_block_has_no_past):
    q = q_ref[...]
    tq = q.shape[0]
    have_past = pl.program_id(1) > 0
    outs = []
    for kvh in range(S_KV_HEADS):
        sl = slice(kvh * S_HEAD_DIM, (kvh + 1) * S_HEAD_DIM)
        qg = jnp.concatenate(
            [q[:, (kvh * S_GROUP + g) * S_HEAD_DIM:(kvh * S_GROUP + g + 1) * S_HEAD_DIM] for g in range(S_GROUP)],
            axis=0).astype(BF16)
        kp, vp = kp_ref[:, sl].astype(BF16), vp_ref[:, sl].astype(BF16)
        kc, vc = kc_ref[:, sl].astype(BF16), vc_ref[:, sl].astype(BF16)
        sp = _dot_nt(qg, kp) * (S_HEAD_DIM ** -0.5) + bp_ref[kvh]
        sc = _dot_nt(qg, kc) * (S_HEAD_DIM ** -0.5) + bc_ref[kvh]
        if first_block_has_no_past:
            sp = jnp.where(have_past, sp, NEG)
        sk = sink_ref[kvh]
        if sp.shape == sc.shape:
            mx = jnp.maximum(jnp.max(jnp.maximum(sp, sc), axis=-1, keepdims=True), sk)
            pp, pc = jnp.exp(sp - mx), jnp.exp(sc - mx)
            den = jnp.sum(pp + pc, axis=-1, keepdims=True) + jnp.exp(sk - mx)
        else:
            mx = jnp.maximum(jnp.maximum(jnp.max(sp, axis=-1, keepdims=True), jnp.max(sc, axis=-1, keepdims=True)), sk)
            pp, pc = jnp.exp(sp - mx), jnp.exp(sc - mx)
            den = jnp.sum(pp, axis=-1, keepdims=True) + jnp.sum(pc, axis=-1, keepdims=True) + jnp.exp(sk - mx)
        o = (_dot(pp.astype(BF16), vp) + _dot(pc.astype(BF16), vc)) / den
        outs.extend(o[g * tq:(g + 1) * tq] for g in range(S_GROUP))
    o_ref[...] = jnp.concatenate(outs, axis=1)


def _rel_bucket(dist):
    nn = np.maximum(dist, 0)
    max_exact = REL_BUCKETS // 2
    large = max_exact + (np.log(np.maximum(nn, 1) / max_exact) / np.log(WINDOW / max_exact)
                         * (REL_BUCKETS - max_exact)).astype(np.int32)
    large = np.minimum(large, REL_BUCKETS - 1)
    return np.where(nn < max_exact, nn, large).astype(np.int32)


def _swa_bias(rel_table, tq):
    t = np.arange(tq)[:, None]
    d_prev = t + WINDOW - np.arange(WINDOW)[None, :]
    d_cur = t - np.arange(tq)[None, :]

    def build(dist, ok):
        onehot = np.eye(REL_BUCKETS, dtype=np.float32)[_rel_bucket(dist).reshape(-1)]
        bias = jnp.dot(onehot, rel_table.astype(F32), precision=HIGHEST).reshape(dist.shape + (S_HEADS,))
        bias = jnp.transpose(bias, (2, 0, 1))
        bias = jnp.where(ok[None], bias, NEG)
        return bias.reshape(S_KV_HEADS, S_GROUP * tq, dist.shape[1])

    return build(d_prev, d_prev <= WINDOW), build(d_cur, d_cur >= 0)


def _sink_cols(w_sink, tq):
    s = w_sink.astype(F32).reshape(DEPTH, S_KV_HEADS, S_GROUP, 1, 1)
    return jnp.broadcast_to(s, (DEPTH, S_KV_HEADS, S_GROUP, tq, 1)).reshape(DEPTH, S_KV_HEADS, S_GROUP * tq, 1)


def _swa_call(kern, grid, in_specs, rows, operands, tq, layer, name):
    bias_specs = [pl.BlockSpec((S_KV_HEADS, S_GROUP * tq, WINDOW), lambda b, n: (0, 0, 0)),
                  pl.BlockSpec((S_KV_HEADS, S_GROUP * tq, tq), lambda b, n: (0, 0, 0)),
                  pl.BlockSpec((None, S_KV_HEADS, S_GROUP * tq, 1), lambda b, n: (layer, 0, 0, 0))]
    return pl.pallas_call(
        kern,
        grid=grid,
        in_specs=in_specs + bias_specs,
        out_specs=pl.BlockSpec((tq, D_MODEL), lambda b, n: (b * grid[1] + n, 0)),
        out_shape=jax.ShapeDtypeStruct((rows, D_MODEL), F32),
        compiler_params=_cparams(("parallel", "arbitrary")),
        name=name,
    )(*operands)


def _swa_prompt(zb, bp, bc, sk, layer, *, batch, seq):
    nb = seq // WINDOW
    row = lambda b, n: b * nb + n
    prev = lambda b, n: jnp.maximum(b * nb + n - 1, 0)
    in_specs = [pl.BlockSpec((WINDOW, 1024), lambda b, n: (row(b, n), ZQS // 1024)),
                pl.BlockSpec((WINDOW, 256), lambda b, n: (row(b, n), ZKS // 256)),
                pl.BlockSpec((WINDOW, 256), lambda b, n: (row(b, n), ZVS // 256)),
                pl.BlockSpec((WINDOW, 256), lambda b, n: (prev(b, n), ZKS // 256)),
                pl.BlockSpec((WINDOW, 256), lambda b, n: (prev(b, n), ZVS // 256))]
    kern = functools.partial(_swa_kernel, first_block_has_no_past=True)
    return _swa_call(kern, (batch, nb), in_specs, batch * seq, (zb, zb, zb, zb, zb, bp, bc, sk), WINDOW, layer,
                     "swa_prompt")


def _swa_sample(zb, buf_k, buf_v, bp, bc, sk, layer, *, row0, batch):
    r0 = row0 // T_PAD
    in_specs = [pl.BlockSpec((T_PAD, 1024), lambda b, n: (r0 + b, ZQS // 1024)),
                pl.BlockSpec((T_PAD, 256), lambda b, n: (r0 + b, ZKS // 256)),
                pl.BlockSpec((T_PAD, 256), lambda b, n: (r0 + b, ZVS // 256)),
                pl.BlockSpec((None, None, WINDOW, 256), lambda b, n: (layer, b, 0, 0)),
                pl.BlockSpec((None, None, WINDOW, 256), lambda b, n: (layer, b, 0, 0))]
    kern = functools.partial(_swa_kernel, first_block_has_no_past=False)
    return _swa_call(kern, (batch, 1), in_specs, batch * T_PAD, (zb, zb, zb, buf_k, buf_v, bp, bc, sk), T_PAD, layer,
                     "swa_sample")


def _merge_kernel(x_ref, yap_ref, yas_ref, ybp_ref, ybs_ref, ga_ref, gb_ref, wa_ref, wb_ref, wo_ref, lng_ref, lnb_ref,
                  wrg_ref, wre_ref, brg_ref, bre_ref, x1_ref, route_ref, wbf, wr_b, br_s, *, prompt_tiles):
    i = pl.program_id(0)

    @pl.when(i == 0)
    def _():
        for j, w_ref in enumerate((wa_ref, wb_ref, wo_ref)):
            _cast_into(wbf.at[j], w_ref)
        wr_b[...] = jnp.zeros_like(wr_b)
        wr_b[:, 0:N_GROUPS] = wrg_ref[...].astype(BF16)
        wr_b[:, N_GROUPS:N_GROUPS + N_EXPERTS] = wre_ref[...].astype(BF16)
        br_s[...] = jnp.zeros_like(br_s)
        br_s[:, 0:N_GROUPS] = brg_ref[...]
        br_s[:, N_GROUPS:N_GROUPS + N_EXPERTS] = bre_ref[...]

    is_prompt = i < prompt_tiles
    ya = jnp.where(is_prompt, yap_ref[...], yas_ref[...])
    yb = jnp.where(is_prompt, ybp_ref[...], ybs_ref[...])
    a = _dot(ya.astype(BF16), wbf[0])
    b = _dot(yb.astype(BF16), wbf[1])
    u = jax.nn.sigmoid(ga_ref[...]) * a + jax.nn.sigmoid(gb_ref[...]) * b
    mix = _dot(u.astype(BF16), wbf[2])
    x1 = _layer_norm(DN_ALPHA * x_ref[...] + mix, lng_ref[0:1, :], lnb_ref[0:1, :])
    x1_ref[...] = x1

    logits = _dot(x1.astype(BF16), wr_b[...]) + br_s[...]
    lane = lax.broadcasted_iota(I32, logits.shape, 1)
    lanef = lane.astype(F32)
    gmask = lane < N_GROUPS
    gl = jnp.where(gmask, logits, NEG)
    gmax = jnp.max(gl, axis=-1, keepdims=True)
    gidx = jnp.min(jnp.where(gmask & (gl == gmax), lanef, 999.0), axis=-1, keepdims=True)
    g_w = 1.0 / jnp.sum(jnp.where(gmask, jnp.exp(gl - gmax), 0.0), axis=-1, keepdims=True)
    lo = N_GROUPS + EXPERTS_PER_GROUP * gidx
    emask = (lanef >= lo) & (lanef < lo + EXPERTS_PER_GROUP)
    el = jnp.where(emask, logits, NEG)
    emax = jnp.max(el, axis=-1, keepdims=True)
    ep = jnp.where(emask, jnp.exp(el - emax), 0.0)
    prob = ep / jnp.sum(ep, axis=-1, keepdims=True)
    prob = jnp.where(emask, prob, -1.0)
    p1 = jnp.max(prob, axis=-1, keepdims=True)
    i1 = jnp.min(jnp.where(prob == p1, lanef, 999.0), axis=-1, keepdims=True)
    prob2 = jnp.where(lanef == i1, -1.0, prob)
    p2 = jnp.max(prob2, axis=-1, keepdims=True)
    i2 = jnp.min(jnp.where(prob2 == p2, lanef, 999.0), axis=-1, keepdims=True)
    tot = p1 + p2
    route_ref[...] = jnp.where(lane == 0, i1 - N_GROUPS,
                               jnp.where(lane == 1, i2 - N_GROUPS,
                                         jnp.where(lane == 2, g_w * p1 / tot,
                                                   jnp.where(lane == 3, g_w * p2 / tot, 0.0))))


def _merge(x, ya_p, ya_s, yb_p, yb_s, zb, w_a, w_b, w_out, ln_g, ln_b, w_rg, w_re, b_rg, b_re, layer):
    m = x.shape[0]
    pt = ya_p.shape[0] // TM
    rowblk = lambda w, j: pl.BlockSpec((TM, w), lambda i: (i, j))
    pblk = pl.BlockSpec((TM, D_MODEL), lambda i: (jnp.minimum(i, pt - 1), 0))
    sblk = pl.BlockSpec((TM, D_MODEL), lambda i: (jnp.maximum(i - pt, 0), 0))
    return pl.pallas_call(
        functools.partial(_merge_kernel, prompt_tiles=pt),
        grid=(m // TM,),
        in_specs=[rowblk(D_MODEL, 0), pblk, sblk, pblk, sblk,
                  rowblk(D_MODEL, ZGA // D_MODEL), rowblk(D_MODEL, ZGB // D_MODEL),
                  _layer_spec((D_MODEL, D_MODEL), layer), _layer_spec((D_MODEL, D_MODEL), layer),
                  _layer_spec((D_MODEL, D_MODEL), layer),
                  _layer_spec((3, D_MODEL), layer), _layer_spec((3, D_MODEL), layer),
                  _layer_spec((D_MODEL, N_GROUPS), layer), _layer_spec((D_MODEL, N_EXPERTS), layer),
                  _layer_spec((1, N_GROUPS), layer), _layer_spec((1, N_EXPERTS), layer)],
        out_specs=[rowblk(D_MODEL, 0), rowblk(LANES, 0)],
        out_shape=[jax.ShapeDtypeStruct((m, D_MODEL), F32), jax.ShapeDtypeStruct((m, LANES), F32)],
        scratch_shapes=[pltpu.VMEM((3, D_MODEL, D_MODEL), BF16), pltpu.VMEM((D_MODEL, LANES), BF16),
                        pltpu.VMEM((1, LANES), F32)],
        compiler_params=_cparams(("arbitrary",)),
        name="merge",
    )(x, ya_p, ya_s, yb_p, yb_s, zb, zb, w_a, w_b, w_out, ln_g, ln_b, w_rg, w_re, b_rg, b_re)


def _route_tables(eid, valid, nblk):
    m = eid.shape[0]
    a = 2 * m
    flat = eid.reshape(a)
    ok = jnp.repeat(valid, 2)
    onehot = ((flat[:, None] == jnp.arange(N_EXPERTS, dtype=I32)[None, :]) & ok[:, None]).astype(I32)
    csum = jnp.cumsum(onehot, axis=0)
    counts = csum[-1]
    rank = jnp.sum(onehot * csum, axis=1) - 1
    padded = (counts + BLK - 1) // BLK * BLK
    pad_end = jnp.cumsum(padded)
    pad_start = pad_end - padded
    pos = jnp.sum(onehot * pad_start[None, :], axis=1) + rank
    pos = jnp.where(ok, pos, nblk * BLK + jnp.arange(a, dtype=I32) % (2 * TM))
    nused = pad_end[-1] // BLK
    unused = nused + jnp.arange(N_EXPERTS, dtype=I32)
    zstart = jnp.concatenate([jnp.where(padded > 0, pad_end - BLK, -1),
                              jnp.where(unused < nblk, unused * BLK, -1)]).astype(I32)
    bstart = jnp.minimum(jnp.arange(nblk + 2, dtype=I32), nused - 1) * BLK
    blk_e = jnp.minimum(jnp.sum((pad_end[None, :] <= bstart[:, None]).astype(I32), axis=1), N_EXPERTS - 1)
    return pos.astype(I32), zstart, blk_e.astype(I32), nused.astype(I32).reshape(1)


def _dispatch_kernel(pos_ref, zstart_ref, x_ref, xs_hbm, zbuf, sem, *, nblk):
    i = pl.program_id(0)
    tm = x_ref.shape[0]
    tail = [nblk * BLK + j * BLK for j in range(2 * tm // BLK)]

    @pl.when(i == 0)
    def _():
        zbuf[...] = jnp.zeros_like(zbuf)

        def clear(start):
            return pltpu.make_async_copy(zbuf, xs_hbm.at[pl.ds(start, BLK)], sem.at[1])

        for phase in ("start", "wait"):
            for e in range(zstart_ref.shape[0]):
                @pl.when(zstart_ref[e] >= 0)
                def _():
                    getattr(clear(pl.multiple_of(zstart_ref[e], BLK)), phase)()
            for start in tail:
                getattr(clear(start), phase)()

    base = 2 * i * tm
    for r in range(tm):
        for k in range(2):
            pltpu.make_async_copy(x_ref.at[pl.ds(r, 1)], xs_hbm.at[pl.ds(pos_ref[base + 2 * r + k], 1)],
                                  sem.at[0]).start()
    for k in range(2):
        pltpu.make_async_copy(x_ref, xs_hbm.at[pl.ds(0, tm)], sem.at[0]).wait()


def _dispatch(x1, pos, zstart, nblk):
    m = x1.shape[0]
    grid_spec = pltpu.PrefetchScalarGridSpec(
        num_scalar_prefetch=2,
        grid=(m // TM,),
        in_specs=[pl.BlockSpec((TM, D_MODEL), lambda i, p, z: (i, 0))],
        out_specs=pl.BlockSpec(memory_space=pl.ANY),
        scratch_shapes=[pltpu.VMEM((BLK, D_MODEL), F32), pltpu.SemaphoreType.DMA((2,))],
    )
    return pl.pallas_call(
        functools.partial(_dispatch_kernel, nblk=nblk),
        grid_spec=grid_spec,
        out_shape=jax.ShapeDtypeStruct((nblk * BLK + 2 * TM, D_MODEL), F32),
        compiler_params=_cparams(("arbitrary",)),
        name="dispatch",
    )(pos, zstart, x1)


def _experts_kernel(blk_e_ref, nused_ref, xs_ref, wg_ref, wu_ref, wd_ref, ys_ref, wgb, wub, wdb):
    b = pl.program_id(0)
    used = b < nused_ref[0]
    new_expert = (b == 0) | (blk_e_ref[b] != blk_e_ref[jnp.maximum(b - 1, 0)])

    @pl.when(used & new_expert)
    def _():
        _cast_into(wgb, wg_ref)
        _cast_into(wub, wu_ref)
        _cast_into(wdb, wd_ref)

    @pl.when(used)
    def _():
        xb = xs_ref[...].astype(BF16)
        g = _dot(xb, wgb[...])
        u = _dot(xb, wub[...])
        ys_ref[...] = _dot((g * jax.nn.sigmoid(g) * u).astype(BF16), wdb[...])

    @pl.when(jnp.logical_not(used))
    def _():
        ys_ref[...] = jnp.zeros_like(ys_ref)


def _experts(xs, blk_e, nused, w_eg, w_eu, w_ed, layer):
    nb = xs.shape[0] // BLK
    wspec = lambda s: pl.BlockSpec((None, None) + s, lambda b, be, nu: (layer, be[b], 0, 0))
    grid_spec = pltpu.PrefetchScalarGridSpec(
        num_scalar_prefetch=2,
        grid=(nb,),
        in_specs=[pl.BlockSpec((BLK, D_MODEL), lambda b, be, nu: (jnp.minimum(b, nu[0] - 1), 0)),
                  wspec((D_MODEL, D_EXPERT)), wspec((D_MODEL, D_EXPERT)), wspec((D_EXPERT, D_MODEL))],
        out_specs=pl.BlockSpec((BLK, D_MODEL), lambda b, be, nu: (b, 0)),
        scratch_shapes=[pltpu.VMEM((D_MODEL, D_EXPERT), BF16), pltpu.VMEM((D_MODEL, D_EXPERT), BF16),
                        pltpu.VMEM((D_EXPERT, D_MODEL), BF16)],
    )
    return pl.pallas_call(
        _experts_kernel,
        grid_spec=grid_spec,
        out_shape=jax.ShapeDtypeStruct(xs.shape, F32),
        compiler_params=_cparams(("arbitrary",)),
        name="experts",
    )(blk_e, nused, xs, w_eg, w_eu, w_ed)


def _post_kernel(pos_ref, x1_ref, route_ref, pp_ref, ps_ref, wpg_ref, wpp_ref, lng_ref, lnb_ref, ys_hbm, o_ref,
                 ybuf, wpg_b, wpp_b, sem, *, prompt_tiles):
    i = pl.program_id(0)
    tm = x1_ref.shape[0]

    def gather(tile, slot):
        base = 2 * tile * tm
        for r in range(tm):
            for k in range(2):
                pltpu.make_async_copy(ys_hbm.at[pl.ds(pos_ref[base + 2 * r + k], 1)],
                                      ybuf.at[slot, k, pl.ds(r, 1)], sem.at[slot]).start()

    @pl.when(i == 0)
    def _():
        _cast_into(wpg_b, wpg_ref)
        _cast_into(wpp_b, wpp_ref)
        gather(0, 0)

    def wait(slot):
        for k in range(2):
            pltpu.make_async_copy(ys_hbm.at[pl.ds(0, tm)], ybuf.at[slot, k], sem.at[slot]).wait()

    last = pl.num_programs(0) - 1
    slot = i % 2
    wait(slot)
    gather(jnp.minimum(i + 1, last), (i + 1) % 2)

    route = route_ref[...]
    moe = route[:, 2:3] * ybuf[slot, 0] + route[:, 3:4] * ybuf[slot, 1]
    x2 = _layer_norm(DN_ALPHA * x1_ref[...] + moe, lng_ref[1:2, :], lnb_ref[1:2, :])
    p = jnp.where(i < prompt_tiles, pp_ref[...], ps_ref[...])
    ple = jax.nn.sigmoid(_dot(x2.astype(BF16), wpg_b[...])) * _dot(p.astype(BF16), wpp_b[...])
    o_ref[...] = _layer_norm(DN_ALPHA * x2 + ple, lng_ref[2:3, :], lnb_ref[2:3, :])

    @pl.when(i == last)
    def _():
        wait((i + 1) % 2)


def _post(x1, ys, pos, route, p_prompt, p_sample, w_pg, w_pp, ln_g, ln_b, layer):
    m = x1.shape[0]
    pt = p_prompt.shape[1] // TM
    rowblk = lambda w: pl.BlockSpec((TM, w), lambda i, p: (i, 0))
    lspec = lambda s: pl.BlockSpec((None,) + s, lambda i, p: (layer,) + (0,) * len(s), pipeline_mode=pl.Buffered(1))
    grid_spec = pltpu.PrefetchScalarGridSpec(
        num_scalar_prefetch=1,
        grid=(m // TM,),
        in_specs=[rowblk(D_MODEL), rowblk(LANES),
                  pl.BlockSpec((None, TM, D_PLE), lambda i, p: (layer, jnp.minimum(i, pt - 1), 0)),
                  pl.BlockSpec((None, TM, D_PLE), lambda i, p: (layer, jnp.maximum(i - pt, 0), 0)),
                  lspec((D_MODEL, D_MODEL)), lspec((D_PLE, D_MODEL)), lspec((3, D_MODEL)), lspec((3, D_MODEL)),
                  pl.BlockSpec(memory_space=pl.ANY)],
        out_specs=rowblk(D_MODEL),
        scratch_shapes=[pltpu.VMEM((2, 2, TM, D_MODEL), F32), pltpu.VMEM((D_MODEL, D_MODEL), BF16),
                        pltpu.VMEM((D_PLE, D_MODEL), BF16), pltpu.SemaphoreType.DMA((2,))],
    )
    return pl.pallas_call(
        functools.partial(_post_kernel, prompt_tiles=pt),
        grid_spec=grid_spec,
        out_shape=jax.ShapeDtypeStruct((m, D_MODEL), F32),
        compiler_params=_cparams(("arbitrary",)),
        name="post",
    )(pos, x1, route, p_prompt, p_sample, w_pg, w_pp, ln_g, ln_b, ys)


def kernel(x_prompt, x_sample, state_mlstm_C, state_mlstm_n, state_mlstm_m, state_swa_k, state_swa_v, p_prompt, p_sample, w_in, b_in, mh_gain, w_a, w_b, w_out, rel_table, w_sink, ln_g, ln_b, w_rg, b_rg, w_re, b_re, w_eg, w_eu, w_ed, w_pg, w_pp):
    bp_, seq = x_prompt.shape[:2]
    bs_, tdec = x_sample.shape[:2]
    mp = bp_ * seq
    msp = bs_ * T_PAD
    mall = mp + msp
    nblk = 2 * (mp + bs_ * tdec) // BLK + N_EXPERTS

    pad_t = lambda t: jnp.pad(t, [(0, 0)] * (t.ndim - 2) + [(0, T_PAD - tdec), (0, 0)])
    x = lax.dynamic_update_slice(jnp.pad(x_prompt.reshape(mp, D_MODEL), ((0, msp), (0, 0))),
                                 pad_t(x_sample).reshape(msp, D_MODEL), (mp, 0))
    valid = jnp.concatenate([jnp.ones((mp,), bool), jnp.tile(jnp.arange(T_PAD) < tdec, bs_)])
    p_prompt = p_prompt.reshape(DEPTH, mp, D_PLE)
    p_sample = pad_t(p_sample).reshape(DEPTH, msp, D_PLE)

    b_in3 = b_in.reshape(DEPTH, 1, -1)
    w_qkv2, b_qkv2 = w_in[:, :, IN_QKV2[0]:IN_QKV2[1]], b_in3[:, :, IN_QKV2[0]:IN_QKV2[1]]
    w_gates, b_gates = w_in[:, :, IN_GATES[0]:IN_GATES[1]], b_in3[:, :, IN_GATES[0]:IN_GATES[1]]
    b_rg, b_re = b_rg.reshape(DEPTH, 1, N_GROUPS), b_re.reshape(DEPTH, 1, N_EXPERTS)
    gain = mh_gain.reshape(DEPTH, 1, D_MODEL)

    bias_p = _swa_bias(rel_table, WINDOW)
    bias_s = _swa_bias(rel_table, T_PAD)
    sink_p = _sink_cols(w_sink, WINDOW)
    sink_s = _sink_cols(w_sink, T_PAD)
    buf_k = state_swa_k.reshape(DEPTH, bs_, WINDOW, S_KV_HEADS * S_HEAD_DIM)
    buf_v = state_swa_v.reshape(DEPTH, bs_, WINDOW, S_KV_HEADS * S_HEAD_DIM)
    m0_s = jnp.broadcast_to(state_mlstm_m[..., None], (DEPTH, bs_, M_HEADS, LANES))
    zc = jnp.zeros((1, bp_, M_HEADS, M_DQK, M_DV), F32)
    zn = jnp.zeros((1, bp_, M_HEADS, LANES), F32)

    st_p, st_s = [], []
    for i in range(DEPTH):
        za = _inproj(x, [(w_in, IN_A)], [(b_in3, IN_A)], i, "inproj_a")
        zb = _inproj(x, [(w_gates, 2048), (w_qkv2, 1536)], [(b_gates, 2048), (b_qkv2, 1536)], i, "inproj_b")

        ya_p, c_p, n_p, m_p = _mlstm(za, gain, i, zc, zn, zn, 0, row0=0, batch=bp_, t_pad=seq, t_valid=seq, chunk=128)
        ya_s, c_s, n_s, m_s = _mlstm(za, gain, i, state_mlstm_C, state_mlstm_n, m0_s, i,
                                     row0=mp, batch=bs_, t_pad=T_PAD, t_valid=tdec, chunk=T_PAD)
        yb_p = _swa_prompt(zb, bias_p[0], bias_p[1], sink_p, i, batch=bp_, seq=seq)
        yb_s = _swa_sample(zb, buf_k, buf_v, bias_s[0], bias_s[1], sink_s, i, row0=mp, batch=bs_)

        x1, route = _merge(x, ya_p, ya_s, yb_p, yb_s, zb, w_a, w_b, w_out, ln_g, ln_b, w_rg, w_re, b_rg, b_re, i)
        pos, zstart, blk_e, nused = _route_tables(route[:, :2].astype(I32), valid, nblk)
        xs = _dispatch(x1, pos, zstart, nblk)
        ys = _experts(xs, blk_e, nused, w_eg, w_eu, w_ed, i)
        x = _post(x1, ys, pos, route, p_prompt, p_sample, w_pg, w_pp, ln_g, ln_b, i)

        kv_p = jnp.stack([lax.slice(zb, ((b + 1) * seq - WINDOW, ZKS), ((b + 1) * seq, IN_B)) for b in range(bp_)])
        kv_s = lax.slice(zb, (mp, ZKS), (mall, IN_B)).reshape(bs_, T_PAD, IN_B - ZKS)[:, :tdec]
        heads = lambda t: t.reshape(t.shape[0], WINDOW, S_KV_HEADS, S_HEAD_DIM)
        st_p.append((c_p, n_p, m_p[:, :, 0], heads(kv_p[..., :256]), heads(kv_p[..., 256:])))
        st_s.append((c_s, n_s, m_s[:, :, 0],
                     heads(jnp.concatenate([buf_k[i][:, tdec:], kv_s[..., :256]], axis=1)),
                     heads(jnp.concatenate([buf_v[i][:, tdec:], kv_s[..., 256:]], axis=1))))

    stk = lambda lst, j: jnp.stack([s[j] for s in lst], axis=0)
    y_p = x[:mp].reshape(bp_, seq, D_MODEL)
    y_s = x[mp:].reshape(bs_, T_PAD, D_MODEL)[:, :tdec]
    return (y_p, y_s,
            stk(st_p, 0), stk(st_p, 1), stk(st_p, 2), stk(st_p, 3), stk(st_p, 4),
            stk(st_s, 0), stk(st_s, 1), stk(st_s, 2), stk(st_s, 3), stk(st_s, 4))
```

```python
import functools

import numpy as np
import jax
import jax.numpy as jnp
from jax import lax
from jax.experimental import pallas as pl
from jax.experimental.pallas import tpu as pltpu

F32 = jnp.float32
BF16 = jnp.bfloat16
I32 = jnp.int32
HIGHEST = lax.Precision.HIGHEST

D_MODEL = 1024
DEPTH = 4
D_PLE = 256
M_HEADS = 4
M_DQK = 128
M_DV = 256
S_HEADS = 16
S_KV_HEADS = 4
S_HEAD_DIM = 64
S_GROUP = 4
D_KV = S_KV_HEADS * S_HEAD_DIM
WINDOW = 128
REL_BUCKETS = 32
N_GROUPS = 4
EXPERTS_PER_GROUP = 8
N_EXPERTS = 32
D_EXPERT = 512
DN_ALPHA = (2 * DEPTH) ** 0.25
LN_EPS = 1e-5
NEG = -1e30

IN_A = 3200
IN_QKV2 = (3080, 4616)
IN_GATES = (4616, 6664)
ZQ, ZK, ZV, ZOG, ZG = 0, 512, 1024, 2048, 3072
IN_B = 3584
ZGA, ZGB, ZQS, ZKS, ZVS = 0, 1024, 2048, 3072, 3328

LANES = 128
T_PAD = 8
SEQ_BLK = LANES // T_PAD
TM = 512
TD = 256
BLK = 256
VMEM_LIMIT = 56 * 1024 * 1024


def _cparams(sem):
    return pltpu.CompilerParams(dimension_semantics=sem, vmem_limit_bytes=VMEM_LIMIT)


def _layer_spec(shape, layer):
    nd = len(shape)
    return pl.BlockSpec((None,) + shape, lambda *_: (layer,) + (0,) * nd, pipeline_mode=pl.Buffered(1))


def _layer_norm(y, g, b):
    mu = jnp.mean(y, axis=-1, keepdims=True)
    d = y - mu
    var = jnp.mean(d * d, axis=-1, keepdims=True)
    return d * lax.rsqrt(var + LN_EPS) * g + b


def _dot(a, b):
    return jnp.dot(a, b, preferred_element_type=F32)


def _dot_nt(a, b):
    return lax.dot_general(a, b, (((1,), (1,)), ((), ())), preferred_element_type=F32)


def _dot_tn(a, b):
    return lax.dot_general(a, b, (((0,), (0,)), ((), ())), preferred_element_type=F32)


def _dot_exact(a, b):
    return jnp.dot(a, b, precision=HIGHEST, preferred_element_type=F32)


def _cast_into(dst_ref, src_ref, col0=0, step=512):
    n = src_ref.shape[-1]
    for j in range(0, n, step):
        w = min(step, n - j)
        dst_ref[:, col0 + j:col0 + j + w] = src_ref[:, j:j + w].astype(BF16)


def _inproj_kernel(*refs, widths, out_widths):
    ng, no = len(widths), len(out_widths)
    x_ref, w_refs, b_refs = refs[0], refs[1:1 + ng], refs[1 + ng:1 + 2 * ng]
    o_refs, wbf = refs[1 + 2 * ng:1 + 2 * ng + no], refs[1 + 2 * ng + no]

    @pl.when(pl.program_id(0) == 0)
    def _():
        c = 0
        for w_ref, n in zip(w_refs, widths):
            _cast_into(wbf, w_ref, c)
            c += n

    out_start = np.cumsum((0,) + tuple(out_widths))

    def store(col, val):
        k = int(np.searchsorted(out_start, col, side='right')) - 1
        lo = col - int(out_start[k])
        o_refs[k][:, lo:lo + val.shape[1]] = val.astype(o_refs[k].dtype)

    xb = x_ref[...].astype(BF16)
    c = 0
    for b_ref, n in zip(b_refs, widths):
        for j in range(0, n, 512):
            w = min(512, n - j)
            store(c + j, _dot(xb, wbf[:, c + j:c + j + w]) + b_ref[:, j:j + w])
        c += n


def _inproj(x, ws, bs, outs, layer, name):
    m = x.shape[0]
    widths = tuple(n for _, n in ws)
    out_widths = tuple(n for n, _ in outs)
    assert sum(widths) == sum(out_widths)
    return pl.pallas_call(
        functools.partial(_inproj_kernel, widths=widths, out_widths=out_widths),
        grid=(m // TM,),
        in_specs=([pl.BlockSpec((TM, D_MODEL), lambda i: (i, 0))]
                  + [_layer_spec((D_MODEL, n), layer) for _, n in ws]
                  + [_layer_spec((1, n), layer) for _, n in bs]),
        out_specs=[pl.BlockSpec((TM, n), lambda i: (i, 0)) for n, _ in outs],
        out_shape=[jax.ShapeDtypeStruct((m, n), dt) for n, dt in outs],
        scratch_shapes=[pltpu.VMEM((D_MODEL, sum(widths)), BF16)],
        compiler_params=_cparams(("arbitrary",)),
        name=name,
    )(x, *[w for w, _ in ws], *[b for b, _ in bs])


def _log_sigmoid(x):
    return jnp.minimum(x, 0.0) - jnp.log(1.0 + jnp.exp(-jnp.abs(x)))


def _mlstm_kernel(zq, zk, zv, zog, zg, gain, c0, n0, m0, *rest, seg, t_valid, aliased):
    ya, co, no, mo, qc_s = rest[aliased:]
    L = zq.shape[0]
    nseg = L // seg

    @pl.when(pl.program_id(1) == 0)
    def _():
        co[...] = c0[...]
        no[...] = n0[...]
        mo[...] = m0[...]

    g_c = zg[...]
    g_r = g_c.T
    ig_c, lf_c = g_c, _log_sigmoid(g_c)
    ig_r, lf_r = g_r, _log_sigmoid(g_r)
    ri = lax.broadcasted_iota(I32, (L, L), 0)
    ci = lax.broadcasted_iota(I32, (L, L), 1)
    shift = seg.bit_length() - 1
    if t_valid < seg:
        rv = (lax.broadcasted_iota(I32, (L, 1), 0) & (seg - 1)) < t_valid
        cv = (lax.broadcasted_iota(I32, (1, L), 1) & (seg - 1)) < t_valid
        ig_c, lf_c = jnp.where(rv, ig_c, NEG), jnp.where(rv, lf_c, 0.0)
        ig_r, lf_r = jnp.where(cv, ig_r, NEG), jnp.where(cv, lf_r, 0.0)
    if nseg > 1:
        same = (ri >> shift) == (ci >> shift)
        causal, causal_t = same & (ri >= ci), same & (ri <= ci)
        same_f = same.astype(F32)
        be_c = _dot_exact(same_f, lf_c)
        be_r = _dot_exact(lf_r, same_f)
        expand = ((lax.broadcasted_iota(I32, (L, nseg), 0) >> shift)
                  == lax.broadcasted_iota(I32, (L, nseg), 1)).astype(F32)
    else:
        causal, causal_t = ri >= ci, ri <= ci
    b_c = _dot_exact(causal.astype(F32), lf_c)
    b_r = _dot_exact(lf_r, causal_t.astype(F32))

    for h in range(M_HEADS):
        icol, irow = ig_c[:, h:h + 1], ig_r[h:h + 1, :]
        bcol, brow = b_c[:, 4 + h:5 + h], b_r[4 + h:5 + h, :]
        q = zq[:, h * M_DQK:(h + 1) * M_DQK].astype(F32) * (M_DQK ** -0.5)
        k = zk[:, h * M_DQK:(h + 1) * M_DQK].astype(F32)
        qb, kb = q.astype(BF16), zk[:, h * M_DQK:(h + 1) * M_DQK]
        vb = zv[:, h * M_DV:(h + 1) * M_DV]
        if nseg > 1:
            mcol = _dot_exact(expand, mo[:, h, :])[:, 0:1]
            nrow = _dot_exact(expand, no[:, h, :])
            becol, berow = be_c[:, 4 + h:5 + h], be_r[4 + h:5 + h, :]
            for j in range(nseg):
                rows = slice(j * seg, (j + 1) * seg)
                qc_s[rows, :] = _dot(q[rows].astype(BF16), co[j, h].astype(BF16))
            q_c = qc_s[...]
        else:
            mcol, nrow = mo[0, h:h + 1, 0:1], no[0, h:h + 1, :]
            becol = bcol[L - 1:L, :]
            q_c = _dot(qb, co[0, h].astype(BF16))

        dmat = jnp.where(causal, bcol - brow + irow, NEG)
        inter = bcol + mcol
        mhat = jnp.maximum(inter, jnp.max(dmat, axis=-1, keepdims=True))
        w_intra = jnp.exp(dmat - mhat)
        w_inter = jnp.exp(inter - mhat)
        s = _dot_nt(qb, kb) * w_intra
        num = _dot(s.astype(BF16), vb) + w_inter * q_c
        den = jnp.sum(s, axis=-1, keepdims=True) + w_inter * jnp.sum(q * nrow, axis=-1, keepdims=True)
        hh = num / jnp.maximum(jnp.abs(den), jnp.exp(-mhat))

        gcol = icol + becol - bcol
        if nseg > 1:
            gmax = jnp.max(jnp.where(same, irow + berow - brow, NEG), axis=-1, keepdims=True)
        else:
            gmax = jnp.max(gcol, axis=0, keepdims=True)
        m_new = jnp.maximum(becol + mcol, gmax)
        decay = jnp.exp(becol + mcol - m_new)
        kw = k * jnp.exp(gcol - m_new)
        vf = vb.astype(F32) if nseg > 1 else vb
        for j in range(nseg):
            rows = slice(j * seg, (j + 1) * seg)
            dj = decay[j * seg:j * seg + 1, :] if nseg > 1 else decay
            mj = m_new[j * seg:j * seg + 1, :] if nseg > 1 else m_new
            co[j, h] = dj * co[j, h] + _dot_tn(kw[rows].astype(BF16), vf[rows].astype(BF16))
            no[j, h:h + 1, :] = dj * no[j, h:h + 1, :] + jnp.sum(kw[rows], axis=0, keepdims=True)
            mo[j, h:h + 1, :] = jnp.broadcast_to(mj, (1, LANES))

        mu = jnp.mean(hh, axis=-1, keepdims=True)
        d = hh - mu
        var = jnp.mean(d * d, axis=-1, keepdims=True)
        hn = d * lax.rsqrt(var + LN_EPS) * gain[:, h * M_DV:(h + 1) * M_DV]
        og = zog[:, h * M_DV:(h + 1) * M_DV].astype(F32)
        ya[:, h * M_DV:(h + 1) * M_DV] = (hn * jax.nn.sigmoid(og)).astype(ya.dtype)


def _mlstm(za, zg, gain, layer, c0, n0, m0, st_layer, outs_prev, out_layers, out_layer, *, row0, nseq, steps,
           seg, t_valid):
    nseg = LANES // seg
    r0 = row0 // LANES
    row = lambda b, c: r0 + b * steps + c
    nstate = nseq * nseg
    aliased = 0 if outs_prev is None else 3
    kern = functools.partial(_mlstm_kernel, seg=seg, t_valid=min(t_valid, seg), aliased=aliased)
    in4 = pl.BlockSpec((None, nseg, M_HEADS, M_DQK, M_DV), lambda b, c: (st_layer, b, 0, 0, 0))
    in3 = pl.BlockSpec((None, nseg, M_HEADS, LANES), lambda b, c: (st_layer, b, 0, 0))
    out4 = pl.BlockSpec((None, nseg, M_HEADS, M_DQK, M_DV), lambda b, c: (out_layer, b, 0, 0, 0))
    out3 = pl.BlockSpec((None, nseg, M_HEADS, LANES), lambda b, c: (out_layer, b, 0, 0))
    anyspec = pl.BlockSpec(memory_space=pl.ANY)
    n_in = 9
    return pl.pallas_call(
        kern,
        grid=(nseq, steps),
        in_specs=[pl.BlockSpec((LANES, 512), lambda b, c: (row(b, c), ZQ // 512)),
                  pl.BlockSpec((LANES, 512), lambda b, c: (row(b, c), ZK // 512)),
                  pl.BlockSpec((LANES, 1024), lambda b, c: (row(b, c), ZV // 1024)),
                  pl.BlockSpec((LANES, 1024), lambda b, c: (row(b, c), ZOG // 1024)),
                  pl.BlockSpec((LANES, LANES), lambda b, c: (row(b, c), 0)),
                  pl.BlockSpec((None, 1, D_MODEL), lambda b, c: (layer, 0, 0)),
                  in4, in3, in3] + [anyspec] * aliased,
        out_specs=[pl.BlockSpec((LANES, D_MODEL), lambda b, c: (b * steps + c, 0)), out4, out3, out3],
        out_shape=[jax.ShapeDtypeStruct((nseq * steps * LANES, D_MODEL), BF16),
                   jax.ShapeDtypeStruct((out_layers, nstate, M_HEADS, M_DQK, M_DV), F32),
                   jax.ShapeDtypeStruct((out_layers, nstate, M_HEADS, LANES), F32),
                   jax.ShapeDtypeStruct((out_layers, nstate, M_HEADS, LANES), F32)],
        scratch_shapes=[pltpu.VMEM((LANES, M_DV), F32)],
        input_output_aliases={n_in + j: 1 + j for j in range(aliased)},
        compiler_params=_cparams(("parallel", "arbitrary")),
        name="mlstm",
    )(za, za, za, za, zg, gain, c0, n0, m0, *(outs_prev or ()))


def _swa_kernel(q_ref, kc_ref, vc_ref, kp_ref, vp_ref, bp_ref, bc_ref, sink_ref, *rest, nseq, t_new,
                first_block_has_no_past, aliased):
    o_ref = rest[aliased]
    tq = q_ref.shape[0] // nseq
    have_past = pl.program_id(1) > 0
    q_all, kc_all, vc_all = q_ref[...], kc_ref[...], vc_ref[...]
    if nseq > 1:
        q_all, kc_all, vc_all = q_all.astype(F32), kc_all.astype(F32), vc_all.astype(F32)
    for b in range(nseq):
        rows = slice(b * tq, (b + 1) * tq)
        q, kc_b, vc_b = q_all[rows], kc_all[rows], vc_all[rows]
        kp_b = kp_ref[b] if nseq > 1 else kp_ref[...]
        vp_b = vp_ref[b] if nseq > 1 else vp_ref[...]
        outs = []
        for kvh in range(S_KV_HEADS):
            sl = slice(kvh * S_HEAD_DIM, (kvh + 1) * S_HEAD_DIM)
            qg = jnp.concatenate(
                [q[:, (kvh * S_GROUP + g) * S_HEAD_DIM:(kvh * S_GROUP + g + 1) * S_HEAD_DIM] for g in range(S_GROUP)],
                axis=0).astype(BF16)
            kp, vp = kp_b[:, sl].astype(BF16), vp_b[:, sl].astype(BF16)
            kc, vc = kc_b[:, sl].astype(BF16), vc_b[:, sl].astype(BF16)
            sp = _dot_nt(qg, kp) * (S_HEAD_DIM ** -0.5) + bp_ref[kvh]
            sc = _dot_nt(qg, kc) * (S_HEAD_DIM ** -0.5) + bc_ref[kvh]
            if first_block_has_no_past:
                sp = jnp.where(have_past, sp, NEG)
            sk = sink_ref[kvh]
            mx = jnp.maximum(jnp.maximum(jnp.max(sp, axis=-1, keepdims=True), jnp.max(sc, axis=-1, keepdims=True)), sk)
            pp, pc = jnp.exp(sp - mx), jnp.exp(sc - mx)
            den = jnp.sum(pp, axis=-1, keepdims=True) + jnp.sum(pc, axis=-1, keepdims=True) + jnp.exp(sk - mx)
            o = (_dot(pp.astype(BF16), vp) + _dot(pc.astype(BF16), vc)) / den
            outs.extend(o[g * tq:(g + 1) * tq] for g in range(S_GROUP))
        o_ref[rows, :] = jnp.concatenate(outs, axis=1).astype(o_ref.dtype)
        if aliased:
            for new, prev, out in ((kc_b, kp_b, rest[aliased + 1]), (vc_b, vp_b, rest[aliased + 2])):
                out[b, 0:WINDOW - t_new, :] = prev[t_new:, :]
                out[b, WINDOW - t_new:WINDOW, :] = new[0:t_new, :]


def _rel_bucket(dist):
    nn = np.maximum(dist, 0)
    max_exact = REL_BUCKETS // 2
    large = max_exact + (np.log(np.maximum(nn, 1) / max_exact) / np.log(WINDOW / max_exact)
                         * (REL_BUCKETS - max_exact)).astype(np.int32)
    large = np.minimum(large, REL_BUCKETS - 1)
    return np.where(nn < max_exact, nn, large).astype(np.int32)


def _swa_bias(rel_table, tq):
    t = np.arange(tq)[:, None]
    d_prev = t + WINDOW - np.arange(WINDOW)[None, :]
    d_cur = t - np.arange(tq)[None, :]

    def build(dist, ok):
        onehot = np.eye(REL_BUCKETS, dtype=np.float32)[_rel_bucket(dist).reshape(-1)]
        bias = jnp.dot(onehot, rel_table.astype(F32), precision=HIGHEST).reshape(dist.shape + (S_HEADS,))
        bias = jnp.transpose(bias, (2, 0, 1))
        bias = jnp.where(ok[None], bias, NEG)
        return bias.reshape(S_KV_HEADS, S_GROUP * tq, dist.shape[1])

    return build(d_prev, d_prev <= WINDOW), build(d_cur, d_cur >= 0)


def _sink_cols(w_sink, tq):
    s = w_sink.astype(F32).reshape(DEPTH, S_KV_HEADS, S_GROUP, 1, 1)
    return jnp.broadcast_to(s, (DEPTH, S_KV_HEADS, S_GROUP, tq, 1)).reshape(DEPTH, S_KV_HEADS, S_GROUP * tq, 1)


def _swa_bias_specs(tq, layer):
    return [pl.BlockSpec((S_KV_HEADS, S_GROUP * tq, WINDOW), lambda b, n: (0, 0, 0)),
            pl.BlockSpec((S_KV_HEADS, S_GROUP * tq, tq), lambda b, n: (0, 0, 0)),
            pl.BlockSpec((None, S_KV_HEADS, S_GROUP * tq, 1), lambda b, n: (layer, 0, 0, 0))]


def _swa_prompt(zb, bp, bc, sk, layer, *, batch, seq):
    nb = seq // WINDOW
    row = lambda b, n: b * nb + n
    prev = lambda b, n: jnp.maximum(b * nb + n - 1, 0)
    in_specs = [pl.BlockSpec((WINDOW, 1024), lambda b, n: (row(b, n), ZQS // 1024)),
                pl.BlockSpec((WINDOW, D_KV), lambda b, n: (row(b, n), ZKS // D_KV)),
                pl.BlockSpec((WINDOW, D_KV), lambda b, n: (row(b, n), ZVS // D_KV)),
                pl.BlockSpec((WINDOW, D_KV), lambda b, n: (prev(b, n), ZKS // D_KV)),
                pl.BlockSpec((WINDOW, D_KV), lambda b, n: (prev(b, n), ZVS // D_KV))]
    kern = functools.partial(_swa_kernel, nseq=1, t_new=0, first_block_has_no_past=True, aliased=0)
    return pl.pallas_call(
        kern,
        grid=(batch, nb),
        in_specs=in_specs + _swa_bias_specs(WINDOW, layer),
        out_specs=pl.BlockSpec((WINDOW, D_MODEL), lambda b, n: (row(b, n), 0)),
        out_shape=jax.ShapeDtypeStruct((batch * seq, D_MODEL), BF16),
        compiler_params=_cparams(("parallel", "arbitrary")),
        name="swa_prompt",
    )(zb, zb, zb, zb, zb, bp, bc, sk)


def _swa_sample(zb, buf_k, buf_v, bp, bc, sk, layer, kv_prev, *, row0, batch, t_new):
    r0 = row0 // LANES
    steps = batch // SEQ_BLK
    state = pl.BlockSpec((None, SEQ_BLK, WINDOW, D_KV), lambda b, n: (layer, b, 0, 0))
    anyspec = pl.BlockSpec(memory_space=pl.ANY)
    in_specs = [pl.BlockSpec((LANES, 1024), lambda b, n: (r0 + b, ZQS // 1024)),
                pl.BlockSpec((LANES, D_KV), lambda b, n: (r0 + b, ZKS // D_KV)),
                pl.BlockSpec((LANES, D_KV), lambda b, n: (r0 + b, ZVS // D_KV)),
                state, state]
    kern = functools.partial(_swa_kernel, nseq=SEQ_BLK, t_new=t_new, first_block_has_no_past=False, aliased=2)
    return pl.pallas_call(
        kern,
        grid=(steps, 1),
        in_specs=in_specs + _swa_bias_specs(T_PAD, layer) + [anyspec, anyspec],
        out_specs=[pl.BlockSpec((LANES, D_MODEL), lambda b, n: (b, 0)), state, state],
        out_shape=[jax.ShapeDtypeStruct((batch * T_PAD, D_MODEL), BF16),
                   jax.ShapeDtypeStruct(buf_k.shape, F32), jax.ShapeDtypeStruct(buf_v.shape, F32)],
        input_output_aliases={8: 1, 9: 2},
        compiler_params=_cparams(("parallel", "arbitrary")),
        name="swa_sample",
    )(zb, zb, zb, buf_k, buf_v, bp, bc, sk, *kv_prev)


def _merge_kernel(x_ref, yap_ref, yas_ref, ybp_ref, ybs_ref, ga_ref, gb_ref, wa_ref, wb_ref, wo_ref, lng_ref, lnb_ref,
                  wrg_ref, wre_ref, brg_ref, bre_ref, x1_ref, route_ref, wbf, wr_b, br_s, *, prompt_tiles):
    i = pl.program_id(0)

    @pl.when(i == 0)
    def _():
        for j, w_ref in enumerate((wa_ref, wb_ref, wo_ref)):
            _cast_into(wbf.at[j], w_ref)
        wr_b[...] = jnp.zeros_like(wr_b)
        wr_b[:, 0:N_GROUPS] = wrg_ref[...].astype(BF16)
        wr_b[:, N_GROUPS:N_GROUPS + N_EXPERTS] = wre_ref[...].astype(BF16)
        br_s[...] = jnp.zeros_like(br_s)
        br_s[:, 0:N_GROUPS] = brg_ref[...]
        br_s[:, N_GROUPS:N_GROUPS + N_EXPERTS] = bre_ref[...]

    is_prompt = i < prompt_tiles
    ya = jnp.where(is_prompt, yap_ref[...], yas_ref[...])
    yb = jnp.where(is_prompt, ybp_ref[...], ybs_ref[...])
    a = _dot(ya, wbf[0])
    b = _dot(yb, wbf[1])
    u = jax.nn.sigmoid(ga_ref[...].astype(F32)) * a + jax.nn.sigmoid(gb_ref[...].astype(F32)) * b
    mix = _dot(u.astype(BF16), wbf[2])
    x1 = _layer_norm(DN_ALPHA * x_ref[...] + mix, lng_ref[0:1, :], lnb_ref[0:1, :])
    x1_ref[...] = x1

    logits = _dot(x1.astype(BF16), wr_b[...]) + br_s[...]
    lane = lax.broadcasted_iota(I32, logits.shape, 1)
    lanef = lane.astype(F32)
    gmask = lane < N_GROUPS
    gl = jnp.where(gmask, logits, NEG)
    gmax = jnp.max(gl, axis=-1, keepdims=True)
    gidx = jnp.min(jnp.where(gmask & (gl == gmax), lanef, 999.0), axis=-1, keepdims=True)
    g_w = 1.0 / jnp.sum(jnp.where(gmask, jnp.exp(gl - gmax), 0.0), axis=-1, keepdims=True)
    lo = N_GROUPS + EXPERTS_PER_GROUP * gidx
    emask = (lanef >= lo) & (lanef < lo + EXPERTS_PER_GROUP)
    el = jnp.where(emask, logits, NEG)
    emax = jnp.max(el, axis=-1, keepdims=True)
    ep = jnp.where(emask, jnp.exp(el - emax), 0.0)
    prob = ep / jnp.sum(ep, axis=-1, keepdims=True)
    prob = jnp.where(emask, prob, -1.0)
    p1 = jnp.max(prob, axis=-1, keepdims=True)
    i1 = jnp.min(jnp.where(prob == p1, lanef, 999.0), axis=-1, keepdims=True)
    prob2 = jnp.where(lanef == i1, -1.0, prob)
    p2 = jnp.max(prob2, axis=-1, keepdims=True)
    i2 = jnp.min(jnp.where(prob2 == p2, lanef, 999.0), axis=-1, keepdims=True)
    tot = p1 + p2
    route_ref[...] = jnp.where(lane == 0, i1 - N_GROUPS,
                               jnp.where(lane == 1, i2 - N_GROUPS,
                                         jnp.where(lane == 2, g_w * p1 / tot,
                                                   jnp.where(lane == 3, g_w * p2 / tot, 0.0))))


def _merge(x, ya_p, ya_s, yb_p, yb_s, zb, w_a, w_b, w_out, ln_g, ln_b, w_rg, w_re, b_rg, b_re, layer):
    m = x.shape[0]
    pt = ya_p.shape[0] // TM
    rowblk = lambda w, j: pl.BlockSpec((TM, w), lambda i: (i, j))
    pblk = pl.BlockSpec((TM, D_MODEL), lambda i: (jnp.minimum(i, pt - 1), 0))
    sblk = pl.BlockSpec((TM, D_MODEL), lambda i: (jnp.maximum(i - pt, 0), 0))
    return pl.pallas_call(
        functools.partial(_merge_kernel, prompt_tiles=pt),
        grid=(m // TM,),
        in_specs=[rowblk(D_MODEL, 0), pblk, sblk, pblk, sblk,
                  rowblk(D_MODEL, ZGA // D_MODEL), rowblk(D_MODEL, ZGB // D_MODEL),
                  _layer_spec((D_MODEL, D_MODEL), layer), _layer_spec((D_MODEL, D_MODEL), layer),
                  _layer_spec((D_MODEL, D_MODEL), layer),
                  _layer_spec((3, D_MODEL), layer), _layer_spec((3, D_MODEL), layer),
                  _layer_spec((D_MODEL, N_GROUPS), layer), _layer_spec((D_MODEL, N_EXPERTS), layer),
                  _layer_spec((1, N_GROUPS), layer), _layer_spec((1, N_EXPERTS), layer)],
        out_specs=[rowblk(D_MODEL, 0), rowblk(LANES, 0)],
        out_shape=[jax.ShapeDtypeStruct((m, D_MODEL), F32), jax.ShapeDtypeStruct((m, LANES), F32)],
        scratch_shapes=[pltpu.VMEM((3, D_MODEL, D_MODEL), BF16), pltpu.VMEM((D_MODEL, LANES), BF16),
                        pltpu.VMEM((1, LANES), F32)],
        compiler_params=_cparams(("arbitrary",)),
        name="merge",
    )(x, ya_p, ya_s, yb_p, yb_s, zb, zb, w_a, w_b, w_out, ln_g, ln_b, w_rg, w_re, b_rg, b_re)


def _route_tables(eid, valid, nblk):
    m = eid.shape[0]
    a = 2 * m
    flat = eid.reshape(a)
    ok = jnp.repeat(valid, 2)
    onehot = ((flat[:, None] == jnp.arange(N_EXPERTS, dtype=I32)[None, :]) & ok[:, None]).astype(I32)
    csum = jnp.cumsum(onehot, axis=0)
    counts = csum[-1]
    rank = jnp.sum(onehot * csum, axis=1) - 1
    padded = (counts + BLK - 1) // BLK * BLK
    pad_end = jnp.cumsum(padded)
    pad_start = pad_end - padded
    pos = jnp.sum(onehot * pad_start[None, :], axis=1) + rank
    pos = jnp.where(ok, pos, nblk * BLK + jnp.arange(a, dtype=I32) % (2 * TD))
    nused = pad_end[-1] // BLK
    unused = nused + jnp.arange(N_EXPERTS, dtype=I32)
    zstart = jnp.concatenate([jnp.where(padded > 0, pad_end - BLK, -1),
                              jnp.where(unused < nblk, unused * BLK, -1)]).astype(I32)
    bstart = jnp.minimum(jnp.arange(nblk + 2, dtype=I32), nused - 1) * BLK
    blk_e = jnp.minimum(jnp.sum((pad_end[None, :] <= bstart[:, None]).astype(I32), axis=1), N_EXPERTS - 1)
    return pos.astype(I32), zstart, blk_e.astype(I32), nused.astype(I32).reshape(1)


def _dispatch_kernel(pos_ref, zstart_ref, x_ref, xs_hbm, zbuf, sem, *, nblk):
    i = pl.program_id(0)
    tm = x_ref.shape[0]
    tail = [nblk * BLK + j * BLK for j in range(2 * tm // BLK)]

    @pl.when(i == 0)
    def _():
        zbuf[...] = jnp.zeros_like(zbuf)

        def clear(start):
            return pltpu.make_async_copy(zbuf, xs_hbm.at[pl.ds(start, BLK)], sem.at[1])

        for phase in ("start", "wait"):
            for e in range(zstart_ref.shape[0]):
                @pl.when(zstart_ref[e] >= 0)
                def _():
                    getattr(clear(pl.multiple_of(zstart_ref[e], BLK)), phase)()
            for start in tail:
                getattr(clear(start), phase)()

    base = 2 * i * tm
    for r in range(tm):
        for k in range(2):
            pltpu.make_async_copy(x_ref.at[pl.ds(r, 1)], xs_hbm.at[pl.ds(pos_ref[base + 2 * r + k], 1)],
                                  sem.at[0]).start()
    for k in range(2):
        pltpu.make_async_copy(x_ref, xs_hbm.at[pl.ds(0, tm)], sem.at[0]).wait()


def _dispatch(x1, pos, zstart, nblk):
    m = x1.shape[0]
    grid_spec = pltpu.PrefetchScalarGridSpec(
        num_scalar_prefetch=2,
        grid=(m // TD,),
        in_specs=[pl.BlockSpec((TD, D_MODEL), lambda i, p, z: (i, 0))],
        out_specs=pl.BlockSpec(memory_space=pl.ANY),
        scratch_shapes=[pltpu.VMEM((BLK, D_MODEL), F32), pltpu.SemaphoreType.DMA((2,))],
    )
    return pl.pallas_call(
        functools.partial(_dispatch_kernel, nblk=nblk),
        grid_spec=grid_spec,
        out_shape=jax.ShapeDtypeStruct((nblk * BLK + 2 * TD, D_MODEL), F32),
        compiler_params=_cparams(("arbitrary",)),
        name="dispatch",
    )(pos, zstart, x1)


def _experts_kernel(blk_e_ref, nused_ref, xs_ref, wg_ref, wu_ref, wd_ref, ys_ref, wgb, wub, wdb):
    b = pl.program_id(0)
    used = b < nused_ref[0]
    new_expert = (b == 0) | (blk_e_ref[b] != blk_e_ref[jnp.maximum(b - 1, 0)])

    @pl.when(used & new_expert)
    def _():
        _cast_into(wgb, wg_ref)
        _cast_into(wub, wu_ref)
        _cast_into(wdb, wd_ref)

    @pl.when(used)
    def _():
        xb = xs_ref[...].astype(BF16)
        g = _dot(xb, wgb[...])
        u = _dot(xb, wub[...])
        ys_ref[...] = _dot((g * jax.nn.sigmoid(g) * u).astype(BF16), wdb[...])

    @pl.when(jnp.logical_not(used))
    def _():
        ys_ref[...] = jnp.zeros_like(ys_ref)


def _experts(xs, blk_e, nused, w_eg, w_eu, w_ed, layer):
    nb = xs.shape[0] // BLK
    wspec = lambda s: pl.BlockSpec((None, None) + s, lambda b, be, nu: (layer, be[b], 0, 0))
    grid_spec = pltpu.PrefetchScalarGridSpec(
        num_scalar_prefetch=2,
        grid=(nb,),
        in_specs=[pl.BlockSpec((BLK, D_MODEL), lambda b, be, nu: (jnp.minimum(b, nu[0] - 1), 0)),
                  wspec((D_MODEL, D_EXPERT)), wspec((D_MODEL, D_EXPERT)), wspec((D_EXPERT, D_MODEL))],
        out_specs=pl.BlockSpec((BLK, D_MODEL), lambda b, be, nu: (b, 0)),
        scratch_shapes=[pltpu.VMEM((D_MODEL, D_EXPERT), BF16), pltpu.VMEM((D_MODEL, D_EXPERT), BF16),
                        pltpu.VMEM((D_EXPERT, D_MODEL), BF16)],
    )
    return pl.pallas_call(
        _experts_kernel,
        grid_spec=grid_spec,
        out_shape=jax.ShapeDtypeStruct(xs.shape, F32),
        compiler_params=_cparams(("arbitrary",)),
        name="experts",
    )(blk_e, nused, xs, w_eg, w_eu, w_ed)


def _post_kernel(pos_ref, x1_ref, route_ref, pp_ref, ps_ref, wpg_ref, wpp_ref, lng_ref, lnb_ref, ys_hbm, o_ref,
                 ybuf, wpg_b, wpp_b, sem, *, prompt_tiles):
    i = pl.program_id(0)
    tm = x1_ref.shape[0]

    def gather(tile, slot):
        base = 2 * tile * tm
        for r in range(tm):
            for k in range(2):
                pltpu.make_async_copy(ys_hbm.at[pl.ds(pos_ref[base + 2 * r + k], 1)],
                                      ybuf.at[slot, k, pl.ds(r, 1)], sem.at[slot]).start()

    def wait(slot):
        for k in range(2):
            pltpu.make_async_copy(ys_hbm.at[pl.ds(0, tm)], ybuf.at[slot, k], sem.at[slot]).wait()

    @pl.when(i == 0)
    def _():
        _cast_into(wpg_b, wpg_ref)
        _cast_into(wpp_b, wpp_ref)
        gather(0, 0)

    last = pl.num_programs(0) - 1
    slot = i % 2
    wait(slot)
    gather(jnp.minimum(i + 1, last), (i + 1) % 2)

    route = route_ref[...]
    moe = route[:, 2:3] * ybuf[slot, 0] + route[:, 3:4] * ybuf[slot, 1]
    x2 = _layer_norm(DN_ALPHA * x1_ref[...] + moe, lng_ref[1:2, :], lnb_ref[1:2, :])
    p = jnp.where(i < prompt_tiles, pp_ref[...], ps_ref[...])
    ple = jax.nn.sigmoid(_dot(x2.astype(BF16), wpg_b[...])) * _dot(p.astype(BF16), wpp_b[...])
    o_ref[...] = _layer_norm(DN_ALPHA * x2 + ple, lng_ref[2:3, :], lnb_ref[2:3, :])

    @pl.when(i == last)
    def _():
        wait((i + 1) % 2)


def _post(x1, ys, pos, route, p_prompt, p_sample, w_pg, w_pp, ln_g, ln_b, layer):
    m = x1.shape[0]
    pt = p_prompt.shape[1] // TD
    rowblk = lambda w: pl.BlockSpec((TD, w), lambda i, p: (i, 0))
    lspec = lambda s: pl.BlockSpec((None,) + s, lambda i, p: (layer,) + (0,) * len(s), pipeline_mode=pl.Buffered(1))
    grid_spec = pltpu.PrefetchScalarGridSpec(
        num_scalar_prefetch=1,
        grid=(m // TD,),
        in_specs=[rowblk(D_MODEL), rowblk(LANES),
                  pl.BlockSpec((None, TD, D_PLE), lambda i, p: (layer, jnp.minimum(i, pt - 1), 0)),
                  pl.BlockSpec((None, TD, D_PLE), lambda i, p: (layer, jnp.maximum(i - pt, 0), 0)),
                  lspec((D_MODEL, D_MODEL)), lspec((D_PLE, D_MODEL)), lspec((3, D_MODEL)), lspec((3, D_MODEL)),
                  pl.BlockSpec(memory_space=pl.ANY)],
        out_specs=rowblk(D_MODEL),
        scratch_shapes=[pltpu.VMEM((2, 2, TD, D_MODEL), F32), pltpu.VMEM((D_MODEL, D_MODEL), BF16),
                        pltpu.VMEM((D_PLE, D_MODEL), BF16), pltpu.SemaphoreType.DMA((2,))],
    )
    return pl.pallas_call(
        functools.partial(_post_kernel, prompt_tiles=pt),
        grid_spec=grid_spec,
        out_shape=jax.ShapeDtypeStruct((m, D_MODEL), F32),
        compiler_params=_cparams(("arbitrary",)),
        name="post",
    )(pos, x1, route, p_prompt, p_sample, w_pg, w_pp, ln_g, ln_b, ys)


def kernel(x_prompt, x_sample, state_mlstm_C, state_mlstm_n, state_mlstm_m, state_swa_k, state_swa_v, p_prompt, p_sample, w_in, b_in, mh_gain, w_a, w_b, w_out, rel_table, w_sink, ln_g, ln_b, w_rg, b_rg, w_re, b_re, w_eg, w_eu, w_ed, w_pg, w_pp):
    bp_, seq = x_prompt.shape[:2]
    bs_, tdec = x_sample.shape[:2]
    mp = bp_ * seq
    msp = bs_ * T_PAD
    mall = mp + msp
    nblk = 2 * (mp + bs_ * tdec) // BLK + N_EXPERTS

    pad_t = lambda t: jnp.pad(t, [(0, 0)] * (t.ndim - 2) + [(0, T_PAD - tdec), (0, 0)])
    x = lax.dynamic_update_slice(jnp.pad(x_prompt.reshape(mp, D_MODEL), ((0, msp), (0, 0))),
                                 pad_t(x_sample).reshape(msp, D_MODEL), (mp, 0))
    valid = jnp.concatenate([jnp.ones((mp,), bool), jnp.tile(jnp.arange(T_PAD) < tdec, bs_)])
    p_prompt = p_prompt.reshape(DEPTH, mp, D_PLE)
    p_sample = pad_t(p_sample).reshape(DEPTH, msp, D_PLE)

    b_in3 = b_in.reshape(DEPTH, 1, -1)
    w_qkv2, b_qkv2 = w_in[:, :, IN_QKV2[0]:IN_QKV2[1]], b_in3[:, :, IN_QKV2[0]:IN_QKV2[1]]
    w_gates, b_gates = w_in[:, :, IN_GATES[0]:IN_GATES[1]], b_in3[:, :, IN_GATES[0]:IN_GATES[1]]
    b_rg, b_re = b_rg.reshape(DEPTH, 1, N_GROUPS), b_re.reshape(DEPTH, 1, N_EXPERTS)
    gain = mh_gain.reshape(DEPTH, 1, D_MODEL)

    bias_p = _swa_bias(rel_table, WINDOW)
    bias_s = _swa_bias(rel_table, T_PAD)
    sink_p = _sink_cols(w_sink, WINDOW)
    sink_s = _sink_cols(w_sink, T_PAD)
    buf_k = state_swa_k.reshape(DEPTH, bs_, WINDOW, D_KV)
    buf_v = state_swa_v.reshape(DEPTH, bs_, WINDOW, D_KV)
    m0_s = jnp.broadcast_to(state_mlstm_m[..., None], (DEPTH, bs_, M_HEADS, LANES))
    zc = jnp.zeros((1, bp_, M_HEADS, M_DQK, M_DV), F32)
    zn = jnp.zeros((1, bp_, M_HEADS, LANES), F32)
    st_s = (jnp.zeros_like(state_mlstm_C), jnp.zeros_like(state_mlstm_n), jnp.zeros_like(m0_s))
    kv_s = (jnp.zeros_like(buf_k), jnp.zeros_like(buf_v))

    st_p = []
    for i in range(DEPTH):
        za, zg = _inproj(x, [(w_in, IN_A)], [(b_in3, IN_A)], [(ZG, BF16), (LANES, F32)], i, "inproj_a")
        zb, = _inproj(x, [(w_gates, 2048), (w_qkv2, 1536)], [(b_gates, 2048), (b_qkv2, 1536)], [(IN_B, BF16)], i,
                      "inproj_b")

        ya_p, c_p, n_p, m_p = _mlstm(za, zg, gain, i, zc, zn, zn, 0, None, 1, 0, row0=0, nseq=bp_,
                                     steps=seq // LANES, seg=LANES, t_valid=LANES)
        ya_s, *st_s = _mlstm(za, zg, gain, i, state_mlstm_C, state_mlstm_n, m0_s, i, st_s, DEPTH, i, row0=mp,
                             nseq=bs_ // SEQ_BLK, steps=1, seg=T_PAD, t_valid=tdec)
        yb_p = _swa_prompt(zb, bias_p[0], bias_p[1], sink_p, i, batch=bp_, seq=seq)
        yb_s, *kv_s = _swa_sample(zb, buf_k, buf_v, bias_s[0], bias_s[1], sink_s, i, kv_s, row0=mp, batch=bs_,
                                  t_new=tdec)

        x1, route = _merge(x, ya_p, ya_s, yb_p, yb_s, zb, w_a, w_b, w_out, ln_g, ln_b, w_rg, w_re, b_rg, b_re, i)
        pos, zstart, blk_e, nused = _route_tables(route[:, :2].astype(I32), valid, nblk)
        xs = _dispatch(x1, pos, zstart, nblk)
        ys = _experts(xs, blk_e, nused, w_eg, w_eu, w_ed, i)
        x = _post(x1, ys, pos, route, p_prompt, p_sample, w_pg, w_pp, ln_g, ln_b, i)

        kv_p = jnp.stack([lax.slice(zb, ((b + 1) * seq - WINDOW, ZKS), ((b + 1) * seq, IN_B)) for b in range(bp_)])
        kv_p = kv_p.astype(F32)
        heads = lambda t: t.reshape(t.shape[:-1] + (S_KV_HEADS, S_HEAD_DIM))
        st_p.append((c_p[0], n_p[0], m_p[0, :, :, 0], heads(kv_p[..., :D_KV]), heads(kv_p[..., D_KV:])))

    stk = lambda j: jnp.stack([s[j] for s in st_p], axis=0)
    y_p = x[:mp].reshape(bp_, seq, D_MODEL)
    y_s = x[mp:].reshape(bs_, T_PAD, D_MODEL)[:, :tdec]
    return (y_p, y_s, stk(0), stk(1), stk(2), stk(3), stk(4),
            st_s[0], st_s[1], st_s[2][..., 0], heads(kv_s[0]), heads(kv_s[1]))
```

```python
import functools

import numpy as np
import jax
import jax.numpy as jnp
from jax import lax
from jax.experimental import pallas as pl
from jax.experimental.pallas import tpu as pltpu

F32 = jnp.float32
BF16 = jnp.bfloat16
I32 = jnp.int32
HIGHEST = lax.Precision.HIGHEST

D_MODEL = 1024
DEPTH = 4
D_PLE = 256
M_HEADS = 4
M_DQK = 128
M_DV = 256
S_HEADS = 16
S_KV_HEADS = 4
S_HEAD_DIM = 64
S_GROUP = 4
D_KV = S_KV_HEADS * S_HEAD_DIM
WINDOW = 128
REL_BUCKETS = 32
N_GROUPS = 4
EXPERTS_PER_GROUP = 8
N_EXPERTS = 32
D_EXPERT = 512
DN_ALPHA = (2 * DEPTH) ** 0.25
LN_EPS = 1e-5
NEG = -1e30

IN_A = 3200
IN_QKV2 = (3080, 4616)
IN_GATES = (4616, 6664)
ZQ, ZK, ZV, ZOG, ZG = 0, 512, 1024, 2048, 3072
IN_B = 3584
ZGA, ZGB, ZQS, ZKS, ZVS = 0, 1024, 2048, 3072, 3328

LANES = 128
T_PAD = 8
SEQ_BLK = LANES // T_PAD
TM = 512
TD = 256
BLK = 256
VMEM_LIMIT = 56 * 1024 * 1024


def _cparams(sem):
    return pltpu.CompilerParams(dimension_semantics=sem, vmem_limit_bytes=VMEM_LIMIT)


def _layer_spec(shape, layer):
    nd = len(shape)
    return pl.BlockSpec((None,) + shape, lambda *_: (layer,) + (0,) * nd, pipeline_mode=pl.Buffered(1))


def _layer_norm(y, g, b):
    mu = jnp.mean(y, axis=-1, keepdims=True)
    d = y - mu
    var = jnp.mean(d * d, axis=-1, keepdims=True)
    return d * lax.rsqrt(var + LN_EPS) * g + b


def _dot(a, b):
    return jnp.dot(a, b, preferred_element_type=F32)


def _dot_nt(a, b):
    return lax.dot_general(a, b, (((1,), (1,)), ((), ())), preferred_element_type=F32)


def _dot_tn(a, b):
    return lax.dot_general(a, b, (((0,), (0,)), ((), ())), preferred_element_type=F32)


def _dot_exact(a, b):
    return jnp.dot(a, b, precision=HIGHEST, preferred_element_type=F32)


def _cast_into(dst_ref, src_ref, col0=0, step=512):
    n = src_ref.shape[-1]
    for j in range(0, n, step):
        w = min(step, n - j)
        dst_ref[:, col0 + j:col0 + j + w] = src_ref[:, j:j + w].astype(BF16)


def _inproj_kernel(*refs, widths, out_widths):
    ng, no = len(widths), len(out_widths)
    x_ref, w_refs, b_refs = refs[0], refs[1:1 + ng], refs[1 + ng:1 + 2 * ng]
    o_refs, wbf = refs[1 + 2 * ng:1 + 2 * ng + no], refs[1 + 2 * ng + no]

    @pl.when(pl.program_id(0) == 0)
    def _():
        c = 0
        for w_ref, n in zip(w_refs, widths):
            _cast_into(wbf, w_ref, c)
            c += n

    out_start = np.cumsum((0,) + tuple(out_widths))

    def store(col, val):
        k = int(np.searchsorted(out_start, col, side='right')) - 1
        lo = col - int(out_start[k])
        o_refs[k][:, lo:lo + val.shape[1]] = val.astype(o_refs[k].dtype)

    xb = x_ref[...].astype(BF16)
    c = 0
    for b_ref, n in zip(b_refs, widths):
        for j in range(0, n, 512):
            w = min(512, n - j)
            store(c + j, _dot(xb, wbf[:, c + j:c + j + w]) + b_ref[:, j:j + w])
        c += n


def _inproj(x, ws, bs, outs, layer, name):
    m = x.shape[0]
    widths = tuple(n for _, n in ws)
    out_widths = tuple(n for n, _ in outs)
    assert sum(widths) == sum(out_widths)
    return pl.pallas_call(
        functools.partial(_inproj_kernel, widths=widths, out_widths=out_widths),
        grid=(m // TM,),
        in_specs=([pl.BlockSpec((TM, D_MODEL), lambda i: (i, 0))]
                  + [_layer_spec((D_MODEL, n), layer) for _, n in ws]
                  + [_layer_spec((1, n), layer) for _, n in bs]),
        out_specs=[pl.BlockSpec((TM, n), lambda i: (i, 0)) for n, _ in outs],
        out_shape=[jax.ShapeDtypeStruct((m, n), dt) for n, dt in outs],
        scratch_shapes=[pltpu.VMEM((D_MODEL, sum(widths)), BF16)],
        compiler_params=_cparams(("arbitrary",)),
        name=name,
    )(x, *[w for w, _ in ws], *[b for b, _ in bs])


def _log_sigmoid(x):
    return jnp.minimum(x, 0.0) - jnp.log(1.0 + jnp.exp(-jnp.abs(x)))


def _mlstm_kernel(zq, zk, zv, zog, zg, gain, c0, n0, m0, *rest, seg, t_valid, aliased):
    ya, co, no, mo, qc_s = rest[aliased:]
    L = zq.shape[0]
    nseg = L // seg

    @pl.when(pl.program_id(1) == 0)
    def _():
        co[...] = c0[...]
        no[...] = n0[...]
        mo[...] = m0[...]

    g_c = zg[...]
    g_r = g_c.T
    ig_c, lf_c = g_c, _log_sigmoid(g_c)
    ig_r, lf_r = g_r, _log_sigmoid(g_r)
    ri = lax.broadcasted_iota(I32, (L, L), 0)
    ci = lax.broadcasted_iota(I32, (L, L), 1)
    shift = seg.bit_length() - 1
    if t_valid < seg:
        rv = (lax.broadcasted_iota(I32, (L, 1), 0) & (seg - 1)) < t_valid
        cv = (lax.broadcasted_iota(I32, (1, L), 1) & (seg - 1)) < t_valid
        ig_c, lf_c = jnp.where(rv, ig_c, NEG), jnp.where(rv, lf_c, 0.0)
        ig_r, lf_r = jnp.where(cv, ig_r, NEG), jnp.where(cv, lf_r, 0.0)
    if nseg > 1:
        same = (ri >> shift) == (ci >> shift)
        causal, causal_t = same & (ri >= ci), same & (ri <= ci)
        same_f = same.astype(F32)
        be_c = _dot_exact(same_f, lf_c)
        be_r = _dot_exact(lf_r, same_f)
        expand = ((lax.broadcasted_iota(I32, (L, nseg), 0) >> shift)
                  == lax.broadcasted_iota(I32, (L, nseg), 1)).astype(F32)
    else:
        causal, causal_t = ri >= ci, ri <= ci
    b_c = _dot_exact(causal.astype(F32), lf_c)
    b_r = _dot_exact(lf_r, causal_t.astype(F32))

    for h in range(M_HEADS):
        icol, irow = ig_c[:, h:h + 1], ig_r[h:h + 1, :]
        bcol, brow = b_c[:, 4 + h:5 + h], b_r[4 + h:5 + h, :]
        q = zq[:, h * M_DQK:(h + 1) * M_DQK].astype(F32) * (M_DQK ** -0.5)
        k = zk[:, h * M_DQK:(h + 1) * M_DQK].astype(F32)
        qb, kb = q.astype(BF16), zk[:, h * M_DQK:(h + 1) * M_DQK]
        vb = zv[:, h * M_DV:(h + 1) * M_DV]
        if nseg > 1:
            mcol = _dot_exact(expand, mo[:, h, :])[:, 0:1]
            nrow = _dot_exact(expand, no[:, h, :])
            becol, berow = be_c[:, 4 + h:5 + h], be_r[4 + h:5 + h, :]
            for j in range(nseg):
                rows = slice(j * seg, (j + 1) * seg)
                qc_s[rows, :] = _dot(q[rows].astype(BF16), co[j, h].astype(BF16))
            q_c = qc_s[...]
        else:
            mcol, nrow = mo[0, h:h + 1, 0:1], no[0, h:h + 1, :]
            becol = bcol[L - 1:L, :]
            q_c = _dot(qb, co[0, h].astype(BF16))

        dmat = jnp.where(causal, bcol - brow + irow, NEG)
        inter = bcol + mcol
        mhat = jnp.maximum(inter, jnp.max(dmat, axis=-1, keepdims=True))
        w_intra = jnp.exp(dmat - mhat)
        w_inter = jnp.exp(inter - mhat)
        s = _dot_nt(qb, kb) * w_intra
        num = _dot(s.astype(BF16), vb) + w_inter * q_c
        den = jnp.sum(s, axis=-1, keepdims=True) + w_inter * jnp.sum(q * nrow, axis=-1, keepdims=True)
        hh = num / jnp.maximum(jnp.abs(den), jnp.exp(-mhat))

        gcol = icol + becol - bcol
        if nseg > 1:
            gmax = jnp.max(jnp.where(same, irow + berow - brow, NEG), axis=-1, keepdims=True)
        else:
            gmax = jnp.max(gcol, axis=0, keepdims=True)
        m_new = jnp.maximum(becol + mcol, gmax)
        decay = jnp.exp(becol + mcol - m_new)
        kw = k * jnp.exp(gcol - m_new)
        vf = vb.astype(F32) if nseg > 1 else vb
        for j in range(nseg):
            rows = slice(j * seg, (j + 1) * seg)
            dj = decay[j * seg:j * seg + 1, :] if nseg > 1 else decay
            mj = m_new[j * seg:j * seg + 1, :] if nseg > 1 else m_new
            co[j, h] = dj * co[j, h] + _dot_tn(kw[rows].astype(BF16), vf[rows].astype(BF16))
            no[j, h:h + 1, :] = dj * no[j, h:h + 1, :] + jnp.sum(kw[rows], axis=0, keepdims=True)
            mo[j, h:h + 1, :] = jnp.broadcast_to(mj, (1, LANES))

        mu = jnp.mean(hh, axis=-1, keepdims=True)
        d = hh - mu
        var = jnp.mean(d * d, axis=-1, keepdims=True)
        hn = d * lax.rsqrt(var + LN_EPS) * gain[:, h * M_DV:(h + 1) * M_DV]
        og = zog[:, h * M_DV:(h + 1) * M_DV].astype(F32)
        ya[:, h * M_DV:(h + 1) * M_DV] = (hn * jax.nn.sigmoid(og)).astype(ya.dtype)


def _mlstm(za, zg, gain, layer, c0, n0, m0, st_layer, outs_prev, out_layers, out_layer, *, row0, nseq, steps,
           seg, t_valid):
    nseg = LANES // seg
    r0 = row0 // LANES
    row = lambda b, c: r0 + b * steps + c
    nstate = nseq * nseg
    aliased = 0 if outs_prev is None else 3
    kern = functools.partial(_mlstm_kernel, seg=seg, t_valid=min(t_valid, seg), aliased=aliased)
    in4 = pl.BlockSpec((None, nseg, M_HEADS, M_DQK, M_DV), lambda b, c: (st_layer, b, 0, 0, 0))
    in3 = pl.BlockSpec((None, nseg, M_HEADS, LANES), lambda b, c: (st_layer, b, 0, 0))
    out4 = pl.BlockSpec((None, nseg, M_HEADS, M_DQK, M_DV), lambda b, c: (out_layer, b, 0, 0, 0))
    out3 = pl.BlockSpec((None, nseg, M_HEADS, LANES), lambda b, c: (out_layer, b, 0, 0))
    anyspec = pl.BlockSpec(memory_space=pl.ANY)
    n_in = 9
    return pl.pallas_call(
        kern,
        grid=(nseq, steps),
        in_specs=[pl.BlockSpec((LANES, 512), lambda b, c: (row(b, c), ZQ // 512)),
                  pl.BlockSpec((LANES, 512), lambda b, c: (row(b, c), ZK // 512)),
                  pl.BlockSpec((LANES, 1024), lambda b, c: (row(b, c), ZV // 1024)),
                  pl.BlockSpec((LANES, 1024), lambda b, c: (row(b, c), ZOG // 1024)),
                  pl.BlockSpec((LANES, LANES), lambda b, c: (row(b, c), 0)),
                  pl.BlockSpec((None, 1, D_MODEL), lambda b, c: (layer, 0, 0)),
                  in4, in3, in3] + [anyspec] * aliased,
        out_specs=[pl.BlockSpec((LANES, D_MODEL), lambda b, c: (b * steps + c, 0)), out4, out3, out3],
        out_shape=[jax.ShapeDtypeStruct((nseq * steps * LANES, D_MODEL), BF16),
                   jax.ShapeDtypeStruct((out_layers, nstate, M_HEADS, M_DQK, M_DV), F32),
                   jax.ShapeDtypeStruct((out_layers, nstate, M_HEADS, LANES), F32),
                   jax.ShapeDtypeStruct((out_layers, nstate, M_HEADS, LANES), F32)],
        scratch_shapes=[pltpu.VMEM((LANES, M_DV), F32)],
        input_output_aliases={n_in + j: 1 + j for j in range(aliased)},
        compiler_params=_cparams(("parallel", "arbitrary")),
        name="mlstm",
    )(za, za, za, za, zg, gain, c0, n0, m0, *(outs_prev or ()))


def _swa_kernel(q_ref, kc_ref, vc_ref, kp_ref, vp_ref, bp_ref, bc_ref, sink_ref, *rest, nseq, t_new,
                first_block_has_no_past, aliased):
    o_ref = rest[aliased]
    tq = q_ref.shape[0] // nseq
    have_past = pl.program_id(1) > 0
    q_all, kc_all, vc_all = q_ref[...], kc_ref[...], vc_ref[...]
    if nseq > 1:
        q_all, kc_all, vc_all = q_all.astype(F32), kc_all.astype(F32), vc_all.astype(F32)
    for b in range(nseq):
        rows = slice(b * tq, (b + 1) * tq)
        q, kc_b, vc_b = q_all[rows], kc_all[rows], vc_all[rows]
        kp_b = kp_ref[b] if nseq > 1 else kp_ref[...]
        vp_b = vp_ref[b] if nseq > 1 else vp_ref[...]
        outs = []
        for kvh in range(S_KV_HEADS):
            sl = slice(kvh * S_HEAD_DIM, (kvh + 1) * S_HEAD_DIM)
            qg = jnp.concatenate(
                [q[:, (kvh * S_GROUP + g) * S_HEAD_DIM:(kvh * S_GROUP + g + 1) * S_HEAD_DIM] for g in range(S_GROUP)],
                axis=0).astype(BF16)
            kp, vp = kp_b[:, sl].astype(BF16), vp_b[:, sl].astype(BF16)
            kc, vc = kc_b[:, sl].astype(BF16), vc_b[:, sl].astype(BF16)
            sp = _dot_nt(qg, kp) * (S_HEAD_DIM ** -0.5) + bp_ref[kvh]
            sc = _dot_nt(qg, kc) * (S_HEAD_DIM ** -0.5) + bc_ref[kvh]
            if first_block_has_no_past:
                sp = jnp.where(have_past, sp, NEG)
            sk = sink_ref[kvh]
            mx = jnp.maximum(jnp.maximum(jnp.max(sp, axis=-1, keepdims=True), jnp.max(sc, axis=-1, keepdims=True)), sk)
            pp, pc = jnp.exp(sp - mx), jnp.exp(sc - mx)
            den = jnp.sum(pp, axis=-1, keepdims=True) + jnp.sum(pc, axis=-1, keepdims=True) + jnp.exp(sk - mx)
            o = (_dot(pp.astype(BF16), vp) + _dot(pc.astype(BF16), vc)) / den
            outs.extend(o[g * tq:(g + 1) * tq] for g in range(S_GROUP))
        o_ref[rows, :] = jnp.concatenate(outs, axis=1).astype(o_ref.dtype)
        if aliased:
            for new, prev, out in ((kc_b, kp_b, rest[aliased + 1]), (vc_b, vp_b, rest[aliased + 2])):
                out[b, 0:WINDOW - t_new, :] = prev[t_new:, :]
                out[b, WINDOW - t_new:WINDOW, :] = new[0:t_new, :]


def _rel_bucket(dist):
    nn = np.maximum(dist, 0)
    max_exact = REL_BUCKETS // 2
    large = max_exact + (np.log(np.maximum(nn, 1) / max_exact) / np.log(WINDOW / max_exact)
                         * (REL_BUCKETS - max_exact)).astype(np.int32)
    large = np.minimum(large, REL_BUCKETS - 1)
    return np.where(nn < max_exact, nn, large).astype(np.int32)


def _swa_bias(rel_table, tq):
    t = np.arange(tq)[:, None]
    d_prev = t + WINDOW - np.arange(WINDOW)[None, :]
    d_cur = t - np.arange(tq)[None, :]

    def build(dist, ok):
        onehot = np.eye(REL_BUCKETS, dtype=np.float32)[_rel_bucket(dist).reshape(-1)]
        bias = jnp.dot(onehot, rel_table.astype(F32), precision=HIGHEST).reshape(dist.shape + (S_HEADS,))
        bias = jnp.transpose(bias, (2, 0, 1))
        bias = jnp.where(ok[None], bias, NEG)
        return bias.reshape(S_KV_HEADS, S_GROUP * tq, dist.shape[1])

    return build(d_prev, d_prev <= WINDOW), build(d_cur, d_cur >= 0)


def _sink_cols(w_sink, tq):
    s = w_sink.astype(F32).reshape(DEPTH, S_KV_HEADS, S_GROUP, 1, 1)
    return jnp.broadcast_to(s, (DEPTH, S_KV_HEADS, S_GROUP, tq, 1)).reshape(DEPTH, S_KV_HEADS, S_GROUP * tq, 1)


def _swa_bias_specs(tq, layer):
    return [pl.BlockSpec((S_KV_HEADS, S_GROUP * tq, WINDOW), lambda b, n: (0, 0, 0)),
            pl.BlockSpec((S_KV_HEADS, S_GROUP * tq, tq), lambda b, n: (0, 0, 0)),
            pl.BlockSpec((None, S_KV_HEADS, S_GROUP * tq, 1), lambda b, n: (layer, 0, 0, 0))]


def _swa_prompt(zb, bp, bc, sk, layer, *, batch, seq):
    nb = seq // WINDOW
    row = lambda b, n: b * nb + n
    prev = lambda b, n: jnp.maximum(b * nb + n - 1, 0)
    in_specs = [pl.BlockSpec((WINDOW, 1024), lambda b, n: (row(b, n), ZQS // 1024)),
                pl.BlockSpec((WINDOW, D_KV), lambda b, n: (row(b, n), ZKS // D_KV)),
                pl.BlockSpec((WINDOW, D_KV), lambda b, n: (row(b, n), ZVS // D_KV)),
                pl.BlockSpec((WINDOW, D_KV), lambda b, n: (prev(b, n), ZKS // D_KV)),
                pl.BlockSpec((WINDOW, D_KV), lambda b, n: (prev(b, n), ZVS // D_KV))]
    kern = functools.partial(_swa_kernel, nseq=1, t_new=0, first_block_has_no_past=True, aliased=0)
    return pl.pallas_call(
        kern,
        grid=(batch, nb),
        in_specs=in_specs + _swa_bias_specs(WINDOW, layer),
        out_specs=pl.BlockSpec((WINDOW, D_MODEL), lambda b, n: (row(b, n), 0)),
        out_shape=jax.ShapeDtypeStruct((batch * seq, D_MODEL), BF16),
        compiler_params=_cparams(("parallel", "arbitrary")),
        name="swa_prompt",
    )(zb, zb, zb, zb, zb, bp, bc, sk)


def _swa_sample(zb, buf_k, buf_v, bp, bc, sk, layer, kv_prev, *, row0, batch, t_new):
    r0 = row0 // LANES
    steps = batch // SEQ_BLK
    state = pl.BlockSpec((None, SEQ_BLK, WINDOW, D_KV), lambda b, n: (layer, b, 0, 0))
    anyspec = pl.BlockSpec(memory_space=pl.ANY)
    in_specs = [pl.BlockSpec((LANES, 1024), lambda b, n: (r0 + b, ZQS // 1024)),
                pl.BlockSpec((LANES, D_KV), lambda b, n: (r0 + b, ZKS // D_KV)),
                pl.BlockSpec((LANES, D_KV), lambda b, n: (r0 + b, ZVS // D_KV)),
                state, state]
    kern = functools.partial(_swa_kernel, nseq=SEQ_BLK, t_new=t_new, first_block_has_no_past=False, aliased=2)
    return pl.pallas_call(
        kern,
        grid=(steps, 1),
        in_specs=in_specs + _swa_bias_specs(T_PAD, layer) + [anyspec, anyspec],
        out_specs=[pl.BlockSpec((LANES, D_MODEL), lambda b, n: (b, 0)), state, state],
        out_shape=[jax.ShapeDtypeStruct((batch * T_PAD, D_MODEL), BF16),
                   jax.ShapeDtypeStruct(buf_k.shape, F32), jax.ShapeDtypeStruct(buf_v.shape, F32)],
        input_output_aliases={8: 1, 9: 2},
        compiler_params=_cparams(("parallel", "arbitrary")),
        name="swa_sample",
    )(zb, zb, zb, buf_k, buf_v, bp, bc, sk, *kv_prev)


def _merge_kernel(x_ref, yap_ref, yas_ref, ybp_ref, ybs_ref, ga_ref, gb_ref, wa_ref, wb_ref, wo_ref, lng_ref, lnb_ref,
                  wrg_ref, wre_ref, brg_ref, bre_ref, x1_ref, route_ref, wbf, wr_b, br_s, *, prompt_tiles):
    i = pl.program_id(0)

    @pl.when(i == 0)
    def _():
        for j, w_ref in enumerate((wa_ref, wb_ref, wo_ref)):
            _cast_into(wbf.at[j], w_ref)
        wr_b[...] = jnp.zeros_like(wr_b)
        wr_b[:, 0:N_GROUPS] = wrg_ref[...].astype(BF16)
        wr_b[:, N_GROUPS:N_GROUPS + N_EXPERTS] = wre_ref[...].astype(BF16)
        br_s[...] = jnp.zeros_like(br_s)
        br_s[:, 0:N_GROUPS] = brg_ref[...]
        br_s[:, N_GROUPS:N_GROUPS + N_EXPERTS] = bre_ref[...]

    is_prompt = i < prompt_tiles
    ya = jnp.where(is_prompt, yap_ref[...], yas_ref[...])
    yb = jnp.where(is_prompt, ybp_ref[...], ybs_ref[...])
    a = _dot(ya, wbf[0])
    b = _dot(yb, wbf[1])
    u = jax.nn.sigmoid(ga_ref[...].astype(F32)) * a + jax.nn.sigmoid(gb_ref[...].astype(F32)) * b
    mix = _dot(u.astype(BF16), wbf[2])
    x1 = _layer_norm(DN_ALPHA * x_ref[...] + mix, lng_ref[0:1, :], lnb_ref[0:1, :])
    x1_ref[...] = x1

    logits = _dot(x1.astype(BF16), wr_b[...]) + br_s[...]
    lane = lax.broadcasted_iota(I32, logits.shape, 1)
    lanef = lane.astype(F32)
    gmask = lane < N_GROUPS
    gl = jnp.where(gmask, logits, NEG)
    gmax = jnp.max(gl, axis=-1, keepdims=True)
    gidx = jnp.min(jnp.where(gmask & (gl == gmax), lanef, 999.0), axis=-1, keepdims=True)
    g_w = 1.0 / jnp.sum(jnp.where(gmask, jnp.exp(gl - gmax), 0.0), axis=-1, keepdims=True)
    lo = N_GROUPS + EXPERTS_PER_GROUP * gidx
    emask = (lanef >= lo) & (lanef < lo + EXPERTS_PER_GROUP)
    el = jnp.where(emask, logits, NEG)
    emax = jnp.max(el, axis=-1, keepdims=True)
    ep = jnp.where(emask, jnp.exp(el - emax), 0.0)
    prob = ep / jnp.sum(ep, axis=-1, keepdims=True)
    prob = jnp.where(emask, prob, -1.0)
    p1 = jnp.max(prob, axis=-1, keepdims=True)
    i1 = jnp.min(jnp.where(prob == p1, lanef, 999.0), axis=-1, keepdims=True)
    prob2 = jnp.where(lanef == i1, -1.0, prob)
    p2 = jnp.max(prob2, axis=-1, keepdims=True)
    i2 = jnp.min(jnp.where(prob2 == p2, lanef, 999.0), axis=-1, keepdims=True)
    tot = p1 + p2
    route_ref[...] = jnp.where(lane == 0, i1 - N_GROUPS,
                               jnp.where(lane == 1, i2 - N_GROUPS,
                                         jnp.where(lane == 2, g_w * p1 / tot,
                                                   jnp.where(lane == 3, g_w * p2 / tot, 0.0))))


def _merge(x, ya_p, ya_s, yb_p, yb_s, zb, w_a, w_b, w_out, ln_g, ln_b, w_rg, w_re, b_rg, b_re, layer):
    m = x.shape[0]
    pt = ya_p.shape[0] // TM
    rowblk = lambda w, j: pl.BlockSpec((TM, w), lambda i: (i, j))
    pblk = pl.BlockSpec((TM, D_MODEL), lambda i: (jnp.minimum(i, pt - 1), 0))
    sblk = pl.BlockSpec((TM, D_MODEL), lambda i: (jnp.maximum(i - pt, 0), 0))
    return pl.pallas_call(
        functools.partial(_merge_kernel, prompt_tiles=pt),
        grid=(m // TM,),
        in_specs=[rowblk(D_MODEL, 0), pblk, sblk, pblk, sblk,
                  rowblk(D_MODEL, ZGA // D_MODEL), rowblk(D_MODEL, ZGB // D_MODEL),
                  _layer_spec((D_MODEL, D_MODEL), layer), _layer_spec((D_MODEL, D_MODEL), layer),
                  _layer_spec((D_MODEL, D_MODEL), layer),
                  _layer_spec((3, D_MODEL), layer), _layer_spec((3, D_MODEL), layer),
                  _layer_spec((D_MODEL, N_GROUPS), layer), _layer_spec((D_MODEL, N_EXPERTS), layer),
                  _layer_spec((1, N_GROUPS), layer), _layer_spec((1, N_EXPERTS), layer)],
        out_specs=[rowblk(D_MODEL, 0), rowblk(LANES, 0)],
        out_shape=[jax.ShapeDtypeStruct((m, D_MODEL), F32), jax.ShapeDtypeStruct((m, LANES), F32)],
        scratch_shapes=[pltpu.VMEM((3, D_MODEL, D_MODEL), BF16), pltpu.VMEM((D_MODEL, LANES), BF16),
                        pltpu.VMEM((1, LANES), F32)],
        compiler_params=_cparams(("arbitrary",)),
        name="merge",
    )(x, ya_p, ya_s, yb_p, yb_s, zb, zb, w_a, w_b, w_out, ln_g, ln_b, w_rg, w_re, b_rg, b_re)


SEG_ALIGN = 8
SEG_BITS = (TD // SEG_ALIGN).bit_length()
LOC_ROWS = 2 * TD + N_EXPERTS * SEG_ALIGN
LOC_BITS = (LOC_ROWS // SEG_ALIGN).bit_length()


def _route_tables(eid, valid, nblk):
    m = eid.shape[0]
    nt = m // TD
    per = BLK // SEG_ALIGN
    onehot = ((eid.reshape(nt, 2 * TD, 1) == jnp.arange(N_EXPERTS, dtype=I32))
              & jnp.repeat(valid, 2).reshape(nt, 2 * TD, 1)).astype(I32)
    cnt = (jnp.sum(onehot, axis=1) + SEG_ALIGN - 1) // SEG_ALIGN
    lofs = jnp.cumsum(cnt, axis=1) - cnt
    ltot = jnp.sum(cnt, axis=1)
    rank = jnp.sum(onehot * jnp.cumsum(onehot, axis=1), axis=2) - 1
    lslot = jnp.sum(onehot * lofs[:, None, :], axis=2) * SEG_ALIGN + rank
    lslot = jnp.where(jnp.sum(onehot, axis=2) > 0, lslot, -1).reshape(m, 2)
    gcnt = jnp.sum(cnt, axis=0)
    gpad = (gcnt + per - 1) // per * per
    gend = jnp.cumsum(gpad)
    gofs = (gend - gpad)[None, :] + jnp.cumsum(cnt, axis=0) - cnt
    nused = gend[-1] // per
    unused = nused + jnp.arange(N_EXPERTS, dtype=I32)
    zstart = jnp.concatenate([jnp.where(gpad > 0, (gend - per) * SEG_ALIGN, -1),
                              jnp.where(unused < nblk, unused * BLK, -1)]).astype(I32)
    bstart = jnp.minimum(jnp.arange(nblk, dtype=I32), nused - 1) * per
    blk_e = jnp.minimum(jnp.sum((gend[None, :] <= bstart[:, None]).astype(I32), axis=1), N_EXPERTS - 1)
    flat = lambda t: t.reshape(-1).astype(I32)
    return (flat(cnt), flat(lofs), flat(gofs), flat(ltot), lslot.astype(I32), zstart, flat(blk_e),
            nused.astype(I32).reshape(1))


def _segment_copies(i, cnt_ref, lofs_ref, gofs_ref, make_copy):
    for e in range(N_EXPERTS):
        c = cnt_ref[i * N_EXPERTS + e]
        lo = lofs_ref[i * N_EXPERTS + e]
        go = gofs_ref[i * N_EXPERTS + e]
        for k in range(SEG_BITS):
            @pl.when(((c >> k) & 1) == 1)
            def _():
                done = c & ((1 << k) - 1)
                make_copy(pl.multiple_of((lo + done) * SEG_ALIGN, SEG_ALIGN),
                          pl.multiple_of((go + done) * SEG_ALIGN, SEG_ALIGN), SEG_ALIGN << k).start()


def _wait_rows(tot, make_copy):
    for k in range(LOC_BITS):
        @pl.when(((tot >> k) & 1) == 1)
        def _():
            make_copy(0, 0, SEG_ALIGN << k).wait()


def _dispatch_kernel(cnt_ref, lofs_ref, gofs_ref, ltot_ref, zstart_ref, x_ref, ls_ref, xs_hbm, xloc, zbuf, sem):
    i = pl.program_id(0)
    last = pl.num_programs(0) - 1
    slot = i % 2

    def copy_from(s):
        return lambda lo, go, n: pltpu.make_async_copy(xloc.at[s, pl.ds(lo, n)], xs_hbm.at[pl.ds(go, n)], sem.at[s])

    @pl.when(i == 0)
    def _():
        zbuf[...] = jnp.zeros_like(zbuf)
        for phase in ("start", "wait"):
            for e in range(zstart_ref.shape[0]):
                @pl.when(zstart_ref[e] >= 0)
                def _():
                    cp = pltpu.make_async_copy(zbuf, xs_hbm.at[pl.ds(pl.multiple_of(zstart_ref[e], BLK), BLK)],
                                               sem.at[2])
                    getattr(cp, phase)()

    @pl.when(i >= 2)
    def _():
        _wait_rows(ltot_ref[jnp.maximum(i - 2, 0)], copy_from(slot))

    ls = ls_ref[0]
    j = lax.broadcasted_iota(I32, (LOC_ROWS, TD), 0)
    onehot = jnp.where((j == ls[0:1, :]) | (j == ls[1:2, :]), 1.0, 0.0).astype(BF16)
    xloc[slot] = _dot(onehot, x_ref[...].astype(BF16))
    _segment_copies(i, cnt_ref, lofs_ref, gofs_ref, copy_from(slot))

    @pl.when(i == last)
    def _():
        _wait_rows(ltot_ref[i], copy_from(slot))

        @pl.when(i >= 1)
        def _():
            _wait_rows(ltot_ref[jnp.maximum(i - 1, 0)], copy_from(1 - slot))


def _dispatch(x1, cnt, lofs, gofs, ltot, zstart, lslot, nblk):
    m = x1.shape[0]
    nt = m // TD
    ls_rows = jnp.pad(lslot.reshape(nt, TD, 2).transpose(0, 2, 1), ((0, 0), (0, 6), (0, 0)), constant_values=-1)
    grid_spec = pltpu.PrefetchScalarGridSpec(
        num_scalar_prefetch=5,
        grid=(nt,),
        in_specs=[pl.BlockSpec((TD, D_MODEL), lambda i, *_: (i, 0)),
                  pl.BlockSpec((1, 8, TD), lambda i, *_: (i, 0, 0))],
        out_specs=pl.BlockSpec(memory_space=pl.ANY),
        scratch_shapes=[pltpu.VMEM((2, LOC_ROWS, D_MODEL), F32), pltpu.VMEM((BLK, D_MODEL), F32),
                        pltpu.SemaphoreType.DMA((3,))],
    )
    return pl.pallas_call(
        _dispatch_kernel,
        grid_spec=grid_spec,
        out_shape=jax.ShapeDtypeStruct((nblk * BLK, D_MODEL), F32),
        compiler_params=_cparams(("arbitrary",)),
        name="dispatch",
    )(cnt, lofs, gofs, ltot, zstart, x1, ls_rows)


def _experts_kernel(blk_e_ref, nused_ref, xs_ref, wg_ref, wu_ref, wd_ref, ys_ref, wgb, wub, wdb):
    b = pl.program_id(0)
    used = b < nused_ref[0]
    new_expert = (b == 0) | (blk_e_ref[b] != blk_e_ref[jnp.maximum(b - 1, 0)])

    @pl.when(used & new_expert)
    def _():
        _cast_into(wgb, wg_ref)
        _cast_into(wub, wu_ref)
        _cast_into(wdb, wd_ref)

    @pl.when(used)
    def _():
        xb = xs_ref[...].astype(BF16)
        g = _dot(xb, wgb[...])
        u = _dot(xb, wub[...])
        ys_ref[...] = _dot((g * jax.nn.sigmoid(g) * u).astype(BF16), wdb[...])

    @pl.when(jnp.logical_not(used))
    def _():
        ys_ref[...] = jnp.zeros_like(ys_ref)


def _experts(xs, blk_e, nused, w_eg, w_eu, w_ed, layer):
    nb = xs.shape[0] // BLK
    wspec = lambda s: pl.BlockSpec((None, None) + s, lambda b, be, nu: (layer, be[b], 0, 0))
    grid_spec = pltpu.PrefetchScalarGridSpec(
        num_scalar_prefetch=2,
        grid=(nb,),
        in_specs=[pl.BlockSpec((BLK, D_MODEL), lambda b, be, nu: (jnp.minimum(b, nu[0] - 1), 0)),
                  wspec((D_MODEL, D_EXPERT)), wspec((D_MODEL, D_EXPERT)), wspec((D_EXPERT, D_MODEL))],
        out_specs=pl.BlockSpec((BLK, D_MODEL), lambda b, be, nu: (b, 0)),
        scratch_shapes=[pltpu.VMEM((D_MODEL, D_EXPERT), BF16), pltpu.VMEM((D_MODEL, D_EXPERT), BF16),
                        pltpu.VMEM((D_EXPERT, D_MODEL), BF16)],
    )
    return pl.pallas_call(
        _experts_kernel,
        grid_spec=grid_spec,
        out_shape=jax.ShapeDtypeStruct(xs.shape, F32),
        compiler_params=_cparams(("arbitrary",)),
        name="experts",
    )(blk_e, nused, xs, w_eg, w_eu, w_ed)


def _post_kernel(cnt_ref, lofs_ref, gofs_ref, ltot_ref, x1_ref, route_ref, ls_ref, pp_ref, ps_ref, wpg_ref, wpp_ref,
                 lng_ref, lnb_ref, ys_hbm, o_ref, yloc, wpg_b, wpp_b, sem, *, prompt_tiles):
    i = pl.program_id(0)
    last = pl.num_programs(0) - 1
    slot = i % 2

    def copy_to(s):
        return lambda lo, go, n: pltpu.make_async_copy(ys_hbm.at[pl.ds(go, n)], yloc.at[s, pl.ds(lo, n)], sem.at[s])

    @pl.when(i == 0)
    def _():
        _cast_into(wpg_b, wpg_ref)
        _cast_into(wpp_b, wpp_ref)
        yloc[...] = jnp.zeros_like(yloc)
        _segment_copies(0, cnt_ref, lofs_ref, gofs_ref, copy_to(0))

    _wait_rows(ltot_ref[i], copy_to(slot))

    @pl.when(i < last)
    def _():
        _segment_copies(i + 1, cnt_ref, lofs_ref, gofs_ref, copy_to(1 - slot))

    route = route_ref[...]
    ls = ls_ref[...]
    j = lax.broadcasted_iota(I32, (ls.shape[0], LOC_ROWS), 1)
    w = jnp.where(j == ls[:, 0:1], route[:, 2:3], 0.0) + jnp.where(j == ls[:, 1:2], route[:, 3:4], 0.0)
    moe = _dot(w.astype(BF16), yloc[slot].astype(BF16))
    x2 = _layer_norm(DN_ALPHA * x1_ref[...] + moe, lng_ref[1:2, :], lnb_ref[1:2, :])
    p = jnp.where(i < prompt_tiles, pp_ref[...], ps_ref[...])
    ple = jax.nn.sigmoid(_dot(x2.astype(BF16), wpg_b[...])) * _dot(p.astype(BF16), wpp_b[...])
    o_ref[...] = _layer_norm(DN_ALPHA * x2 + ple, lng_ref[2:3, :], lnb_ref[2:3, :])


def _post(x1, ys, cnt, lofs, gofs, ltot, lslot, route, p_prompt, p_sample, w_pg, w_pp, ln_g, ln_b, layer):
    m = x1.shape[0]
    pt = p_prompt.shape[1] // TD
    rowblk = lambda w: pl.BlockSpec((TD, w), lambda i, *_: (i, 0))
    lspec = lambda s: pl.BlockSpec((None,) + s, lambda i, *_: (layer,) + (0,) * len(s), pipeline_mode=pl.Buffered(1))
    grid_spec = pltpu.PrefetchScalarGridSpec(
        num_scalar_prefetch=4,
        grid=(m // TD,),
        in_specs=[rowblk(D_MODEL), rowblk(LANES), rowblk(2),
                  pl.BlockSpec((None, TD, D_PLE), lambda i, *_: (layer, jnp.minimum(i, pt - 1), 0)),
                  pl.BlockSpec((None, TD, D_PLE), lambda i, *_: (layer, jnp.maximum(i - pt, 0), 0)),
                  lspec((D_MODEL, D_MODEL)), lspec((D_PLE, D_MODEL)), lspec((3, D_MODEL)), lspec((3, D_MODEL)),
                  pl.BlockSpec(memory_space=pl.ANY)],
        out_specs=rowblk(D_MODEL),
        scratch_shapes=[pltpu.VMEM((2, LOC_ROWS, D_MODEL), F32), pltpu.VMEM((D_MODEL, D_MODEL), BF16),
                        pltpu.VMEM((D_PLE, D_MODEL), BF16), pltpu.SemaphoreType.DMA((2,))],
    )
    return pl.pallas_call(
        functools.partial(_post_kernel, prompt_tiles=pt),
        grid_spec=grid_spec,
        out_shape=jax.ShapeDtypeStruct((m, D_MODEL), F32),
        compiler_params=_cparams(("arbitrary",)),
        name="post",
    )(cnt, lofs, gofs, ltot, x1, route, lslot, p_prompt, p_sample, w_pg, w_pp, ln_g, ln_b, ys)


def kernel(x_prompt, x_sample, state_mlstm_C, state_mlstm_n, state_mlstm_m, state_swa_k, state_swa_v, p_prompt, p_sample, w_in, b_in, mh_gain, w_a, w_b, w_out, rel_table, w_sink, ln_g, ln_b, w_rg, b_rg, w_re, b_re, w_eg, w_eu, w_ed, w_pg, w_pp):
    bp_, seq = x_prompt.shape[:2]
    bs_, tdec = x_sample.shape[:2]
    mp = bp_ * seq
    msp = bs_ * T_PAD
    mall = mp + msp
    nblk = -(-(2 * (mp + bs_ * tdec) + (mall // TD) * N_EXPERTS * (SEG_ALIGN - 1)) // BLK) + N_EXPERTS

    pad_t = lambda t: jnp.pad(t, [(0, 0)] * (t.ndim - 2) + [(0, T_PAD - tdec), (0, 0)])
    x = lax.dynamic_update_slice(jnp.pad(x_prompt.reshape(mp, D_MODEL), ((0, msp), (0, 0))),
                                 pad_t(x_sample).reshape(msp, D_MODEL), (mp, 0))
    valid = jnp.concatenate([jnp.ones((mp,), bool), jnp.tile(jnp.arange(T_PAD) < tdec, bs_)])
    p_prompt = p_prompt.reshape(DEPTH, mp, D_PLE)
    p_sample = pad_t(p_sample).reshape(DEPTH, msp, D_PLE)

    b_in3 = b_in.reshape(DEPTH, 1, -1)
    w_qkv2, b_qkv2 = w_in[:, :, IN_QKV2[0]:IN_QKV2[1]], b_in3[:, :, IN_QKV2[0]:IN_QKV2[1]]
    w_gates, b_gates = w_in[:, :, IN_GATES[0]:IN_GATES[1]], b_in3[:, :, IN_GATES[0]:IN_GATES[1]]
    b_rg, b_re = b_rg.reshape(DEPTH, 1, N_GROUPS), b_re.reshape(DEPTH, 1, N_EXPERTS)
    gain = mh_gain.reshape(DEPTH, 1, D_MODEL)

    bias_p = _swa_bias(rel_table, WINDOW)
    bias_s = _swa_bias(rel_table, T_PAD)
    sink_p = _sink_cols(w_sink, WINDOW)
    sink_s = _sink_cols(w_sink, T_PAD)
    buf_k = state_swa_k.reshape(DEPTH, bs_, WINDOW, D_KV)
    buf_v = state_swa_v.reshape(DEPTH, bs_, WINDOW, D_KV)
    m0_s = jnp.broadcast_to(state_mlstm_m[..., None], (DEPTH, bs_, M_HEADS, LANES))
    zc = jnp.zeros((1, bp_, M_HEADS, M_DQK, M_DV), F32)
    zn = jnp.zeros((1, bp_, M_HEADS, LANES), F32)
    st_s = (jnp.zeros_like(state_mlstm_C), jnp.zeros_like(state_mlstm_n), jnp.zeros_like(m0_s))
    kv_s = (jnp.zeros_like(buf_k), jnp.zeros_like(buf_v))

    st_p = []
    for i in range(DEPTH):
        za, zg = _inproj(x, [(w_in, IN_A)], [(b_in3, IN_A)], [(ZG, BF16), (LANES, F32)], i, "inproj_a")
        zb, = _inproj(x, [(w_gates, 2048), (w_qkv2, 1536)], [(b_gates, 2048), (b_qkv2, 1536)], [(IN_B, BF16)], i,
                      "inproj_b")

        ya_p, c_p, n_p, m_p = _mlstm(za, zg, gain, i, zc, zn, zn, 0, None, 1, 0, row0=0, nseq=bp_,
                                     steps=seq // LANES, seg=LANES, t_valid=LANES)
        ya_s, *st_s = _mlstm(za, zg, gain, i, state_mlstm_C, state_mlstm_n, m0_s, i, st_s, DEPTH, i, row0=mp,
                             nseq=bs_ // SEQ_BLK, steps=1, seg=T_PAD, t_valid=tdec)
        yb_p = _swa_prompt(zb, bias_p[0], bias_p[1], sink_p, i, batch=bp_, seq=seq)
        yb_s, *kv_s = _swa_sample(zb, buf_k, buf_v, bias_s[0], bias_s[1], sink_s, i, kv_s, row0=mp, batch=bs_,
                                  t_new=tdec)

        x1, route = _merge(x, ya_p, ya_s, yb_p, yb_s, zb, w_a, w_b, w_out, ln_g, ln_b, w_rg, w_re, b_rg, b_re, i)
        cnt, lofs, gofs, ltot, lslot, zstart, blk_e, nused = _route_tables(route[:, :2].astype(I32), valid, nblk)
        xs = _dispatch(x1, cnt, lofs, gofs, ltot, zstart, lslot, nblk)
        ys = _experts(xs, blk_e, nused, w_eg, w_eu, w_ed, i)
        x = _post(x1, ys, cnt, lofs, gofs, ltot, lslot, route, p_prompt, p_sample, w_pg, w_pp, ln_g, ln_b, i)

        kv_p = jnp.stack([lax.slice(zb, ((b + 1) * seq - WINDOW, ZKS), ((b + 1) * seq, IN_B)) for b in range(bp_)])
        kv_p = kv_p.astype(F32)
        heads = lambda t: t.reshape(t.shape[:-1] + (S_KV_HEADS, S_HEAD_DIM))
        st_p.append((c_p[0], n_p[0], m_p[0, :, :, 0], heads(kv_p[..., :D_KV]), heads(kv_p[..., D_KV:])))

    stk = lambda j: jnp.stack([s[j] for s in st_p], axis=0)
    y_p = x[:mp].reshape(bp_, seq, D_MODEL)
    y_s = x[mp:].reshape(bs_, T_PAD, D_MODEL)[:, :tdec]
    return (y_p, y_s, stk(0), stk(1), stk(2), stk(3), stk(4),
            st_s[0], st_s[1], st_s[2][..., 0], heads(kv_s[0]), heads(kv_s[1]))
```

```python
import functools

import numpy as np
import jax
import jax.numpy as jnp
from jax import lax
from jax.experimental import pallas as pl
from jax.experimental.pallas import tpu as pltpu

F32 = jnp.float32
BF16 = jnp.bfloat16
I32 = jnp.int32
HIGHEST = lax.Precision.HIGHEST

D_MODEL = 1024
DEPTH = 4
D_PLE = 256
M_HEADS = 4
M_DQK = 128
M_DV = 256
S_HEADS = 16
S_KV_HEADS = 4
S_HEAD_DIM = 64
S_GROUP = 4
D_KV = S_KV_HEADS * S_HEAD_DIM
WINDOW = 128
REL_BUCKETS = 32
N_GROUPS = 4
EXPERTS_PER_GROUP = 8
N_EXPERTS = 32
D_EXPERT = 512
DN_ALPHA = (2 * DEPTH) ** 0.25
LN_EPS = 1e-5
NEG = -1e30

IN_A = 3200
IN_QKV2 = (3080, 4616)
IN_GATES = (4616, 6664)
ZQ, ZK, ZV, ZOG, ZG = 0, 512, 1024, 2048, 3072
IN_B = 3584
ZGA, ZGB, ZQS, ZKS, ZVS = 0, 1024, 2048, 3072, 3328

LANES = 128
T_PAD = 8
SEQ_BLK = LANES // T_PAD
TM = 512
TD = 256
BLK = 256
VMEM_LIMIT = 56 * 1024 * 1024


def _cparams(sem):
    return pltpu.CompilerParams(dimension_semantics=sem, vmem_limit_bytes=VMEM_LIMIT)


def _layer_spec(shape, layer):
    nd = len(shape)
    return pl.BlockSpec((None,) + shape, lambda *_: (layer,) + (0,) * nd, pipeline_mode=pl.Buffered(1))


def _layer_norm(y, g, b):
    mu = jnp.mean(y, axis=-1, keepdims=True)
    d = y - mu
    var = jnp.mean(d * d, axis=-1, keepdims=True)
    return d * lax.rsqrt(var + LN_EPS) * g + b


def _dot(a, b):
    return jnp.dot(a, b, preferred_element_type=F32)


def _dot_nt(a, b):
    return lax.dot_general(a, b, (((1,), (1,)), ((), ())), preferred_element_type=F32)


def _dot_tn(a, b):
    return lax.dot_general(a, b, (((0,), (0,)), ((), ())), preferred_element_type=F32)


def _dot_exact(a, b):
    return jnp.dot(a, b, precision=HIGHEST, preferred_element_type=F32)


def _cast_into(dst_ref, src_ref, col0=0, step=512):
    n = src_ref.shape[-1]
    for j in range(0, n, step):
        w = min(step, n - j)
        dst_ref[:, col0 + j:col0 + j + w] = src_ref[:, j:j + w].astype(BF16)


def _inproj_kernel(*refs, widths, out_widths):
    ng, no = len(widths), len(out_widths)
    x_ref, w_refs, b_refs = refs[0], refs[1:1 + ng], refs[1 + ng:1 + 2 * ng]
    o_refs, wbf = refs[1 + 2 * ng:1 + 2 * ng + no], refs[1 + 2 * ng + no]

    @pl.when(pl.program_id(0) == 0)
    def _():
        c = 0
        for w_ref, n in zip(w_refs, widths):
            _cast_into(wbf, w_ref, c)
            c += n

    out_start = np.cumsum((0,) + tuple(out_widths))

    def store(col, val):
        k = int(np.searchsorted(out_start, col, side='right')) - 1
        lo = col - int(out_start[k])
        o_refs[k][:, lo:lo + val.shape[1]] = val.astype(o_refs[k].dtype)

    xb = x_ref[...].astype(BF16)
    c = 0
    for b_ref, n in zip(b_refs, widths):
        for j in range(0, n, 512):
            w = min(512, n - j)
            store(c + j, _dot(xb, wbf[:, c + j:c + j + w]) + b_ref[:, j:j + w])
        c += n


def _inproj(x, ws, bs, outs, layer, name):
    m = x.shape[0]
    widths = tuple(n for _, n in ws)
    out_widths = tuple(n for n, _ in outs)
    assert sum(widths) == sum(out_widths)
    return pl.pallas_call(
        functools.partial(_inproj_kernel, widths=widths, out_widths=out_widths),
        grid=(m // TM,),
        in_specs=([pl.BlockSpec((TM, D_MODEL), lambda i: (i, 0))]
                  + [_layer_spec((D_MODEL, n), layer) for _, n in ws]
                  + [_layer_spec((1, n), layer) for _, n in bs]),
        out_specs=[pl.BlockSpec((TM, n), lambda i: (i, 0)) for n, _ in outs],
        out_shape=[jax.ShapeDtypeStruct((m, n), dt) for n, dt in outs],
        scratch_shapes=[pltpu.VMEM((D_MODEL, sum(widths)), BF16)],
        compiler_params=_cparams(("arbitrary",)),
        name=name,
    )(x, *[w for w, _ in ws], *[b for b, _ in bs])


def _log_sigmoid(x):
    return jnp.minimum(x, 0.0) - jnp.log(1.0 + jnp.exp(-jnp.abs(x)))


def _mlstm_kernel(zq, zk, zv, zog, zg, gain, c0, n0, m0, *rest, seg, t_valid, aliased):
    ya, co, no, mo, qc_s = rest[aliased:]
    L = zq.shape[0]
    nseg = L // seg

    @pl.when(pl.program_id(1) == 0)
    def _():
        co[...] = c0[...]
        no[...] = n0[...]
        mo[...] = m0[...]

    g_c = zg[...]
    g_r = g_c.T
    ig_c, lf_c = g_c, _log_sigmoid(g_c)
    ig_r, lf_r = g_r, _log_sigmoid(g_r)
    ri = lax.broadcasted_iota(I32, (L, L), 0)
    ci = lax.broadcasted_iota(I32, (L, L), 1)
    shift = seg.bit_length() - 1
    if t_valid < seg:
        rv = (lax.broadcasted_iota(I32, (L, 1), 0) & (seg - 1)) < t_valid
        cv = (lax.broadcasted_iota(I32, (1, L), 1) & (seg - 1)) < t_valid
        ig_c, lf_c = jnp.where(rv, ig_c, NEG), jnp.where(rv, lf_c, 0.0)
        ig_r, lf_r = jnp.where(cv, ig_r, NEG), jnp.where(cv, lf_r, 0.0)
    if nseg > 1:
        same = (ri >> shift) == (ci >> shift)
        causal, causal_t = same & (ri >= ci), same & (ri <= ci)
        same_f = same.astype(F32)
        be_c = _dot_exact(same_f, lf_c)
        be_r = _dot_exact(lf_r, same_f)
        expand = ((lax.broadcasted_iota(I32, (L, nseg), 0) >> shift)
                  == lax.broadcasted_iota(I32, (L, nseg), 1)).astype(F32)
    else:
        causal, causal_t = ri >= ci, ri <= ci
    b_c = _dot_exact(causal.astype(F32), lf_c)
    b_r = _dot_exact(lf_r, causal_t.astype(F32))

    for h in range(M_HEADS):
        icol, irow = ig_c[:, h:h + 1], ig_r[h:h + 1, :]
        bcol, brow = b_c[:, 4 + h:5 + h], b_r[4 + h:5 + h, :]
        q = zq[:, h * M_DQK:(h + 1) * M_DQK].astype(F32) * (M_DQK ** -0.5)
        k = zk[:, h * M_DQK:(h + 1) * M_DQK].astype(F32)
        qb, kb = q.astype(BF16), zk[:, h * M_DQK:(h + 1) * M_DQK]
        vb = zv[:, h * M_DV:(h + 1) * M_DV]
        if nseg > 1:
            mcol = _dot_exact(expand, mo[:, h, :])[:, 0:1]
            nrow = _dot_exact(expand, no[:, h, :])
            becol, berow = be_c[:, 4 + h:5 + h], be_r[4 + h:5 + h, :]
            for j in range(nseg):
                rows = slice(j * seg, (j + 1) * seg)
                qc_s[rows, :] = _dot(q[rows].astype(BF16), co[j, h].astype(BF16))
            q_c = qc_s[...]
        else:
            mcol, nrow = mo[0, h:h + 1, 0:1], no[0, h:h + 1, :]
            becol = bcol[L - 1:L, :]
            q_c = _dot(qb, co[0, h].astype(BF16))

        dmat = jnp.where(causal, bcol - brow + irow, NEG)
        inter = bcol + mcol
        mhat = jnp.maximum(inter, jnp.max(dmat, axis=-1, keepdims=True))
        w_intra = jnp.exp(dmat - mhat)
        w_inter = jnp.exp(inter - mhat)
        s = _dot_nt(qb, kb) * w_intra
        num = _dot(s.astype(BF16), vb) + w_inter * q_c
        den = jnp.sum(s, axis=-1, keepdims=True) + w_inter * jnp.sum(q * nrow, axis=-1, keepdims=True)
        hh = num / jnp.maximum(jnp.abs(den), jnp.exp(-mhat))

        gcol = icol + becol - bcol
        if nseg > 1:
            gmax = jnp.max(jnp.where(same, irow + berow - brow, NEG), axis=-1, keepdims=True)
        else:
            gmax = jnp.max(gcol, axis=0, keepdims=True)
        m_new = jnp.maximum(becol + mcol, gmax)
        decay = jnp.exp(becol + mcol - m_new)
        kw = k * jnp.exp(gcol - m_new)
        vf = vb.astype(F32) if nseg > 1 else vb
        for j in range(nseg):
            rows = slice(j * seg, (j + 1) * seg)
            dj = decay[j * seg:j * seg + 1, :] if nseg > 1 else decay
            mj = m_new[j * seg:j * seg + 1, :] if nseg > 1 else m_new
            co[j, h] = dj * co[j, h] + _dot_tn(kw[rows].astype(BF16), vf[rows].astype(BF16))
            no[j, h:h + 1, :] = dj * no[j, h:h + 1, :] + jnp.sum(kw[rows], axis=0, keepdims=True)
            mo[j, h:h + 1, :] = jnp.broadcast_to(mj, (1, LANES))

        mu = jnp.mean(hh, axis=-1, keepdims=True)
        d = hh - mu
        var = jnp.mean(d * d, axis=-1, keepdims=True)
        hn = d * lax.rsqrt(var + LN_EPS) * gain[:, h * M_DV:(h + 1) * M_DV]
        og = zog[:, h * M_DV:(h + 1) * M_DV].astype(F32)
        ya[:, h * M_DV:(h + 1) * M_DV] = (hn * jax.nn.sigmoid(og)).astype(ya.dtype)


def _mlstm(za, zg, gain, layer, c0, n0, m0, st_layer, outs_prev, out_layers, out_layer, *, row0, nseq, steps,
           seg, t_valid):
    nseg = LANES // seg
    r0 = row0 // LANES
    row = lambda b, c: r0 + b * steps + c
    nstate = nseq * nseg
    aliased = 0 if outs_prev is None else 3
    kern = functools.partial(_mlstm_kernel, seg=seg, t_valid=min(t_valid, seg), aliased=aliased)
    in4 = pl.BlockSpec((None, nseg, M_HEADS, M_DQK, M_DV), lambda b, c: (st_layer, b, 0, 0, 0))
    in3 = pl.BlockSpec((None, nseg, M_HEADS, LANES), lambda b, c: (st_layer, b, 0, 0))
    out4 = pl.BlockSpec((None, nseg, M_HEADS, M_DQK, M_DV), lambda b, c: (out_layer, b, 0, 0, 0))
    out3 = pl.BlockSpec((None, nseg, M_HEADS, LANES), lambda b, c: (out_layer, b, 0, 0))
    anyspec = pl.BlockSpec(memory_space=pl.ANY)
    n_in = 9
    return pl.pallas_call(
        kern,
        grid=(nseq, steps),
        in_specs=[pl.BlockSpec((LANES, 512), lambda b, c: (row(b, c), ZQ // 512)),
                  pl.BlockSpec((LANES, 512), lambda b, c: (row(b, c), ZK // 512)),
                  pl.BlockSpec((LANES, 1024), lambda b, c: (row(b, c), ZV // 1024)),
                  pl.BlockSpec((LANES, 1024), lambda b, c: (row(b, c), ZOG // 1024)),
                  pl.BlockSpec((LANES, LANES), lambda b, c: (row(b, c), 0)),
                  pl.BlockSpec((None, 1, D_MODEL), lambda b, c: (layer, 0, 0)),
                  in4, in3, in3] + [anyspec] * aliased,
        out_specs=[pl.BlockSpec((LANES, D_MODEL), lambda b, c: (b * steps + c, 0)), out4, out3, out3],
        out_shape=[jax.ShapeDtypeStruct((nseq * steps * LANES, D_MODEL), BF16),
                   jax.ShapeDtypeStruct((out_layers, nstate, M_HEADS, M_DQK, M_DV), F32),
                   jax.ShapeDtypeStruct((out_layers, nstate, M_HEADS, LANES), F32),
                   jax.ShapeDtypeStruct((out_layers, nstate, M_HEADS, LANES), F32)],
        scratch_shapes=[pltpu.VMEM((LANES, M_DV), F32)],
        input_output_aliases={n_in + j: 1 + j for j in range(aliased)},
        compiler_params=_cparams(("parallel", "arbitrary")),
        name="mlstm",
    )(za, za, za, za, zg, gain, c0, n0, m0, *(outs_prev or ()))


def _swa_kernel(q_ref, kc_ref, vc_ref, kp_ref, vp_ref, bp_ref, bc_ref, sink_ref, *rest, nseq, t_new,
                first_block_has_no_past, aliased):
    o_ref = rest[aliased]
    rows_n = q_ref.shape[0]
    tq = rows_n // nseq
    have_past = pl.program_id(1) > 0
    lane = lax.broadcasted_iota(I32, (1, LANES), 1)
    in_lo = lane < S_HEAD_DIM
    if nseq > 1:
        kp_all = kp_ref[...].reshape(nseq * WINDOW, D_KV).astype(BF16)
        vp_all = vp_ref[...].reshape(nseq * WINDOW, D_KV).astype(BF16)
        row_seq = lax.broadcasted_iota(I32, (rows_n, LANES), 0) >> (tq.bit_length() - 1)
    else:
        kp_all, vp_all = kp_ref[...], vp_ref[...]
    kc_all, vc_all = kc_ref[...], vc_ref[...]

    def placed(x, src_half, ones):
        swapped = jnp.concatenate([x[:, S_HEAD_DIM:], x[:, :S_HEAD_DIM]], axis=1)
        zero = jnp.zeros_like(x)
        lo = jnp.where(in_lo, x if src_half == 0 else swapped, zero)
        hi = jnp.where(in_lo, zero, x if src_half == 1 else swapped)
        if ones:
            lo = jnp.where(lane == S_HEAD_DIM, jnp.ones_like(x), lo)
            hi = jnp.where(lane == 0, jnp.ones_like(x), hi)
        return lo, hi

    kv_per_pass = S_KV_HEADS if nseq == 1 else 1
    for kv0 in range(0, S_KV_HEADS, kv_per_pass):
        kp_v, kc_v, vp_v, vc_v = {}, {}, {}, {}
        for kvh in range(kv0, kv0 + kv_per_pass):
            tile = slice((kvh // 2) * LANES, (kvh // 2 + 1) * LANES)
            kp_v[kvh] = placed(kp_all[:, tile], kvh % 2, False)
            kc_v[kvh] = placed(kc_all[:, tile], kvh % 2, False)
            vp_v[kvh] = placed(vp_all[:, tile], kvh % 2, True)
            vc_v[kvh] = placed(vc_all[:, tile], kvh % 2, True)
        heads = list(range(kv0 * S_GROUP, (kv0 + kv_per_pass) * S_GROUP))
        scores = []
        for h in heads:
            q2 = q_ref[:, (h // 2) * LANES:(h // 2 + 1) * LANES]
            scores.append((_dot_nt(q2, kp_v[h // S_GROUP][h % 2]), _dot_nt(q2, kc_v[h // S_GROUP][h % 2])))
        probs = []
        for h, (sp, sc) in zip(heads, scores):
            if nseq > 1:
                sp = jnp.concatenate([sp[b * tq:(b + 1) * tq, b * WINDOW:(b + 1) * WINDOW] for b in range(nseq)],
                                     axis=0)
            sp = sp * (S_HEAD_DIM ** -0.5) + bp_ref[h]
            sc = sc * (S_HEAD_DIM ** -0.5) + bc_ref[h]
            if first_block_has_no_past:
                sp = jnp.where(have_past, sp, NEG)
            sk = sink_ref[h][:, 0:1]
            mx = jnp.maximum(jnp.max(jnp.maximum(sp, sc), axis=-1, keepdims=True), sk)
            pp, pc = jnp.exp(sp - mx), jnp.exp(sc - mx)
            if nseq > 1:
                pp = jnp.concatenate([jnp.where(row_seq == b, pp, 0.0) for b in range(nseq)], axis=1)
            probs.append((pp.astype(BF16), pc.astype(BF16), jnp.exp(sk - mx)))
        outs = [_dot(pp, vp_v[h // S_GROUP][h % 2]) + _dot(pc, vc_v[h // S_GROUP][h % 2])
                for h, (pp, pc, _) in zip(heads, probs)]
        for pair in range(len(heads) // 2):
            halves = []
            for half in range(2):
                o, sink_term = outs[2 * pair + half], probs[2 * pair + half][2]
                ones_lane = S_HEAD_DIM if half == 0 else 0
                halves.append(o / (o[:, ones_lane:ones_lane + 1] + sink_term))
            qt = heads[2 * pair] // 2
            o_ref[:, qt * LANES:(qt + 1) * LANES] = jnp.where(in_lo, halves[0], halves[1]).astype(o_ref.dtype)

    if aliased:
        kc_f, vc_f = kc_all.astype(F32), vc_all.astype(F32)
        for b in range(nseq):
            for new, prev, out in ((kc_f, kp_ref, rest[aliased + 1]), (vc_f, vp_ref, rest[aliased + 2])):
                out[b, 0:WINDOW - t_new, :] = prev[b, t_new:, :]
                out[b, WINDOW - t_new:WINDOW, :] = new[b * tq:b * tq + t_new, :]


def _rel_bucket(dist):
    nn = np.maximum(dist, 0)
    max_exact = REL_BUCKETS // 2
    large = max_exact + (np.log(np.maximum(nn, 1) / max_exact) / np.log(WINDOW / max_exact)
                         * (REL_BUCKETS - max_exact)).astype(np.int32)
    large = np.minimum(large, REL_BUCKETS - 1)
    return np.where(nn < max_exact, nn, large).astype(np.int32)


def _swa_bias(rel_table, tq):
    nseq = LANES // tq
    t = np.tile(np.arange(tq), nseq)[:, None]
    seq = np.repeat(np.arange(nseq), tq)
    d_prev = t + WINDOW - np.arange(WINDOW)[None, :]
    d_cur = t - t.T
    own = seq[:, None] == seq[None, :]

    def build(dist, ok):
        onehot = np.eye(REL_BUCKETS, dtype=np.float32)[_rel_bucket(dist).reshape(-1)]
        bias = jnp.dot(onehot, rel_table.astype(F32), precision=HIGHEST).reshape(dist.shape + (S_HEADS,))
        return jnp.where(ok[None], jnp.transpose(bias, (2, 0, 1)), NEG)

    return build(d_prev, d_prev <= WINDOW), build(d_cur, own & (d_cur >= 0))


def _sink_lanes(w_sink):
    return jnp.broadcast_to(w_sink.astype(F32)[:, :, None, None], (DEPTH, S_HEADS, 1, LANES))


def _swa_bias_specs(layer):
    return [pl.BlockSpec((S_HEADS, LANES, WINDOW), lambda b, n: (0, 0, 0)),
            pl.BlockSpec((S_HEADS, LANES, LANES), lambda b, n: (0, 0, 0)),
            pl.BlockSpec((None, S_HEADS, 1, LANES), lambda b, n: (layer, 0, 0, 0))]


def _swa_prompt(zb, bp, bc, sk, layer, *, batch, seq):
    nb = seq // WINDOW
    row = lambda b, n: b * nb + n
    prev = lambda b, n: jnp.maximum(b * nb + n - 1, 0)
    in_specs = [pl.BlockSpec((WINDOW, 1024), lambda b, n: (row(b, n), ZQS // 1024)),
                pl.BlockSpec((WINDOW, D_KV), lambda b, n: (row(b, n), ZKS // D_KV)),
                pl.BlockSpec((WINDOW, D_KV), lambda b, n: (row(b, n), ZVS // D_KV)),
                pl.BlockSpec((WINDOW, D_KV), lambda b, n: (prev(b, n), ZKS // D_KV)),
                pl.BlockSpec((WINDOW, D_KV), lambda b, n: (prev(b, n), ZVS // D_KV))]
    kern = functools.partial(_swa_kernel, nseq=1, t_new=0, first_block_has_no_past=True, aliased=0)
    return pl.pallas_call(
        kern,
        grid=(batch, nb),
        in_specs=in_specs + _swa_bias_specs(layer),
        out_specs=pl.BlockSpec((WINDOW, D_MODEL), lambda b, n: (row(b, n), 0)),
        out_shape=jax.ShapeDtypeStruct((batch * seq, D_MODEL), BF16),
        compiler_params=_cparams(("parallel", "arbitrary")),
        name="swa_prompt",
    )(zb, zb, zb, zb, zb, bp, bc, sk)


def _swa_sample(zb, buf_k, buf_v, bp, bc, sk, layer, kv_prev, *, row0, batch, t_new):
    r0 = row0 // LANES
    steps = batch // SEQ_BLK
    state = pl.BlockSpec((None, SEQ_BLK, WINDOW, D_KV), lambda b, n: (layer, b, 0, 0))
    anyspec = pl.BlockSpec(memory_space=pl.ANY)
    in_specs = [pl.BlockSpec((LANES, 1024), lambda b, n: (r0 + b, ZQS // 1024)),
                pl.BlockSpec((LANES, D_KV), lambda b, n: (r0 + b, ZKS // D_KV)),
                pl.BlockSpec((LANES, D_KV), lambda b, n: (r0 + b, ZVS // D_KV)),
                state, state]
    kern = functools.partial(_swa_kernel, nseq=SEQ_BLK, t_new=t_new, first_block_has_no_past=False, aliased=2)
    return pl.pallas_call(
        kern,
        grid=(steps, 1),
        in_specs=in_specs + _swa_bias_specs(layer) + [anyspec, anyspec],
        out_specs=[pl.BlockSpec((LANES, D_MODEL), lambda b, n: (b, 0)), state, state],
        out_shape=[jax.ShapeDtypeStruct((batch * T_PAD, D_MODEL), BF16),
                   jax.ShapeDtypeStruct(buf_k.shape, F32), jax.ShapeDtypeStruct(buf_v.shape, F32)],
        input_output_aliases={8: 1, 9: 2},
        compiler_params=_cparams(("parallel", "arbitrary")),
        name="swa_sample",
    )(zb, zb, zb, buf_k, buf_v, bp, bc, sk, *kv_prev)


def _merge_kernel(x_ref, yap_ref, yas_ref, ybp_ref, ybs_ref, ga_ref, gb_ref, wa_ref, wb_ref, wo_ref, lng_ref, lnb_ref,
                  wrg_ref, wre_ref, brg_ref, bre_ref, x1_ref, route_ref, wbf, wr_b, br_s, *, prompt_tiles):
    i = pl.program_id(0)

    @pl.when(i == 0)
    def _():
        for j, w_ref in enumerate((wa_ref, wb_ref, wo_ref)):
            _cast_into(wbf.at[j], w_ref)
        wr_b[...] = jnp.zeros_like(wr_b)
        wr_b[:, 0:N_GROUPS] = wrg_ref[...].astype(BF16)
        wr_b[:, N_GROUPS:N_GROUPS + N_EXPERTS] = wre_ref[...].astype(BF16)
        br_s[...] = jnp.zeros_like(br_s)
        br_s[:, 0:N_GROUPS] = brg_ref[...]
        br_s[:, N_GROUPS:N_GROUPS + N_EXPERTS] = bre_ref[...]

    is_prompt = i < prompt_tiles
    ya = jnp.where(is_prompt, yap_ref[...], yas_ref[...])
    yb = jnp.where(is_prompt, ybp_ref[...], ybs_ref[...])
    a = _dot(ya, wbf[0])
    b = _dot(yb, wbf[1])
    u = jax.nn.sigmoid(ga_ref[...].astype(F32)) * a + jax.nn.sigmoid(gb_ref[...].astype(F32)) * b
    mix = _dot(u.astype(BF16), wbf[2])
    x1 = _layer_norm(DN_ALPHA * x_ref[...] + mix, lng_ref[0:1, :], lnb_ref[0:1, :])
    x1_ref[...] = x1

    logits = _dot(x1.astype(BF16), wr_b[...]) + br_s[...]
    lane = lax.broadcasted_iota(I32, logits.shape, 1)
    lanef = lane.astype(F32)
    gmask = lane < N_GROUPS
    gl = jnp.where(gmask, logits, NEG)
    gmax = jnp.max(gl, axis=-1, keepdims=True)
    gidx = jnp.min(jnp.where(gmask & (gl == gmax), lanef, 999.0), axis=-1, keepdims=True)
    g_w = 1.0 / jnp.sum(jnp.where(gmask, jnp.exp(gl - gmax), 0.0), axis=-1, keepdims=True)
    lo = N_GROUPS + EXPERTS_PER_GROUP * gidx
    emask = (lanef >= lo) & (lanef < lo + EXPERTS_PER_GROUP)
    el = jnp.where(emask, logits, NEG)
    emax = jnp.max(el, axis=-1, keepdims=True)
    ep = jnp.where(emask, jnp.exp(el - emax), 0.0)
    prob = ep / jnp.sum(ep, axis=-1, keepdims=True)
    prob = jnp.where(emask, prob, -1.0)
    p1 = jnp.max(prob, axis=-1, keepdims=True)
    i1 = jnp.min(jnp.where(prob == p1, lanef, 999.0), axis=-1, keepdims=True)
    prob2 = jnp.where(lanef == i1, -1.0, prob)
    p2 = jnp.max(prob2, axis=-1, keepdims=True)
    i2 = jnp.min(jnp.where(prob2 == p2, lanef, 999.0), axis=-1, keepdims=True)
    tot = p1 + p2
    route_ref[...] = jnp.where(lane == 0, i1 - N_GROUPS,
                               jnp.where(lane == 1, i2 - N_GROUPS,
                                         jnp.where(lane == 2, g_w * p1 / tot,
                                                   jnp.where(lane == 3, g_w * p2 / tot, 0.0))))


def _merge(x, ya_p, ya_s, yb_p, yb_s, zb, w_a, w_b, w_out, ln_g, ln_b, w_rg, w_re, b_rg, b_re, layer):
    m = x.shape[0]
    pt = ya_p.shape[0] // TM
    rowblk = lambda w, j: pl.BlockSpec((TM, w), lambda i: (i, j))
    pblk = pl.BlockSpec((TM, D_MODEL), lambda i: (jnp.minimum(i, pt - 1), 0))
    sblk = pl.BlockSpec((TM, D_MODEL), lambda i: (jnp.maximum(i - pt, 0), 0))
    return pl.pallas_call(
        functools.partial(_merge_kernel, prompt_tiles=pt),
        grid=(m // TM,),
        in_specs=[rowblk(D_MODEL, 0), pblk, sblk, pblk, sblk,
                  rowblk(D_MODEL, ZGA // D_MODEL), rowblk(D_MODEL, ZGB // D_MODEL),
                  _layer_spec((D_MODEL, D_MODEL), layer), _layer_spec((D_MODEL, D_MODEL), layer),
                  _layer_spec((D_MODEL, D_MODEL), layer),
                  _layer_spec((3, D_MODEL), layer), _layer_spec((3, D_MODEL), layer),
                  _layer_spec((D_MODEL, N_GROUPS), layer), _layer_spec((D_MODEL, N_EXPERTS), layer),
                  _layer_spec((1, N_GROUPS), layer), _layer_spec((1, N_EXPERTS), layer)],
        out_specs=[rowblk(D_MODEL, 0), rowblk(LANES, 0)],
        out_shape=[jax.ShapeDtypeStruct((m, D_MODEL), F32), jax.ShapeDtypeStruct((m, LANES), F32)],
        scratch_shapes=[pltpu.VMEM((3, D_MODEL, D_MODEL), BF16), pltpu.VMEM((D_MODEL, LANES), BF16),
                        pltpu.VMEM((1, LANES), F32)],
        compiler_params=_cparams(("arbitrary",)),
        name="merge",
    )(x, ya_p, ya_s, yb_p, yb_s, zb, zb, w_a, w_b, w_out, ln_g, ln_b, w_rg, w_re, b_rg, b_re)


SEG_ALIGN = 8
SEG_BITS = (TD // SEG_ALIGN).bit_length()
LOC_ROWS = 2 * TD + N_EXPERTS * SEG_ALIGN
LOC_BITS = (LOC_ROWS // SEG_ALIGN).bit_length()


def _route_tables(eid, valid, nblk):
    m = eid.shape[0]
    nt = m // TD
    per = BLK // SEG_ALIGN
    onehot = ((eid.reshape(nt, 2 * TD, 1) == jnp.arange(N_EXPERTS, dtype=I32))
              & jnp.repeat(valid, 2).reshape(nt, 2 * TD, 1)).astype(I32)
    raw = jnp.sum(onehot, axis=1)
    cnt = (raw + SEG_ALIGN - 1) // SEG_ALIGN
    lofs = jnp.cumsum(cnt, axis=1) - cnt
    ltot = jnp.sum(cnt, axis=1)
    run = jnp.cumsum(onehot.reshape(nt * 2 * TD, N_EXPERTS), axis=0).reshape(onehot.shape)
    run = run - (jnp.cumsum(raw, axis=0) - raw)[:, None, :]
    rank = jnp.sum(onehot * run, axis=2) - 1
    lslot = jnp.sum(onehot * lofs[:, None, :], axis=2) * SEG_ALIGN + rank
    lslot = jnp.where(jnp.sum(onehot, axis=2) > 0, lslot, -1).reshape(m, 2)
    gcnt = jnp.sum(cnt, axis=0)
    gpad = (gcnt + per - 1) // per * per
    gend = jnp.cumsum(gpad)
    gofs = (gend - gpad)[None, :] + jnp.cumsum(cnt, axis=0) - cnt
    nused = gend[-1] // per
    unused = nused + jnp.arange(N_EXPERTS, dtype=I32)
    zstart = jnp.concatenate([jnp.where(gpad > 0, (gend - per) * SEG_ALIGN, -1),
                              jnp.where(unused < nblk, unused * BLK, -1)]).astype(I32)
    bstart = jnp.minimum(jnp.arange(nblk, dtype=I32), nused - 1) * per
    blk_e = jnp.minimum(jnp.sum((gend[None, :] <= bstart[:, None]).astype(I32), axis=1), N_EXPERTS - 1)
    flat = lambda t: t.reshape(-1).astype(I32)
    return (flat(cnt), flat(lofs), flat(gofs), flat(ltot), lslot.astype(I32), zstart, flat(blk_e),
            nused.astype(I32).reshape(1))


def _segment_copies(i, cnt_ref, lofs_ref, gofs_ref, make_copy):
    for e in range(N_EXPERTS):
        c = cnt_ref[i * N_EXPERTS + e]
        lo = lofs_ref[i * N_EXPERTS + e]
        go = gofs_ref[i * N_EXPERTS + e]
        for k in range(SEG_BITS):
            @pl.when(((c >> k) & 1) == 1)
            def _():
                done = c & ((1 << k) - 1)
                make_copy(pl.multiple_of((lo + done) * SEG_ALIGN, SEG_ALIGN),
                          pl.multiple_of((go + done) * SEG_ALIGN, SEG_ALIGN), SEG_ALIGN << k).start()


def _wait_rows(tot, make_copy):
    for k in range(LOC_BITS):
        @pl.when(((tot >> k) & 1) == 1)
        def _():
            make_copy(0, 0, SEG_ALIGN << k).wait()


def _dispatch_kernel(cnt_ref, lofs_ref, gofs_ref, ltot_ref, zstart_ref, x_ref, ls_ref, xs_hbm, xloc, zbuf, sem):
    i = pl.program_id(0)
    last = pl.num_programs(0) - 1
    slot = i % 2

    def copy_from(s):
        return lambda lo, go, n: pltpu.make_async_copy(xloc.at[s, pl.ds(lo, n)], xs_hbm.at[pl.ds(go, n)], sem.at[s])

    @pl.when(i == 0)
    def _():
        zbuf[...] = jnp.zeros_like(zbuf)
        for phase in ("start", "wait"):
            for e in range(zstart_ref.shape[0]):
                @pl.when(zstart_ref[e] >= 0)
                def _():
                    cp = pltpu.make_async_copy(zbuf, xs_hbm.at[pl.ds(pl.multiple_of(zstart_ref[e], BLK), BLK)],
                                               sem.at[2])
                    getattr(cp, phase)()

    @pl.when(i >= 2)
    def _():
        _wait_rows(ltot_ref[jnp.maximum(i - 2, 0)], copy_from(slot))

    ls = ls_ref[0]
    j = lax.broadcasted_iota(I32, (LOC_ROWS, TD), 0)
    onehot = jnp.where((j == ls[0:1, :]) | (j == ls[1:2, :]), 1.0, 0.0).astype(BF16)
    xloc[slot] = _dot(onehot, x_ref[...].astype(BF16))
    _segment_copies(i, cnt_ref, lofs_ref, gofs_ref, copy_from(slot))

    @pl.when(i == last)
    def _():
        _wait_rows(ltot_ref[i], copy_from(slot))

        @pl.when(i >= 1)
        def _():
            _wait_rows(ltot_ref[jnp.maximum(i - 1, 0)], copy_from(1 - slot))


def _dispatch(x1, cnt, lofs, gofs, ltot, zstart, lslot, nblk):
    m = x1.shape[0]
    nt = m // TD
    ls_rows = jnp.pad(lslot.reshape(nt, TD, 2).transpose(0, 2, 1), ((0, 0), (0, 6), (0, 0)), constant_values=-1)
    grid_spec = pltpu.PrefetchScalarGridSpec(
        num_scalar_prefetch=5,
        grid=(nt,),
        in_specs=[pl.BlockSpec((TD, D_MODEL), lambda i, *_: (i, 0)),
                  pl.BlockSpec((1, 8, TD), lambda i, *_: (i, 0, 0))],
        out_specs=pl.BlockSpec(memory_space=pl.ANY),
        scratch_shapes=[pltpu.VMEM((2, LOC_ROWS, D_MODEL), F32), pltpu.VMEM((BLK, D_MODEL), F32),
                        pltpu.SemaphoreType.DMA((3,))],
    )
    return pl.pallas_call(
        _dispatch_kernel,
        grid_spec=grid_spec,
        out_shape=jax.ShapeDtypeStruct((nblk * BLK, D_MODEL), F32),
        compiler_params=_cparams(("arbitrary",)),
        name="dispatch",
    )(cnt, lofs, gofs, ltot, zstart, x1, ls_rows)


def _experts_kernel(blk_e_ref, nused_ref, xs_ref, wg_ref, wu_ref, wd_ref, ys_ref, wgb, wub, wdb):
    b = pl.program_id(0)
    used = b < nused_ref[0]
    new_expert = (b == 0) | (blk_e_ref[b] != blk_e_ref[jnp.maximum(b - 1, 0)])

    @pl.when(used & new_expert)
    def _():
        _cast_into(wgb, wg_ref)
        _cast_into(wub, wu_ref)
        _cast_into(wdb, wd_ref)

    @pl.when(used)
    def _():
        xb = xs_ref[...].astype(BF16)
        g = _dot(xb, wgb[...])
        u = _dot(xb, wub[...])
        ys_ref[...] = _dot((g * jax.nn.sigmoid(g) * u).astype(BF16), wdb[...])

    @pl.when(jnp.logical_not(used))
    def _():
        ys_ref[...] = jnp.zeros_like(ys_ref)


def _experts(xs, blk_e, nused, w_eg, w_eu, w_ed, layer):
    nb = xs.shape[0] // BLK
    wspec = lambda s: pl.BlockSpec((None, None) + s, lambda b, be, nu: (layer, be[b], 0, 0))
    grid_spec = pltpu.PrefetchScalarGridSpec(
        num_scalar_prefetch=2,
        grid=(nb,),
        in_specs=[pl.BlockSpec((BLK, D_MODEL), lambda b, be, nu: (jnp.minimum(b, nu[0] - 1), 0)),
                  wspec((D_MODEL, D_EXPERT)), wspec((D_MODEL, D_EXPERT)), wspec((D_EXPERT, D_MODEL))],
        out_specs=pl.BlockSpec((BLK, D_MODEL), lambda b, be, nu: (b, 0)),
        scratch_shapes=[pltpu.VMEM((D_MODEL, D_EXPERT), BF16), pltpu.VMEM((D_MODEL, D_EXPERT), BF16),
                        pltpu.VMEM((D_EXPERT, D_MODEL), BF16)],
    )
    return pl.pallas_call(
        _experts_kernel,
        grid_spec=grid_spec,
        out_shape=jax.ShapeDtypeStruct(xs.shape, F32),
        compiler_params=_cparams(("arbitrary",)),
        name="experts",
    )(blk_e, nused, xs, w_eg, w_eu, w_ed)


def _post_kernel(cnt_ref, lofs_ref, gofs_ref, ltot_ref, x1_ref, route_ref, ls_ref, pp_ref, ps_ref, wpg_ref, wpp_ref,
                 lng_ref, lnb_ref, ys_hbm, o_ref, yloc, wpg_b, wpp_b, sem, *, prompt_tiles):
    i = pl.program_id(0)
    last = pl.num_programs(0) - 1
    slot = i % 2

    def copy_to(s):
        return lambda lo, go, n: pltpu.make_async_copy(ys_hbm.at[pl.ds(go, n)], yloc.at[s, pl.ds(lo, n)], sem.at[s])

    @pl.when(i == 0)
    def _():
        _cast_into(wpg_b, wpg_ref)
        _cast_into(wpp_b, wpp_ref)
        yloc[...] = jnp.zeros_like(yloc)
        _segment_copies(0, cnt_ref, lofs_ref, gofs_ref, copy_to(0))

    _wait_rows(ltot_ref[i], copy_to(slot))

    @pl.when(i < last)
    def _():
        _segment_copies(i + 1, cnt_ref, lofs_ref, gofs_ref, copy_to(1 - slot))

    route = route_ref[...]
    ls = ls_ref[...]
    j = lax.broadcasted_iota(I32, (ls.shape[0], LOC_ROWS), 1)
    w = jnp.where(j == ls[:, 0:1], route[:, 2:3], 0.0) + jnp.where(j == ls[:, 1:2], route[:, 3:4], 0.0)
    moe = _dot(w.astype(BF16), yloc[slot].astype(BF16))
    x2 = _layer_norm(DN_ALPHA * x1_ref[...] + moe, lng_ref[1:2, :], lnb_ref[1:2, :])
    p = jnp.where(i < prompt_tiles, pp_ref[...], ps_ref[...])
    ple = jax.nn.sigmoid(_dot(x2.astype(BF16), wpg_b[...])) * _dot(p.astype(BF16), wpp_b[...])
    o_ref[...] = _layer_norm(DN_ALPHA * x2 + ple, lng_ref[2:3, :], lnb_ref[2:3, :])


def _post(x1, ys, cnt, lofs, gofs, ltot, lslot, route, p_prompt, p_sample, w_pg, w_pp, ln_g, ln_b, layer):
    m = x1.shape[0]
    pt = p_prompt.shape[1] // TD
    rowblk = lambda w: pl.BlockSpec((TD, w), lambda i, *_: (i, 0))
    lspec = lambda s: pl.BlockSpec((None,) + s, lambda i, *_: (layer,) + (0,) * len(s), pipeline_mode=pl.Buffered(1))
    grid_spec = pltpu.PrefetchScalarGridSpec(
        num_scalar_prefetch=4,
        grid=(m // TD,),
        in_specs=[rowblk(D_MODEL), rowblk(LANES), rowblk(2),
                  pl.BlockSpec((None, TD, D_PLE), lambda i, *_: (layer, jnp.minimum(i, pt - 1), 0)),
                  pl.BlockSpec((None, TD, D_PLE), lambda i, *_: (layer, jnp.maximum(i - pt, 0), 0)),
                  lspec((D_MODEL, D_MODEL)), lspec((D_PLE, D_MODEL)), lspec((3, D_MODEL)), lspec((3, D_MODEL)),
                  pl.BlockSpec(memory_space=pl.ANY)],
        out_specs=rowblk(D_MODEL),
        scratch_shapes=[pltpu.VMEM((2, LOC_ROWS, D_MODEL), F32), pltpu.VMEM((D_MODEL, D_MODEL), BF16),
                        pltpu.VMEM((D_PLE, D_MODEL), BF16), pltpu.SemaphoreType.DMA((2,))],
    )
    return pl.pallas_call(
        functools.partial(_post_kernel, prompt_tiles=pt),
        grid_spec=grid_spec,
        out_shape=jax.ShapeDtypeStruct((m, D_MODEL), F32),
        compiler_params=_cparams(("arbitrary",)),
        name="post",
    )(cnt, lofs, gofs, ltot, x1, route, lslot, p_prompt, p_sample, w_pg, w_pp, ln_g, ln_b, ys)


def kernel(x_prompt, x_sample, state_mlstm_C, state_mlstm_n, state_mlstm_m, state_swa_k, state_swa_v, p_prompt, p_sample, w_in, b_in, mh_gain, w_a, w_b, w_out, rel_table, w_sink, ln_g, ln_b, w_rg, b_rg, w_re, b_re, w_eg, w_eu, w_ed, w_pg, w_pp):
    bp_, seq = x_prompt.shape[:2]
    bs_, tdec = x_sample.shape[:2]
    mp = bp_ * seq
    msp = bs_ * T_PAD
    mall = mp + msp
    nblk = -(-(2 * (mp + bs_ * tdec) + (mall // TD) * N_EXPERTS * (SEG_ALIGN - 1)) // BLK) + N_EXPERTS

    pad_t = lambda t: jnp.pad(t, [(0, 0)] * (t.ndim - 2) + [(0, T_PAD - tdec), (0, 0)])
    x = lax.dynamic_update_slice(jnp.pad(x_prompt.reshape(mp, D_MODEL), ((0, msp), (0, 0))),
                                 pad_t(x_sample).reshape(msp, D_MODEL), (mp, 0))
    valid = jnp.concatenate([jnp.ones((mp,), bool), jnp.tile(jnp.arange(T_PAD) < tdec, bs_)])
    p_prompt = p_prompt.reshape(DEPTH, mp, D_PLE)
    p_sample = pad_t(p_sample).reshape(DEPTH, msp, D_PLE)

    b_in3 = b_in.reshape(DEPTH, 1, -1)
    w_qkv2, b_qkv2 = w_in[:, :, IN_QKV2[0]:IN_QKV2[1]], b_in3[:, :, IN_QKV2[0]:IN_QKV2[1]]
    w_gates, b_gates = w_in[:, :, IN_GATES[0]:IN_GATES[1]], b_in3[:, :, IN_GATES[0]:IN_GATES[1]]
    b_rg, b_re = b_rg.reshape(DEPTH, 1, N_GROUPS), b_re.reshape(DEPTH, 1, N_EXPERTS)
    gain = mh_gain.reshape(DEPTH, 1, D_MODEL)

    bias_p = _swa_bias(rel_table, WINDOW)
    bias_s = _swa_bias(rel_table, T_PAD)
    sink_p = sink_s = _sink_lanes(w_sink)
    buf_k = state_swa_k.reshape(DEPTH, bs_, WINDOW, D_KV)
    buf_v = state_swa_v.reshape(DEPTH, bs_, WINDOW, D_KV)
    m0_s = jnp.broadcast_to(state_mlstm_m[..., None], (DEPTH, bs_, M_HEADS, LANES))
    zc = jnp.zeros((1, bp_, M_HEADS, M_DQK, M_DV), F32)
    zn = jnp.zeros((1, bp_, M_HEADS, LANES), F32)
    st_s = (jnp.zeros_like(state_mlstm_C), jnp.zeros_like(state_mlstm_n), jnp.zeros_like(m0_s))
    kv_s = (jnp.zeros_like(buf_k), jnp.zeros_like(buf_v))

    st_p = []
    for i in range(DEPTH):
        za, zg = _inproj(x, [(w_in, IN_A)], [(b_in3, IN_A)], [(ZG, BF16), (LANES, F32)], i, "inproj_a")
        zb, = _inproj(x, [(w_gates, 2048), (w_qkv2, 1536)], [(b_gates, 2048), (b_qkv2, 1536)], [(IN_B, BF16)], i,
                      "inproj_b")

        ya_p, c_p, n_p, m_p = _mlstm(za, zg, gain, i, zc, zn, zn, 0, None, 1, 0, row0=0, nseq=bp_,
                                     steps=seq // LANES, seg=LANES, t_valid=LANES)
        ya_s, *st_s = _mlstm(za, zg, gain, i, state_mlstm_C, state_mlstm_n, m0_s, i, st_s, DEPTH, i, row0=mp,
                             nseq=bs_ // SEQ_BLK, steps=1, seg=T_PAD, t_valid=tdec)
        yb_p = _swa_prompt(zb, bias_p[0], bias_p[1], sink_p, i, batch=bp_, seq=seq)
        yb_s, *kv_s = _swa_sample(zb, buf_k, buf_v, bias_s[0], bias_s[1], sink_s, i, kv_s, row0=mp, batch=bs_,
                                  t_new=tdec)

        x1, route = _merge(x, ya_p, ya_s, yb_p, yb_s, zb, w_a, w_b, w_out, ln_g, ln_b, w_rg, w_re, b_rg, b_re, i)
        cnt, lofs, gofs, ltot, lslot, zstart, blk_e, nused = _route_tables(route[:, :2].astype(I32), valid, nblk)
        xs = _dispatch(x1, cnt, lofs, gofs, ltot, zstart, lslot, nblk)
        ys = _experts(xs, blk_e, nused, w_eg, w_eu, w_ed, i)
        x = _post(x1, ys, cnt, lofs, gofs, ltot, lslot, route, p_prompt, p_sample, w_pg, w_pp, ln_g, ln_b, i)

        kv_p = jnp.stack([lax.slice(zb, ((b + 1) * seq - WINDOW, ZKS), ((b + 1) * seq, IN_B)) for b in range(bp_)])
        kv_p = kv_p.astype(F32)
        heads = lambda t: t.reshape(t.shape[:-1] + (S_KV_HEADS, S_HEAD_DIM))
        st_p.append((c_p[0], n_p[0], m_p[0, :, :, 0], heads(kv_p[..., :D_KV]), heads(kv_p[..., D_KV:])))

    stk = lambda j: jnp.stack([s[j] for s in st_p], axis=0)
    y_p = x[:mp].reshape(bp_, seq, D_MODEL)
    y_s = x[mp:].reshape(bs_, T_PAD, D_MODEL)[:, :tdec]
    return (y_p, y_s, stk(0), stk(1), stk(2), stk(3), stk(4),
            st_s[0], st_s[1], st_s[2][..., 0], heads(kv_s[0]), heads(kv_s[1]))
```

```python
import functools

import numpy as np
import jax
import jax.numpy as jnp
from jax import lax
from jax.experimental import pallas as pl
from jax.experimental.pallas import tpu as pltpu

F32 = jnp.float32
BF16 = jnp.bfloat16
I32 = jnp.int32
HIGHEST = lax.Precision.HIGHEST

D_MODEL = 1024
DEPTH = 4
D_PLE = 256
M_HEADS = 4
M_DQK = 128
M_DV = 256
S_HEADS = 16
S_KV_HEADS = 4
S_HEAD_DIM = 64
S_GROUP = 4
D_KV = S_KV_HEADS * S_HEAD_DIM
WINDOW = 128
REL_BUCKETS = 32
N_GROUPS = 4
EXPERTS_PER_GROUP = 8
N_EXPERTS = 32
D_EXPERT = 512
DN_ALPHA = (2 * DEPTH) ** 0.25
LN_EPS = 1e-5
NEG = -1e30

IN_A = 3200
IN_QKV2 = (3080, 4616)
IN_GATES = (4616, 6664)
ZQ, ZK, ZV, ZOG, ZG = 0, 512, 1024, 2048, 3072
IN_B = 3584
ZGA, ZGB, ZQS, ZKS, ZVS = 0, 1024, 2048, 3072, 3328

LANES = 128
T_PAD = 8
SEQ_BLK = LANES // T_PAD
TM = 512
TD = 256
BLK = 512
VMEM_LIMIT = 56 * 1024 * 1024


def _cparams(sem):
    return pltpu.CompilerParams(dimension_semantics=sem, vmem_limit_bytes=VMEM_LIMIT)


def _layer_spec(shape, layer):
    nd = len(shape)
    return pl.BlockSpec((None,) + shape, lambda *_: (layer,) + (0,) * nd, pipeline_mode=pl.Buffered(1))


def _layer_norm(y, g, b):
    mu = jnp.mean(y, axis=-1, keepdims=True)
    d = y - mu
    var = jnp.mean(d * d, axis=-1, keepdims=True)
    return d * lax.rsqrt(var + LN_EPS) * g + b


def _dot(a, b):
    return jnp.dot(a, b, preferred_element_type=F32)


def _dot_nt(a, b):
    return lax.dot_general(a, b, (((1,), (1,)), ((), ())), preferred_element_type=F32)


def _dot_tn(a, b):
    return lax.dot_general(a, b, (((0,), (0,)), ((), ())), preferred_element_type=F32)


def _dot_exact(a, b):
    return jnp.dot(a, b, precision=HIGHEST, preferred_element_type=F32)


def _cast_into(dst_ref, src_ref, col0=0, step=512):
    n = src_ref.shape[-1]
    for j in range(0, n, step):
        w = min(step, n - j)
        dst_ref[:, col0 + j:col0 + j + w] = src_ref[:, j:j + w].astype(BF16)


def _inproj_kernel(*refs, widths, out_widths):
    ng, no = len(widths), len(out_widths)
    x_ref, w_refs, b_refs = refs[0], refs[1:1 + ng], refs[1 + ng:1 + 2 * ng]
    o_refs, wbf = refs[1 + 2 * ng:1 + 2 * ng + no], refs[1 + 2 * ng + no]

    @pl.when(pl.program_id(0) == 0)
    def _():
        c = 0
        for w_ref, n in zip(w_refs, widths):
            _cast_into(wbf, w_ref, c)
            c += n

    out_start = np.cumsum((0,) + tuple(out_widths))

    def store(col, val):
        k = int(np.searchsorted(out_start, col, side='right')) - 1
        lo = col - int(out_start[k])
        o_refs[k][:, lo:lo + val.shape[1]] = val.astype(o_refs[k].dtype)

    xb = x_ref[...].astype(BF16)
    c = 0
    for b_ref, n in zip(b_refs, widths):
        for j in range(0, n, 512):
            w = min(512, n - j)
            store(c + j, _dot(xb, wbf[:, c + j:c + j + w]) + b_ref[:, j:j + w])
        c += n


def _inproj(x, ws, bs, outs, layer, name):
    m = x.shape[0]
    widths = tuple(n for _, n in ws)
    out_widths = tuple(n for n, _ in outs)
    assert sum(widths) == sum(out_widths)
    return pl.pallas_call(
        functools.partial(_inproj_kernel, widths=widths, out_widths=out_widths),
        grid=(m // TM,),
        in_specs=([pl.BlockSpec((TM, D_MODEL), lambda i: (i, 0))]
                  + [_layer_spec((D_MODEL, n), layer) for _, n in ws]
                  + [_layer_spec((1, n), layer) for _, n in bs]),
        out_specs=[pl.BlockSpec((TM, n), lambda i: (i, 0)) for n, _ in outs],
        out_shape=[jax.ShapeDtypeStruct((m, n), dt) for n, dt in outs],
        scratch_shapes=[pltpu.VMEM((D_MODEL, sum(widths)), BF16)],
        compiler_params=_cparams(("arbitrary",)),
        name=name,
    )(x, *[w for w, _ in ws], *[b for b, _ in bs])


def _log_sigmoid(x):
    return jnp.minimum(x, 0.0) - jnp.log(1.0 + jnp.exp(-jnp.abs(x)))


def _mlstm_kernel(zq, zk, zv, zog, zg, gain, c0, n0, m0, *rest, seg, t_valid, aliased):
    ya, co, no, mo, qc_s = rest[aliased:]
    L = zq.shape[0]
    nseg = L // seg

    @pl.when(pl.program_id(1) == 0)
    def _():
        co[...] = c0[...]
        no[...] = n0[...]
        mo[...] = m0[...]

    g_c = zg[...]
    g_r = g_c.T
    ig_c, lf_c = g_c, _log_sigmoid(g_c)
    ig_r, lf_r = g_r, _log_sigmoid(g_r)
    ri = lax.broadcasted_iota(I32, (L, L), 0)
    ci = lax.broadcasted_iota(I32, (L, L), 1)
    shift = seg.bit_length() - 1
    if t_valid < seg:
        rv = (lax.broadcasted_iota(I32, (L, 1), 0) & (seg - 1)) < t_valid
        cv = (lax.broadcasted_iota(I32, (1, L), 1) & (seg - 1)) < t_valid
        ig_c, lf_c = jnp.where(rv, ig_c, NEG), jnp.where(rv, lf_c, 0.0)
        ig_r, lf_r = jnp.where(cv, ig_r, NEG), jnp.where(cv, lf_r, 0.0)
    if nseg > 1:
        same = (ri >> shift) == (ci >> shift)
        causal, causal_t = same & (ri >= ci), same & (ri <= ci)
        same_f = same.astype(F32)
        be_c = _dot_exact(same_f, lf_c)
        be_r = _dot_exact(lf_r, same_f)
        expand = ((lax.broadcasted_iota(I32, (L, nseg), 0) >> shift)
                  == lax.broadcasted_iota(I32, (L, nseg), 1)).astype(F32)
    else:
        causal, causal_t = ri >= ci, ri <= ci
    b_c = _dot_exact(causal.astype(F32), lf_c)
    b_r = _dot_exact(lf_r, causal_t.astype(F32))

    for h in range(M_HEADS):
        icol, irow = ig_c[:, h:h + 1], ig_r[h:h + 1, :]
        bcol, brow = b_c[:, 4 + h:5 + h], b_r[4 + h:5 + h, :]
        q = zq[:, h * M_DQK:(h + 1) * M_DQK].astype(F32) * (M_DQK ** -0.5)
        k = zk[:, h * M_DQK:(h + 1) * M_DQK].astype(F32)
        qb, kb = q.astype(BF16), zk[:, h * M_DQK:(h + 1) * M_DQK]
        vb = zv[:, h * M_DV:(h + 1) * M_DV]
        if nseg > 1:
            mcol = _dot_exact(expand, mo[:, h, :])[:, 0:1]
            nrow = _dot_exact(expand, no[:, h, :])
            becol, berow = be_c[:, 4 + h:5 + h], be_r[4 + h:5 + h, :]
            for j in range(nseg):
                rows = slice(j * seg, (j + 1) * seg)
                qc_s[rows, :] = _dot(q[rows].astype(BF16), co[j, h].astype(BF16))
            q_c = qc_s[...]
        else:
            mcol, nrow = mo[0, h:h + 1, 0:1], no[0, h:h + 1, :]
            becol = bcol[L - 1:L, :]
            q_c = _dot(qb, co[0, h].astype(BF16))

        dmat = jnp.where(causal, bcol - brow + irow, NEG)
        inter = bcol + mcol
        mhat = jnp.maximum(inter, jnp.max(dmat, axis=-1, keepdims=True))
        w_intra = jnp.exp(dmat - mhat)
        w_inter = jnp.exp(inter - mhat)
        s = _dot_nt(qb, kb) * w_intra
        num = _dot(s.astype(BF16), vb) + w_inter * q_c
        den = jnp.sum(s, axis=-1, keepdims=True) + w_inter * jnp.sum(q * nrow, axis=-1, keepdims=True)
        hh = num / jnp.maximum(jnp.abs(den), jnp.exp(-mhat))

        gcol = icol + becol - bcol
        if nseg > 1:
            gmax = jnp.max(jnp.where(same, irow + berow - brow, NEG), axis=-1, keepdims=True)
        else:
            gmax = jnp.max(gcol, axis=0, keepdims=True)
        m_new = jnp.maximum(becol + mcol, gmax)
        decay = jnp.exp(becol + mcol - m_new)
        kw = k * jnp.exp(gcol - m_new)
        vf = vb.astype(F32) if nseg > 1 else vb
        for j in range(nseg):
            rows = slice(j * seg, (j + 1) * seg)
            dj = decay[j * seg:j * seg + 1, :] if nseg > 1 else decay
            mj = m_new[j * seg:j * seg + 1, :] if nseg > 1 else m_new
            co[j, h] = dj * co[j, h] + _dot_tn(kw[rows].astype(BF16), vf[rows].astype(BF16))
            no[j, h:h + 1, :] = dj * no[j, h:h + 1, :] + jnp.sum(kw[rows], axis=0, keepdims=True)
            mo[j, h:h + 1, :] = jnp.broadcast_to(mj, (1, LANES))

        mu = jnp.mean(hh, axis=-1, keepdims=True)
        d = hh - mu
        var = jnp.mean(d * d, axis=-1, keepdims=True)
        hn = d * lax.rsqrt(var + LN_EPS) * gain[:, h * M_DV:(h + 1) * M_DV]
        og = zog[:, h * M_DV:(h + 1) * M_DV].astype(F32)
        ya[:, h * M_DV:(h + 1) * M_DV] = (hn * jax.nn.sigmoid(og)).astype(ya.dtype)


def _mlstm(za, zg, gain, layer, c0, n0, m0, st_layer, outs_prev, out_layers, out_layer, *, row0, nseq, steps,
           seg, t_valid):
    nseg = LANES // seg
    r0 = row0 // LANES
    row = lambda b, c: r0 + b * steps + c
    nstate = nseq * nseg
    aliased = 0 if outs_prev is None else 3
    kern = functools.partial(_mlstm_kernel, seg=seg, t_valid=min(t_valid, seg), aliased=aliased)
    in4 = pl.BlockSpec((None, nseg, M_HEADS, M_DQK, M_DV), lambda b, c: (st_layer, b, 0, 0, 0))
    in3 = pl.BlockSpec((None, nseg, M_HEADS, LANES), lambda b, c: (st_layer, b, 0, 0))
    out4 = pl.BlockSpec((None, nseg, M_HEADS, M_DQK, M_DV), lambda b, c: (out_layer, b, 0, 0, 0))
    out3 = pl.BlockSpec((None, nseg, M_HEADS, LANES), lambda b, c: (out_layer, b, 0, 0))
    anyspec = pl.BlockSpec(memory_space=pl.ANY)
    n_in = 9
    return pl.pallas_call(
        kern,
        grid=(nseq, steps),
        in_specs=[pl.BlockSpec((LANES, 512), lambda b, c: (row(b, c), ZQ // 512)),
                  pl.BlockSpec((LANES, 512), lambda b, c: (row(b, c), ZK // 512)),
                  pl.BlockSpec((LANES, 1024), lambda b, c: (row(b, c), ZV // 1024)),
                  pl.BlockSpec((LANES, 1024), lambda b, c: (row(b, c), ZOG // 1024)),
                  pl.BlockSpec((LANES, LANES), lambda b, c: (row(b, c), 0)),
                  pl.BlockSpec((None, 1, D_MODEL), lambda b, c: (layer, 0, 0)),
                  in4, in3, in3] + [anyspec] * aliased,
        out_specs=[pl.BlockSpec((LANES, D_MODEL), lambda b, c: (b * steps + c, 0)), out4, out3, out3],
        out_shape=[jax.ShapeDtypeStruct((nseq * steps * LANES, D_MODEL), BF16),
                   jax.ShapeDtypeStruct((out_layers, nstate, M_HEADS, M_DQK, M_DV), F32),
                   jax.ShapeDtypeStruct((out_layers, nstate, M_HEADS, LANES), F32),
                   jax.ShapeDtypeStruct((out_layers, nstate, M_HEADS, LANES), F32)],
        scratch_shapes=[pltpu.VMEM((LANES, M_DV), F32)],
        input_output_aliases={n_in + j: 1 + j for j in range(aliased)},
        compiler_params=_cparams(("parallel", "arbitrary")),
        name="mlstm",
    )(za, za, za, za, zg, gain, c0, n0, m0, *(outs_prev or ()))


def _swa_kernel(q_ref, kc_ref, vc_ref, kp_ref, vp_ref, bp_ref, bc_ref, sink_ref, *rest, nseq, t_new,
                first_block_has_no_past, aliased):
    o_ref = rest[aliased]
    rows_n = q_ref.shape[0]
    tq = rows_n // nseq
    have_past = pl.program_id(1) > 0
    lane = lax.broadcasted_iota(I32, (1, LANES), 1)
    in_lo = lane < S_HEAD_DIM
    if nseq > 1:
        kp_all = kp_ref[...].reshape(nseq * WINDOW, D_KV).astype(BF16)
        vp_all = vp_ref[...].reshape(nseq * WINDOW, D_KV).astype(BF16)
        row_seq = lax.broadcasted_iota(I32, (rows_n, LANES), 0) >> (tq.bit_length() - 1)
    else:
        kp_all, vp_all = kp_ref[...], vp_ref[...]
    kc_all, vc_all = kc_ref[...], vc_ref[...]

    def placed(x, src_half, ones):
        swapped = jnp.concatenate([x[:, S_HEAD_DIM:], x[:, :S_HEAD_DIM]], axis=1)
        zero = jnp.zeros_like(x)
        lo = jnp.where(in_lo, x if src_half == 0 else swapped, zero)
        hi = jnp.where(in_lo, zero, x if src_half == 1 else swapped)
        if ones:
            lo = jnp.where(lane == S_HEAD_DIM, jnp.ones_like(x), lo)
            hi = jnp.where(lane == 0, jnp.ones_like(x), hi)
        return lo, hi

    kv_per_pass = S_KV_HEADS if nseq == 1 else 1
    for kv0 in range(0, S_KV_HEADS, kv_per_pass):
        kp_v, kc_v, vp_v, vc_v = {}, {}, {}, {}
        for kvh in range(kv0, kv0 + kv_per_pass):
            tile = slice((kvh // 2) * LANES, (kvh // 2 + 1) * LANES)
            kp_v[kvh] = placed(kp_all[:, tile], kvh % 2, False)
            kc_v[kvh] = placed(kc_all[:, tile], kvh % 2, False)
            vp_v[kvh] = placed(vp_all[:, tile], kvh % 2, True)
            vc_v[kvh] = placed(vc_all[:, tile], kvh % 2, True)
        heads = list(range(kv0 * S_GROUP, (kv0 + kv_per_pass) * S_GROUP))
        scores = []
        for h in heads:
            q2 = q_ref[:, (h // 2) * LANES:(h // 2 + 1) * LANES]
            scores.append((_dot_nt(q2, kp_v[h // S_GROUP][h % 2]), _dot_nt(q2, kc_v[h // S_GROUP][h % 2])))
        probs = []
        for h, (sp, sc) in zip(heads, scores):
            if nseq > 1:
                sp = jnp.concatenate([sp[b * tq:(b + 1) * tq, b * WINDOW:(b + 1) * WINDOW] for b in range(nseq)],
                                     axis=0)
            sp = sp * (S_HEAD_DIM ** -0.5) + bp_ref[h]
            sc = sc * (S_HEAD_DIM ** -0.5) + bc_ref[h]
            if first_block_has_no_past:
                sp = jnp.where(have_past, sp, NEG)
            sk = sink_ref[h][:, 0:1]
            mx = jnp.maximum(jnp.max(jnp.maximum(sp, sc), axis=-1, keepdims=True), sk)
            pp, pc = jnp.exp(sp - mx), jnp.exp(sc - mx)
            if nseq > 1:
                pp = jnp.concatenate([jnp.where(row_seq == b, pp, 0.0) for b in range(nseq)], axis=1)
            probs.append((pp.astype(BF16), pc.astype(BF16), jnp.exp(sk - mx)))
        outs = [_dot(pp, vp_v[h // S_GROUP][h % 2]) + _dot(pc, vc_v[h // S_GROUP][h % 2])
                for h, (pp, pc, _) in zip(heads, probs)]
        for pair in range(len(heads) // 2):
            halves = []
            for half in range(2):
                o, sink_term = outs[2 * pair + half], probs[2 * pair + half][2]
                ones_lane = S_HEAD_DIM if half == 0 else 0
                halves.append(o / (o[:, ones_lane:ones_lane + 1] + sink_term))
            qt = heads[2 * pair] // 2
            o_ref[:, qt * LANES:(qt + 1) * LANES] = jnp.where(in_lo, halves[0], halves[1]).astype(o_ref.dtype)

    if aliased:
        kc_f, vc_f = kc_all.astype(F32), vc_all.astype(F32)
        for b in range(nseq):
            for new, prev, out in ((kc_f, kp_ref, rest[aliased + 1]), (vc_f, vp_ref, rest[aliased + 2])):
                out[b, 0:WINDOW - t_new, :] = prev[b, t_new:, :]
                out[b, WINDOW - t_new:WINDOW, :] = new[b * tq:b * tq + t_new, :]


def _rel_bucket(dist):
    nn = np.maximum(dist, 0)
    max_exact = REL_BUCKETS // 2
    large = max_exact + (np.log(np.maximum(nn, 1) / max_exact) / np.log(WINDOW / max_exact)
                         * (REL_BUCKETS - max_exact)).astype(np.int32)
    large = np.minimum(large, REL_BUCKETS - 1)
    return np.where(nn < max_exact, nn, large).astype(np.int32)


def _swa_bias(rel_table, tq):
    nseq = LANES // tq
    t = np.tile(np.arange(tq), nseq)[:, None]
    seq = np.repeat(np.arange(nseq), tq)
    d_prev = t + WINDOW - np.arange(WINDOW)[None, :]
    d_cur = t - t.T
    own = seq[:, None] == seq[None, :]

    def build(dist, ok):
        onehot = np.eye(REL_BUCKETS, dtype=np.float32)[_rel_bucket(dist).reshape(-1)]
        bias = jnp.dot(onehot, rel_table.astype(F32), precision=HIGHEST).reshape(dist.shape + (S_HEADS,))
        return jnp.where(ok[None], jnp.transpose(bias, (2, 0, 1)), NEG)

    return build(d_prev, d_prev <= WINDOW), build(d_cur, own & (d_cur >= 0))


def _sink_lanes(w_sink):
    return jnp.broadcast_to(w_sink.astype(F32)[:, :, None, None], (DEPTH, S_HEADS, 1, LANES))


def _swa_bias_specs(layer):
    return [pl.BlockSpec((S_HEADS, LANES, WINDOW), lambda b, n: (0, 0, 0)),
            pl.BlockSpec((S_HEADS, LANES, LANES), lambda b, n: (0, 0, 0)),
            pl.BlockSpec((None, S_HEADS, 1, LANES), lambda b, n: (layer, 0, 0, 0))]


def _swa_prompt(zb, bp, bc, sk, layer, *, batch, seq):
    nb = seq // WINDOW
    row = lambda b, n: b * nb + n
    prev = lambda b, n: jnp.maximum(b * nb + n - 1, 0)
    in_specs = [pl.BlockSpec((WINDOW, 1024), lambda b, n: (row(b, n), ZQS // 1024)),
                pl.BlockSpec((WINDOW, D_KV), lambda b, n: (row(b, n), ZKS // D_KV)),
                pl.BlockSpec((WINDOW, D_KV), lambda b, n: (row(b, n), ZVS // D_KV)),
                pl.BlockSpec((WINDOW, D_KV), lambda b, n: (prev(b, n), ZKS // D_KV)),
                pl.BlockSpec((WINDOW, D_KV), lambda b, n: (prev(b, n), ZVS // D_KV))]
    kern = functools.partial(_swa_kernel, nseq=1, t_new=0, first_block_has_no_past=True, aliased=0)
    return pl.pallas_call(
        kern,
        grid=(batch, nb),
        in_specs=in_specs + _swa_bias_specs(layer),
        out_specs=pl.BlockSpec((WINDOW, D_MODEL), lambda b, n: (row(b, n), 0)),
        out_shape=jax.ShapeDtypeStruct((batch * seq, D_MODEL), BF16),
        compiler_params=_cparams(("parallel", "arbitrary")),
        name="swa_prompt",
    )(zb, zb, zb, zb, zb, bp, bc, sk)


def _swa_sample(zb, buf_k, buf_v, bp, bc, sk, layer, kv_prev, *, row0, batch, t_new):
    r0 = row0 // LANES
    steps = batch // SEQ_BLK
    state = pl.BlockSpec((None, SEQ_BLK, WINDOW, D_KV), lambda b, n: (layer, b, 0, 0))
    anyspec = pl.BlockSpec(memory_space=pl.ANY)
    in_specs = [pl.BlockSpec((LANES, 1024), lambda b, n: (r0 + b, ZQS // 1024)),
                pl.BlockSpec((LANES, D_KV), lambda b, n: (r0 + b, ZKS // D_KV)),
                pl.BlockSpec((LANES, D_KV), lambda b, n: (r0 + b, ZVS // D_KV)),
                state, state]
    kern = functools.partial(_swa_kernel, nseq=SEQ_BLK, t_new=t_new, first_block_has_no_past=False, aliased=2)
    return pl.pallas_call(
        kern,
        grid=(steps, 1),
        in_specs=in_specs + _swa_bias_specs(layer) + [anyspec, anyspec],
        out_specs=[pl.BlockSpec((LANES, D_MODEL), lambda b, n: (b, 0)), state, state],
        out_shape=[jax.ShapeDtypeStruct((batch * T_PAD, D_MODEL), BF16),
                   jax.ShapeDtypeStruct(buf_k.shape, F32), jax.ShapeDtypeStruct(buf_v.shape, F32)],
        input_output_aliases={8: 1, 9: 2},
        compiler_params=_cparams(("parallel", "arbitrary")),
        name="swa_sample",
    )(zb, zb, zb, buf_k, buf_v, bp, bc, sk, *kv_prev)


def _merge_kernel(x_ref, yap_ref, yas_ref, ybp_ref, ybs_ref, ga_ref, gb_ref, wa_ref, wb_ref, wo_ref, lng_ref, lnb_ref,
                  wrg_ref, wre_ref, brg_ref, bre_ref, x1_ref, route_ref, wbf, wr_b, br_s, *, prompt_tiles):
    i = pl.program_id(0)

    @pl.when(i == 0)
    def _():
        for j, w_ref in enumerate((wa_ref, wb_ref, wo_ref)):
            _cast_into(wbf.at[j], w_ref)
        wr_b[...] = jnp.zeros_like(wr_b)
        wr_b[:, 0:N_GROUPS] = wrg_ref[...].astype(BF16)
        wr_b[:, N_GROUPS:N_GROUPS + N_EXPERTS] = wre_ref[...].astype(BF16)
        br_s[...] = jnp.zeros_like(br_s)
        br_s[:, 0:N_GROUPS] = brg_ref[...]
        br_s[:, N_GROUPS:N_GROUPS + N_EXPERTS] = bre_ref[...]

    is_prompt = i < prompt_tiles
    ya = jnp.where(is_prompt, yap_ref[...], yas_ref[...])
    yb = jnp.where(is_prompt, ybp_ref[...], ybs_ref[...])
    a = _dot(ya, wbf[0])
    b = _dot(yb, wbf[1])
    u = jax.nn.sigmoid(ga_ref[...].astype(F32)) * a + jax.nn.sigmoid(gb_ref[...].astype(F32)) * b
    mix = _dot(u.astype(BF16), wbf[2])
    x1 = _layer_norm(DN_ALPHA * x_ref[...] + mix, lng_ref[0:1, :], lnb_ref[0:1, :])
    x1_ref[...] = x1

    logits = _dot(x1.astype(BF16), wr_b[...]) + br_s[...]
    lane = lax.broadcasted_iota(I32, logits.shape, 1)
    lanef = lane.astype(F32)
    gmask = lane < N_GROUPS
    gl = jnp.where(gmask, logits, NEG)
    gmax = jnp.max(gl, axis=-1, keepdims=True)
    gidx = jnp.min(jnp.where(gmask & (gl == gmax), lanef, 999.0), axis=-1, keepdims=True)
    g_w = 1.0 / jnp.sum(jnp.where(gmask, jnp.exp(gl - gmax), 0.0), axis=-1, keepdims=True)
    lo = N_GROUPS + EXPERTS_PER_GROUP * gidx
    emask = (lanef >= lo) & (lanef < lo + EXPERTS_PER_GROUP)
    el = jnp.where(emask, logits, NEG)
    emax = jnp.max(el, axis=-1, keepdims=True)
    ep = jnp.where(emask, jnp.exp(el - emax), 0.0)
    prob = ep / jnp.sum(ep, axis=-1, keepdims=True)
    prob = jnp.where(emask, prob, -1.0)
    p1 = jnp.max(prob, axis=-1, keepdims=True)
    i1 = jnp.min(jnp.where(prob == p1, lanef, 999.0), axis=-1, keepdims=True)
    prob2 = jnp.where(lanef == i1, -1.0, prob)
    p2 = jnp.max(prob2, axis=-1, keepdims=True)
    i2 = jnp.min(jnp.where(prob2 == p2, lanef, 999.0), axis=-1, keepdims=True)
    tot = p1 + p2
    route_ref[...] = jnp.where(lane == 0, i1 - N_GROUPS,
                               jnp.where(lane == 1, i2 - N_GROUPS,
                                         jnp.where(lane == 2, g_w * p1 / tot,
                                                   jnp.where(lane == 3, g_w * p2 / tot, 0.0))))


def _merge(x, ya_p, ya_s, yb_p, yb_s, zb, w_a, w_b, w_out, ln_g, ln_b, w_rg, w_re, b_rg, b_re, layer):
    m = x.shape[0]
    pt = ya_p.shape[0] // TM
    rowblk = lambda w, j: pl.BlockSpec((TM, w), lambda i: (i, j))
    pblk = pl.BlockSpec((TM, D_MODEL), lambda i: (jnp.minimum(i, pt - 1), 0))
    sblk = pl.BlockSpec((TM, D_MODEL), lambda i: (jnp.maximum(i - pt, 0), 0))
    return pl.pallas_call(
        functools.partial(_merge_kernel, prompt_tiles=pt),
        grid=(m // TM,),
        in_specs=[rowblk(D_MODEL, 0), pblk, sblk, pblk, sblk,
                  rowblk(D_MODEL, ZGA // D_MODEL), rowblk(D_MODEL, ZGB // D_MODEL),
                  _layer_spec((D_MODEL, D_MODEL), layer), _layer_spec((D_MODEL, D_MODEL), layer),
                  _layer_spec((D_MODEL, D_MODEL), layer),
                  _layer_spec((3, D_MODEL), layer), _layer_spec((3, D_MODEL), layer),
                  _layer_spec((D_MODEL, N_GROUPS), layer), _layer_spec((D_MODEL, N_EXPERTS), layer),
                  _layer_spec((1, N_GROUPS), layer), _layer_spec((1, N_EXPERTS), layer)],
        out_specs=[rowblk(D_MODEL, 0), rowblk(LANES, 0)],
        out_shape=[jax.ShapeDtypeStruct((m, D_MODEL), F32), jax.ShapeDtypeStruct((m, LANES), F32)],
        scratch_shapes=[pltpu.VMEM((3, D_MODEL, D_MODEL), BF16), pltpu.VMEM((D_MODEL, LANES), BF16),
                        pltpu.VMEM((1, LANES), F32)],
        compiler_params=_cparams(("arbitrary",)),
        name="merge",
    )(x, ya_p, ya_s, yb_p, yb_s, zb, zb, w_a, w_b, w_out, ln_g, ln_b, w_rg, w_re, b_rg, b_re)


SEG_ALIGN = 8
LOC_ROWS = 2 * TD + N_EXPERTS * SEG_ALIGN
LOC_BITS = (LOC_ROWS // SEG_ALIGN).bit_length()


def _route_tables(eid, valid, nblk):
    m = eid.shape[0]
    nt = m // TD
    per = BLK // SEG_ALIGN
    onehot = ((eid.reshape(nt, 2 * TD, 1) == jnp.arange(N_EXPERTS, dtype=I32))
              & jnp.repeat(valid, 2).reshape(nt, 2 * TD, 1)).astype(I32)
    raw = jnp.sum(onehot, axis=1)
    cnt = (raw + SEG_ALIGN - 1) // SEG_ALIGN
    lofs = jnp.cumsum(cnt, axis=1) - cnt
    ltot = jnp.sum(cnt, axis=1)
    run = jnp.cumsum(onehot.reshape(nt * 2 * TD, N_EXPERTS), axis=0).reshape(onehot.shape)
    run = run - (jnp.cumsum(raw, axis=0) - raw)[:, None, :]
    rank = jnp.sum(onehot * run, axis=2) - 1
    lslot = jnp.sum(onehot * lofs[:, None, :], axis=2) * SEG_ALIGN + rank
    lslot = jnp.where(jnp.sum(onehot, axis=2) > 0, lslot, -1).reshape(m, 2)
    gcnt = jnp.sum(cnt, axis=0)
    gpad = (gcnt + per - 1) // per * per
    gend = jnp.cumsum(gpad)
    gofs = (gend - gpad)[None, :] + jnp.cumsum(cnt, axis=0) - cnt
    nused = gend[-1] // per
    unused = nused + jnp.arange(N_EXPERTS, dtype=I32)
    zstart = jnp.concatenate([jnp.where(gpad > 0, (gend - per) * SEG_ALIGN, -1),
                              jnp.where(unused < nblk, unused * BLK, -1)]).astype(I32)
    bstart = jnp.minimum(jnp.arange(nblk, dtype=I32), nused - 1) * per
    blk_e = jnp.minimum(jnp.sum((gend[None, :] <= bstart[:, None]).astype(I32), axis=1), N_EXPERTS - 1)
    flat = lambda t: t.reshape(-1).astype(I32)
    return (flat(cnt), flat(lofs), flat(gofs), flat(ltot), lslot.astype(I32), zstart, flat(blk_e),
            nused.astype(I32).reshape(1))


def _segment_copies(i, cnt_ref, lofs_ref, gofs_ref, make_copy):
    for e in range(N_EXPERTS):
        c = cnt_ref[i * N_EXPERTS + e]
        lo = lofs_ref[i * N_EXPERTS + e]
        go = gofs_ref[i * N_EXPERTS + e]

        @pl.when(c > 0)
        def _():
            make_copy(pl.multiple_of(lo * SEG_ALIGN, SEG_ALIGN), pl.multiple_of(go * SEG_ALIGN, SEG_ALIGN),
                      pl.multiple_of(c * SEG_ALIGN, SEG_ALIGN)).start()


def _wait_rows(tot, make_copy):
    for k in range(LOC_BITS):
        @pl.when(((tot >> k) & 1) == 1)
        def _():
            make_copy(0, 0, SEG_ALIGN << k).wait()


def _dispatch_kernel(cnt_ref, lofs_ref, gofs_ref, ltot_ref, zstart_ref, x_ref, ls_ref, xs_hbm, xloc, zbuf, sem):
    i = pl.program_id(0)
    last = pl.num_programs(0) - 1
    slot = i % 2

    def copy_from(s):
        return lambda lo, go, n: pltpu.make_async_copy(xloc.at[s, pl.ds(lo, n)], xs_hbm.at[pl.ds(go, n)], sem.at[s])

    @pl.when(i == 0)
    def _():
        zbuf[...] = jnp.zeros_like(zbuf)
        for phase in ("start", "wait"):
            for e in range(zstart_ref.shape[0]):
                @pl.when(zstart_ref[e] >= 0)
                def _():
                    cp = pltpu.make_async_copy(zbuf, xs_hbm.at[pl.ds(pl.multiple_of(zstart_ref[e], BLK), BLK)],
                                               sem.at[2])
                    getattr(cp, phase)()

    @pl.when(i >= 2)
    def _():
        _wait_rows(ltot_ref[jnp.maximum(i - 2, 0)], copy_from(slot))

    ls = ls_ref[0]
    j = lax.broadcasted_iota(I32, (LOC_ROWS, TD), 0)
    onehot = jnp.where((j == ls[0:1, :]) | (j == ls[1:2, :]), 1.0, 0.0).astype(BF16)
    xloc[slot] = _dot(onehot, x_ref[...].astype(BF16))
    _segment_copies(i, cnt_ref, lofs_ref, gofs_ref, copy_from(slot))

    @pl.when(i == last)
    def _():
        _wait_rows(ltot_ref[i], copy_from(slot))

        @pl.when(i >= 1)
        def _():
            _wait_rows(ltot_ref[jnp.maximum(i - 1, 0)], copy_from(1 - slot))


def _dispatch(x1, cnt, lofs, gofs, ltot, zstart, lslot, nblk):
    m = x1.shape[0]
    nt = m // TD
    ls_rows = jnp.pad(lslot.reshape(nt, TD, 2).transpose(0, 2, 1), ((0, 0), (0, 6), (0, 0)), constant_values=-1)
    grid_spec = pltpu.PrefetchScalarGridSpec(
        num_scalar_prefetch=5,
        grid=(nt,),
        in_specs=[pl.BlockSpec((TD, D_MODEL), lambda i, *_: (i, 0)),
                  pl.BlockSpec((1, 8, TD), lambda i, *_: (i, 0, 0))],
        out_specs=pl.BlockSpec(memory_space=pl.ANY),
        scratch_shapes=[pltpu.VMEM((2, LOC_ROWS, D_MODEL), F32), pltpu.VMEM((BLK, D_MODEL), F32),
                        pltpu.SemaphoreType.DMA((3,))],
    )
    return pl.pallas_call(
        _dispatch_kernel,
        grid_spec=grid_spec,
        out_shape=jax.ShapeDtypeStruct((nblk * BLK, D_MODEL), F32),
        compiler_params=_cparams(("arbitrary",)),
        name="dispatch",
    )(cnt, lofs, gofs, ltot, zstart, x1, ls_rows)


def _experts_kernel(blk_e_ref, nused_ref, xs_ref, wg_ref, wu_ref, wd_ref, ys_ref, wgb, wub, wdb):
    b = pl.program_id(0)
    used = b < nused_ref[0]
    new_expert = (b == 0) | (blk_e_ref[b] != blk_e_ref[jnp.maximum(b - 1, 0)])

    @pl.when(used & new_expert)
    def _():
        _cast_into(wgb, wg_ref)
        _cast_into(wub, wu_ref)
        _cast_into(wdb, wd_ref)

    @pl.when(used)
    def _():
        xb = xs_ref[...].astype(BF16)
        g = _dot(xb, wgb[...])
        u = _dot(xb, wub[...])
        ys_ref[...] = _dot((g * jax.nn.sigmoid(g) * u).astype(BF16), wdb[...])

    @pl.when(jnp.logical_not(used))
    def _():
        ys_ref[...] = jnp.zeros_like(ys_ref)


def _experts(xs, blk_e, nused, w_eg, w_eu, w_ed, layer):
    nb = xs.shape[0] // BLK
    wspec = lambda s: pl.BlockSpec((None, None) + s, lambda b, be, nu: (layer, be[b], 0, 0))
    grid_spec = pltpu.PrefetchScalarGridSpec(
        num_scalar_prefetch=2,
        grid=(nb,),
        in_specs=[pl.BlockSpec((BLK, D_MODEL), lambda b, be, nu: (jnp.minimum(b, nu[0] - 1), 0)),
                  wspec((D_MODEL, D_EXPERT)), wspec((D_MODEL, D_EXPERT)), wspec((D_EXPERT, D_MODEL))],
        out_specs=pl.BlockSpec((BLK, D_MODEL), lambda b, be, nu: (b, 0)),
        scratch_shapes=[pltpu.VMEM((D_MODEL, D_EXPERT), BF16), pltpu.VMEM((D_MODEL, D_EXPERT), BF16),
                        pltpu.VMEM((D_EXPERT, D_MODEL), BF16)],
    )
    return pl.pallas_call(
        _experts_kernel,
        grid_spec=grid_spec,
        out_shape=jax.ShapeDtypeStruct(xs.shape, F32),
        compiler_params=_cparams(("arbitrary",)),
        name="experts",
    )(blk_e, nused, xs, w_eg, w_eu, w_ed)


def _post_kernel(cnt_ref, lofs_ref, gofs_ref, ltot_ref, x1_ref, route_ref, ls_ref, pp_ref, ps_ref, wpg_ref, wpp_ref,
                 lng_ref, lnb_ref, ys_hbm, o_ref, yloc, wpg_b, wpp_b, sem, *, prompt_tiles):
    i = pl.program_id(0)
    last = pl.num_programs(0) - 1
    slot = i % 2

    def copy_to(s):
        return lambda lo, go, n: pltpu.make_async_copy(ys_hbm.at[pl.ds(go, n)], yloc.at[s, pl.ds(lo, n)], sem.at[s])

    @pl.when(i == 0)
    def _():
        _cast_into(wpg_b, wpg_ref)
        _cast_into(wpp_b, wpp_ref)
        yloc[...] = jnp.zeros_like(yloc)
        _segment_copies(0, cnt_ref, lofs_ref, gofs_ref, copy_to(0))

    _wait_rows(ltot_ref[i], copy_to(slot))

    @pl.when(i < last)
    def _():
        _segment_copies(i + 1, cnt_ref, lofs_ref, gofs_ref, copy_to(1 - slot))

    route = route_ref[...]
    ls = ls_ref[...]
    j = lax.broadcasted_iota(I32, (ls.shape[0], LOC_ROWS), 1)
    w = jnp.where(j == ls[:, 0:1], route[:, 2:3], 0.0) + jnp.where(j == ls[:, 1:2], route[:, 3:4], 0.0)
    moe = _dot(w.astype(BF16), yloc[slot].astype(BF16))
    x2 = _layer_norm(DN_ALPHA * x1_ref[...] + moe, lng_ref[1:2, :], lnb_ref[1:2, :])
    p = jnp.where(i < prompt_tiles, pp_ref[...], ps_ref[...])
    ple = jax.nn.sigmoid(_dot(x2.astype(BF16), wpg_b[...])) * _dot(p.astype(BF16), wpp_b[...])
    o_ref[...] = _layer_norm(DN_ALPHA * x2 + ple, lng_ref[2:3, :], lnb_ref[2:3, :])


def _post(x1, ys, cnt, lofs, gofs, ltot, lslot, route, p_prompt, p_sample, w_pg, w_pp, ln_g, ln_b, layer):
    m = x1.shape[0]
    pt = p_prompt.shape[1] // TD
    rowblk = lambda w: pl.BlockSpec((TD, w), lambda i, *_: (i, 0))
    lspec = lambda s: pl.BlockSpec((None,) + s, lambda i, *_: (layer,) + (0,) * len(s), pipeline_mode=pl.Buffered(1))
    grid_spec = pltpu.PrefetchScalarGridSpec(
        num_scalar_prefetch=4,
        grid=(m // TD,),
        in_specs=[rowblk(D_MODEL), rowblk(LANES), rowblk(2),
                  pl.BlockSpec((None, TD, D_PLE), lambda i, *_: (layer, jnp.minimum(i, pt - 1), 0)),
                  pl.BlockSpec((None, TD, D_PLE), lambda i, *_: (layer, jnp.maximum(i - pt, 0), 0)),
                  lspec((D_MODEL, D_MODEL)), lspec((D_PLE, D_MODEL)), lspec((3, D_MODEL)), lspec((3, D_MODEL)),
                  pl.BlockSpec(memory_space=pl.ANY)],
        out_specs=rowblk(D_MODEL),
        scratch_shapes=[pltpu.VMEM((2, LOC_ROWS, D_MODEL), F32), pltpu.VMEM((D_MODEL, D_MODEL), BF16),
                        pltpu.VMEM((D_PLE, D_MODEL), BF16), pltpu.SemaphoreType.DMA((2,))],
    )
    return pl.pallas_call(
        functools.partial(_post_kernel, prompt_tiles=pt),
        grid_spec=grid_spec,
        out_shape=jax.ShapeDtypeStruct((m, D_MODEL), F32),
        compiler_params=_cparams(("arbitrary",)),
        name="post",
    )(cnt, lofs, gofs, ltot, x1, route, lslot, p_prompt, p_sample, w_pg, w_pp, ln_g, ln_b, ys)


def kernel(x_prompt, x_sample, state_mlstm_C, state_mlstm_n, state_mlstm_m, state_swa_k, state_swa_v, p_prompt, p_sample, w_in, b_in, mh_gain, w_a, w_b, w_out, rel_table, w_sink, ln_g, ln_b, w_rg, b_rg, w_re, b_re, w_eg, w_eu, w_ed, w_pg, w_pp):
    bp_, seq = x_prompt.shape[:2]
    bs_, tdec = x_sample.shape[:2]
    mp = bp_ * seq
    msp = bs_ * T_PAD
    mall = mp + msp
    nblk = -(-(2 * (mp + bs_ * tdec) + (mall // TD) * N_EXPERTS * (SEG_ALIGN - 1)) // BLK) + N_EXPERTS

    pad_t = lambda t: jnp.pad(t, [(0, 0)] * (t.ndim - 2) + [(0, T_PAD - tdec), (0, 0)])
    x = lax.dynamic_update_slice(jnp.pad(x_prompt.reshape(mp, D_MODEL), ((0, msp), (0, 0))),
                                 pad_t(x_sample).reshape(msp, D_MODEL), (mp, 0))
    valid = jnp.concatenate([jnp.ones((mp,), bool), jnp.tile(jnp.arange(T_PAD) < tdec, bs_)])
    p_prompt = p_prompt.reshape(DEPTH, mp, D_PLE)
    p_sample = pad_t(p_sample).reshape(DEPTH, msp, D_PLE)

    b_in3 = b_in.reshape(DEPTH, 1, -1)
    w_qkv2, b_qkv2 = w_in[:, :, IN_QKV2[0]:IN_QKV2[1]], b_in3[:, :, IN_QKV2[0]:IN_QKV2[1]]
    w_gates, b_gates = w_in[:, :, IN_GATES[0]:IN_GATES[1]], b_in3[:, :, IN_GATES[0]:IN_GATES[1]]
    b_rg, b_re = b_rg.reshape(DEPTH, 1, N_GROUPS), b_re.reshape(DEPTH, 1, N_EXPERTS)
    gain = mh_gain.reshape(DEPTH, 1, D_MODEL)

    bias_p = _swa_bias(rel_table, WINDOW)
    bias_s = _swa_bias(rel_table, T_PAD)
    sink_p = sink_s = _sink_lanes(w_sink)
    buf_k = state_swa_k.reshape(DEPTH, bs_, WINDOW, D_KV)
    buf_v = state_swa_v.reshape(DEPTH, bs_, WINDOW, D_KV)
    m0_s = jnp.broadcast_to(state_mlstm_m[..., None], (DEPTH, bs_, M_HEADS, LANES))
    zc = jnp.zeros((1, bp_, M_HEADS, M_DQK, M_DV), F32)
    zn = jnp.zeros((1, bp_, M_HEADS, LANES), F32)
    st_s = (jnp.zeros_like(state_mlstm_C), jnp.zeros_like(state_mlstm_n), jnp.zeros_like(m0_s))
    kv_s = (jnp.zeros_like(buf_k), jnp.zeros_like(buf_v))

    st_p = []
    for i in range(DEPTH):
        za, zg = _inproj(x, [(w_in, IN_A)], [(b_in3, IN_A)], [(ZG, BF16), (LANES, F32)], i, "inproj_a")
        zb, = _inproj(x, [(w_gates, 2048), (w_qkv2, 1536)], [(b_gates, 2048), (b_qkv2, 1536)], [(IN_B, BF16)], i,
                      "inproj_b")

        ya_p, c_p, n_p, m_p = _mlstm(za, zg, gain, i, zc, zn, zn, 0, None, 1, 0, row0=0, nseq=bp_,
                                     steps=seq // LANES, seg=LANES, t_valid=LANES)
        ya_s, *st_s = _mlstm(za, zg, gain, i, state_mlstm_C, state_mlstm_n, m0_s, i, st_s, DEPTH, i, row0=mp,
                             nseq=bs_ // SEQ_BLK, steps=1, seg=T_PAD, t_valid=tdec)
        yb_p = _swa_prompt(zb, bias_p[0], bias_p[1], sink_p, i, batch=bp_, seq=seq)
        yb_s, *kv_s = _swa_sample(zb, buf_k, buf_v, bias_s[0], bias_s[1], sink_s, i, kv_s, row0=mp, batch=bs_,
                                  t_new=tdec)

        x1, route = _merge(x, ya_p, ya_s, yb_p, yb_s, zb, w_a, w_b, w_out, ln_g, ln_b, w_rg, w_re, b_rg, b_re, i)
        cnt, lofs, gofs, ltot, lslot, zstart, blk_e, nused = _route_tables(route[:, :2].astype(I32), valid, nblk)
        xs = _dispatch(x1, cnt, lofs, gofs, ltot, zstart, lslot, nblk)
        ys = _experts(xs, blk_e, nused, w_eg, w_eu, w_ed, i)
        x = _post(x1, ys, cnt, lofs, gofs, ltot, lslot, route, p_prompt, p_sample, w_pg, w_pp, ln_g, ln_b, i)

        kv_p = jnp.stack([lax.slice(zb, ((b + 1) * seq - WINDOW, ZKS), ((b + 1) * seq, IN_B)) for b in range(bp_)])
        kv_p = kv_p.astype(F32)
        heads = lambda t: t.reshape(t.shape[:-1] + (S_KV_HEADS, S_HEAD_DIM))
        st_p.append((c_p[0], n_p[0], m_p[0, :, :, 0], heads(kv_p[..., :D_KV]), heads(kv_p[..., D_KV:])))

    stk = lambda j: jnp.stack([s[j] for s in st_p], axis=0)
    y_p = x[:mp].reshape(bp_, seq, D_MODEL)
    y_s = x[mp:].reshape(bs_, T_PAD, D_MODEL)[:, :tdec]
    return (y_p, y_s, stk(0), stk(1), stk(2), stk(3), stk(4),
            st_s[0], st_s[1], st_s[2][..., 0], heads(kv_s[0]), heads(kv_s[1]))
```

```python
import functools

import numpy as np
import jax
import jax.numpy as jnp
from jax import lax
from jax.experimental import pallas as pl
from jax.experimental.pallas import tpu as pltpu

F32 = jnp.float32
BF16 = jnp.bfloat16
I32 = jnp.int32
HIGHEST = lax.Precision.HIGHEST

D_MODEL = 1024
DEPTH = 4
D_PLE = 256
M_HEADS = 4
M_DQK = 128
M_DV = 256
S_HEADS = 16
S_KV_HEADS = 4
S_HEAD_DIM = 64
S_GROUP = 4
D_KV = S_KV_HEADS * S_HEAD_DIM
WINDOW = 128
REL_BUCKETS = 32
N_GROUPS = 4
EXPERTS_PER_GROUP = 8
N_EXPERTS = 32
D_EXPERT = 512
DN_ALPHA = (2 * DEPTH) ** 0.25
LN_EPS = 1e-5
NEG = -1e30

IN_A = 3200
IN_QKV2 = (3080, 4616)
IN_GATES = (4616, 6664)
ZQ, ZK, ZV, ZOG, ZG = 0, 512, 1024, 2048, 3072
IN_B = 3584
ZGA, ZGB, ZQS, ZKS, ZVS = 0, 1024, 2048, 3072, 3328

LANES = 128
T_PAD = 8
SEQ_BLK = LANES // T_PAD
TM = 512
TD = 256
BLK = 512
VMEM_LIMIT = 56 * 1024 * 1024


def _cparams(sem):
    return pltpu.CompilerParams(dimension_semantics=sem, vmem_limit_bytes=VMEM_LIMIT)


def _layer_spec(shape, layer):
    nd = len(shape)
    return pl.BlockSpec((None,) + shape, lambda *_: (layer,) + (0,) * nd, pipeline_mode=pl.Buffered(1))


def _layer_norm(y, g, b):
    mu = jnp.mean(y, axis=-1, keepdims=True)
    d = y - mu
    var = jnp.mean(d * d, axis=-1, keepdims=True)
    return d * lax.rsqrt(var + LN_EPS) * g + b


def _dot(a, b):
    return jnp.dot(a, b, preferred_element_type=F32)


def _dot_nt(a, b):
    return lax.dot_general(a, b, (((1,), (1,)), ((), ())), preferred_element_type=F32)


def _dot_tn(a, b):
    return lax.dot_general(a, b, (((0,), (0,)), ((), ())), preferred_element_type=F32)


def _dot_exact(a, b):
    return jnp.dot(a, b, precision=HIGHEST, preferred_element_type=F32)


def _cast_into(dst_ref, src_ref, col0=0, step=512):
    n = src_ref.shape[-1]
    for j in range(0, n, step):
        w = min(step, n - j)
        dst_ref[:, col0 + j:col0 + j + w] = src_ref[:, j:j + w].astype(BF16)


def _inproj_kernel(*refs, widths, out_widths):
    ng, no = len(widths), len(out_widths)
    x_ref, w_refs, b_refs = refs[0], refs[1:1 + ng], refs[1 + ng:1 + 2 * ng]
    o_refs, wbf = refs[1 + 2 * ng:1 + 2 * ng + no], refs[1 + 2 * ng + no]

    @pl.when(pl.program_id(0) == 0)
    def _():
        c = 0
        for w_ref, n in zip(w_refs, widths):
            _cast_into(wbf, w_ref, c)
            c += n

    out_start = np.cumsum((0,) + tuple(out_widths))

    def store(col, val):
        k = int(np.searchsorted(out_start, col, side='right')) - 1
        lo = col - int(out_start[k])
        o_refs[k][:, lo:lo + val.shape[1]] = val.astype(o_refs[k].dtype)

    xb = x_ref[...].astype(BF16)
    c = 0
    for b_ref, n in zip(b_refs, widths):
        for j in range(0, n, 512):
            w = min(512, n - j)
            store(c + j, _dot(xb, wbf[:, c + j:c + j + w]) + b_ref[:, j:j + w])
        c += n


def _inproj(x, ws, bs, outs, layer, name):
    m = x.shape[0]
    widths = tuple(n for _, n in ws)
    out_widths = tuple(n for n, _ in outs)
    assert sum(widths) == sum(out_widths)
    return pl.pallas_call(
        functools.partial(_inproj_kernel, widths=widths, out_widths=out_widths),
        grid=(m // TM,),
        in_specs=([pl.BlockSpec((TM, D_MODEL), lambda i: (i, 0))]
                  + [_layer_spec((D_MODEL, n), layer) for _, n in ws]
                  + [_layer_spec((1, n), layer) for _, n in bs]),
        out_specs=[pl.BlockSpec((TM, n), lambda i: (i, 0)) for n, _ in outs],
        out_shape=[jax.ShapeDtypeStruct((m, n), dt) for n, dt in outs],
        scratch_shapes=[pltpu.VMEM((D_MODEL, sum(widths)), BF16)],
        compiler_params=_cparams(("arbitrary",)),
        name=name,
    )(x, *[w for w, _ in ws], *[b for b, _ in bs])


def _log_sigmoid(x):
    return jnp.minimum(x, 0.0) - jnp.log(1.0 + jnp.exp(-jnp.abs(x)))


def _mlstm_kernel(zq, zk, zv, zog, zg, gain, c0, n0, m0, *rest, seg, t_valid, aliased):
    ya, co, no, mo, qc_s = rest[aliased:]
    L = zq.shape[0]
    nseg = L // seg

    @pl.when(pl.program_id(1) == 0)
    def _():
        co[...] = c0[...]
        no[...] = n0[...]
        mo[...] = m0[...]

    g_c = zg[...]
    g_r = g_c.T
    ig_c, lf_c = g_c, _log_sigmoid(g_c)
    ig_r, lf_r = g_r, _log_sigmoid(g_r)
    ri = lax.broadcasted_iota(I32, (L, L), 0)
    ci = lax.broadcasted_iota(I32, (L, L), 1)
    shift = seg.bit_length() - 1
    if t_valid < seg:
        rv = (lax.broadcasted_iota(I32, (L, 1), 0) & (seg - 1)) < t_valid
        cv = (lax.broadcasted_iota(I32, (1, L), 1) & (seg - 1)) < t_valid
        ig_c, lf_c = jnp.where(rv, ig_c, NEG), jnp.where(rv, lf_c, 0.0)
        ig_r, lf_r = jnp.where(cv, ig_r, NEG), jnp.where(cv, lf_r, 0.0)
    if nseg > 1:
        same = (ri >> shift) == (ci >> shift)
        causal, causal_t = same & (ri >= ci), same & (ri <= ci)
        same_f = same.astype(F32)
        be_c = _dot_exact(same_f, lf_c)
        be_r = _dot_exact(lf_r, same_f)
        expand = ((lax.broadcasted_iota(I32, (L, nseg), 0) >> shift)
                  == lax.broadcasted_iota(I32, (L, nseg), 1)).astype(F32)
    else:
        causal, causal_t = ri >= ci, ri <= ci
    b_c = _dot_exact(causal.astype(F32), lf_c)
    b_r = _dot_exact(lf_r, causal_t.astype(F32))

    for h in range(M_HEADS):
        icol, irow = ig_c[:, h:h + 1], ig_r[h:h + 1, :]
        bcol, brow = b_c[:, 4 + h:5 + h], b_r[4 + h:5 + h, :]
        q = zq[:, h * M_DQK:(h + 1) * M_DQK].astype(F32) * (M_DQK ** -0.5)
        k = zk[:, h * M_DQK:(h + 1) * M_DQK].astype(F32)
        qb, kb = q.astype(BF16), zk[:, h * M_DQK:(h + 1) * M_DQK]
        vb = zv[:, h * M_DV:(h + 1) * M_DV]
        if nseg > 1:
            mcol = _dot_exact(expand, mo[:, h, :])[:, 0:1]
            nrow = _dot_exact(expand, no[:, h, :])
            becol, berow = be_c[:, 4 + h:5 + h], be_r[4 + h:5 + h, :]
            for j in range(nseg):
                rows = slice(j * seg, (j + 1) * seg)
                qc_s[rows, :] = _dot(q[rows].astype(BF16), co[j, h].astype(BF16))
            q_c = qc_s[...]
        else:
            mcol, nrow = mo[0, h:h + 1, 0:1], no[0, h:h + 1, :]
            becol = bcol[L - 1:L, :]
            q_c = _dot(qb, co[0, h].astype(BF16))

        dmat = jnp.where(causal, bcol - brow + irow, NEG)
        inter = bcol + mcol
        mhat = jnp.maximum(inter, jnp.max(dmat, axis=-1, keepdims=True))
        w_intra = jnp.exp(dmat - mhat)
        w_inter = jnp.exp(inter - mhat)
        s = _dot_nt(qb, kb) * w_intra
        num = _dot(s.astype(BF16), vb) + w_inter * q_c
        den = jnp.sum(s, axis=-1, keepdims=True) + w_inter * jnp.sum(q * nrow, axis=-1, keepdims=True)
        hh = num / jnp.maximum(jnp.abs(den), jnp.exp(-mhat))

        gcol = icol + becol - bcol
        if nseg > 1:
            gmax = jnp.max(jnp.where(same, irow + berow - brow, NEG), axis=-1, keepdims=True)
        else:
            gmax = jnp.max(gcol, axis=0, keepdims=True)
        m_new = jnp.maximum(becol + mcol, gmax)
        decay = jnp.exp(becol + mcol - m_new)
        kw = k * jnp.exp(gcol - m_new)
        vf = vb.astype(F32) if nseg > 1 else vb
        for j in range(nseg):
            rows = slice(j * seg, (j + 1) * seg)
            dj = decay[j * seg:j * seg + 1, :] if nseg > 1 else decay
            mj = m_new[j * seg:j * seg + 1, :] if nseg > 1 else m_new
            co[j, h] = dj * co[j, h] + _dot_tn(kw[rows].astype(BF16), vf[rows].astype(BF16))
            no[j, h:h + 1, :] = dj * no[j, h:h + 1, :] + jnp.sum(kw[rows], axis=0, keepdims=True)
            mo[j, h:h + 1, :] = jnp.broadcast_to(mj, (1, LANES))

        mu = jnp.mean(hh, axis=-1, keepdims=True)
        d = hh - mu
        var = jnp.mean(d * d, axis=-1, keepdims=True)
        hn = d * lax.rsqrt(var + LN_EPS) * gain[:, h * M_DV:(h + 1) * M_DV]
        og = zog[:, h * M_DV:(h + 1) * M_DV].astype(F32)
        ya[:, h * M_DV:(h + 1) * M_DV] = (hn * jax.nn.sigmoid(og)).astype(ya.dtype)


def _mlstm(za, zg, gain, layer, c0, n0, m0, st_layer, outs_prev, out_layers, out_layer, *, row0, nseq, steps,
           seg, t_valid):
    nseg = LANES // seg
    r0 = row0 // LANES
    row = lambda b, c: r0 + b * steps + c
    nstate = nseq * nseg
    aliased = 0 if outs_prev is None else 3
    kern = functools.partial(_mlstm_kernel, seg=seg, t_valid=min(t_valid, seg), aliased=aliased)
    in4 = pl.BlockSpec((None, nseg, M_HEADS, M_DQK, M_DV), lambda b, c: (st_layer, b, 0, 0, 0))
    in3 = pl.BlockSpec((None, nseg, M_HEADS, LANES), lambda b, c: (st_layer, b, 0, 0))
    out4 = pl.BlockSpec((None, nseg, M_HEADS, M_DQK, M_DV), lambda b, c: (out_layer, b, 0, 0, 0))
    out3 = pl.BlockSpec((None, nseg, M_HEADS, LANES), lambda b, c: (out_layer, b, 0, 0))
    anyspec = pl.BlockSpec(memory_space=pl.ANY)
    n_in = 9
    return pl.pallas_call(
        kern,
        grid=(nseq, steps),
        in_specs=[pl.BlockSpec((LANES, 512), lambda b, c: (row(b, c), ZQ // 512)),
                  pl.BlockSpec((LANES, 512), lambda b, c: (row(b, c), ZK // 512)),
                  pl.BlockSpec((LANES, 1024), lambda b, c: (row(b, c), ZV // 1024)),
                  pl.BlockSpec((LANES, 1024), lambda b, c: (row(b, c), ZOG // 1024)),
                  pl.BlockSpec((LANES, LANES), lambda b, c: (row(b, c), 0)),
                  pl.BlockSpec((None, 1, D_MODEL), lambda b, c: (layer, 0, 0)),
                  in4, in3, in3] + [anyspec] * aliased,
        out_specs=[pl.BlockSpec((LANES, D_MODEL), lambda b, c: (b * steps + c, 0)), out4, out3, out3],
        out_shape=[jax.ShapeDtypeStruct((nseq * steps * LANES, D_MODEL), BF16),
                   jax.ShapeDtypeStruct((out_layers, nstate, M_HEADS, M_DQK, M_DV), F32),
                   jax.ShapeDtypeStruct((out_layers, nstate, M_HEADS, LANES), F32),
                   jax.ShapeDtypeStruct((out_layers, nstate, M_HEADS, LANES), F32)],
        scratch_shapes=[pltpu.VMEM((LANES, M_DV), F32)],
        input_output_aliases={n_in + j: 1 + j for j in range(aliased)},
        compiler_params=_cparams(("parallel", "arbitrary")),
        name="mlstm",
    )(za, za, za, za, zg, gain, c0, n0, m0, *(outs_prev or ()))


def _swa_kernel(q_ref, kc_ref, vc_ref, kp_ref, vp_ref, bp_ref, bc_ref, sink_ref, *rest, nseq, t_new,
                first_block_has_no_past, aliased):
    o_ref = rest[aliased]
    rows_n = q_ref.shape[0]
    tq = rows_n // nseq
    have_past = pl.program_id(1) > 0
    lane = lax.broadcasted_iota(I32, (1, LANES), 1)
    in_lo = lane < S_HEAD_DIM
    if nseq > 1:
        kp_all = kp_ref[...].reshape(nseq * WINDOW, D_KV).astype(BF16)
        vp_all = vp_ref[...].reshape(nseq * WINDOW, D_KV).astype(BF16)
        row_seq = lax.broadcasted_iota(I32, (rows_n, LANES), 0) >> (tq.bit_length() - 1)
    else:
        kp_all, vp_all = kp_ref[...], vp_ref[...]
    kc_all, vc_all = kc_ref[...], vc_ref[...]

    def placed(x, src_half, ones):
        swapped = jnp.concatenate([x[:, S_HEAD_DIM:], x[:, :S_HEAD_DIM]], axis=1)
        zero = jnp.zeros_like(x)
        lo = jnp.where(in_lo, x if src_half == 0 else swapped, zero)
        hi = jnp.where(in_lo, zero, x if src_half == 1 else swapped)
        if ones:
            lo = jnp.where(lane == S_HEAD_DIM, jnp.ones_like(x), lo)
            hi = jnp.where(lane == 0, jnp.ones_like(x), hi)
        return lo, hi

    kv_per_pass = S_KV_HEADS if nseq == 1 else 1
    for kv0 in range(0, S_KV_HEADS, kv_per_pass):
        kp_v, kc_v, vp_v, vc_v = {}, {}, {}, {}
        for kvh in range(kv0, kv0 + kv_per_pass):
            tile = slice((kvh // 2) * LANES, (kvh // 2 + 1) * LANES)
            kp_v[kvh] = placed(kp_all[:, tile], kvh % 2, False)
            kc_v[kvh] = placed(kc_all[:, tile], kvh % 2, False)
            vp_v[kvh] = placed(vp_all[:, tile], kvh % 2, True)
            vc_v[kvh] = placed(vc_all[:, tile], kvh % 2, True)
        heads = list(range(kv0 * S_GROUP, (kv0 + kv_per_pass) * S_GROUP))
        scores = []
        for h in heads:
            q2 = q_ref[:, (h // 2) * LANES:(h // 2 + 1) * LANES]
            scores.append((_dot_nt(q2, kp_v[h // S_GROUP][h % 2]), _dot_nt(q2, kc_v[h // S_GROUP][h % 2])))
        probs = []
        for h, (sp, sc) in zip(heads, scores):
            if nseq > 1:
                sp = jnp.concatenate([sp[b * tq:(b + 1) * tq, b * WINDOW:(b + 1) * WINDOW] for b in range(nseq)],
                                     axis=0)
            sp = sp * (S_HEAD_DIM ** -0.5) + bp_ref[h]
            sc = sc * (S_HEAD_DIM ** -0.5) + bc_ref[h]
            if first_block_has_no_past:
                sp = jnp.where(have_past, sp, NEG)
            sk = sink_ref[h][:, 0:1]
            mx = jnp.maximum(jnp.max(jnp.maximum(sp, sc), axis=-1, keepdims=True), sk)
            pp, pc = jnp.exp(sp - mx), jnp.exp(sc - mx)
            if nseq > 1:
                pp = jnp.concatenate([jnp.where(row_seq == b, pp, 0.0) for b in range(nseq)], axis=1)
            probs.append((pp.astype(BF16), pc.astype(BF16), jnp.exp(sk - mx)))
        outs = [_dot(pp, vp_v[h // S_GROUP][h % 2]) + _dot(pc, vc_v[h // S_GROUP][h % 2])
                for h, (pp, pc, _) in zip(heads, probs)]
        for pair in range(len(heads) // 2):
            halves = []
            for half in range(2):
                o, sink_term = outs[2 * pair + half], probs[2 * pair + half][2]
                ones_lane = S_HEAD_DIM if half == 0 else 0
                halves.append(o / (o[:, ones_lane:ones_lane + 1] + sink_term))
            qt = heads[2 * pair] // 2
            o_ref[:, qt * LANES:(qt + 1) * LANES] = jnp.where(in_lo, halves[0], halves[1]).astype(o_ref.dtype)

    if aliased:
        kc_f, vc_f = kc_all.astype(F32), vc_all.astype(F32)
        for b in range(nseq):
            for new, prev, out in ((kc_f, kp_ref, rest[aliased + 1]), (vc_f, vp_ref, rest[aliased + 2])):
                out[b, 0:WINDOW - t_new, :] = prev[b, t_new:, :]
                out[b, WINDOW - t_new:WINDOW, :] = new[b * tq:b * tq + t_new, :]


def _rel_bucket(dist):
    nn = np.maximum(dist, 0)
    max_exact = REL_BUCKETS // 2
    large = max_exact + (np.log(np.maximum(nn, 1) / max_exact) / np.log(WINDOW / max_exact)
                         * (REL_BUCKETS - max_exact)).astype(np.int32)
    large = np.minimum(large, REL_BUCKETS - 1)
    return np.where(nn < max_exact, nn, large).astype(np.int32)


def _swa_bias(rel_table, tq):
    nseq = LANES // tq
    t = np.tile(np.arange(tq), nseq)[:, None]
    seq = np.repeat(np.arange(nseq), tq)
    d_prev = t + WINDOW - np.arange(WINDOW)[None, :]
    d_cur = t - t.T
    own = seq[:, None] == seq[None, :]

    def build(dist, ok):
        onehot = np.eye(REL_BUCKETS, dtype=np.float32)[_rel_bucket(dist).reshape(-1)]
        bias = jnp.dot(onehot, rel_table.astype(F32), precision=HIGHEST).reshape(dist.shape + (S_HEADS,))
        return jnp.where(ok[None], jnp.transpose(bias, (2, 0, 1)), NEG)

    return build(d_prev, d_prev <= WINDOW), build(d_cur, own & (d_cur >= 0))


def _sink_lanes(w_sink):
    return jnp.broadcast_to(w_sink.astype(F32)[:, :, None, None], (DEPTH, S_HEADS, 1, LANES))


def _swa_bias_specs(layer):
    return [pl.BlockSpec((S_HEADS, LANES, WINDOW), lambda b, n: (0, 0, 0)),
            pl.BlockSpec((S_HEADS, LANES, LANES), lambda b, n: (0, 0, 0)),
            pl.BlockSpec((None, S_HEADS, 1, LANES), lambda b, n: (layer, 0, 0, 0))]


def _swa_prompt(zb, bp, bc, sk, layer, *, batch, seq):
    nb = seq // WINDOW
    row = lambda b, n: b * nb + n
    prev = lambda b, n: jnp.maximum(b * nb + n - 1, 0)
    in_specs = [pl.BlockSpec((WINDOW, 1024), lambda b, n: (row(b, n), ZQS // 1024)),
                pl.BlockSpec((WINDOW, D_KV), lambda b, n: (row(b, n), ZKS // D_KV)),
                pl.BlockSpec((WINDOW, D_KV), lambda b, n: (row(b, n), ZVS // D_KV)),
                pl.BlockSpec((WINDOW, D_KV), lambda b, n: (prev(b, n), ZKS // D_KV)),
                pl.BlockSpec((WINDOW, D_KV), lambda b, n: (prev(b, n), ZVS // D_KV))]
    kern = functools.partial(_swa_kernel, nseq=1, t_new=0, first_block_has_no_past=True, aliased=0)
    return pl.pallas_call(
        kern,
        grid=(batch, nb),
        in_specs=in_specs + _swa_bias_specs(layer),
        out_specs=pl.BlockSpec((WINDOW, D_MODEL), lambda b, n: (row(b, n), 0)),
        out_shape=jax.ShapeDtypeStruct((batch * seq, D_MODEL), BF16),
        compiler_params=_cparams(("parallel", "arbitrary")),
        name="swa_prompt",
    )(zb, zb, zb, zb, zb, bp, bc, sk)


def _swa_sample(zb, buf_k, buf_v, bp, bc, sk, layer, kv_prev, *, row0, batch, t_new):
    r0 = row0 // LANES
    steps = batch // SEQ_BLK
    state = pl.BlockSpec((None, SEQ_BLK, WINDOW, D_KV), lambda b, n: (layer, b, 0, 0))
    anyspec = pl.BlockSpec(memory_space=pl.ANY)
    in_specs = [pl.BlockSpec((LANES, 1024), lambda b, n: (r0 + b, ZQS // 1024)),
                pl.BlockSpec((LANES, D_KV), lambda b, n: (r0 + b, ZKS // D_KV)),
                pl.BlockSpec((LANES, D_KV), lambda b, n: (r0 + b, ZVS // D_KV)),
                state, state]
    kern = functools.partial(_swa_kernel, nseq=SEQ_BLK, t_new=t_new, first_block_has_no_past=False, aliased=2)
    return pl.pallas_call(
        kern,
        grid=(steps, 1),
        in_specs=in_specs + _swa_bias_specs(layer) + [anyspec, anyspec],
        out_specs=[pl.BlockSpec((LANES, D_MODEL), lambda b, n: (b, 0)), state, state],
        out_shape=[jax.ShapeDtypeStruct((batch * T_PAD, D_MODEL), BF16),
                   jax.ShapeDtypeStruct(buf_k.shape, F32), jax.ShapeDtypeStruct(buf_v.shape, F32)],
        input_output_aliases={8: 1, 9: 2},
        compiler_params=_cparams(("parallel", "arbitrary")),
        name="swa_sample",
    )(zb, zb, zb, buf_k, buf_v, bp, bc, sk, *kv_prev)


def _merge_kernel(x_ref, yap_ref, yas_ref, ybp_ref, ybs_ref, ga_ref, gb_ref, wa_ref, wb_ref, wo_ref, lng_ref, lnb_ref,
                  wrg_ref, wre_ref, brg_ref, bre_ref, x1_ref, route_ref, wbf, wr_b, br_s, *, prompt_tiles):
    i = pl.program_id(0)

    @pl.when(i == 0)
    def _():
        for j, w_ref in enumerate((wa_ref, wb_ref, wo_ref)):
            _cast_into(wbf.at[j], w_ref)
        wr_b[...] = jnp.zeros_like(wr_b)
        wr_b[:, 0:N_GROUPS] = wrg_ref[...].astype(BF16)
        wr_b[:, N_GROUPS:N_GROUPS + N_EXPERTS] = wre_ref[...].astype(BF16)
        br_s[...] = jnp.zeros_like(br_s)
        br_s[:, 0:N_GROUPS] = brg_ref[...]
        br_s[:, N_GROUPS:N_GROUPS + N_EXPERTS] = bre_ref[...]

    is_prompt = i < prompt_tiles
    half = x_ref.shape[0] // 2
    for r0 in (0, half):
        rs = slice(r0, r0 + half)
        ya = jnp.where(is_prompt, yap_ref[rs, :], yas_ref[rs, :])
        yb = jnp.where(is_prompt, ybp_ref[rs, :], ybs_ref[rs, :])
        a = _dot(ya, wbf[0])
        b = _dot(yb, wbf[1])
        u = jax.nn.sigmoid(ga_ref[rs, :].astype(F32)) * a + jax.nn.sigmoid(gb_ref[rs, :].astype(F32)) * b
        mix = _dot(u.astype(BF16), wbf[2])
        x1_ref[rs, :] = _layer_norm(DN_ALPHA * x_ref[rs, :] + mix, lng_ref[0:1, :], lnb_ref[0:1, :])
    x1 = x1_ref[...]

    logits = _dot(x1.astype(BF16), wr_b[...]) + br_s[...]
    lane = lax.broadcasted_iota(I32, logits.shape, 1)
    lanef = lane.astype(F32)
    gmask = lane < N_GROUPS
    gl = jnp.where(gmask, logits, NEG)
    gmax = jnp.max(gl, axis=-1, keepdims=True)
    gidx = jnp.min(jnp.where(gmask & (gl == gmax), lanef, 999.0), axis=-1, keepdims=True)
    g_w = 1.0 / jnp.sum(jnp.where(gmask, jnp.exp(gl - gmax), 0.0), axis=-1, keepdims=True)
    lo = N_GROUPS + EXPERTS_PER_GROUP * gidx
    emask = (lanef >= lo) & (lanef < lo + EXPERTS_PER_GROUP)
    el = jnp.where(emask, logits, NEG)
    emax = jnp.max(el, axis=-1, keepdims=True)
    ep = jnp.where(emask, jnp.exp(el - emax), 0.0)
    prob = ep / jnp.sum(ep, axis=-1, keepdims=True)
    prob = jnp.where(emask, prob, -1.0)
    p1 = jnp.max(prob, axis=-1, keepdims=True)
    i1 = jnp.min(jnp.where(prob == p1, lanef, 999.0), axis=-1, keepdims=True)
    prob2 = jnp.where(lanef == i1, -1.0, prob)
    p2 = jnp.max(prob2, axis=-1, keepdims=True)
    i2 = jnp.min(jnp.where(prob2 == p2, lanef, 999.0), axis=-1, keepdims=True)
    tot = p1 + p2
    route_ref[...] = jnp.where(lane == 0, i1 - N_GROUPS,
                               jnp.where(lane == 1, i2 - N_GROUPS,
                                         jnp.where(lane == 2, g_w * p1 / tot,
                                                   jnp.where(lane == 3, g_w * p2 / tot, 0.0))))


def _merge(x, ya_p, ya_s, yb_p, yb_s, zb, w_a, w_b, w_out, ln_g, ln_b, w_rg, w_re, b_rg, b_re, layer):
    m = x.shape[0]
    pt = ya_p.shape[0] // TM
    rowblk = lambda w, j: pl.BlockSpec((TM, w), lambda i: (i, j))
    pblk = pl.BlockSpec((TM, D_MODEL), lambda i: (jnp.minimum(i, pt - 1), 0))
    sblk = pl.BlockSpec((TM, D_MODEL), lambda i: (jnp.maximum(i - pt, 0), 0))
    return pl.pallas_call(
        functools.partial(_merge_kernel, prompt_tiles=pt),
        grid=(m // TM,),
        in_specs=[rowblk(D_MODEL, 0), pblk, sblk, pblk, sblk,
                  rowblk(D_MODEL, ZGA // D_MODEL), rowblk(D_MODEL, ZGB // D_MODEL),
                  _layer_spec((D_MODEL, D_MODEL), layer), _layer_spec((D_MODEL, D_MODEL), layer),
                  _layer_spec((D_MODEL, D_MODEL), layer),
                  _layer_spec((3, D_MODEL), layer), _layer_spec((3, D_MODEL), layer),
                  _layer_spec((D_MODEL, N_GROUPS), layer), _layer_spec((D_MODEL, N_EXPERTS), layer),
                  _layer_spec((1, N_GROUPS), layer), _layer_spec((1, N_EXPERTS), layer)],
        out_specs=[rowblk(D_MODEL, 0), rowblk(LANES, 0)],
        out_shape=[jax.ShapeDtypeStruct((m, D_MODEL), F32), jax.ShapeDtypeStruct((m, LANES), F32)],
        scratch_shapes=[pltpu.VMEM((3, D_MODEL, D_MODEL), BF16), pltpu.VMEM((D_MODEL, LANES), BF16),
                        pltpu.VMEM((1, LANES), F32)],
        compiler_params=_cparams(("arbitrary",)),
        name="merge",
    )(x, ya_p, ya_s, yb_p, yb_s, zb, zb, w_a, w_b, w_out, ln_g, ln_b, w_rg, w_re, b_rg, b_re)


SEG_ALIGN = 8
LOC_ROWS = 2 * TD + N_EXPERTS * SEG_ALIGN
LOC_BITS = (LOC_ROWS // SEG_ALIGN).bit_length()


def _route_tables(eid, valid, nblk):
    m = eid.shape[0]
    nt = m // TD
    per = BLK // SEG_ALIGN
    experts = jnp.arange(N_EXPERTS, dtype=I32)
    eflat = jnp.where(jnp.repeat(valid, 2), eid.reshape(2 * m), -1)
    hit = eflat[:, None] == experts
    rank_all = jnp.sum(jnp.where(hit, jnp.cumsum(hit.astype(I32), axis=0), 0), axis=1) - 1
    e3 = eflat.reshape(nt, 2 * TD)
    own = e3[:, :, None] == experts
    raw = jnp.sum(own.astype(I32), axis=1)
    cnt = (raw + SEG_ALIGN - 1) // SEG_ALIGN
    lofs = jnp.cumsum(cnt, axis=1) - cnt
    ltot = jnp.sum(cnt, axis=1)
    before = jnp.cumsum(raw, axis=0) - raw
    pick = lambda tbl: jnp.sum(jnp.where(own, tbl[:, None, :], 0), axis=2)
    lslot = pick(lofs) * SEG_ALIGN + rank_all.reshape(nt, 2 * TD) - pick(before)
    lslot = jnp.where(e3 >= 0, lslot, -1).reshape(m, 2)
    gcnt = jnp.sum(cnt, axis=0)
    gpad = (gcnt + per - 1) // per * per
    gend = jnp.cumsum(gpad)
    gofs = (gend - gpad)[None, :] + jnp.cumsum(cnt, axis=0) - cnt
    nused = gend[-1] // per
    unused = nused + jnp.arange(N_EXPERTS, dtype=I32)
    zstart = jnp.concatenate([jnp.where(gpad > 0, (gend - per) * SEG_ALIGN, -1),
                              jnp.where(unused < nblk, unused * BLK, -1)]).astype(I32)
    bstart = jnp.minimum(jnp.arange(nblk, dtype=I32), nused - 1) * per
    blk_e = jnp.minimum(jnp.sum((gend[None, :] <= bstart[:, None]).astype(I32), axis=1), N_EXPERTS - 1)
    flat = lambda t: t.reshape(-1).astype(I32)
    return (flat(cnt), flat(lofs), flat(gofs), flat(ltot), lslot.astype(I32), zstart, flat(blk_e),
            nused.astype(I32).reshape(1))


def _segment_copies(i, cnt_ref, lofs_ref, gofs_ref, make_copy):
    for e in range(N_EXPERTS):
        c = cnt_ref[i * N_EXPERTS + e]
        lo = lofs_ref[i * N_EXPERTS + e]
        go = gofs_ref[i * N_EXPERTS + e]

        @pl.when(c > 0)
        def _():
            make_copy(pl.multiple_of(lo * SEG_ALIGN, SEG_ALIGN), pl.multiple_of(go * SEG_ALIGN, SEG_ALIGN),
                      pl.multiple_of(c * SEG_ALIGN, SEG_ALIGN)).start()


def _wait_rows(tot, make_copy):
    for k in range(LOC_BITS):
        @pl.when(((tot >> k) & 1) == 1)
        def _():
            make_copy(0, 0, SEG_ALIGN << k).wait()


def _dispatch_kernel(cnt_ref, lofs_ref, gofs_ref, ltot_ref, zstart_ref, x_ref, ls_ref, xs_hbm, xloc, zbuf, sem):
    i = pl.program_id(0)
    last = pl.num_programs(0) - 1
    slot = i % 2

    def copy_from(s):
        return lambda lo, go, n: pltpu.make_async_copy(xloc.at[s, pl.ds(lo, n)], xs_hbm.at[pl.ds(go, n)], sem.at[s])

    @pl.when(i == 0)
    def _():
        zbuf[...] = jnp.zeros_like(zbuf)
        for phase in ("start", "wait"):
            for e in range(zstart_ref.shape[0]):
                @pl.when(zstart_ref[e] >= 0)
                def _():
                    cp = pltpu.make_async_copy(zbuf, xs_hbm.at[pl.ds(pl.multiple_of(zstart_ref[e], BLK), BLK)],
                                               sem.at[2])
                    getattr(cp, phase)()

    @pl.when(i >= 2)
    def _():
        _wait_rows(ltot_ref[jnp.maximum(i - 2, 0)], copy_from(slot))

    ls = ls_ref[0]
    j = lax.broadcasted_iota(I32, (LOC_ROWS, TD), 0)
    onehot = jnp.where((j == ls[0:1, :]) | (j == ls[1:2, :]), 1.0, 0.0).astype(BF16)
    xloc[slot] = _dot(onehot, x_ref[...].astype(BF16))
    _segment_copies(i, cnt_ref, lofs_ref, gofs_ref, copy_from(slot))

    @pl.when(i == last)
    def _():
        _wait_rows(ltot_ref[i], copy_from(slot))

        @pl.when(i >= 1)
        def _():
            _wait_rows(ltot_ref[jnp.maximum(i - 1, 0)], copy_from(1 - slot))


def _dispatch(x1, cnt, lofs, gofs, ltot, zstart, lslot, nblk):
    m = x1.shape[0]
    nt = m // TD
    ls_rows = jnp.pad(lslot.reshape(nt, TD, 2).transpose(0, 2, 1), ((0, 0), (0, 6), (0, 0)), constant_values=-1)
    grid_spec = pltpu.PrefetchScalarGridSpec(
        num_scalar_prefetch=5,
        grid=(nt,),
        in_specs=[pl.BlockSpec((TD, D_MODEL), lambda i, *_: (i, 0)),
                  pl.BlockSpec((1, 8, TD), lambda i, *_: (i, 0, 0))],
        out_specs=pl.BlockSpec(memory_space=pl.ANY),
        scratch_shapes=[pltpu.VMEM((2, LOC_ROWS, D_MODEL), F32), pltpu.VMEM((BLK, D_MODEL), F32),
                        pltpu.SemaphoreType.DMA((3,))],
    )
    return pl.pallas_call(
        _dispatch_kernel,
        grid_spec=grid_spec,
        out_shape=jax.ShapeDtypeStruct((nblk * BLK, D_MODEL), F32),
        compiler_params=_cparams(("arbitrary",)),
        name="dispatch",
    )(cnt, lofs, gofs, ltot, zstart, x1, ls_rows)


def _experts_kernel(blk_e_ref, nused_ref, xs_ref, wg_ref, wu_ref, wd_ref, ys_ref, wgb, wub, wdb):
    b = pl.program_id(0)
    used = b < nused_ref[0]
    new_expert = (b == 0) | (blk_e_ref[b] != blk_e_ref[jnp.maximum(b - 1, 0)])

    @pl.when(used & new_expert)
    def _():
        _cast_into(wgb, wg_ref)
        _cast_into(wub, wu_ref)
        _cast_into(wdb, wd_ref)

    @pl.when(used)
    def _():
        xb = xs_ref[...].astype(BF16)
        g = _dot(xb, wgb[...])
        u = _dot(xb, wub[...])
        ys_ref[...] = _dot((g * jax.nn.sigmoid(g) * u).astype(BF16), wdb[...])

    @pl.when(jnp.logical_not(used))
    def _():
        ys_ref[...] = jnp.zeros_like(ys_ref)


def _experts(xs, blk_e, nused, w_eg, w_eu, w_ed, layer):
    nb = xs.shape[0] // BLK
    wspec = lambda s: pl.BlockSpec((None, None) + s, lambda b, be, nu: (layer, be[b], 0, 0))
    grid_spec = pltpu.PrefetchScalarGridSpec(
        num_scalar_prefetch=2,
        grid=(nb,),
        in_specs=[pl.BlockSpec((BLK, D_MODEL), lambda b, be, nu: (jnp.minimum(b, nu[0] - 1), 0)),
                  wspec((D_MODEL, D_EXPERT)), wspec((D_MODEL, D_EXPERT)), wspec((D_EXPERT, D_MODEL))],
        out_specs=pl.BlockSpec((BLK, D_MODEL), lambda b, be, nu: (b, 0)),
        scratch_shapes=[pltpu.VMEM((D_MODEL, D_EXPERT), BF16), pltpu.VMEM((D_MODEL, D_EXPERT), BF16),
                        pltpu.VMEM((D_EXPERT, D_MODEL), BF16)],
    )
    return pl.pallas_call(
        _experts_kernel,
        grid_spec=grid_spec,
        out_shape=jax.ShapeDtypeStruct(xs.shape, F32),
        compiler_params=_cparams(("arbitrary",)),
        name="experts",
    )(blk_e, nused, xs, w_eg, w_eu, w_ed)


def _post_kernel(cnt_ref, lofs_ref, gofs_ref, ltot_ref, x1_ref, route_ref, ls_ref, pp_ref, ps_ref, wpg_ref, wpp_ref,
                 lng_ref, lnb_ref, ys_hbm, o_ref, yloc, wpg_b, wpp_b, sem, *, prompt_tiles):
    i = pl.program_id(0)
    last = pl.num_programs(0) - 1
    slot = i % 2

    def copy_to(s):
        return lambda lo, go, n: pltpu.make_async_copy(ys_hbm.at[pl.ds(go, n)], yloc.at[s, pl.ds(lo, n)], sem.at[s])

    @pl.when(i == 0)
    def _():
        _cast_into(wpg_b, wpg_ref)
        _cast_into(wpp_b, wpp_ref)
        yloc[...] = jnp.zeros_like(yloc)
        _segment_copies(0, cnt_ref, lofs_ref, gofs_ref, copy_to(0))

    _wait_rows(ltot_ref[i], copy_to(slot))

    @pl.when(i < last)
    def _():
        _segment_copies(i + 1, cnt_ref, lofs_ref, gofs_ref, copy_to(1 - slot))

    route = route_ref[...]
    ls = ls_ref[...]
    j = lax.broadcasted_iota(I32, (ls.shape[0], LOC_ROWS), 1)
    w = jnp.where(j == ls[:, 0:1], route[:, 2:3], 0.0) + jnp.where(j == ls[:, 1:2], route[:, 3:4], 0.0)
    moe = _dot(w.astype(BF16), yloc[slot].astype(BF16))
    x2 = _layer_norm(DN_ALPHA * x1_ref[...] + moe, lng_ref[1:2, :], lnb_ref[1:2, :])
    p = jnp.where(i < prompt_tiles, pp_ref[...], ps_ref[...])
    ple = jax.nn.sigmoid(_dot(x2.astype(BF16), wpg_b[...])) * _dot(p.astype(BF16), wpp_b[...])
    o_ref[...] = _layer_norm(DN_ALPHA * x2 + ple, lng_ref[2:3, :], lnb_ref[2:3, :])


def _post(x1, ys, cnt, lofs, gofs, ltot, lslot, route, p_prompt, p_sample, w_pg, w_pp, ln_g, ln_b, layer):
    m = x1.shape[0]
    pt = p_prompt.shape[1] // TD
    rowblk = lambda w: pl.BlockSpec((TD, w), lambda i, *_: (i, 0))
    lspec = lambda s: pl.BlockSpec((None,) + s, lambda i, *_: (layer,) + (0,) * len(s), pipeline_mode=pl.Buffered(1))
    grid_spec = pltpu.PrefetchScalarGridSpec(
        num_scalar_prefetch=4,
        grid=(m // TD,),
        in_specs=[rowblk(D_MODEL), rowblk(LANES), rowblk(2),
                  pl.BlockSpec((None, TD, D_PLE), lambda i, *_: (layer, jnp.minimum(i, pt - 1), 0)),
                  pl.BlockSpec((None, TD, D_PLE), lambda i, *_: (layer, jnp.maximum(i - pt, 0), 0)),
                  lspec((D_MODEL, D_MODEL)), lspec((D_PLE, D_MODEL)), lspec((3, D_MODEL)), lspec((3, D_MODEL)),
                  pl.BlockSpec(memory_space=pl.ANY)],
        out_specs=rowblk(D_MODEL),
        scratch_shapes=[pltpu.VMEM((2, LOC_ROWS, D_MODEL), F32), pltpu.VMEM((D_MODEL, D_MODEL), BF16),
                        pltpu.VMEM((D_PLE, D_MODEL), BF16), pltpu.SemaphoreType.DMA((2,))],
    )
    return pl.pallas_call(
        functools.partial(_post_kernel, prompt_tiles=pt),
        grid_spec=grid_spec,
        out_shape=jax.ShapeDtypeStruct((m, D_MODEL), F32),
        compiler_params=_cparams(("arbitrary",)),
        name="post",
    )(cnt, lofs, gofs, ltot, x1, route, lslot, p_prompt, p_sample, w_pg, w_pp, ln_g, ln_b, ys)


def kernel(x_prompt, x_sample, state_mlstm_C, state_mlstm_n, state_mlstm_m, state_swa_k, state_swa_v, p_prompt, p_sample, w_in, b_in, mh_gain, w_a, w_b, w_out, rel_table, w_sink, ln_g, ln_b, w_rg, b_rg, w_re, b_re, w_eg, w_eu, w_ed, w_pg, w_pp):
    bp_, seq = x_prompt.shape[:2]
    bs_, tdec = x_sample.shape[:2]
    mp = bp_ * seq
    msp = bs_ * T_PAD
    mall = mp + msp
    nblk = -(-(2 * (mp + bs_ * tdec) + (mall // TD) * N_EXPERTS * (SEG_ALIGN - 1)) // BLK) + N_EXPERTS

    pad_t = lambda t: jnp.pad(t, [(0, 0)] * (t.ndim - 2) + [(0, T_PAD - tdec), (0, 0)])
    x = lax.dynamic_update_slice(jnp.pad(x_prompt.reshape(mp, D_MODEL), ((0, msp), (0, 0))),
                                 pad_t(x_sample).reshape(msp, D_MODEL), (mp, 0))
    valid = jnp.concatenate([jnp.ones((mp,), bool), jnp.tile(jnp.arange(T_PAD) < tdec, bs_)])
    p_prompt = p_prompt.reshape(DEPTH, mp, D_PLE)
    p_sample = pad_t(p_sample).reshape(DEPTH, msp, D_PLE)

    b_in3 = b_in.reshape(DEPTH, 1, -1)
    w_qkv2, b_qkv2 = w_in[:, :, IN_QKV2[0]:IN_QKV2[1]], b_in3[:, :, IN_QKV2[0]:IN_QKV2[1]]
    w_gates, b_gates = w_in[:, :, IN_GATES[0]:IN_GATES[1]], b_in3[:, :, IN_GATES[0]:IN_GATES[1]]
    b_rg, b_re = b_rg.reshape(DEPTH, 1, N_GROUPS), b_re.reshape(DEPTH, 1, N_EXPERTS)
    gain = mh_gain.reshape(DEPTH, 1, D_MODEL)

    bias_p = _swa_bias(rel_table, WINDOW)
    bias_s = _swa_bias(rel_table, T_PAD)
    sink_p = sink_s = _sink_lanes(w_sink)
    buf_k = state_swa_k.reshape(DEPTH, bs_, WINDOW, D_KV)
    buf_v = state_swa_v.reshape(DEPTH, bs_, WINDOW, D_KV)
    m0_s = jnp.broadcast_to(state_mlstm_m[..., None], (DEPTH, bs_, M_HEADS, LANES))
    zc = jnp.zeros((1, bp_, M_HEADS, M_DQK, M_DV), F32)
    zn = jnp.zeros((1, bp_, M_HEADS, LANES), F32)
    st_s = (jnp.zeros_like(state_mlstm_C), jnp.zeros_like(state_mlstm_n), jnp.zeros_like(m0_s))
    kv_s = (jnp.zeros_like(buf_k), jnp.zeros_like(buf_v))

    st_p = []
    for i in range(DEPTH):
        za, zg = _inproj(x, [(w_in, IN_A)], [(b_in3, IN_A)], [(ZG, BF16), (LANES, F32)], i, "inproj_a")
        zb, = _inproj(x, [(w_gates, 2048), (w_qkv2, 1536)], [(b_gates, 2048), (b_qkv2, 1536)], [(IN_B, BF16)], i,
                      "inproj_b")

        ya_p, c_p, n_p, m_p = _mlstm(za, zg, gain, i, zc, zn, zn, 0, None, 1, 0, row0=0, nseq=bp_,
                                     steps=seq // LANES, seg=LANES, t_valid=LANES)
        ya_s, *st_s = _mlstm(za, zg, gain, i, state_mlstm_C, state_mlstm_n, m0_s, i, st_s, DEPTH, i, row0=mp,
                             nseq=bs_ // SEQ_BLK, steps=1, seg=T_PAD, t_valid=tdec)
        yb_p = _swa_prompt(zb, bias_p[0], bias_p[1], sink_p, i, batch=bp_, seq=seq)
        yb_s, *kv_s = _swa_sample(zb, buf_k, buf_v, bias_s[0], bias_s[1], sink_s, i, kv_s, row0=mp, batch=bs_,
                                  t_new=tdec)

        x1, route = _merge(x, ya_p, ya_s, yb_p, yb_s, zb, w_a, w_b, w_out, ln_g, ln_b, w_rg, w_re, b_rg, b_re, i)
        cnt, lofs, gofs, ltot, lslot, zstart, blk_e, nused = _route_tables(route[:, :2].astype(I32), valid, nblk)
        xs = _dispatch(x1, cnt, lofs, gofs, ltot, zstart, lslot, nblk)
        ys = _experts(xs, blk_e, nused, w_eg, w_eu, w_ed, i)
        x = _post(x1, ys, cnt, lofs, gofs, ltot, lslot, route, p_prompt, p_sample, w_pg, w_pp, ln_g, ln_b, i)

        kv_p = jnp.stack([lax.slice(zb, ((b + 1) * seq - WINDOW, ZKS), ((b + 1) * seq, IN_B)) for b in range(bp_)])
        kv_p = kv_p.astype(F32)
        heads = lambda t: t.reshape(t.shape[:-1] + (S_KV_HEADS, S_HEAD_DIM))
        st_p.append((c_p[0], n_p[0], m_p[0, :, :, 0], heads(kv_p[..., :D_KV]), heads(kv_p[..., D_KV:])))

    stk = lambda j: jnp.stack([s[j] for s in st_p], axis=0)
    y_p = x[:mp].reshape(bp_, seq, D_MODEL)
    y_s = x[mp:].reshape(bs_, T_PAD, D_MODEL)[:, :tdec]
    return (y_p, y_s, stk(0), stk(1), stk(2), stk(3), stk(4),
            st_s[0], st_s[1], st_s[2][..., 0], heads(kv_s[0]), heads(kv_s[1]))
```

```python
import functools

import numpy as np
import jax
import jax.numpy as jnp
from jax import lax
from jax.experimental import pallas as pl
from jax.experimental.pallas import tpu as pltpu

F32 = jnp.float32
BF16 = jnp.bfloat16
I32 = jnp.int32
HIGHEST = lax.Precision.HIGHEST

D_MODEL = 1024
DEPTH = 4
D_PLE = 256
M_HEADS = 4
M_DQK = 128
M_DV = 256
S_HEADS = 16
S_KV_HEADS = 4
S_HEAD_DIM = 64
S_GROUP = 4
D_KV = S_KV_HEADS * S_HEAD_DIM
WINDOW = 128
REL_BUCKETS = 32
N_GROUPS = 4
EXPERTS_PER_GROUP = 8
N_EXPERTS = 32
D_EXPERT = 512
DN_ALPHA = (2 * DEPTH) ** 0.25
LN_EPS = 1e-5
NEG = -1e30

IN_A = 3200
IN_QKV2 = (3080, 4616)
IN_GATES = (4616, 6664)
ZQ, ZK, ZV, ZOG, ZG = 0, 512, 1024, 2048, 3072
IN_B = 3584
ZGA, ZGB, ZQS, ZKS, ZVS = 0, 1024, 2048, 3072, 3328

LANES = 128
T_PAD = 8
SEQ_BLK = LANES // T_PAD
TM = 512
TD = 256
BLK = 512
VMEM_LIMIT = 56 * 1024 * 1024


def _cparams(sem):
    return pltpu.CompilerParams(dimension_semantics=sem, vmem_limit_bytes=VMEM_LIMIT)


def _layer_spec(shape, layer):
    nd = len(shape)
    return pl.BlockSpec((None,) + shape, lambda *_: (layer,) + (0,) * nd, pipeline_mode=pl.Buffered(1))


def _layer_norm(y, g, b):
    mu = jnp.mean(y, axis=-1, keepdims=True)
    d = y - mu
    var = jnp.mean(d * d, axis=-1, keepdims=True)
    return d * lax.rsqrt(var + LN_EPS) * g + b


def _dot(a, b):
    return jnp.dot(a, b, preferred_element_type=F32)


def _dot_nt(a, b):
    return lax.dot_general(a, b, (((1,), (1,)), ((), ())), preferred_element_type=F32)


def _dot_tn(a, b):
    return lax.dot_general(a, b, (((0,), (0,)), ((), ())), preferred_element_type=F32)


def _dot_exact(a, b):
    return jnp.dot(a, b, precision=HIGHEST, preferred_element_type=F32)


def _cast_into(dst_ref, src_ref, col0=0, step=512):
    n = src_ref.shape[-1]
    for j in range(0, n, step):
        w = min(step, n - j)
        dst_ref[:, col0 + j:col0 + j + w] = src_ref[:, j:j + w].astype(BF16)


def _inproj_kernel(*refs, widths, out_widths):
    ng, no = len(widths), len(out_widths)
    x_ref, w_refs, b_refs = refs[0], refs[1:1 + ng], refs[1 + ng:1 + 2 * ng]
    o_refs, wbf = refs[1 + 2 * ng:1 + 2 * ng + no], refs[1 + 2 * ng + no]

    @pl.when(pl.program_id(0) == 0)
    def _():
        c = 0
        for w_ref, n in zip(w_refs, widths):
            _cast_into(wbf, w_ref, c)
            c += n

    out_start = np.cumsum((0,) + tuple(out_widths))

    def store(col, val):
        k = int(np.searchsorted(out_start, col, side='right')) - 1
        lo = col - int(out_start[k])
        o_refs[k][:, lo:lo + val.shape[1]] = val.astype(o_refs[k].dtype)

    xb = x_ref[...].astype(BF16)
    c = 0
    for b_ref, n in zip(b_refs, widths):
        for j in range(0, n, 512):
            w = min(512, n - j)
            store(c + j, _dot(xb, wbf[:, c + j:c + j + w]) + b_ref[:, j:j + w])
        c += n


def _inproj(x, ws, bs, outs, layer, name):
    m = x.shape[0]
    widths = tuple(n for _, n in ws)
    out_widths = tuple(n for n, _ in outs)
    assert sum(widths) == sum(out_widths)
    return pl.pallas_call(
        functools.partial(_inproj_kernel, widths=widths, out_widths=out_widths),
        grid=(m // TM,),
        in_specs=([pl.BlockSpec((TM, D_MODEL), lambda i: (i, 0))]
                  + [_layer_spec((D_MODEL, n), layer) for _, n in ws]
                  + [_layer_spec((1, n), layer) for _, n in bs]),
        out_specs=[pl.BlockSpec((TM, n), lambda i: (i, 0)) for n, _ in outs],
        out_shape=[jax.ShapeDtypeStruct((m, n), dt) for n, dt in outs],
        scratch_shapes=[pltpu.VMEM((D_MODEL, sum(widths)), BF16)],
        compiler_params=_cparams(("arbitrary",)),
        name=name,
    )(x, *[w for w, _ in ws], *[b for b, _ in bs])


def _inproj_b_kernel(x_ref, b_ref, w_hbm, z_ref, wwin, wbf, sem, *, layer):
    lead = IN_QKV2[0] % LANES
    n_qkv = IN_QKV2[1] - IN_QKV2[0]

    @pl.when(pl.program_id(0) == 0)
    def _():
        cp = pltpu.make_async_copy(w_hbm.at[layer, :, pl.ds(IN_QKV2[0] - lead, wwin.shape[1])], wwin, sem.at[0])
        cp.start()
        cp.wait()
        for dst, src, n in ((0, n_qkv, IN_B - n_qkv), (IN_B - n_qkv, 0, n_qkv)):
            for j in range(0, n, 512):
                w = min(512, n - j)
                wbf[:, dst + j:dst + j + w] = wwin[:, lead + src + j:lead + src + j + w].astype(BF16)

    xb = x_ref[...].astype(BF16)
    for j in range(0, IN_B, 512):
        w = min(512, IN_B - j)
        z_ref[:, j:j + w] = (_dot(xb, wbf[:, j:j + w]) + b_ref[:, j:j + w]).astype(z_ref.dtype)


def _inproj_b(x, w_in, b_b, layer):
    m = x.shape[0]
    win = w_in.shape[2] - (IN_QKV2[0] - IN_QKV2[0] % LANES)
    return pl.pallas_call(
        functools.partial(_inproj_b_kernel, layer=layer),
        grid=(m // TM,),
        in_specs=[pl.BlockSpec((TM, D_MODEL), lambda i: (i, 0)), _layer_spec((1, IN_B), layer),
                  pl.BlockSpec(memory_space=pl.ANY)],
        out_specs=pl.BlockSpec((TM, IN_B), lambda i: (i, 0)),
        out_shape=jax.ShapeDtypeStruct((m, IN_B), BF16),
        scratch_shapes=[pltpu.VMEM((D_MODEL, win), F32), pltpu.VMEM((D_MODEL, IN_B), BF16),
                        pltpu.SemaphoreType.DMA((1,))],
        compiler_params=_cparams(("arbitrary",)),
        name="inproj_b",
    )(x, b_b, w_in)


def _log_sigmoid(x):
    return jnp.minimum(x, 0.0) - jnp.log(1.0 + jnp.exp(-jnp.abs(x)))


def _mlstm_kernel(zq, zk, zv, zog, zg, gain, c0, n0, m0, *rest, seg, t_valid, aliased):
    ya, co, no, mo, qc_s = rest[aliased:]
    L = zq.shape[0]
    nseg = L // seg

    @pl.when(pl.program_id(1) == 0)
    def _():
        co[...] = c0[...]
        no[...] = n0[...]
        mo[...] = m0[...]

    g_c = zg[...]
    g_r = g_c.T
    ig_c, lf_c = g_c, _log_sigmoid(g_c)
    ig_r, lf_r = g_r, _log_sigmoid(g_r)
    ri = lax.broadcasted_iota(I32, (L, L), 0)
    ci = lax.broadcasted_iota(I32, (L, L), 1)
    shift = seg.bit_length() - 1
    if t_valid < seg:
        rv = (lax.broadcasted_iota(I32, (L, 1), 0) & (seg - 1)) < t_valid
        cv = (lax.broadcasted_iota(I32, (1, L), 1) & (seg - 1)) < t_valid
        ig_c, lf_c = jnp.where(rv, ig_c, NEG), jnp.where(rv, lf_c, 0.0)
        ig_r, lf_r = jnp.where(cv, ig_r, NEG), jnp.where(cv, lf_r, 0.0)
    if nseg > 1:
        same = (ri >> shift) == (ci >> shift)
        causal, causal_t = same & (ri >= ci), same & (ri <= ci)
        same_f = same.astype(F32)
        be_c = _dot_exact(same_f, lf_c)
        be_r = _dot_exact(lf_r, same_f)
        expand = ((lax.broadcasted_iota(I32, (L, nseg), 0) >> shift)
                  == lax.broadcasted_iota(I32, (L, nseg), 1)).astype(F32)
    else:
        causal, causal_t = ri >= ci, ri <= ci
    b_c = _dot_exact(causal.astype(F32), lf_c)
    b_r = _dot_exact(lf_r, causal_t.astype(F32))

    for h in range(M_HEADS):
        icol, irow = ig_c[:, h:h + 1], ig_r[h:h + 1, :]
        bcol, brow = b_c[:, 4 + h:5 + h], b_r[4 + h:5 + h, :]
        q = zq[:, h * M_DQK:(h + 1) * M_DQK].astype(F32) * (M_DQK ** -0.5)
        k = zk[:, h * M_DQK:(h + 1) * M_DQK].astype(F32)
        qb, kb = q.astype(BF16), zk[:, h * M_DQK:(h + 1) * M_DQK]
        vb = zv[:, h * M_DV:(h + 1) * M_DV]
        if nseg > 1:
            mcol = _dot_exact(expand, mo[:, h, :])[:, 0:1]
            nrow = _dot_exact(expand, no[:, h, :])
            becol, berow = be_c[:, 4 + h:5 + h], be_r[4 + h:5 + h, :]
            for j in range(nseg):
                rows = slice(j * seg, (j + 1) * seg)
                qc_s[rows, :] = _dot(q[rows].astype(BF16), co[j, h].astype(BF16))
            q_c = qc_s[...]
        else:
            mcol, nrow = mo[0, h:h + 1, 0:1], no[0, h:h + 1, :]
            becol = bcol[L - 1:L, :]
            q_c = _dot(qb, co[0, h].astype(BF16))

        dmat = jnp.where(causal, bcol - brow + irow, NEG)
        inter = bcol + mcol
        mhat = jnp.maximum(inter, jnp.max(dmat, axis=-1, keepdims=True))
        w_intra = jnp.exp(dmat - mhat)
        w_inter = jnp.exp(inter - mhat)
        s = _dot_nt(qb, kb) * w_intra
        num = _dot(s.astype(BF16), vb) + w_inter * q_c
        den = jnp.sum(s, axis=-1, keepdims=True) + w_inter * jnp.sum(q * nrow, axis=-1, keepdims=True)
        hh = num / jnp.maximum(jnp.abs(den), jnp.exp(-mhat))

        gcol = icol + becol - bcol
        if nseg > 1:
            gmax = jnp.max(jnp.where(same, irow + berow - brow, NEG), axis=-1, keepdims=True)
        else:
            gmax = jnp.max(gcol, axis=0, keepdims=True)
        m_new = jnp.maximum(becol + mcol, gmax)
        decay = jnp.exp(becol + mcol - m_new)
        kw = k * jnp.exp(gcol - m_new)
        vf = vb.astype(F32) if nseg > 1 else vb
        for j in range(nseg):
            rows = slice(j * seg, (j + 1) * seg)
            dj = decay[j * seg:j * seg + 1, :] if nseg > 1 else decay
            mj = m_new[j * seg:j * seg + 1, :] if nseg > 1 else m_new
            co[j, h] = dj * co[j, h] + _dot_tn(kw[rows].astype(BF16), vf[rows].astype(BF16))
            no[j, h:h + 1, :] = dj * no[j, h:h + 1, :] + jnp.sum(kw[rows], axis=0, keepdims=True)
            mo[j, h:h + 1, :] = jnp.broadcast_to(mj, (1, LANES))

        mu = jnp.mean(hh, axis=-1, keepdims=True)
        d = hh - mu
        var = jnp.mean(d * d, axis=-1, keepdims=True)
        hn = d * lax.rsqrt(var + LN_EPS) * gain[:, h * M_DV:(h + 1) * M_DV]
        og = zog[:, h * M_DV:(h + 1) * M_DV].astype(F32)
        ya[:, h * M_DV:(h + 1) * M_DV] = (hn * jax.nn.sigmoid(og)).astype(ya.dtype)


def _mlstm(za, zg, gain, layer, c0, n0, m0, st_layer, outs_prev, out_layers, out_layer, *, row0, nseq, steps,
           seg, t_valid):
    nseg = LANES // seg
    r0 = row0 // LANES
    row = lambda b, c: r0 + b * steps + c
    nstate = nseq * nseg
    aliased = 0 if outs_prev is None else 3
    kern = functools.partial(_mlstm_kernel, seg=seg, t_valid=min(t_valid, seg), aliased=aliased)
    in4 = pl.BlockSpec((None, nseg, M_HEADS, M_DQK, M_DV), lambda b, c: (st_layer, b, 0, 0, 0))
    in3 = pl.BlockSpec((None, nseg, M_HEADS, LANES), lambda b, c: (st_layer, b, 0, 0))
    out4 = pl.BlockSpec((None, nseg, M_HEADS, M_DQK, M_DV), lambda b, c: (out_layer, b, 0, 0, 0))
    out3 = pl.BlockSpec((None, nseg, M_HEADS, LANES), lambda b, c: (out_layer, b, 0, 0))
    anyspec = pl.BlockSpec(memory_space=pl.ANY)
    n_in = 9
    return pl.pallas_call(
        kern,
        grid=(nseq, steps),
        in_specs=[pl.BlockSpec((LANES, 512), lambda b, c: (row(b, c), ZQ // 512)),
                  pl.BlockSpec((LANES, 512), lambda b, c: (row(b, c), ZK // 512)),
                  pl.BlockSpec((LANES, 1024), lambda b, c: (row(b, c), ZV // 1024)),
                  pl.BlockSpec((LANES, 1024), lambda b, c: (row(b, c), ZOG // 1024)),
                  pl.BlockSpec((LANES, LANES), lambda b, c: (row(b, c), 0)),
                  pl.BlockSpec((None, 1, D_MODEL), lambda b, c: (layer, 0, 0)),
                  in4, in3, in3] + [anyspec] * aliased,
        out_specs=[pl.BlockSpec((LANES, D_MODEL), lambda b, c: (b * steps + c, 0)), out4, out3, out3],
        out_shape=[jax.ShapeDtypeStruct((nseq * steps * LANES, D_MODEL), BF16),
                   jax.ShapeDtypeStruct((out_layers, nstate, M_HEADS, M_DQK, M_DV), F32),
                   jax.ShapeDtypeStruct((out_layers, nstate, M_HEADS, LANES), F32),
                   jax.ShapeDtypeStruct((out_layers, nstate, M_HEADS, LANES), F32)],
        scratch_shapes=[pltpu.VMEM((LANES, M_DV), F32)],
        input_output_aliases={n_in + j: 1 + j for j in range(aliased)},
        compiler_params=_cparams(("parallel", "arbitrary")),
        name="mlstm",
    )(za, za, za, za, zg, gain, c0, n0, m0, *(outs_prev or ()))


def _swa_kernel(q_ref, kc_ref, vc_ref, kp_ref, vp_ref, bp_ref, bc_ref, sink_ref, *rest, nseq, t_new,
                first_block_has_no_past, aliased):
    o_ref = rest[aliased]
    rows_n = q_ref.shape[0]
    tq = rows_n // nseq
    have_past = pl.program_id(1) > 0
    lane = lax.broadcasted_iota(I32, (1, LANES), 1)
    in_lo = lane < S_HEAD_DIM
    if nseq > 1:
        kp_all = kp_ref[...].reshape(nseq * WINDOW, D_KV).astype(BF16)
        vp_all = vp_ref[...].reshape(nseq * WINDOW, D_KV).astype(BF16)
        row_seq = lax.broadcasted_iota(I32, (rows_n, LANES), 0) >> (tq.bit_length() - 1)
    else:
        kp_all, vp_all = kp_ref[...], vp_ref[...]
    kc_all, vc_all = kc_ref[...], vc_ref[...]

    def placed(x, src_half, ones):
        swapped = jnp.concatenate([x[:, S_HEAD_DIM:], x[:, :S_HEAD_DIM]], axis=1)
        zero = jnp.zeros_like(x)
        lo = jnp.where(in_lo, x if src_half == 0 else swapped, zero)
        hi = jnp.where(in_lo, zero, x if src_half == 1 else swapped)
        if ones:
            lo = jnp.where(lane == S_HEAD_DIM, jnp.ones_like(x), lo)
            hi = jnp.where(lane == 0, jnp.ones_like(x), hi)
        return lo, hi

    kv_per_pass = S_KV_HEADS if nseq == 1 else 1
    for kv0 in range(0, S_KV_HEADS, kv_per_pass):
        kp_v, kc_v, vp_v, vc_v = {}, {}, {}, {}
        for kvh in range(kv0, kv0 + kv_per_pass):
            tile = slice((kvh // 2) * LANES, (kvh // 2 + 1) * LANES)
            kp_v[kvh] = placed(kp_all[:, tile], kvh % 2, False)
            kc_v[kvh] = placed(kc_all[:, tile], kvh % 2, False)
            vp_v[kvh] = placed(vp_all[:, tile], kvh % 2, True)
            vc_v[kvh] = placed(vc_all[:, tile], kvh % 2, True)
        heads = list(range(kv0 * S_GROUP, (kv0 + kv_per_pass) * S_GROUP))
        scores = []
        for h in heads:
            q2 = q_ref[:, (h // 2) * LANES:(h // 2 + 1) * LANES]
            scores.append((_dot_nt(q2, kp_v[h // S_GROUP][h % 2]), _dot_nt(q2, kc_v[h // S_GROUP][h % 2])))
        probs = []
        for h, (sp, sc) in zip(heads, scores):
            if nseq > 1:
                sp = jnp.concatenate([sp[b * tq:(b + 1) * tq, b * WINDOW:(b + 1) * WINDOW] for b in range(nseq)],
                                     axis=0)
            sp = sp * (S_HEAD_DIM ** -0.5) + bp_ref[h]
            sc = sc * (S_HEAD_DIM ** -0.5) + bc_ref[h]
            if first_block_has_no_past:
                sp = jnp.where(have_past, sp, NEG)
            sk = sink_ref[h][:, 0:1]
            mx = jnp.maximum(jnp.max(jnp.maximum(sp, sc), axis=-1, keepdims=True), sk)
            pp, pc = jnp.exp(sp - mx), jnp.exp(sc - mx)
            if nseq > 1:
                pp = jnp.concatenate([jnp.where(row_seq == b, pp, 0.0) for b in range(nseq)], axis=1)
            probs.append((pp.astype(BF16), pc.astype(BF16), jnp.exp(sk - mx)))
        outs = [_dot(pp, vp_v[h // S_GROUP][h % 2]) + _dot(pc, vc_v[h // S_GROUP][h % 2])
                for h, (pp, pc, _) in zip(heads, probs)]
        for pair in range(len(heads) // 2):
            halves = []
            for half in range(2):
                o, sink_term = outs[2 * pair + half], probs[2 * pair + half][2]
                ones_lane = S_HEAD_DIM if half == 0 else 0
                halves.append(o / (o[:, ones_lane:ones_lane + 1] + sink_term))
            qt = heads[2 * pair] // 2
            o_ref[:, qt * LANES:(qt + 1) * LANES] = jnp.where(in_lo, halves[0], halves[1]).astype(o_ref.dtype)

    if aliased:
        kc_f, vc_f = kc_all.astype(F32), vc_all.astype(F32)
        for b in range(nseq):
            for new, prev, out in ((kc_f, kp_ref, rest[aliased + 1]), (vc_f, vp_ref, rest[aliased + 2])):
                out[b, 0:WINDOW - t_new, :] = prev[b, t_new:, :]
                out[b, WINDOW - t_new:WINDOW, :] = new[b * tq:b * tq + t_new, :]


def _rel_bucket(dist):
    nn = np.maximum(dist, 0)
    max_exact = REL_BUCKETS // 2
    large = max_exact + (np.log(np.maximum(nn, 1) / max_exact) / np.log(WINDOW / max_exact)
                         * (REL_BUCKETS - max_exact)).astype(np.int32)
    large = np.minimum(large, REL_BUCKETS - 1)
    return np.where(nn < max_exact, nn, large).astype(np.int32)


def _swa_bias(rel_table, tq):
    nseq = LANES // tq
    t = np.tile(np.arange(tq), nseq)[:, None]
    seq = np.repeat(np.arange(nseq), tq)
    d_prev = t + WINDOW - np.arange(WINDOW)[None, :]
    d_cur = t - t.T
    own = seq[:, None] == seq[None, :]

    def build(dist, ok):
        onehot = np.eye(REL_BUCKETS, dtype=np.float32)[_rel_bucket(dist).reshape(-1)]
        bias = jnp.dot(onehot, rel_table.astype(F32), precision=HIGHEST).reshape(dist.shape + (S_HEADS,))
        return jnp.where(ok[None], jnp.transpose(bias, (2, 0, 1)), NEG)

    return build(d_prev, d_prev <= WINDOW), build(d_cur, own & (d_cur >= 0))


def _sink_lanes(w_sink):
    return jnp.broadcast_to(w_sink.astype(F32)[:, :, None, None], (DEPTH, S_HEADS, 1, LANES))


def _swa_bias_specs(layer):
    return [pl.BlockSpec((S_HEADS, LANES, WINDOW), lambda b, n: (0, 0, 0)),
            pl.BlockSpec((S_HEADS, LANES, LANES), lambda b, n: (0, 0, 0)),
            pl.BlockSpec((None, S_HEADS, 1, LANES), lambda b, n: (layer, 0, 0, 0))]


def _swa_prompt(zb, bp, bc, sk, layer, *, batch, seq):
    nb = seq // WINDOW
    row = lambda b, n: b * nb + n
    prev = lambda b, n: jnp.maximum(b * nb + n - 1, 0)
    in_specs = [pl.BlockSpec((WINDOW, 1024), lambda b, n: (row(b, n), ZQS // 1024)),
                pl.BlockSpec((WINDOW, D_KV), lambda b, n: (row(b, n), ZKS // D_KV)),
                pl.BlockSpec((WINDOW, D_KV), lambda b, n: (row(b, n), ZVS // D_KV)),
                pl.BlockSpec((WINDOW, D_KV), lambda b, n: (prev(b, n), ZKS // D_KV)),
                pl.BlockSpec((WINDOW, D_KV), lambda b, n: (prev(b, n), ZVS // D_KV))]
    kern = functools.partial(_swa_kernel, nseq=1, t_new=0, first_block_has_no_past=True, aliased=0)
    return pl.pallas_call(
        kern,
        grid=(batch, nb),
        in_specs=in_specs + _swa_bias_specs(layer),
        out_specs=pl.BlockSpec((WINDOW, D_MODEL), lambda b, n: (row(b, n), 0)),
        out_shape=jax.ShapeDtypeStruct((batch * seq, D_MODEL), BF16),
        compiler_params=_cparams(("parallel", "arbitrary")),
        name="swa_prompt",
    )(zb, zb, zb, zb, zb, bp, bc, sk)


def _swa_sample(zb, buf_k, buf_v, bp, bc, sk, layer, kv_prev, *, row0, batch, t_new):
    r0 = row0 // LANES
    steps = batch // SEQ_BLK
    state = pl.BlockSpec((None, SEQ_BLK, WINDOW, D_KV), lambda b, n: (layer, b, 0, 0))
    anyspec = pl.BlockSpec(memory_space=pl.ANY)
    in_specs = [pl.BlockSpec((LANES, 1024), lambda b, n: (r0 + b, ZQS // 1024)),
                pl.BlockSpec((LANES, D_KV), lambda b, n: (r0 + b, ZKS // D_KV)),
                pl.BlockSpec((LANES, D_KV), lambda b, n: (r0 + b, ZVS // D_KV)),
                state, state]
    kern = functools.partial(_swa_kernel, nseq=SEQ_BLK, t_new=t_new, first_block_has_no_past=False, aliased=2)
    return pl.pallas_call(
        kern,
        grid=(steps, 1),
        in_specs=in_specs + _swa_bias_specs(layer) + [anyspec, anyspec],
        out_specs=[pl.BlockSpec((LANES, D_MODEL), lambda b, n: (b, 0)), state, state],
        out_shape=[jax.ShapeDtypeStruct((batch * T_PAD, D_MODEL), BF16),
                   jax.ShapeDtypeStruct(buf_k.shape, F32), jax.ShapeDtypeStruct(buf_v.shape, F32)],
        input_output_aliases={8: 1, 9: 2},
        compiler_params=_cparams(("parallel", "arbitrary")),
        name="swa_sample",
    )(zb, zb, zb, buf_k, buf_v, bp, bc, sk, *kv_prev)


def _merge_kernel(x_ref, yap_ref, yas_ref, ybp_ref, ybs_ref, ga_ref, gb_ref, wa_ref, wb_ref, wo_ref, lng_ref, lnb_ref,
                  wrg_ref, wre_ref, brg_ref, bre_ref, x1_ref, route_ref, wbf, wr_b, br_s, *, prompt_tiles):
    i = pl.program_id(0)

    @pl.when(i == 0)
    def _():
        for j, w_ref in enumerate((wa_ref, wb_ref, wo_ref)):
            _cast_into(wbf.at[j], w_ref)
        wr_b[...] = jnp.zeros_like(wr_b)
        wr_b[:, 0:N_GROUPS] = wrg_ref[...].astype(BF16)
        wr_b[:, N_GROUPS:N_GROUPS + N_EXPERTS] = wre_ref[...].astype(BF16)
        br_s[...] = jnp.zeros_like(br_s)
        br_s[:, 0:N_GROUPS] = brg_ref[...]
        br_s[:, N_GROUPS:N_GROUPS + N_EXPERTS] = bre_ref[...]

    is_prompt = i < prompt_tiles
    half = x_ref.shape[0] // 2
    for r0 in (0, half):
        rs = slice(r0, r0 + half)
        ya = jnp.where(is_prompt, yap_ref[rs, :], yas_ref[rs, :])
        yb = jnp.where(is_prompt, ybp_ref[rs, :], ybs_ref[rs, :])
        a = _dot(ya, wbf[0])
        b = _dot(yb, wbf[1])
        u = jax.nn.sigmoid(ga_ref[rs, :].astype(F32)) * a + jax.nn.sigmoid(gb_ref[rs, :].astype(F32)) * b
        mix = _dot(u.astype(BF16), wbf[2])
        x1_ref[rs, :] = _layer_norm(DN_ALPHA * x_ref[rs, :] + mix, lng_ref[0:1, :], lnb_ref[0:1, :])
    x1 = x1_ref[...]

    logits = _dot(x1.astype(BF16), wr_b[...]) + br_s[...]
    lane = lax.broadcasted_iota(I32, logits.shape, 1)
    lanef = lane.astype(F32)
    gmask = lane < N_GROUPS
    gl = jnp.where(gmask, logits, NEG)
    gmax = jnp.max(gl, axis=-1, keepdims=True)
    gidx = jnp.min(jnp.where(gmask & (gl == gmax), lanef, 999.0), axis=-1, keepdims=True)
    g_w = 1.0 / jnp.sum(jnp.where(gmask, jnp.exp(gl - gmax), 0.0), axis=-1, keepdims=True)
    lo = N_GROUPS + EXPERTS_PER_GROUP * gidx
    emask = (lanef >= lo) & (lanef < lo + EXPERTS_PER_GROUP)
    el = jnp.where(emask, logits, NEG)
    emax = jnp.max(el, axis=-1, keepdims=True)
    ep = jnp.where(emask, jnp.exp(el - emax), 0.0)
    prob = ep / jnp.sum(ep, axis=-1, keepdims=True)
    prob = jnp.where(emask, prob, -1.0)
    p1 = jnp.max(prob, axis=-1, keepdims=True)
    i1 = jnp.min(jnp.where(prob == p1, lanef, 999.0), axis=-1, keepdims=True)
    prob2 = jnp.where(lanef == i1, -1.0, prob)
    p2 = jnp.max(prob2, axis=-1, keepdims=True)
    i2 = jnp.min(jnp.where(prob2 == p2, lanef, 999.0), axis=-1, keepdims=True)
    tot = p1 + p2
    route_ref[...] = jnp.where(lane == 0, i1 - N_GROUPS,
                               jnp.where(lane == 1, i2 - N_GROUPS,
                                         jnp.where(lane == 2, g_w * p1 / tot,
                                                   jnp.where(lane == 3, g_w * p2 / tot, 0.0))))


def _merge(x, ya_p, ya_s, yb_p, yb_s, zb, w_a, w_b, w_out, ln_g, ln_b, w_rg, w_re, b_rg, b_re, layer):
    m = x.shape[0]
    pt = ya_p.shape[0] // TM
    rowblk = lambda w, j: pl.BlockSpec((TM, w), lambda i: (i, j))
    pblk = pl.BlockSpec((TM, D_MODEL), lambda i: (jnp.minimum(i, pt - 1), 0))
    sblk = pl.BlockSpec((TM, D_MODEL), lambda i: (jnp.maximum(i - pt, 0), 0))
    return pl.pallas_call(
        functools.partial(_merge_kernel, prompt_tiles=pt),
        grid=(m // TM,),
        in_specs=[rowblk(D_MODEL, 0), pblk, sblk, pblk, sblk,
                  rowblk(D_MODEL, ZGA // D_MODEL), rowblk(D_MODEL, ZGB // D_MODEL),
                  _layer_spec((D_MODEL, D_MODEL), layer), _layer_spec((D_MODEL, D_MODEL), layer),
                  _layer_spec((D_MODEL, D_MODEL), layer),
                  _layer_spec((3, D_MODEL), layer), _layer_spec((3, D_MODEL), layer),
                  _layer_spec((D_MODEL, N_GROUPS), layer), _layer_spec((D_MODEL, N_EXPERTS), layer),
                  _layer_spec((1, N_GROUPS), layer), _layer_spec((1, N_EXPERTS), layer)],
        out_specs=[rowblk(D_MODEL, 0), rowblk(LANES, 0)],
        out_shape=[jax.ShapeDtypeStruct((m, D_MODEL), F32), jax.ShapeDtypeStruct((m, LANES), F32)],
        scratch_shapes=[pltpu.VMEM((3, D_MODEL, D_MODEL), BF16), pltpu.VMEM((D_MODEL, LANES), BF16),
                        pltpu.VMEM((1, LANES), F32)],
        compiler_params=_cparams(("arbitrary",)),
        name="merge",
    )(x, ya_p, ya_s, yb_p, yb_s, zb, zb, w_a, w_b, w_out, ln_g, ln_b, w_rg, w_re, b_rg, b_re)


SEG_ALIGN = 8
LOC_ROWS = 2 * TD + N_EXPERTS * SEG_ALIGN
LOC_BITS = (LOC_ROWS // SEG_ALIGN).bit_length()


def _route_tables(eid, valid, nblk):
    m = eid.shape[0]
    nt = m // TD
    per = BLK // SEG_ALIGN
    experts = jnp.arange(N_EXPERTS, dtype=I32)
    eflat = jnp.where(jnp.repeat(valid, 2), eid.reshape(2 * m), -1)
    hit = eflat[:, None] == experts
    rank_all = jnp.sum(jnp.where(hit, jnp.cumsum(hit.astype(I32), axis=0), 0), axis=1) - 1
    e3 = eflat.reshape(nt, 2 * TD)
    own = e3[:, :, None] == experts
    raw = jnp.sum(own.astype(I32), axis=1)
    cnt = (raw + SEG_ALIGN - 1) // SEG_ALIGN
    lofs = jnp.cumsum(cnt, axis=1) - cnt
    ltot = jnp.sum(cnt, axis=1)
    before = jnp.cumsum(raw, axis=0) - raw
    pick = lambda tbl: jnp.sum(jnp.where(own, tbl[:, None, :], 0), axis=2)
    lslot = pick(lofs) * SEG_ALIGN + rank_all.reshape(nt, 2 * TD) - pick(before)
    lslot = jnp.where(e3 >= 0, lslot, -1).reshape(m, 2)
    gcnt = jnp.sum(cnt, axis=0)
    gpad = (gcnt + per - 1) // per * per
    gend = jnp.cumsum(gpad)
    gofs = (gend - gpad)[None, :] + jnp.cumsum(cnt, axis=0) - cnt
    nused = gend[-1] // per
    unused = nused + jnp.arange(N_EXPERTS, dtype=I32)
    zstart = jnp.concatenate([jnp.where(gpad > 0, (gend - per) * SEG_ALIGN, -1),
                              jnp.where(unused < nblk, unused * BLK, -1)]).astype(I32)
    bstart = jnp.minimum(jnp.arange(nblk, dtype=I32), nused - 1) * per
    blk_e = jnp.minimum(jnp.sum((gend[None, :] <= bstart[:, None]).astype(I32), axis=1), N_EXPERTS - 1)
    flat = lambda t: t.reshape(-1).astype(I32)
    return (flat(cnt), flat(lofs), flat(gofs), flat(ltot), lslot.astype(I32), zstart, flat(blk_e),
            nused.astype(I32).reshape(1))


def _segment_copies(i, cnt_ref, lofs_ref, gofs_ref, make_copy):
    for e in range(N_EXPERTS):
        c = cnt_ref[i * N_EXPERTS + e]
        lo = lofs_ref[i * N_EXPERTS + e]
        go = gofs_ref[i * N_EXPERTS + e]

        @pl.when(c > 0)
        def _():
            make_copy(pl.multiple_of(lo * SEG_ALIGN, SEG_ALIGN), pl.multiple_of(go * SEG_ALIGN, SEG_ALIGN),
                      pl.multiple_of(c * SEG_ALIGN, SEG_ALIGN)).start()


def _wait_rows(tot, make_copy):
    for k in range(LOC_BITS):
        @pl.when(((tot >> k) & 1) == 1)
        def _():
            make_copy(0, 0, SEG_ALIGN << k).wait()


def _dispatch_kernel(cnt_ref, lofs_ref, gofs_ref, ltot_ref, zstart_ref, x_ref, ls_ref, xs_hbm, xloc, zbuf, sem):
    i = pl.program_id(0)
    last = pl.num_programs(0) - 1
    slot = i % 2

    def copy_from(s):
        return lambda lo, go, n: pltpu.make_async_copy(xloc.at[s, pl.ds(lo, n)], xs_hbm.at[pl.ds(go, n)], sem.at[s])

    @pl.when(i == 0)
    def _():
        zbuf[...] = jnp.zeros_like(zbuf)
        for phase in ("start", "wait"):
            for e in range(zstart_ref.shape[0]):
                @pl.when(zstart_ref[e] >= 0)
                def _():
                    cp = pltpu.make_async_copy(zbuf, xs_hbm.at[pl.ds(pl.multiple_of(zstart_ref[e], BLK), BLK)],
                                               sem.at[2])
                    getattr(cp, phase)()

    @pl.when(i >= 2)
    def _():
        _wait_rows(ltot_ref[jnp.maximum(i - 2, 0)], copy_from(slot))

    ls = ls_ref[0]
    j = lax.broadcasted_iota(I32, (LOC_ROWS, TD), 0)
    onehot = jnp.where((j == ls[0:1, :]) | (j == ls[1:2, :]), 1.0, 0.0).astype(BF16)
    xloc[slot] = _dot(onehot, x_ref[...].astype(BF16))
    _segment_copies(i, cnt_ref, lofs_ref, gofs_ref, copy_from(slot))

    @pl.when(i == last)
    def _():
        _wait_rows(ltot_ref[i], copy_from(slot))

        @pl.when(i >= 1)
        def _():
            _wait_rows(ltot_ref[jnp.maximum(i - 1, 0)], copy_from(1 - slot))


def _dispatch(x1, cnt, lofs, gofs, ltot, zstart, lslot, nblk):
    m = x1.shape[0]
    nt = m // TD
    ls_rows = jnp.pad(lslot.reshape(nt, TD, 2).transpose(0, 2, 1), ((0, 0), (0, 6), (0, 0)), constant_values=-1)
    grid_spec = pltpu.PrefetchScalarGridSpec(
        num_scalar_prefetch=5,
        grid=(nt,),
        in_specs=[pl.BlockSpec((TD, D_MODEL), lambda i, *_: (i, 0)),
                  pl.BlockSpec((1, 8, TD), lambda i, *_: (i, 0, 0))],
        out_specs=pl.BlockSpec(memory_space=pl.ANY),
        scratch_shapes=[pltpu.VMEM((2, LOC_ROWS, D_MODEL), F32), pltpu.VMEM((BLK, D_MODEL), F32),
                        pltpu.SemaphoreType.DMA((3,))],
    )
    return pl.pallas_call(
        _dispatch_kernel,
        grid_spec=grid_spec,
        out_shape=jax.ShapeDtypeStruct((nblk * BLK, D_MODEL), F32),
        compiler_params=_cparams(("arbitrary",)),
        name="dispatch",
    )(cnt, lofs, gofs, ltot, zstart, x1, ls_rows)


def _experts_kernel(blk_e_ref, nused_ref, xs_ref, wg_ref, wu_ref, wd_ref, ys_ref, wgb, wub, wdb):
    b = pl.program_id(0)
    used = b < nused_ref[0]
    new_expert = (b == 0) | (blk_e_ref[b] != blk_e_ref[jnp.maximum(b - 1, 0)])

    @pl.when(used & new_expert)
    def _():
        _cast_into(wgb, wg_ref)
        _cast_into(wub, wu_ref)
        _cast_into(wdb, wd_ref)

    @pl.when(used)
    def _():
        xb = xs_ref[...].astype(BF16)
        g = _dot(xb, wgb[...])
        u = _dot(xb, wub[...])
        ys_ref[...] = _dot((g * jax.nn.sigmoid(g) * u).astype(BF16), wdb[...])

    @pl.when(jnp.logical_not(used))
    def _():
        ys_ref[...] = jnp.zeros_like(ys_ref)


def _experts(xs, blk_e, nused, w_eg, w_eu, w_ed, layer):
    nb = xs.shape[0] // BLK
    wspec = lambda s: pl.BlockSpec((None, None) + s, lambda b, be, nu: (layer, be[b], 0, 0))
    grid_spec = pltpu.PrefetchScalarGridSpec(
        num_scalar_prefetch=2,
        grid=(nb,),
        in_specs=[pl.BlockSpec((BLK, D_MODEL), lambda b, be, nu: (jnp.minimum(b, nu[0] - 1), 0)),
                  wspec((D_MODEL, D_EXPERT)), wspec((D_MODEL, D_EXPERT)), wspec((D_EXPERT, D_MODEL))],
        out_specs=pl.BlockSpec((BLK, D_MODEL), lambda b, be, nu: (b, 0)),
        scratch_shapes=[pltpu.VMEM((D_MODEL, D_EXPERT), BF16), pltpu.VMEM((D_MODEL, D_EXPERT), BF16),
                        pltpu.VMEM((D_EXPERT, D_MODEL), BF16)],
    )
    return pl.pallas_call(
        _experts_kernel,
        grid_spec=grid_spec,
        out_shape=jax.ShapeDtypeStruct(xs.shape, F32),
        compiler_params=_cparams(("arbitrary",)),
        name="experts",
    )(blk_e, nused, xs, w_eg, w_eu, w_ed)


def _post_kernel(cnt_ref, lofs_ref, gofs_ref, ltot_ref, x1_ref, route_ref, ls_ref, pp_ref, ps_ref, wpg_ref, wpp_ref,
                 lng_ref, lnb_ref, ys_hbm, o_ref, yloc, wpg_b, wpp_b, sem, *, prompt_tiles):
    i = pl.program_id(0)
    last = pl.num_programs(0) - 1
    slot = i % 2

    def copy_to(s):
        return lambda lo, go, n: pltpu.make_async_copy(ys_hbm.at[pl.ds(go, n)], yloc.at[s, pl.ds(lo, n)], sem.at[s])

    @pl.when(i == 0)
    def _():
        _cast_into(wpg_b, wpg_ref)
        _cast_into(wpp_b, wpp_ref)
        yloc[...] = jnp.zeros_like(yloc)
        _segment_copies(0, cnt_ref, lofs_ref, gofs_ref, copy_to(0))

    _wait_rows(ltot_ref[i], copy_to(slot))

    @pl.when(i < last)
    def _():
        _segment_copies(i + 1, cnt_ref, lofs_ref, gofs_ref, copy_to(1 - slot))

    route = route_ref[...]
    ls = ls_ref[...]
    j = lax.broadcasted_iota(I32, (ls.shape[0], LOC_ROWS), 1)
    w = jnp.where(j == ls[:, 0:1], route[:, 2:3], 0.0) + jnp.where(j == ls[:, 1:2], route[:, 3:4], 0.0)
    moe = _dot(w.astype(BF16), yloc[slot].astype(BF16))
    x2 = _layer_norm(DN_ALPHA * x1_ref[...] + moe, lng_ref[1:2, :], lnb_ref[1:2, :])
    p = jnp.where(i < prompt_tiles, pp_ref[...], ps_ref[...])
    ple = jax.nn.sigmoid(_dot(x2.astype(BF16), wpg_b[...])) * _dot(p.astype(BF16), wpp_b[...])
    o_ref[...] = _layer_norm(DN_ALPHA * x2 + ple, lng_ref[2:3, :], lnb_ref[2:3, :])


def _post(x1, ys, cnt, lofs, gofs, ltot, lslot, route, p_prompt, p_sample, w_pg, w_pp, ln_g, ln_b, layer):
    m = x1.shape[0]
    pt = p_prompt.shape[1] // TD
    rowblk = lambda w: pl.BlockSpec((TD, w), lambda i, *_: (i, 0))
    lspec = lambda s: pl.BlockSpec((None,) + s, lambda i, *_: (layer,) + (0,) * len(s), pipeline_mode=pl.Buffered(1))
    grid_spec = pltpu.PrefetchScalarGridSpec(
        num_scalar_prefetch=4,
        grid=(m // TD,),
        in_specs=[rowblk(D_MODEL), rowblk(LANES), rowblk(2),
                  pl.BlockSpec((None, TD, D_PLE), lambda i, *_: (layer, jnp.minimum(i, pt - 1), 0)),
                  pl.BlockSpec((None, TD, D_PLE), lambda i, *_: (layer, jnp.maximum(i - pt, 0), 0)),
                  lspec((D_MODEL, D_MODEL)), lspec((D_PLE, D_MODEL)), lspec((3, D_MODEL)), lspec((3, D_MODEL)),
                  pl.BlockSpec(memory_space=pl.ANY)],
        out_specs=rowblk(D_MODEL),
        scratch_shapes=[pltpu.VMEM((2, LOC_ROWS, D_MODEL), F32), pltpu.VMEM((D_MODEL, D_MODEL), BF16),
                        pltpu.VMEM((D_PLE, D_MODEL), BF16), pltpu.SemaphoreType.DMA((2,))],
    )
    return pl.pallas_call(
        functools.partial(_post_kernel, prompt_tiles=pt),
        grid_spec=grid_spec,
        out_shape=jax.ShapeDtypeStruct((m, D_MODEL), F32),
        compiler_params=_cparams(("arbitrary",)),
        name="post",
    )(cnt, lofs, gofs, ltot, x1, route, lslot, p_prompt, p_sample, w_pg, w_pp, ln_g, ln_b, ys)


def kernel(x_prompt, x_sample, state_mlstm_C, state_mlstm_n, state_mlstm_m, state_swa_k, state_swa_v, p_prompt, p_sample, w_in, b_in, mh_gain, w_a, w_b, w_out, rel_table, w_sink, ln_g, ln_b, w_rg, b_rg, w_re, b_re, w_eg, w_eu, w_ed, w_pg, w_pp):
    bp_, seq = x_prompt.shape[:2]
    bs_, tdec = x_sample.shape[:2]
    mp = bp_ * seq
    msp = bs_ * T_PAD
    mall = mp + msp
    nblk = -(-(2 * (mp + bs_ * tdec) + (mall // TD) * N_EXPERTS * (SEG_ALIGN - 1)) // BLK) + N_EXPERTS

    pad_t = lambda t: jnp.pad(t, [(0, 0)] * (t.ndim - 2) + [(0, T_PAD - tdec), (0, 0)])
    x = lax.dynamic_update_slice(jnp.pad(x_prompt.reshape(mp, D_MODEL), ((0, msp), (0, 0))),
                                 pad_t(x_sample).reshape(msp, D_MODEL), (mp, 0))
    valid = jnp.concatenate([jnp.ones((mp,), bool), jnp.tile(jnp.arange(T_PAD) < tdec, bs_)])
    p_prompt = p_prompt.reshape(DEPTH, mp, D_PLE)
    p_sample = pad_t(p_sample).reshape(DEPTH, msp, D_PLE)

    b_in3 = b_in.reshape(DEPTH, 1, -1)
    b_b = jnp.concatenate([b_in3[:, :, IN_GATES[0]:IN_GATES[1]], b_in3[:, :, IN_QKV2[0]:IN_QKV2[1]]], axis=-1)
    b_rg, b_re = b_rg.reshape(DEPTH, 1, N_GROUPS), b_re.reshape(DEPTH, 1, N_EXPERTS)
    gain = mh_gain.reshape(DEPTH, 1, D_MODEL)

    bias_p = _swa_bias(rel_table, WINDOW)
    bias_s = _swa_bias(rel_table, T_PAD)
    sink_p = sink_s = _sink_lanes(w_sink)
    buf_k = state_swa_k.reshape(DEPTH, bs_, WINDOW, D_KV)
    buf_v = state_swa_v.reshape(DEPTH, bs_, WINDOW, D_KV)
    m0_s = jnp.broadcast_to(state_mlstm_m[..., None], (DEPTH, bs_, M_HEADS, LANES))
    zc = jnp.zeros((1, bp_, M_HEADS, M_DQK, M_DV), F32)
    zn = jnp.zeros((1, bp_, M_HEADS, LANES), F32)
    st_s = (jnp.zeros_like(state_mlstm_C), jnp.zeros_like(state_mlstm_n), jnp.zeros_like(m0_s))
    kv_s = (jnp.zeros_like(buf_k), jnp.zeros_like(buf_v))

    st_p = []
    for i in range(DEPTH):
        za, zg = _inproj(x, [(w_in, IN_A)], [(b_in3, IN_A)], [(ZG, BF16), (LANES, F32)], i, "inproj_a")
        zb = _inproj_b(x, w_in, b_b, i)

        ya_p, c_p, n_p, m_p = _mlstm(za, zg, gain, i, zc, zn, zn, 0, None, 1, 0, row0=0, nseq=bp_,
                                     steps=seq // LANES, seg=LANES, t_valid=LANES)
        ya_s, *st_s = _mlstm(za, zg, gain, i, state_mlstm_C, state_mlstm_n, m0_s, i, st_s, DEPTH, i, row0=mp,
                             nseq=bs_ // SEQ_BLK, steps=1, seg=T_PAD, t_valid=tdec)
        yb_p = _swa_prompt(zb, bias_p[0], bias_p[1], sink_p, i, batch=bp_, seq=seq)
        yb_s, *kv_s = _swa_sample(zb, buf_k, buf_v, bias_s[0], bias_s[1], sink_s, i, kv_s, row0=mp, batch=bs_,
                                  t_new=tdec)

        x1, route = _merge(x, ya_p, ya_s, yb_p, yb_s, zb, w_a, w_b, w_out, ln_g, ln_b, w_rg, w_re, b_rg, b_re, i)
        cnt, lofs, gofs, ltot, lslot, zstart, blk_e, nused = _route_tables(route[:, :2].astype(I32), valid, nblk)
        xs = _dispatch(x1, cnt, lofs, gofs, ltot, zstart, lslot, nblk)
        ys = _experts(xs, blk_e, nused, w_eg, w_eu, w_ed, i)
        x = _post(x1, ys, cnt, lofs, gofs, ltot, lslot, route, p_prompt, p_sample, w_pg, w_pp, ln_g, ln_b, i)

        kv_p = jnp.stack([lax.slice(zb, ((b + 1) * seq - WINDOW, ZKS), ((b + 1) * seq, IN_B)) for b in range(bp_)])
        kv_p = kv_p.astype(F32)
        heads = lambda t: t.reshape(t.shape[:-1] + (S_KV_HEADS, S_HEAD_DIM))
        st_p.append((c_p[0], n_p[0], m_p[0, :, :, 0], heads(kv_p[..., :D_KV]), heads(kv_p[..., D_KV:])))

    stk = lambda j: jnp.stack([s[j] for s in st_p], axis=0)
    y_p = x[:mp].reshape(bp_, seq, D_MODEL)
    y_s = x[mp:].reshape(bs_, T_PAD, D_MODEL)[:, :tdec]
    return (y_p, y_s, stk(0), stk(1), stk(2), stk(3), stk(4),
            st_s[0], st_s[1], st_s[2][..., 0], heads(kv_s[0]), heads(kv_s[1]))
```

```python
import functools

import numpy as np
import jax
import jax.numpy as jnp
from jax import lax
from jax.experimental import pallas as pl
from jax.experimental.pallas import tpu as pltpu

F32 = jnp.float32
BF16 = jnp.bfloat16
I32 = jnp.int32
HIGHEST = lax.Precision.HIGHEST

D_MODEL = 1024
DEPTH = 4
D_PLE = 256
M_HEADS = 4
M_DQK = 128
M_DV = 256
S_HEADS = 16
S_KV_HEADS = 4
S_HEAD_DIM = 64
S_GROUP = 4
D_KV = S_KV_HEADS * S_HEAD_DIM
WINDOW = 128
REL_BUCKETS = 32
N_GROUPS = 4
EXPERTS_PER_GROUP = 8
N_EXPERTS = 32
D_EXPERT = 512
DN_ALPHA = (2 * DEPTH) ** 0.25
LN_EPS = 1e-5
NEG = -1e30

IN_A = 3200
IN_QKV2 = (3080, 4616)
IN_GATES = (4616, 6664)
ZQ, ZK, ZV, ZOG, ZG = 0, 512, 1024, 2048, 3072
IN_B = 3584
ZGA, ZGB, ZQS, ZKS, ZVS = 0, 1024, 2048, 3072, 3328

LANES = 128
T_PAD = 8
SEQ_BLK = LANES // T_PAD
TM = 512
TD = 256
BLK = 512
VMEM_LIMIT = 56 * 1024 * 1024


def _cparams(sem):
    return pltpu.CompilerParams(dimension_semantics=sem, vmem_limit_bytes=VMEM_LIMIT)


def _layer_spec(shape, layer):
    nd = len(shape)
    return pl.BlockSpec((None,) + shape, lambda *_: (layer,) + (0,) * nd, pipeline_mode=pl.Buffered(1))


def _layer_norm(y, g, b):
    mu = jnp.mean(y, axis=-1, keepdims=True)
    d = y - mu
    var = jnp.mean(d * d, axis=-1, keepdims=True)
    return d * lax.rsqrt(var + LN_EPS) * g + b


def _dot(a, b):
    return jnp.dot(a, b, preferred_element_type=F32)


def _dot_nt(a, b):
    return lax.dot_general(a, b, (((1,), (1,)), ((), ())), preferred_element_type=F32)


def _dot_tn(a, b):
    return lax.dot_general(a, b, (((0,), (0,)), ((), ())), preferred_element_type=F32)


def _dot_exact(a, b):
    return jnp.dot(a, b, precision=HIGHEST, preferred_element_type=F32)


def _cast_into(dst_ref, src_ref, col0=0, step=512):
    n = src_ref.shape[-1]
    for j in range(0, n, step):
        w = min(step, n - j)
        dst_ref[:, col0 + j:col0 + j + w] = src_ref[:, j:j + w].astype(BF16)


def _inproj_kernel(*refs, widths, out_widths):
    ng, no = len(widths), len(out_widths)
    x_ref, w_refs, b_refs = refs[0], refs[1:1 + ng], refs[1 + ng:1 + 2 * ng]
    o_refs, wbf = refs[1 + 2 * ng:1 + 2 * ng + no], refs[1 + 2 * ng + no]

    @pl.when(pl.program_id(0) == 0)
    def _():
        c = 0
        for w_ref, n in zip(w_refs, widths):
            _cast_into(wbf, w_ref, c)
            c += n

    out_start = np.cumsum((0,) + tuple(out_widths))

    def store(col, val):
        k = int(np.searchsorted(out_start, col, side='right')) - 1
        lo = col - int(out_start[k])
        o_refs[k][:, lo:lo + val.shape[1]] = val.astype(o_refs[k].dtype)

    xb = x_ref[...].astype(BF16)
    c = 0
    for b_ref, n in zip(b_refs, widths):
        for j in range(0, n, 512):
            w = min(512, n - j)
            store(c + j, _dot(xb, wbf[:, c + j:c + j + w]) + b_ref[:, j:j + w])
        c += n


def _inproj(x, ws, bs, outs, layer, name):
    m = x.shape[0]
    widths = tuple(n for _, n in ws)
    out_widths = tuple(n for n, _ in outs)
    assert sum(widths) == sum(out_widths)
    return pl.pallas_call(
        functools.partial(_inproj_kernel, widths=widths, out_widths=out_widths),
        grid=(m // TM,),
        in_specs=([pl.BlockSpec((TM, D_MODEL), lambda i: (i, 0))]
                  + [_layer_spec((D_MODEL, n), layer) for _, n in ws]
                  + [_layer_spec((1, n), layer) for _, n in bs]),
        out_specs=[pl.BlockSpec((TM, n), lambda i: (i, 0)) for n, _ in outs],
        out_shape=[jax.ShapeDtypeStruct((m, n), dt) for n, dt in outs],
        scratch_shapes=[pltpu.VMEM((D_MODEL, sum(widths)), BF16)],
        compiler_params=_cparams(("arbitrary",)),
        name=name,
    )(x, *[w for w, _ in ws], *[b for b, _ in bs])


def _inproj_b_kernel(x_ref, b_ref, w_hbm, z_ref, wwin, wbf, sem, *, layer):
    lead = IN_QKV2[0] % LANES
    n_qkv = IN_QKV2[1] - IN_QKV2[0]

    @pl.when(pl.program_id(0) == 0)
    def _():
        cp = pltpu.make_async_copy(w_hbm.at[layer, :, pl.ds(IN_QKV2[0] - lead, wwin.shape[1])], wwin, sem.at[0])
        cp.start()
        cp.wait()
        for dst, src, n in ((0, n_qkv, IN_B - n_qkv), (IN_B - n_qkv, 0, n_qkv)):
            for j in range(0, n, 512):
                w = min(512, n - j)
                wbf[:, dst + j:dst + j + w] = wwin[:, lead + src + j:lead + src + j + w].astype(BF16)

    xb = x_ref[...].astype(BF16)
    for j in range(0, IN_B, 512):
        w = min(512, IN_B - j)
        z_ref[:, j:j + w] = (_dot(xb, wbf[:, j:j + w]) + b_ref[:, j:j + w]).astype(z_ref.dtype)


def _inproj_b(x, w_in, b_b, layer):
    m = x.shape[0]
    win = w_in.shape[2] - (IN_QKV2[0] - IN_QKV2[0] % LANES)
    return pl.pallas_call(
        functools.partial(_inproj_b_kernel, layer=layer),
        grid=(m // TM,),
        in_specs=[pl.BlockSpec((TM, D_MODEL), lambda i: (i, 0)), _layer_spec((1, IN_B), layer),
                  pl.BlockSpec(memory_space=pl.ANY)],
        out_specs=pl.BlockSpec((TM, IN_B), lambda i: (i, 0)),
        out_shape=jax.ShapeDtypeStruct((m, IN_B), BF16),
        scratch_shapes=[pltpu.VMEM((D_MODEL, win), F32), pltpu.VMEM((D_MODEL, IN_B), BF16),
                        pltpu.SemaphoreType.DMA((1,))],
        compiler_params=_cparams(("arbitrary",)),
        name="inproj_b",
    )(x, b_b, w_in)


def _log_sigmoid(x):
    return jnp.minimum(x, 0.0) - jnp.log(1.0 + jnp.exp(-jnp.abs(x)))


def _mlstm_kernel(zq, zk, zv, zog, zg, gain, c0, n0, m0, *rest, seg, t_valid, aliased):
    ya, co, no, mo, qc_s = rest[aliased:]

    @pl.when(pl.program_id(1) == 0)
    def _():
        co[...] = c0[...]
        no[...] = n0[...]
        mo[...] = m0[...]

    for r0 in range(0, zq.shape[0], LANES):
        rows = pl.ds(r0, LANES)
        _mlstm_block(zq.at[rows], zk.at[rows], zv.at[rows], zog.at[rows], zg.at[rows], gain, ya.at[rows],
                     co, no, mo, qc_s, seg=seg, t_valid=t_valid)


def _mlstm_block(zq, zk, zv, zog, zg, gain, ya, co, no, mo, qc_s, *, seg, t_valid):
    L = zq.shape[0]
    nseg = L // seg
    g_c = zg[...]
    g_r = g_c.T
    ig_c, lf_c = g_c, _log_sigmoid(g_c)
    ig_r, lf_r = g_r, _log_sigmoid(g_r)
    ri = lax.broadcasted_iota(I32, (L, L), 0)
    ci = lax.broadcasted_iota(I32, (L, L), 1)
    shift = seg.bit_length() - 1
    if t_valid < seg:
        rv = (lax.broadcasted_iota(I32, (L, 1), 0) & (seg - 1)) < t_valid
        cv = (lax.broadcasted_iota(I32, (1, L), 1) & (seg - 1)) < t_valid
        ig_c, lf_c = jnp.where(rv, ig_c, NEG), jnp.where(rv, lf_c, 0.0)
        ig_r, lf_r = jnp.where(cv, ig_r, NEG), jnp.where(cv, lf_r, 0.0)
    if nseg > 1:
        same = (ri >> shift) == (ci >> shift)
        causal, causal_t = same & (ri >= ci), same & (ri <= ci)
        same_f = same.astype(F32)
        be_c = _dot_exact(same_f, lf_c)
        be_r = _dot_exact(lf_r, same_f)
        expand = ((lax.broadcasted_iota(I32, (L, nseg), 0) >> shift)
                  == lax.broadcasted_iota(I32, (L, nseg), 1)).astype(F32)
    else:
        causal, causal_t = ri >= ci, ri <= ci
    b_c = _dot_exact(causal.astype(F32), lf_c)
    b_r = _dot_exact(lf_r, causal_t.astype(F32))

    for h in range(M_HEADS):
        icol, irow = ig_c[:, h:h + 1], ig_r[h:h + 1, :]
        bcol, brow = b_c[:, 4 + h:5 + h], b_r[4 + h:5 + h, :]
        q = zq[:, h * M_DQK:(h + 1) * M_DQK].astype(F32) * (M_DQK ** -0.5)
        k = zk[:, h * M_DQK:(h + 1) * M_DQK].astype(F32)
        qb, kb = q.astype(BF16), zk[:, h * M_DQK:(h + 1) * M_DQK]
        vb = zv[:, h * M_DV:(h + 1) * M_DV]
        if nseg > 1:
            mcol = _dot_exact(expand, mo[:, h, :])[:, 0:1]
            nrow = _dot_exact(expand, no[:, h, :])
            becol, berow = be_c[:, 4 + h:5 + h], be_r[4 + h:5 + h, :]
            for j in range(nseg):
                rows = slice(j * seg, (j + 1) * seg)
                qc_s[rows, :] = _dot(q[rows].astype(BF16), co[j, h].astype(BF16))
            q_c = qc_s[...]
        else:
            mcol, nrow = mo[0, h:h + 1, 0:1], no[0, h:h + 1, :]
            becol = bcol[L - 1:L, :]
            q_c = _dot(qb, co[0, h].astype(BF16))

        dmat = jnp.where(causal, bcol - brow + irow, NEG)
        inter = bcol + mcol
        mhat = jnp.maximum(inter, jnp.max(dmat, axis=-1, keepdims=True))
        w_intra = jnp.exp(dmat - mhat)
        w_inter = jnp.exp(inter - mhat)
        s = _dot_nt(qb, kb) * w_intra
        num = _dot(s.astype(BF16), vb) + w_inter * q_c
        den = jnp.sum(s, axis=-1, keepdims=True) + w_inter * jnp.sum(q * nrow, axis=-1, keepdims=True)
        hh = num / jnp.maximum(jnp.abs(den), jnp.exp(-mhat))

        gcol = icol + becol - bcol
        if nseg > 1:
            gmax = jnp.max(jnp.where(same, irow + berow - brow, NEG), axis=-1, keepdims=True)
        else:
            gmax = jnp.max(gcol, axis=0, keepdims=True)
        m_new = jnp.maximum(becol + mcol, gmax)
        decay = jnp.exp(becol + mcol - m_new)
        kw = k * jnp.exp(gcol - m_new)
        vf = vb.astype(F32) if nseg > 1 else vb
        for j in range(nseg):
            rows = slice(j * seg, (j + 1) * seg)
            dj = decay[j * seg:j * seg + 1, :] if nseg > 1 else decay
            mj = m_new[j * seg:j * seg + 1, :] if nseg > 1 else m_new
            co[j, h] = dj * co[j, h] + _dot_tn(kw[rows].astype(BF16), vf[rows].astype(BF16))
            no[j, h:h + 1, :] = dj * no[j, h:h + 1, :] + jnp.sum(kw[rows], axis=0, keepdims=True)
            mo[j, h:h + 1, :] = jnp.broadcast_to(mj, (1, LANES))

        mu = jnp.mean(hh, axis=-1, keepdims=True)
        d = hh - mu
        var = jnp.mean(d * d, axis=-1, keepdims=True)
        hn = d * lax.rsqrt(var + LN_EPS) * gain[:, h * M_DV:(h + 1) * M_DV]
        og = zog[:, h * M_DV:(h + 1) * M_DV].astype(F32)
        ya[:, h * M_DV:(h + 1) * M_DV] = (hn * jax.nn.sigmoid(og)).astype(ya.dtype)


def _mlstm(za, zg, gain, layer, c0, n0, m0, st_layer, outs_prev, out_layers, out_layer, *, row0, nseq, steps,
           seg, t_valid, blocks_per_step=1):
    nseg = LANES // seg
    rows = LANES * blocks_per_step
    r0 = row0 // rows
    row = lambda b, c: r0 + b * steps + c
    nstate = nseq * nseg
    aliased = 0 if outs_prev is None else 3
    kern = functools.partial(_mlstm_kernel, seg=seg, t_valid=min(t_valid, seg), aliased=aliased)
    in4 = pl.BlockSpec((None, nseg, M_HEADS, M_DQK, M_DV), lambda b, c: (st_layer, b, 0, 0, 0))
    in3 = pl.BlockSpec((None, nseg, M_HEADS, LANES), lambda b, c: (st_layer, b, 0, 0))
    out4 = pl.BlockSpec((None, nseg, M_HEADS, M_DQK, M_DV), lambda b, c: (out_layer, b, 0, 0, 0))
    out3 = pl.BlockSpec((None, nseg, M_HEADS, LANES), lambda b, c: (out_layer, b, 0, 0))
    anyspec = pl.BlockSpec(memory_space=pl.ANY)
    n_in = 9
    return pl.pallas_call(
        kern,
        grid=(nseq, steps),
        in_specs=[pl.BlockSpec((rows, 512), lambda b, c: (row(b, c), ZQ // 512)),
                  pl.BlockSpec((rows, 512), lambda b, c: (row(b, c), ZK // 512)),
                  pl.BlockSpec((rows, 1024), lambda b, c: (row(b, c), ZV // 1024)),
                  pl.BlockSpec((rows, 1024), lambda b, c: (row(b, c), ZOG // 1024)),
                  pl.BlockSpec((rows, LANES), lambda b, c: (row(b, c), 0)),
                  pl.BlockSpec((None, 1, D_MODEL), lambda b, c: (layer, 0, 0)),
                  in4, in3, in3] + [anyspec] * aliased,
        out_specs=[pl.BlockSpec((rows, D_MODEL), lambda b, c: (b * steps + c, 0)), out4, out3, out3],
        out_shape=[jax.ShapeDtypeStruct((nseq * steps * rows, D_MODEL), BF16),
                   jax.ShapeDtypeStruct((out_layers, nstate, M_HEADS, M_DQK, M_DV), F32),
                   jax.ShapeDtypeStruct((out_layers, nstate, M_HEADS, LANES), F32),
                   jax.ShapeDtypeStruct((out_layers, nstate, M_HEADS, LANES), F32)],
        scratch_shapes=[pltpu.VMEM((LANES, M_DV), F32)],
        input_output_aliases={n_in + j: 1 + j for j in range(aliased)},
        compiler_params=_cparams(("parallel", "arbitrary")),
        name="mlstm",
    )(za, za, za, za, zg, gain, c0, n0, m0, *(outs_prev or ()))


def _swa_kernel(q_ref, kc_ref, vc_ref, kp_ref, vp_ref, bp_ref, bc_ref, sink_ref, *rest, nseq, t_new,
                first_block_has_no_past, aliased):
    o_ref = rest[aliased]
    rows_n = q_ref.shape[0]
    tq = rows_n // nseq
    have_past = pl.program_id(1) > 0
    lane = lax.broadcasted_iota(I32, (1, LANES), 1)
    in_lo = lane < S_HEAD_DIM
    if nseq > 1:
        kp_all = kp_ref[...].reshape(nseq * WINDOW, D_KV).astype(BF16)
        vp_all = vp_ref[...].reshape(nseq * WINDOW, D_KV).astype(BF16)
        row_seq = lax.broadcasted_iota(I32, (rows_n, LANES), 0) >> (tq.bit_length() - 1)
    else:
        kp_all, vp_all = kp_ref[...], vp_ref[...]
    kc_all, vc_all = kc_ref[...], vc_ref[...]

    def placed(x, src_half, ones):
        swapped = jnp.concatenate([x[:, S_HEAD_DIM:], x[:, :S_HEAD_DIM]], axis=1)
        zero = jnp.zeros_like(x)
        lo = jnp.where(in_lo, x if src_half == 0 else swapped, zero)
        hi = jnp.where(in_lo, zero, x if src_half == 1 else swapped)
        if ones:
            lo = jnp.where(lane == S_HEAD_DIM, jnp.ones_like(x), lo)
            hi = jnp.where(lane == 0, jnp.ones_like(x), hi)
        return lo, hi

    kv_per_pass = S_KV_HEADS if nseq == 1 else 1
    for kv0 in range(0, S_KV_HEADS, kv_per_pass):
        kp_v, kc_v, vp_v, vc_v = {}, {}, {}, {}
        for kvh in range(kv0, kv0 + kv_per_pass):
            tile = slice((kvh // 2) * LANES, (kvh // 2 + 1) * LANES)
            kp_v[kvh] = placed(kp_all[:, tile], kvh % 2, False)
            kc_v[kvh] = placed(kc_all[:, tile], kvh % 2, False)
            vp_v[kvh] = placed(vp_all[:, tile], kvh % 2, True)
            vc_v[kvh] = placed(vc_all[:, tile], kvh % 2, True)
        heads = list(range(kv0 * S_GROUP, (kv0 + kv_per_pass) * S_GROUP))
        scores = []
        for h in heads:
            q2 = q_ref[:, (h // 2) * LANES:(h // 2 + 1) * LANES]
            scores.append((_dot_nt(q2, kp_v[h // S_GROUP][h % 2]), _dot_nt(q2, kc_v[h // S_GROUP][h % 2])))
        probs = []
        for h, (sp, sc) in zip(heads, scores):
            if nseq > 1:
                sp = jnp.concatenate([sp[b * tq:(b + 1) * tq, b * WINDOW:(b + 1) * WINDOW] for b in range(nseq)],
                                     axis=0)
            sp = sp * (S_HEAD_DIM ** -0.5) + bp_ref[h]
            sc = sc * (S_HEAD_DIM ** -0.5) + bc_ref[h]
            if first_block_has_no_past:
                sp = jnp.where(have_past, sp, NEG)
            sk = sink_ref[h][:, 0:1]
            mx = jnp.maximum(jnp.max(jnp.maximum(sp, sc), axis=-1, keepdims=True), sk)
            pp, pc = jnp.exp(sp - mx), jnp.exp(sc - mx)
            if nseq > 1:
                pp = jnp.concatenate([jnp.where(row_seq == b, pp, 0.0) for b in range(nseq)], axis=1)
            probs.append((pp.astype(BF16), pc.astype(BF16), jnp.exp(sk - mx)))
        outs = [_dot(pp, vp_v[h // S_GROUP][h % 2]) + _dot(pc, vc_v[h // S_GROUP][h % 2])
                for h, (pp, pc, _) in zip(heads, probs)]
        for pair in range(len(heads) // 2):
            halves = []
            for half in range(2):
                o, sink_term = outs[2 * pair + half], probs[2 * pair + half][2]
                ones_lane = S_HEAD_DIM if half == 0 else 0
                halves.append(o / (o[:, ones_lane:ones_lane + 1] + sink_term))
            qt = heads[2 * pair] // 2
            o_ref[:, qt * LANES:(qt + 1) * LANES] = jnp.where(in_lo, halves[0], halves[1]).astype(o_ref.dtype)

    if aliased:
        kc_f, vc_f = kc_all.astype(F32), vc_all.astype(F32)
        for b in range(nseq):
            for new, prev, out in ((kc_f, kp_ref, rest[aliased + 1]), (vc_f, vp_ref, rest[aliased + 2])):
                out[b, 0:WINDOW - t_new, :] = prev[b, t_new:, :]
                out[b, WINDOW - t_new:WINDOW, :] = new[b * tq:b * tq + t_new, :]


def _rel_bucket(dist):
    nn = np.maximum(dist, 0)
    max_exact = REL_BUCKETS // 2
    large = max_exact + (np.log(np.maximum(nn, 1) / max_exact) / np.log(WINDOW / max_exact)
                         * (REL_BUCKETS - max_exact)).astype(np.int32)
    large = np.minimum(large, REL_BUCKETS - 1)
    return np.where(nn < max_exact, nn, large).astype(np.int32)


def _swa_bias(rel_table, tq):
    nseq = LANES // tq
    t = np.tile(np.arange(tq), nseq)[:, None]
    seq = np.repeat(np.arange(nseq), tq)
    d_prev = t + WINDOW - np.arange(WINDOW)[None, :]
    d_cur = t - t.T
    own = seq[:, None] == seq[None, :]

    def build(dist, ok):
        onehot = np.eye(REL_BUCKETS, dtype=np.float32)[_rel_bucket(dist).reshape(-1)]
        bias = jnp.dot(onehot, rel_table.astype(F32), precision=HIGHEST).reshape(dist.shape + (S_HEADS,))
        return jnp.where(ok[None], jnp.transpose(bias, (2, 0, 1)), NEG)

    return build(d_prev, d_prev <= WINDOW), build(d_cur, own & (d_cur >= 0))


def _sink_lanes(w_sink):
    return jnp.broadcast_to(w_sink.astype(F32)[:, :, None, None], (DEPTH, S_HEADS, 1, LANES))


def _swa_bias_specs(layer):
    return [pl.BlockSpec((S_HEADS, LANES, WINDOW), lambda b, n: (0, 0, 0)),
            pl.BlockSpec((S_HEADS, LANES, LANES), lambda b, n: (0, 0, 0)),
            pl.BlockSpec((None, S_HEADS, 1, LANES), lambda b, n: (layer, 0, 0, 0))]


def _swa_prompt(zb, bp, bc, sk, layer, *, batch, seq):
    nb = seq // WINDOW
    row = lambda b, n: b * nb + n
    prev = lambda b, n: jnp.maximum(b * nb + n - 1, 0)
    in_specs = [pl.BlockSpec((WINDOW, 1024), lambda b, n: (row(b, n), ZQS // 1024)),
                pl.BlockSpec((WINDOW, D_KV), lambda b, n: (row(b, n), ZKS // D_KV)),
                pl.BlockSpec((WINDOW, D_KV), lambda b, n: (row(b, n), ZVS // D_KV)),
                pl.BlockSpec((WINDOW, D_KV), lambda b, n: (prev(b, n), ZKS // D_KV)),
                pl.BlockSpec((WINDOW, D_KV), lambda b, n: (prev(b, n), ZVS // D_KV))]
    kern = functools.partial(_swa_kernel, nseq=1, t_new=0, first_block_has_no_past=True, aliased=0)
    return pl.pallas_call(
        kern,
        grid=(batch, nb),
        in_specs=in_specs + _swa_bias_specs(layer),
        out_specs=pl.BlockSpec((WINDOW, D_MODEL), lambda b, n: (row(b, n), 0)),
        out_shape=jax.ShapeDtypeStruct((batch * seq, D_MODEL), BF16),
        compiler_params=_cparams(("parallel", "arbitrary")),
        name="swa_prompt",
    )(zb, zb, zb, zb, zb, bp, bc, sk)


def _swa_sample(zb, buf_k, buf_v, bp, bc, sk, layer, kv_prev, *, row0, batch, t_new):
    r0 = row0 // LANES
    steps = batch // SEQ_BLK
    state = pl.BlockSpec((None, SEQ_BLK, WINDOW, D_KV), lambda b, n: (layer, b, 0, 0))
    anyspec = pl.BlockSpec(memory_space=pl.ANY)
    in_specs = [pl.BlockSpec((LANES, 1024), lambda b, n: (r0 + b, ZQS // 1024)),
                pl.BlockSpec((LANES, D_KV), lambda b, n: (r0 + b, ZKS // D_KV)),
                pl.BlockSpec((LANES, D_KV), lambda b, n: (r0 + b, ZVS // D_KV)),
                state, state]
    kern = functools.partial(_swa_kernel, nseq=SEQ_BLK, t_new=t_new, first_block_has_no_past=False, aliased=2)
    return pl.pallas_call(
        kern,
        grid=(steps, 1),
        in_specs=in_specs + _swa_bias_specs(layer) + [anyspec, anyspec],
        out_specs=[pl.BlockSpec((LANES, D_MODEL), lambda b, n: (b, 0)), state, state],
        out_shape=[jax.ShapeDtypeStruct((batch * T_PAD, D_MODEL), BF16),
                   jax.ShapeDtypeStruct(buf_k.shape, F32), jax.ShapeDtypeStruct(buf_v.shape, F32)],
        input_output_aliases={8: 1, 9: 2},
        compiler_params=_cparams(("parallel", "arbitrary")),
        name="swa_sample",
    )(zb, zb, zb, buf_k, buf_v, bp, bc, sk, *kv_prev)


def _merge_kernel(x_ref, yap_ref, yas_ref, ybp_ref, ybs_ref, ga_ref, gb_ref, wa_ref, wb_ref, wo_ref, lng_ref, lnb_ref,
                  wrg_ref, wre_ref, brg_ref, bre_ref, x1_ref, route_ref, wbf, wr_b, br_s, *, prompt_tiles):
    i = pl.program_id(0)

    @pl.when(i == 0)
    def _():
        for j, w_ref in enumerate((wa_ref, wb_ref, wo_ref)):
            _cast_into(wbf.at[j], w_ref)
        wr_b[...] = jnp.zeros_like(wr_b)
        wr_b[:, 0:N_GROUPS] = wrg_ref[...].astype(BF16)
        wr_b[:, N_GROUPS:N_GROUPS + N_EXPERTS] = wre_ref[...].astype(BF16)
        br_s[...] = jnp.zeros_like(br_s)
        br_s[:, 0:N_GROUPS] = brg_ref[...]
        br_s[:, N_GROUPS:N_GROUPS + N_EXPERTS] = bre_ref[...]

    is_prompt = i < prompt_tiles
    half = x_ref.shape[0] // 2
    for r0 in (0, half):
        rs = slice(r0, r0 + half)
        ya = jnp.where(is_prompt, yap_ref[rs, :], yas_ref[rs, :])
        yb = jnp.where(is_prompt, ybp_ref[rs, :], ybs_ref[rs, :])
        a = _dot(ya, wbf[0])
        b = _dot(yb, wbf[1])
        u = jax.nn.sigmoid(ga_ref[rs, :].astype(F32)) * a + jax.nn.sigmoid(gb_ref[rs, :].astype(F32)) * b
        mix = _dot(u.astype(BF16), wbf[2])
        x1_ref[rs, :] = _layer_norm(DN_ALPHA * x_ref[rs, :] + mix, lng_ref[0:1, :], lnb_ref[0:1, :])
    x1 = x1_ref[...]

    logits = _dot(x1.astype(BF16), wr_b[...]) + br_s[...]
    lane = lax.broadcasted_iota(I32, logits.shape, 1)
    lanef = lane.astype(F32)
    gmask = lane < N_GROUPS
    gl = jnp.where(gmask, logits, NEG)
    gmax = jnp.max(gl, axis=-1, keepdims=True)
    gidx = jnp.min(jnp.where(gmask & (gl == gmax), lanef, 999.0), axis=-1, keepdims=True)
    g_w = 1.0 / jnp.sum(jnp.where(gmask, jnp.exp(gl - gmax), 0.0), axis=-1, keepdims=True)
    lo = N_GROUPS + EXPERTS_PER_GROUP * gidx
    emask = (lanef >= lo) & (lanef < lo + EXPERTS_PER_GROUP)
    el = jnp.where(emask, logits, NEG)
    emax = jnp.max(el, axis=-1, keepdims=True)
    ep = jnp.where(emask, jnp.exp(el - emax), 0.0)
    prob = ep / jnp.sum(ep, axis=-1, keepdims=True)
    prob = jnp.where(emask, prob, -1.0)
    p1 = jnp.max(prob, axis=-1, keepdims=True)
    i1 = jnp.min(jnp.where(prob == p1, lanef, 999.0), axis=-1, keepdims=True)
    prob2 = jnp.where(lanef == i1, -1.0, prob)
    p2 = jnp.max(prob2, axis=-1, keepdims=True)
    i2 = jnp.min(jnp.where(prob2 == p2, lanef, 999.0), axis=-1, keepdims=True)
    tot = p1 + p2
    route = jnp.where(lane == 0, i1 - N_GROUPS,
                      jnp.where(lane == 1, i2 - N_GROUPS,
                                jnp.where(lane == 2, g_w * p1 / tot,
                                          jnp.where(lane == 3, g_w * p2 / tot, 0.0))))
    route_ref[...] = route.T[0:8, :]


def _merge(x, ya_p, ya_s, yb_p, yb_s, zb, w_a, w_b, w_out, ln_g, ln_b, w_rg, w_re, b_rg, b_re, layer):
    m = x.shape[0]
    pt = ya_p.shape[0] // TM
    rowblk = lambda w, j: pl.BlockSpec((TM, w), lambda i: (i, j))
    pblk = pl.BlockSpec((TM, D_MODEL), lambda i: (jnp.minimum(i, pt - 1), 0))
    sblk = pl.BlockSpec((TM, D_MODEL), lambda i: (jnp.maximum(i - pt, 0), 0))
    return pl.pallas_call(
        functools.partial(_merge_kernel, prompt_tiles=pt),
        grid=(m // TM,),
        in_specs=[rowblk(D_MODEL, 0), pblk, sblk, pblk, sblk,
                  rowblk(D_MODEL, ZGA // D_MODEL), rowblk(D_MODEL, ZGB // D_MODEL),
                  _layer_spec((D_MODEL, D_MODEL), layer), _layer_spec((D_MODEL, D_MODEL), layer),
                  _layer_spec((D_MODEL, D_MODEL), layer),
                  _layer_spec((3, D_MODEL), layer), _layer_spec((3, D_MODEL), layer),
                  _layer_spec((D_MODEL, N_GROUPS), layer), _layer_spec((D_MODEL, N_EXPERTS), layer),
                  _layer_spec((1, N_GROUPS), layer), _layer_spec((1, N_EXPERTS), layer)],
        out_specs=[rowblk(D_MODEL, 0), pl.BlockSpec((8, TM), lambda i: (0, i))],
        out_shape=[jax.ShapeDtypeStruct((m, D_MODEL), F32), jax.ShapeDtypeStruct((8, m), F32)],
        scratch_shapes=[pltpu.VMEM((3, D_MODEL, D_MODEL), BF16), pltpu.VMEM((D_MODEL, LANES), BF16),
                        pltpu.VMEM((1, LANES), F32)],
        compiler_params=_cparams(("arbitrary",)),
        name="merge",
    )(x, ya_p, ya_s, yb_p, yb_s, zb, zb, w_a, w_b, w_out, ln_g, ln_b, w_rg, w_re, b_rg, b_re)


SEG_ALIGN = 8
LOC_ROWS = 2 * TD + N_EXPERTS * SEG_ALIGN
LOC_BITS = (LOC_ROWS // SEG_ALIGN).bit_length()


def _route_tables(route, valid, nblk):
    m = route.shape[1]
    nt = m // TD
    per = BLK // SEG_ALIGN
    experts = jnp.arange(N_EXPERTS, dtype=I32)
    eid = jnp.where(valid[None, :], route[0:2].astype(I32), -1).reshape(2, nt, TD)
    e3 = jnp.concatenate([eid[0], eid[1]], axis=1)
    eflat = e3.reshape(2 * m)
    hit = eflat[:, None] == experts
    rank_all = jnp.sum(jnp.where(hit, jnp.cumsum(hit.astype(I32), axis=0), 0), axis=1) - 1
    own = e3[:, :, None] == experts
    raw = jnp.sum(own.astype(I32), axis=1)
    cnt = (raw + SEG_ALIGN - 1) // SEG_ALIGN
    lofs = jnp.cumsum(cnt, axis=1) - cnt
    ltot = jnp.sum(cnt, axis=1)
    before = jnp.cumsum(raw, axis=0) - raw
    pick = lambda tbl: jnp.sum(jnp.where(own, tbl[:, None, :], 0), axis=2)
    lslot = pick(lofs) * SEG_ALIGN + rank_all.reshape(nt, 2 * TD) - pick(before)
    lslot = jnp.where(e3 >= 0, lslot, -1).astype(F32)
    gates = route[2:4].reshape(2, nt, TD)
    table = jnp.stack([lslot[:, :TD], lslot[:, TD:], gates[0], gates[1]] + [jnp.zeros((nt, TD), F32)] * 4, axis=1)
    gcnt = jnp.sum(cnt, axis=0)
    gpad = (gcnt + per - 1) // per * per
    gend = jnp.cumsum(gpad)
    gofs = (gend - gpad)[None, :] + jnp.cumsum(cnt, axis=0) - cnt
    nused = gend[-1] // per
    unused = nused + jnp.arange(N_EXPERTS, dtype=I32)
    zstart = jnp.concatenate([(gend - gpad + gcnt) * SEG_ALIGN, jnp.minimum(unused, nblk - 1) * BLK]).astype(I32)
    zlen = jnp.concatenate([(gpad - gcnt) * SEG_ALIGN, jnp.where(unused < nblk, BLK, 0)]).astype(I32)
    zstart = jnp.concatenate([zstart, zlen])
    bstart = jnp.minimum(jnp.arange(nblk, dtype=I32), nused - 1) * per
    blk_e = jnp.minimum(jnp.sum((gend[None, :] <= bstart[:, None]).astype(I32), axis=1), N_EXPERTS - 1)
    flat = lambda t: t.reshape(-1).astype(I32)
    return (flat(cnt), flat(lofs), flat(gofs), flat(ltot), table, zstart, flat(blk_e),
            nused.astype(I32).reshape(1))


def _segment_copies(i, cnt_ref, lofs_ref, gofs_ref, make_copy):
    for e in range(N_EXPERTS):
        c = cnt_ref[i * N_EXPERTS + e]
        lo = lofs_ref[i * N_EXPERTS + e]
        go = gofs_ref[i * N_EXPERTS + e]

        @pl.when(c > 0)
        def _():
            make_copy(pl.multiple_of(lo * SEG_ALIGN, SEG_ALIGN), pl.multiple_of(go * SEG_ALIGN, SEG_ALIGN),
                      pl.multiple_of(c * SEG_ALIGN, SEG_ALIGN)).start()


def _wait_rows(tot, make_copy):
    for k in range(LOC_BITS):
        @pl.when(((tot >> k) & 1) == 1)
        def _():
            make_copy(0, 0, SEG_ALIGN << k).wait()


def _dispatch_kernel(cnt_ref, lofs_ref, gofs_ref, ltot_ref, zstart_ref, x_ref, ls_ref, xs_hbm, xloc, zbuf, sem):
    i = pl.program_id(0)
    last = pl.num_programs(0) - 1
    slot = i % 2

    def copy_from(s):
        return lambda lo, go, n: pltpu.make_async_copy(xloc.at[s, pl.ds(lo, n)], xs_hbm.at[pl.ds(go, n)], sem.at[s])

    @pl.when(i == 0)
    def _():
        zbuf[...] = jnp.zeros_like(zbuf)
        nz = zstart_ref.shape[0] // 2
        for phase in ("start", "wait"):
            for e in range(nz):
                n = pl.multiple_of(zstart_ref[nz + e], SEG_ALIGN)

                @pl.when(n > 0)
                def _():
                    start = pl.multiple_of(zstart_ref[e], SEG_ALIGN)
                    cp = pltpu.make_async_copy(zbuf.at[pl.ds(0, n)], xs_hbm.at[pl.ds(start, n)], sem.at[2])
                    getattr(cp, phase)()

    @pl.when(i >= 2)
    def _():
        _wait_rows(ltot_ref[jnp.maximum(i - 2, 0)], copy_from(slot))

    ls = ls_ref[0, 0:2, :].astype(I32)
    j = lax.broadcasted_iota(I32, (LOC_ROWS, TD), 0)
    onehot = jnp.where((j == ls[0:1, :]) | (j == ls[1:2, :]), 1.0, 0.0).astype(BF16)
    xloc[slot] = _dot(onehot, x_ref[...].astype(BF16))
    _segment_copies(i, cnt_ref, lofs_ref, gofs_ref, copy_from(slot))

    @pl.when(i == last)
    def _():
        _wait_rows(ltot_ref[i], copy_from(slot))

        @pl.when(i >= 1)
        def _():
            _wait_rows(ltot_ref[jnp.maximum(i - 1, 0)], copy_from(1 - slot))


def _dispatch(x1, cnt, lofs, gofs, ltot, zstart, table, nblk):
    m = x1.shape[0]
    nt = m // TD
    grid_spec = pltpu.PrefetchScalarGridSpec(
        num_scalar_prefetch=5,
        grid=(nt,),
        in_specs=[pl.BlockSpec((TD, D_MODEL), lambda i, *_: (i, 0)),
                  pl.BlockSpec((1, 8, TD), lambda i, *_: (i, 0, 0))],
        out_specs=pl.BlockSpec(memory_space=pl.ANY),
        scratch_shapes=[pltpu.VMEM((2, LOC_ROWS, D_MODEL), F32), pltpu.VMEM((BLK, D_MODEL), F32),
                        pltpu.SemaphoreType.DMA((3,))],
    )
    return pl.pallas_call(
        _dispatch_kernel,
        grid_spec=grid_spec,
        out_shape=jax.ShapeDtypeStruct((nblk * BLK, D_MODEL), F32),
        compiler_params=_cparams(("arbitrary",)),
        name="dispatch",
    )(cnt, lofs, gofs, ltot, zstart, x1, table)


def _experts_kernel(blk_e_ref, nused_ref, xs_ref, wg_ref, wu_ref, wd_ref, ys_ref, wgb, wub, wdb):
    b = pl.program_id(0)
    used = b < nused_ref[0]
    new_expert = (b == 0) | (blk_e_ref[b] != blk_e_ref[jnp.maximum(b - 1, 0)])

    @pl.when(used & new_expert)
    def _():
        _cast_into(wgb, wg_ref)
        _cast_into(wub, wu_ref)
        _cast_into(wdb, wd_ref)

    @pl.when(used)
    def _():
        xb = xs_ref[...].astype(BF16)
        g = _dot(xb, wgb[...])
        u = _dot(xb, wub[...])
        ys_ref[...] = _dot((g * jax.nn.sigmoid(g) * u).astype(BF16), wdb[...])

    @pl.when(jnp.logical_not(used))
    def _():
        ys_ref[...] = jnp.zeros_like(ys_ref)


def _experts(xs, blk_e, nused, w_eg, w_eu, w_ed, layer):
    nb = xs.shape[0] // BLK
    wspec = lambda s: pl.BlockSpec((None, None) + s, lambda b, be, nu: (layer, be[b], 0, 0))
    grid_spec = pltpu.PrefetchScalarGridSpec(
        num_scalar_prefetch=2,
        grid=(nb,),
        in_specs=[pl.BlockSpec((BLK, D_MODEL), lambda b, be, nu: (jnp.minimum(b, nu[0] - 1), 0)),
                  wspec((D_MODEL, D_EXPERT)), wspec((D_MODEL, D_EXPERT)), wspec((D_EXPERT, D_MODEL))],
        out_specs=pl.BlockSpec((BLK, D_MODEL), lambda b, be, nu: (b, 0)),
        scratch_shapes=[pltpu.VMEM((D_MODEL, D_EXPERT), BF16), pltpu.VMEM((D_MODEL, D_EXPERT), BF16),
                        pltpu.VMEM((D_EXPERT, D_MODEL), BF16)],
    )
    return pl.pallas_call(
        _experts_kernel,
        grid_spec=grid_spec,
        out_shape=jax.ShapeDtypeStruct(xs.shape, F32),
        compiler_params=_cparams(("arbitrary",)),
        name="experts",
    )(blk_e, nused, xs, w_eg, w_eu, w_ed)


def _post_kernel(cnt_ref, lofs_ref, gofs_ref, ltot_ref, x1_ref, tab_ref, pp_ref, ps_ref, wpg_ref, wpp_ref,
                 lng_ref, lnb_ref, ys_hbm, o_ref, yloc, wpg_b, wpp_b, sem, *, prompt_tiles):
    i = pl.program_id(0)
    last = pl.num_programs(0) - 1
    slot = i % 2

    def copy_to(s):
        return lambda lo, go, n: pltpu.make_async_copy(ys_hbm.at[pl.ds(go, n)], yloc.at[s, pl.ds(lo, n)], sem.at[s])

    @pl.when(i == 0)
    def _():
        _cast_into(wpg_b, wpg_ref)
        _cast_into(wpp_b, wpp_ref)
        yloc[...] = jnp.zeros_like(yloc)
        _segment_copies(0, cnt_ref, lofs_ref, gofs_ref, copy_to(0))

    _wait_rows(ltot_ref[i], copy_to(slot))

    @pl.when(i < last)
    def _():
        _segment_copies(i + 1, cnt_ref, lofs_ref, gofs_ref, copy_to(1 - slot))

    eye = (lax.broadcasted_iota(I32, (8, 8), 0) == lax.broadcasted_iota(I32, (8, 8), 1)).astype(F32)
    tcol = lax.dot_general(tab_ref[0], eye, (((0,), (0,)), ((), ())), precision=HIGHEST,
                           preferred_element_type=F32)
    j = lax.broadcasted_iota(I32, (tcol.shape[0], LOC_ROWS), 1).astype(F32)
    w = jnp.where(j == tcol[:, 0:1], tcol[:, 2:3], 0.0) + jnp.where(j == tcol[:, 1:2], tcol[:, 3:4], 0.0)
    moe = _dot(w.astype(BF16), yloc[slot].astype(BF16))
    x2 = _layer_norm(DN_ALPHA * x1_ref[...] + moe, lng_ref[1:2, :], lnb_ref[1:2, :])
    p = jnp.where(i < prompt_tiles, pp_ref[...], ps_ref[...])
    ple = jax.nn.sigmoid(_dot(x2.astype(BF16), wpg_b[...])) * _dot(p.astype(BF16), wpp_b[...])
    o_ref[...] = _layer_norm(DN_ALPHA * x2 + ple, lng_ref[2:3, :], lnb_ref[2:3, :])


def _post(x1, ys, cnt, lofs, gofs, ltot, table, p_prompt, p_sample, w_pg, w_pp, ln_g, ln_b, layer):
    m = x1.shape[0]
    pt = p_prompt.shape[1] // TD
    rowblk = lambda w: pl.BlockSpec((TD, w), lambda i, *_: (i, 0))
    lspec = lambda s: pl.BlockSpec((None,) + s, lambda i, *_: (layer,) + (0,) * len(s), pipeline_mode=pl.Buffered(1))
    grid_spec = pltpu.PrefetchScalarGridSpec(
        num_scalar_prefetch=4,
        grid=(m // TD,),
        in_specs=[rowblk(D_MODEL), pl.BlockSpec((1, 8, TD), lambda i, *_: (i, 0, 0)),
                  pl.BlockSpec((None, TD, D_PLE), lambda i, *_: (layer, jnp.minimum(i, pt - 1), 0)),
                  pl.BlockSpec((None, TD, D_PLE), lambda i, *_: (layer, jnp.maximum(i - pt, 0), 0)),
                  lspec((D_MODEL, D_MODEL)), lspec((D_PLE, D_MODEL)), lspec((3, D_MODEL)), lspec((3, D_MODEL)),
                  pl.BlockSpec(memory_space=pl.ANY)],
        out_specs=rowblk(D_MODEL),
        scratch_shapes=[pltpu.VMEM((2, LOC_ROWS, D_MODEL), F32), pltpu.VMEM((D_MODEL, D_MODEL), BF16),
                        pltpu.VMEM((D_PLE, D_MODEL), BF16), pltpu.SemaphoreType.DMA((2,))],
    )
    return pl.pallas_call(
        functools.partial(_post_kernel, prompt_tiles=pt),
        grid_spec=grid_spec,
        out_shape=jax.ShapeDtypeStruct((m, D_MODEL), F32),
        compiler_params=_cparams(("arbitrary",)),
        name="post",
    )(cnt, lofs, gofs, ltot, x1, table, p_prompt, p_sample, w_pg, w_pp, ln_g, ln_b, ys)


def kernel(x_prompt, x_sample, state_mlstm_C, state_mlstm_n, state_mlstm_m, state_swa_k, state_swa_v, p_prompt, p_sample, w_in, b_in, mh_gain, w_a, w_b, w_out, rel_table, w_sink, ln_g, ln_b, w_rg, b_rg, w_re, b_re, w_eg, w_eu, w_ed, w_pg, w_pp):
    bp_, seq = x_prompt.shape[:2]
    bs_, tdec = x_sample.shape[:2]
    mp = bp_ * seq
    msp = bs_ * T_PAD
    mall = mp + msp
    nblk = -(-(2 * (mp + bs_ * tdec) + (mall // TD) * N_EXPERTS * (SEG_ALIGN - 1)) // BLK) + N_EXPERTS

    pad_t = lambda t: jnp.pad(t, [(0, 0)] * (t.ndim - 2) + [(0, T_PAD - tdec), (0, 0)])
    x = lax.dynamic_update_slice(jnp.pad(x_prompt.reshape(mp, D_MODEL), ((0, msp), (0, 0))),
                                 pad_t(x_sample).reshape(msp, D_MODEL), (mp, 0))
    valid = jnp.concatenate([jnp.ones((mp,), bool), jnp.tile(jnp.arange(T_PAD) < tdec, bs_)])
    p_prompt = p_prompt.reshape(DEPTH, mp, D_PLE)
    p_sample = pad_t(p_sample).reshape(DEPTH, msp, D_PLE)

    b_in3 = b_in.reshape(DEPTH, 1, -1)
    b_b = jnp.concatenate([b_in3[:, :, IN_GATES[0]:IN_GATES[1]], b_in3[:, :, IN_QKV2[0]:IN_QKV2[1]]], axis=-1)
    b_rg, b_re = b_rg.reshape(DEPTH, 1, N_GROUPS), b_re.reshape(DEPTH, 1, N_EXPERTS)
    gain = mh_gain.reshape(DEPTH, 1, D_MODEL)

    bias_p = _swa_bias(rel_table, WINDOW)
    bias_s = _swa_bias(rel_table, T_PAD)
    sink_p = sink_s = _sink_lanes(w_sink)
    buf_k = state_swa_k.reshape(DEPTH, bs_, WINDOW, D_KV)
    buf_v = state_swa_v.reshape(DEPTH, bs_, WINDOW, D_KV)
    m0_s = jnp.broadcast_to(state_mlstm_m[..., None], (DEPTH, bs_, M_HEADS, LANES))
    zc = jnp.zeros((1, bp_, M_HEADS, M_DQK, M_DV), F32)
    zn = jnp.zeros((1, bp_, M_HEADS, LANES), F32)
    st_s = (jnp.zeros_like(state_mlstm_C), jnp.zeros_like(state_mlstm_n), jnp.zeros_like(m0_s))
    kv_s = (jnp.zeros_like(buf_k), jnp.zeros_like(buf_v))

    st_p = []
    for i in range(DEPTH):
        za, zg = _inproj(x, [(w_in, IN_A)], [(b_in3, IN_A)], [(ZG, BF16), (LANES, F32)], i, "inproj_a")
        zb = _inproj_b(x, w_in, b_b, i)

        ya_p, c_p, n_p, m_p = _mlstm(za, zg, gain, i, zc, zn, zn, 0, None, 1, 0, row0=0, nseq=bp_,
                                     steps=seq // LANES, seg=LANES, t_valid=LANES)
        ya_s, *st_s = _mlstm(za, zg, gain, i, state_mlstm_C, state_mlstm_n, m0_s, i, st_s, DEPTH, i, row0=mp,
                             nseq=bs_ // SEQ_BLK, steps=1, seg=T_PAD, t_valid=tdec)
        yb_p = _swa_prompt(zb, bias_p[0], bias_p[1], sink_p, i, batch=bp_, seq=seq)
        yb_s, *kv_s = _swa_sample(zb, buf_k, buf_v, bias_s[0], bias_s[1], sink_s, i, kv_s, row0=mp, batch=bs_,
                                  t_new=tdec)

        x1, route = _merge(x, ya_p, ya_s, yb_p, yb_s, zb, w_a, w_b, w_out, ln_g, ln_b, w_rg, w_re, b_rg, b_re, i)
        cnt, lofs, gofs, ltot, table, zstart, blk_e, nused = _route_tables(route, valid, nblk)
        xs = _dispatch(x1, cnt, lofs, gofs, ltot, zstart, table, nblk)
        ys = _experts(xs, blk_e, nused, w_eg, w_eu, w_ed, i)
        x = _post(x1, ys, cnt, lofs, gofs, ltot, table, p_prompt, p_sample, w_pg, w_pp, ln_g, ln_b, i)

        kv_p = jnp.stack([lax.slice(zb, ((b + 1) * seq - WINDOW, ZKS), ((b + 1) * seq, IN_B)) for b in range(bp_)])
        kv_p = kv_p.astype(F32)
        heads = lambda t: t.reshape(t.shape[:-1] + (S_KV_HEADS, S_HEAD_DIM))
        st_p.append((c_p[0], n_p[0], m_p[0, :, :, 0], heads(kv_p[..., :D_KV]), heads(kv_p[..., D_KV:])))

    stk = lambda j: jnp.stack([s[j] for s in st_p], axis=0)
    y_p = x[:mp].reshape(bp_, seq, D_MODEL)
    y_s = x[mp:].reshape(bs_, T_PAD, D_MODEL)[:, :tdec]
    return (y_p, y_s, stk(0), stk(1), stk(2), stk(3), stk(4),
            st_s[0], st_s[1], st_s[2][..., 0], heads(kv_s[0]), heads(kv_s[1]))
```

```python
import functools

import numpy as np
import jax
import jax.numpy as jnp
from jax import lax
from jax.experimental import pallas as pl
from jax.experimental.pallas import tpu as pltpu

F32 = jnp.float32
BF16 = jnp.bfloat16
I32 = jnp.int32
HIGHEST = lax.Precision.HIGHEST

D_MODEL = 1024
DEPTH = 4
D_PLE = 256
M_HEADS = 4
M_DQK = 128
M_DV = 256
S_HEADS = 16
S_KV_HEADS = 4
S_HEAD_DIM = 64
S_GROUP = 4
D_KV = S_KV_HEADS * S_HEAD_DIM
WINDOW = 128
REL_BUCKETS = 32
N_GROUPS = 4
EXPERTS_PER_GROUP = 8
N_EXPERTS = 32
D_EXPERT = 512
DN_ALPHA = (2 * DEPTH) ** 0.25
LN_EPS = 1e-5
NEG = -1e30

IN_A = 3200
IN_QKV2 = (3080, 4616)
IN_GATES = (4616, 6664)
ZQ, ZK, ZV, ZOG, ZG = 0, 512, 1024, 2048, 3072
IN_B = 3584
ZGA, ZGB, ZQS, ZKS, ZVS = 0, 1024, 2048, 3072, 3328

LANES = 128
T_PAD = 8
SEQ_BLK = LANES // T_PAD
TM = 512
TD = 256
BLK = 512
VMEM_LIMIT = 56 * 1024 * 1024


def _cparams(sem):
    return pltpu.CompilerParams(dimension_semantics=sem, vmem_limit_bytes=VMEM_LIMIT)


def _layer_spec(shape, layer):
    nd = len(shape)
    return pl.BlockSpec((None,) + shape, lambda *_: (layer,) + (0,) * nd, pipeline_mode=pl.Buffered(1))


def _layer_norm(y, g, b):
    mu = jnp.mean(y, axis=-1, keepdims=True)
    d = y - mu
    var = jnp.mean(d * d, axis=-1, keepdims=True)
    return d * lax.rsqrt(var + LN_EPS) * g + b


def _dot(a, b):
    return jnp.dot(a, b, preferred_element_type=F32)


def _dot_nt(a, b):
    return lax.dot_general(a, b, (((1,), (1,)), ((), ())), preferred_element_type=F32)


def _dot_tn(a, b):
    return lax.dot_general(a, b, (((0,), (0,)), ((), ())), preferred_element_type=F32)


def _dot_exact(a, b):
    return jnp.dot(a, b, precision=HIGHEST, preferred_element_type=F32)


def _cast_into(dst_ref, src_ref, col0=0, step=512):
    n = src_ref.shape[-1]
    for j in range(0, n, step):
        w = min(step, n - j)
        dst_ref[:, col0 + j:col0 + j + w] = src_ref[:, j:j + w].astype(BF16)


def _inproj_kernel(*refs, widths, out_widths):
    ng, no = len(widths), len(out_widths)
    x_ref, w_refs, b_refs = refs[0], refs[1:1 + ng], refs[1 + ng:1 + 2 * ng]
    o_refs, wbf = refs[1 + 2 * ng:1 + 2 * ng + no], refs[1 + 2 * ng + no]

    @pl.when(pl.program_id(0) == 0)
    def _():
        c = 0
        for w_ref, n in zip(w_refs, widths):
            _cast_into(wbf, w_ref, c)
            c += n

    out_start = np.cumsum((0,) + tuple(out_widths))

    def store(col, val):
        k = int(np.searchsorted(out_start, col, side='right')) - 1
        lo = col - int(out_start[k])
        o_refs[k][:, lo:lo + val.shape[1]] = val.astype(o_refs[k].dtype)

    xb = x_ref[...].astype(BF16)
    c = 0
    for b_ref, n in zip(b_refs, widths):
        for j in range(0, n, 512):
            w = min(512, n - j)
            store(c + j, _dot(xb, wbf[:, c + j:c + j + w]) + b_ref[:, j:j + w])
        c += n


def _inproj(x, ws, bs, outs, layer, name):
    m = x.shape[0]
    widths = tuple(n for _, n in ws)
    out_widths = tuple(n for n, _ in outs)
    assert sum(widths) == sum(out_widths)
    return pl.pallas_call(
        functools.partial(_inproj_kernel, widths=widths, out_widths=out_widths),
        grid=(m // TM,),
        in_specs=([pl.BlockSpec((TM, D_MODEL), lambda i: (i, 0))]
                  + [_layer_spec((D_MODEL, n), layer) for _, n in ws]
                  + [_layer_spec((1, n), layer) for _, n in bs]),
        out_specs=[pl.BlockSpec((TM, n), lambda i: (i, 0)) for n, _ in outs],
        out_shape=[jax.ShapeDtypeStruct((m, n), dt) for n, dt in outs],
        scratch_shapes=[pltpu.VMEM((D_MODEL, sum(widths)), BF16)],
        compiler_params=_cparams(("arbitrary",)),
        name=name,
    )(x, *[w for w, _ in ws], *[b for b, _ in bs])


def _inproj_b_kernel(x_ref, b_ref, w_hbm, z_ref, wwin, wbf, sem, *, layer):
    lead = IN_QKV2[0] % LANES
    n_qkv = IN_QKV2[1] - IN_QKV2[0]

    @pl.when(pl.program_id(0) == 0)
    def _():
        cp = pltpu.make_async_copy(w_hbm.at[layer, :, pl.ds(IN_QKV2[0] - lead, wwin.shape[1])], wwin, sem.at[0])
        cp.start()
        cp.wait()
        for dst, src, n in ((0, n_qkv, IN_B - n_qkv), (IN_B - n_qkv, 0, n_qkv)):
            for j in range(0, n, 512):
                w = min(512, n - j)
                wbf[:, dst + j:dst + j + w] = wwin[:, lead + src + j:lead + src + j + w].astype(BF16)

    xb = x_ref[...].astype(BF16)
    for j in range(0, IN_B, 512):
        w = min(512, IN_B - j)
        z_ref[:, j:j + w] = (_dot(xb, wbf[:, j:j + w]) + b_ref[:, j:j + w]).astype(z_ref.dtype)


def _inproj_b(x, w_in, b_b, layer):
    m = x.shape[0]
    win = w_in.shape[2] - (IN_QKV2[0] - IN_QKV2[0] % LANES)
    return pl.pallas_call(
        functools.partial(_inproj_b_kernel, layer=layer),
        grid=(m // TM,),
        in_specs=[pl.BlockSpec((TM, D_MODEL), lambda i: (i, 0)), _layer_spec((1, IN_B), layer),
                  pl.BlockSpec(memory_space=pl.ANY)],
        out_specs=pl.BlockSpec((TM, IN_B), lambda i: (i, 0)),
        out_shape=jax.ShapeDtypeStruct((m, IN_B), BF16),
        scratch_shapes=[pltpu.VMEM((D_MODEL, win), F32), pltpu.VMEM((D_MODEL, IN_B), BF16),
                        pltpu.SemaphoreType.DMA((1,))],
        compiler_params=_cparams(("arbitrary",)),
        name="inproj_b",
    )(x, b_b, w_in)


def _log_sigmoid(x):
    return jnp.minimum(x, 0.0) - jnp.log(1.0 + jnp.exp(-jnp.abs(x)))


def _mlstm_kernel(zq, zk, zv, zog, zg, gain, c0, n0, m0, *rest, seg, t_valid, aliased):
    ya, co, no, mo, qc_s = rest[aliased:]

    @pl.when(pl.program_id(1) == 0)
    def _():
        co[...] = c0[...]
        no[...] = n0[...]
        mo[...] = m0[...]

    for r0 in range(0, zq.shape[0], LANES):
        rows = pl.ds(r0, LANES)
        _mlstm_block(zq.at[rows], zk.at[rows], zv.at[rows], zog.at[rows], zg.at[rows], gain, ya.at[rows],
                     co, no, mo, qc_s, seg=seg, t_valid=t_valid)


def _mlstm_block(zq, zk, zv, zog, zg, gain, ya, co, no, mo, qc_s, *, seg, t_valid):
    L = zq.shape[0]
    nseg = L // seg
    g_c = zg[...]
    g_r = g_c.T
    ig_c, lf_c = g_c, _log_sigmoid(g_c)
    ig_r, lf_r = g_r, _log_sigmoid(g_r)
    ri = lax.broadcasted_iota(I32, (L, L), 0)
    ci = lax.broadcasted_iota(I32, (L, L), 1)
    shift = seg.bit_length() - 1
    if t_valid < seg:
        rv = (lax.broadcasted_iota(I32, (L, 1), 0) & (seg - 1)) < t_valid
        cv = (lax.broadcasted_iota(I32, (1, L), 1) & (seg - 1)) < t_valid
        ig_c, lf_c = jnp.where(rv, ig_c, NEG), jnp.where(rv, lf_c, 0.0)
        ig_r, lf_r = jnp.where(cv, ig_r, NEG), jnp.where(cv, lf_r, 0.0)
    if nseg > 1:
        same = (ri >> shift) == (ci >> shift)
        causal, causal_t = same & (ri >= ci), same & (ri <= ci)
        same_f = same.astype(F32)
        be_c = _dot_exact(same_f, lf_c)
        be_r = _dot_exact(lf_r, same_f)
        expand = ((lax.broadcasted_iota(I32, (L, nseg), 0) >> shift)
                  == lax.broadcasted_iota(I32, (L, nseg), 1)).astype(F32)
    else:
        causal, causal_t = ri >= ci, ri <= ci
    b_c = _dot_exact(causal.astype(F32), lf_c)
    b_r = _dot_exact(lf_r, causal_t.astype(F32))

    for h in range(M_HEADS):
        icol, irow = ig_c[:, h:h + 1], ig_r[h:h + 1, :]
        bcol, brow = b_c[:, 4 + h:5 + h], b_r[4 + h:5 + h, :]
        q = zq[:, h * M_DQK:(h + 1) * M_DQK].astype(F32) * (M_DQK ** -0.5)
        k = zk[:, h * M_DQK:(h + 1) * M_DQK].astype(F32)
        qb, kb = q.astype(BF16), zk[:, h * M_DQK:(h + 1) * M_DQK]
        vb = zv[:, h * M_DV:(h + 1) * M_DV]
        if nseg > 1:
            mcol = _dot_exact(expand, mo[:, h, :])[:, 0:1]
            nrow = _dot_exact(expand, no[:, h, :])
            becol, berow = be_c[:, 4 + h:5 + h], be_r[4 + h:5 + h, :]
            for j in range(nseg):
                rows = slice(j * seg, (j + 1) * seg)
                qc_s[rows, :] = _dot(q[rows].astype(BF16), co[j, h].astype(BF16))
            q_c = qc_s[...]
        else:
            mcol, nrow = mo[0, h:h + 1, 0:1], no[0, h:h + 1, :]
            becol = bcol[L - 1:L, :]
            q_c = _dot(qb, co[0, h].astype(BF16))

        dmat = jnp.where(causal, bcol - brow + irow, NEG)
        inter = bcol + mcol
        mhat = jnp.maximum(inter, jnp.max(dmat, axis=-1, keepdims=True))
        w_intra = jnp.exp(dmat - mhat)
        w_inter = jnp.exp(inter - mhat)
        s = _dot_nt(qb, kb) * w_intra
        num = _dot(s.astype(BF16), vb) + w_inter * q_c
        den = jnp.sum(s, axis=-1, keepdims=True) + w_inter * jnp.sum(q * nrow, axis=-1, keepdims=True)
        hh = num / jnp.maximum(jnp.abs(den), jnp.exp(-mhat))

        gcol = icol + becol - bcol
        if nseg > 1:
            gmax = jnp.max(jnp.where(same, irow + berow - brow, NEG), axis=-1, keepdims=True)
        else:
            gmax = jnp.max(gcol, axis=0, keepdims=True)
        m_new = jnp.maximum(becol + mcol, gmax)
        decay = jnp.exp(becol + mcol - m_new)
        kw = k * jnp.exp(gcol - m_new)
        vf = vb.astype(F32) if nseg > 1 else vb
        for j in range(nseg):
            rows = slice(j * seg, (j + 1) * seg)
            dj = decay[j * seg:j * seg + 1, :] if nseg > 1 else decay
            mj = m_new[j * seg:j * seg + 1, :] if nseg > 1 else m_new
            co[j, h] = dj * co[j, h] + _dot_tn(kw[rows].astype(BF16), vf[rows].astype(BF16))
            no[j, h:h + 1, :] = dj * no[j, h:h + 1, :] + jnp.sum(kw[rows], axis=0, keepdims=True)
            mo[j, h:h + 1, :] = jnp.broadcast_to(mj, (1, LANES))

        mu = jnp.mean(hh, axis=-1, keepdims=True)
        d = hh - mu
        var = jnp.mean(d * d, axis=-1, keepdims=True)
        hn = d * lax.rsqrt(var + LN_EPS) * gain[:, h * M_DV:(h + 1) * M_DV]
        og = zog[:, h * M_DV:(h + 1) * M_DV].astype(F32)
        ya[:, h * M_DV:(h + 1) * M_DV] = (hn * jax.nn.sigmoid(og)).astype(ya.dtype)


def _mlstm(za, zg, gain, layer, c0, n0, m0, st_layer, outs_prev, out_layers, out_layer, *, row0, nseq, steps,
           seg, t_valid, blocks_per_step=1):
    nseg = LANES // seg
    rows = LANES * blocks_per_step
    r0 = row0 // rows
    row = lambda b, c: r0 + b * steps + c
    nstate = nseq * nseg
    aliased = 0 if outs_prev is None else 3
    kern = functools.partial(_mlstm_kernel, seg=seg, t_valid=min(t_valid, seg), aliased=aliased)
    in4 = pl.BlockSpec((None, nseg, M_HEADS, M_DQK, M_DV), lambda b, c: (st_layer, b, 0, 0, 0))
    in3 = pl.BlockSpec((None, nseg, M_HEADS, LANES), lambda b, c: (st_layer, b, 0, 0))
    out4 = pl.BlockSpec((None, nseg, M_HEADS, M_DQK, M_DV), lambda b, c: (out_layer, b, 0, 0, 0))
    out3 = pl.BlockSpec((None, nseg, M_HEADS, LANES), lambda b, c: (out_layer, b, 0, 0))
    anyspec = pl.BlockSpec(memory_space=pl.ANY)
    n_in = 9
    return pl.pallas_call(
        kern,
        grid=(nseq, steps),
        in_specs=[pl.BlockSpec((rows, 512), lambda b, c: (row(b, c), ZQ // 512)),
                  pl.BlockSpec((rows, 512), lambda b, c: (row(b, c), ZK // 512)),
                  pl.BlockSpec((rows, 1024), lambda b, c: (row(b, c), ZV // 1024)),
                  pl.BlockSpec((rows, 1024), lambda b, c: (row(b, c), ZOG // 1024)),
                  pl.BlockSpec((rows, LANES), lambda b, c: (row(b, c), 0)),
                  pl.BlockSpec((None, 1, D_MODEL), lambda b, c: (layer, 0, 0)),
                  in4, in3, in3] + [anyspec] * aliased,
        out_specs=[pl.BlockSpec((rows, D_MODEL), lambda b, c: (b * steps + c, 0)), out4, out3, out3],
        out_shape=[jax.ShapeDtypeStruct((nseq * steps * rows, D_MODEL), BF16),
                   jax.ShapeDtypeStruct((out_layers, nstate, M_HEADS, M_DQK, M_DV), F32),
                   jax.ShapeDtypeStruct((out_layers, nstate, M_HEADS, LANES), F32),
                   jax.ShapeDtypeStruct((out_layers, nstate, M_HEADS, LANES), F32)],
        scratch_shapes=[pltpu.VMEM((LANES, M_DV), F32)],
        input_output_aliases={n_in + j: 1 + j for j in range(aliased)},
        compiler_params=_cparams(("parallel", "arbitrary")),
        name="mlstm",
    )(za, za, za, za, zg, gain, c0, n0, m0, *(outs_prev or ()))


def _swa_kernel(q_ref, kc_ref, vc_ref, kp_ref, vp_ref, bp_ref, bc_ref, sink_ref, *rest, nseq, t_new,
                first_block_has_no_past, aliased):
    o_ref = rest[aliased]
    rows_n = q_ref.shape[0]
    tq = rows_n // nseq
    have_past = pl.program_id(1) > 0
    lane = lax.broadcasted_iota(I32, (1, LANES), 1)
    in_lo = lane < S_HEAD_DIM
    if nseq > 1:
        kp_all = kp_ref[...].reshape(nseq * WINDOW, D_KV).astype(BF16)
        vp_all = vp_ref[...].reshape(nseq * WINDOW, D_KV).astype(BF16)
        row_seq = lax.broadcasted_iota(I32, (rows_n, LANES), 0) >> (tq.bit_length() - 1)
    else:
        kp_all, vp_all = kp_ref[...], vp_ref[...]
    kc_all, vc_all = kc_ref[...], vc_ref[...]

    def placed(x, src_half, ones):
        swapped = jnp.concatenate([x[:, S_HEAD_DIM:], x[:, :S_HEAD_DIM]], axis=1)
        zero = jnp.zeros_like(x)
        lo = jnp.where(in_lo, x if src_half == 0 else swapped, zero)
        hi = jnp.where(in_lo, zero, x if src_half == 1 else swapped)
        if ones:
            lo = jnp.where(lane == S_HEAD_DIM, jnp.ones_like(x), lo)
            hi = jnp.where(lane == 0, jnp.ones_like(x), hi)
        return lo, hi

    kv_per_pass = S_KV_HEADS if nseq == 1 else 1
    for kv0 in range(0, S_KV_HEADS, kv_per_pass):
        kp_v, kc_v, vp_v, vc_v = {}, {}, {}, {}
        for kvh in range(kv0, kv0 + kv_per_pass):
            tile = slice((kvh // 2) * LANES, (kvh // 2 + 1) * LANES)
            kp_v[kvh] = placed(kp_all[:, tile], kvh % 2, False)
            kc_v[kvh] = placed(kc_all[:, tile], kvh % 2, False)
            vp_v[kvh] = placed(vp_all[:, tile], kvh % 2, True)
            vc_v[kvh] = placed(vc_all[:, tile], kvh % 2, True)
        heads = list(range(kv0 * S_GROUP, (kv0 + kv_per_pass) * S_GROUP))
        scores = []
        for h in heads:
            q2 = q_ref[:, (h // 2) * LANES:(h // 2 + 1) * LANES]
            scores.append((_dot_nt(q2, kp_v[h // S_GROUP][h % 2]), _dot_nt(q2, kc_v[h // S_GROUP][h % 2])))
        probs = []
        for h, (sp, sc) in zip(heads, scores):
            if nseq > 1:
                sp = jnp.concatenate([sp[b * tq:(b + 1) * tq, b * WINDOW:(b + 1) * WINDOW] for b in range(nseq)],
                                     axis=0)
            sp = sp * (S_HEAD_DIM ** -0.5) + bp_ref[h]
            sc = sc * (S_HEAD_DIM ** -0.5) + bc_ref[h]
            if first_block_has_no_past:
                sp = jnp.where(have_past, sp, NEG)
            sk = sink_ref[h][:, 0:1]
            mx = jnp.maximum(jnp.max(jnp.maximum(sp, sc), axis=-1, keepdims=True), sk)
            pp, pc = jnp.exp(sp - mx), jnp.exp(sc - mx)
            if nseq > 1:
                pp = jnp.concatenate([jnp.where(row_seq == b, pp, 0.0) for b in range(nseq)], axis=1)
            probs.append((pp.astype(BF16), pc.astype(BF16), jnp.exp(sk - mx)))
        outs = [_dot(pp, vp_v[h // S_GROUP][h % 2]) + _dot(pc, vc_v[h // S_GROUP][h % 2])
                for h, (pp, pc, _) in zip(heads, probs)]
        for pair in range(len(heads) // 2):
            halves = []
            for half in range(2):
                o, sink_term = outs[2 * pair + half], probs[2 * pair + half][2]
                ones_lane = S_HEAD_DIM if half == 0 else 0
                halves.append(o / (o[:, ones_lane:ones_lane + 1] + sink_term))
            qt = heads[2 * pair] // 2
            o_ref[:, qt * LANES:(qt + 1) * LANES] = jnp.where(in_lo, halves[0], halves[1]).astype(o_ref.dtype)

    if t_new:
        kc_f, vc_f = kc_all.astype(F32), vc_all.astype(F32)
        for b in range(nseq):
            for new, prev, out in ((kc_f, kp_ref, rest[aliased + 1]), (vc_f, vp_ref, rest[aliased + 2])):
                out[b, 0:WINDOW - t_new, :] = prev[b, t_new:, :]
                out[b, WINDOW - t_new:WINDOW, :] = new[b * tq:b * tq + t_new, :]


def _rel_bucket(dist):
    nn = np.maximum(dist, 0)
    max_exact = REL_BUCKETS // 2
    large = max_exact + (np.log(np.maximum(nn, 1) / max_exact) / np.log(WINDOW / max_exact)
                         * (REL_BUCKETS - max_exact)).astype(np.int32)
    large = np.minimum(large, REL_BUCKETS - 1)
    return np.where(nn < max_exact, nn, large).astype(np.int32)


def _swa_bias(rel_table, tq):
    nseq = LANES // tq
    t = np.tile(np.arange(tq), nseq)[:, None]
    seq = np.repeat(np.arange(nseq), tq)
    d_prev = t + WINDOW - np.arange(WINDOW)[None, :]
    d_cur = t - t.T
    own = seq[:, None] == seq[None, :]

    def build(dist, ok):
        onehot = np.eye(REL_BUCKETS, dtype=np.float32)[_rel_bucket(dist).reshape(-1)]
        bias = jnp.dot(onehot, rel_table.astype(F32), precision=HIGHEST).reshape(dist.shape + (S_HEADS,))
        return jnp.where(ok[None], jnp.transpose(bias, (2, 0, 1)), NEG)

    return build(d_prev, d_prev <= WINDOW), build(d_cur, own & (d_cur >= 0))


def _sink_lanes(w_sink):
    return jnp.broadcast_to(w_sink.astype(F32)[:, :, None, None], (DEPTH, S_HEADS, 1, LANES))


def _swa_bias_specs(layer):
    return [pl.BlockSpec((S_HEADS, LANES, WINDOW), lambda b, n: (0, 0, 0)),
            pl.BlockSpec((S_HEADS, LANES, LANES), lambda b, n: (0, 0, 0)),
            pl.BlockSpec((None, S_HEADS, 1, LANES), lambda b, n: (layer, 0, 0, 0))]


def _swa_prompt(zb, bp, bc, sk, layer, *, batch, seq):
    nb = seq // WINDOW
    row = lambda b, n: b * nb + n
    prev = lambda b, n: jnp.maximum(b * nb + n - 1, 0)
    in_specs = [pl.BlockSpec((WINDOW, 1024), lambda b, n: (row(b, n), ZQS // 1024)),
                pl.BlockSpec((WINDOW, D_KV), lambda b, n: (row(b, n), ZKS // D_KV)),
                pl.BlockSpec((WINDOW, D_KV), lambda b, n: (row(b, n), ZVS // D_KV)),
                pl.BlockSpec((WINDOW, D_KV), lambda b, n: (prev(b, n), ZKS // D_KV)),
                pl.BlockSpec((WINDOW, D_KV), lambda b, n: (prev(b, n), ZVS // D_KV))]
    kern = functools.partial(_swa_kernel, nseq=1, t_new=0, first_block_has_no_past=True, aliased=0)
    return pl.pallas_call(
        kern,
        grid=(batch, nb),
        in_specs=in_specs + _swa_bias_specs(layer),
        out_specs=pl.BlockSpec((WINDOW, D_MODEL), lambda b, n: (row(b, n), 0)),
        out_shape=jax.ShapeDtypeStruct((batch * seq, D_MODEL), BF16),
        compiler_params=_cparams(("parallel", "arbitrary")),
        name="swa_prompt",
    )(zb, zb, zb, zb, zb, bp, bc, sk)


def _swa_sample(zb, buf_k, buf_v, bp, bc, sk, layer, kv_prev, *, row0, batch, t_new):
    r0 = row0 // LANES
    steps = batch // SEQ_BLK
    state = pl.BlockSpec((None, SEQ_BLK, WINDOW, D_KV), lambda b, n: (layer, b, 0, 0))
    anyspec = pl.BlockSpec(memory_space=pl.ANY)
    in_specs = [pl.BlockSpec((LANES, 1024), lambda b, n: (r0 + b, ZQS // 1024)),
                pl.BlockSpec((LANES, D_KV), lambda b, n: (r0 + b, ZKS // D_KV)),
                pl.BlockSpec((LANES, D_KV), lambda b, n: (r0 + b, ZVS // D_KV)),
                state, state]
    aliased = 0 if kv_prev is None else 2
    kern = functools.partial(_swa_kernel, nseq=SEQ_BLK, t_new=t_new, first_block_has_no_past=False, aliased=aliased)
    return pl.pallas_call(
        kern,
        grid=(steps, 1),
        in_specs=in_specs + _swa_bias_specs(layer) + [anyspec] * aliased,
        out_specs=[pl.BlockSpec((LANES, D_MODEL), lambda b, n: (b, 0)), state, state],
        out_shape=[jax.ShapeDtypeStruct((batch * T_PAD, D_MODEL), BF16),
                   jax.ShapeDtypeStruct(buf_k.shape, F32), jax.ShapeDtypeStruct(buf_v.shape, F32)],
        input_output_aliases={8 + j: 1 + j for j in range(aliased)},
        compiler_params=_cparams(("parallel", "arbitrary")),
        name="swa_sample",
    )(zb, zb, zb, buf_k, buf_v, bp, bc, sk, *(kv_prev or ()))


def _merge_kernel(x_ref, yap_ref, yas_ref, ybp_ref, ybs_ref, ga_ref, gb_ref, wa_ref, wb_ref, wo_ref, lng_ref, lnb_ref,
                  wrg_ref, wre_ref, brg_ref, bre_ref, x1_ref, route_ref, wbf, wr_b, br_s, *, prompt_tiles):
    i = pl.program_id(0)

    @pl.when(i == 0)
    def _():
        for j, w_ref in enumerate((wa_ref, wb_ref, wo_ref)):
            _cast_into(wbf.at[j], w_ref)
        wr_b[...] = jnp.zeros_like(wr_b)
        wr_b[:, 0:N_GROUPS] = wrg_ref[...].astype(BF16)
        wr_b[:, N_GROUPS:N_GROUPS + N_EXPERTS] = wre_ref[...].astype(BF16)
        br_s[...] = jnp.zeros_like(br_s)
        br_s[:, 0:N_GROUPS] = brg_ref[...]
        br_s[:, N_GROUPS:N_GROUPS + N_EXPERTS] = bre_ref[...]

    is_prompt = i < prompt_tiles
    half = x_ref.shape[0] // 2
    for r0 in (0, half):
        rs = slice(r0, r0 + half)
        ya = jnp.where(is_prompt, yap_ref[rs, :], yas_ref[rs, :])
        yb = jnp.where(is_prompt, ybp_ref[rs, :], ybs_ref[rs, :])
        a = _dot(ya, wbf[0])
        b = _dot(yb, wbf[1])
        u = jax.nn.sigmoid(ga_ref[rs, :].astype(F32)) * a + jax.nn.sigmoid(gb_ref[rs, :].astype(F32)) * b
        mix = _dot(u.astype(BF16), wbf[2])
        x1_ref[rs, :] = _layer_norm(DN_ALPHA * x_ref[rs, :] + mix, lng_ref[0:1, :], lnb_ref[0:1, :])
    x1 = x1_ref[...]

    logits = _dot(x1.astype(BF16), wr_b[...]) + br_s[...]
    lane = lax.broadcasted_iota(I32, logits.shape, 1)
    lanef = lane.astype(F32)
    gmask = lane < N_GROUPS
    gl = jnp.where(gmask, logits, NEG)
    gmax = jnp.max(gl, axis=-1, keepdims=True)
    gidx = jnp.min(jnp.where(gmask & (gl == gmax), lanef, 999.0), axis=-1, keepdims=True)
    g_w = 1.0 / jnp.sum(jnp.where(gmask, jnp.exp(gl - gmax), 0.0), axis=-1, keepdims=True)
    lo = N_GROUPS + EXPERTS_PER_GROUP * gidx
    emask = (lanef >= lo) & (lanef < lo + EXPERTS_PER_GROUP)
    el = jnp.where(emask, logits, NEG)
    emax = jnp.max(el, axis=-1, keepdims=True)
    ep = jnp.where(emask, jnp.exp(el - emax), 0.0)
    prob = ep / jnp.sum(ep, axis=-1, keepdims=True)
    prob = jnp.where(emask, prob, -1.0)
    p1 = jnp.max(prob, axis=-1, keepdims=True)
    i1 = jnp.min(jnp.where(prob == p1, lanef, 999.0), axis=-1, keepdims=True)
    prob2 = jnp.where(lanef == i1, -1.0, prob)
    p2 = jnp.max(prob2, axis=-1, keepdims=True)
    i2 = jnp.min(jnp.where(prob2 == p2, lanef, 999.0), axis=-1, keepdims=True)
    tot = p1 + p2
    route = jnp.where(lane == 0, i1 - N_GROUPS,
                      jnp.where(lane == 1, i2 - N_GROUPS,
                                jnp.where(lane == 2, g_w * p1 / tot,
                                          jnp.where(lane == 3, g_w * p2 / tot, 0.0))))
    route_ref[...] = route.T[0:8, :]


def _merge(x, ya_p, ya_s, yb_p, yb_s, zb, w_a, w_b, w_out, ln_g, ln_b, w_rg, w_re, b_rg, b_re, layer):
    m = x.shape[0]
    pt = ya_p.shape[0] // TM
    rowblk = lambda w, j: pl.BlockSpec((TM, w), lambda i: (i, j))
    pblk = pl.BlockSpec((TM, D_MODEL), lambda i: (jnp.minimum(i, pt - 1), 0))
    sblk = pl.BlockSpec((TM, D_MODEL), lambda i: (jnp.maximum(i - pt, 0), 0))
    return pl.pallas_call(
        functools.partial(_merge_kernel, prompt_tiles=pt),
        grid=(m // TM,),
        in_specs=[rowblk(D_MODEL, 0), pblk, sblk, pblk, sblk,
                  rowblk(D_MODEL, ZGA // D_MODEL), rowblk(D_MODEL, ZGB // D_MODEL),
                  _layer_spec((D_MODEL, D_MODEL), layer), _layer_spec((D_MODEL, D_MODEL), layer),
                  _layer_spec((D_MODEL, D_MODEL), layer),
                  _layer_spec((3, D_MODEL), layer), _layer_spec((3, D_MODEL), layer),
                  _layer_spec((D_MODEL, N_GROUPS), layer), _layer_spec((D_MODEL, N_EXPERTS), layer),
                  _layer_spec((1, N_GROUPS), layer), _layer_spec((1, N_EXPERTS), layer)],
        out_specs=[rowblk(D_MODEL, 0), pl.BlockSpec((8, TM), lambda i: (0, i))],
        out_shape=[jax.ShapeDtypeStruct((m, D_MODEL), F32), jax.ShapeDtypeStruct((8, m), F32)],
        scratch_shapes=[pltpu.VMEM((3, D_MODEL, D_MODEL), BF16), pltpu.VMEM((D_MODEL, LANES), BF16),
                        pltpu.VMEM((1, LANES), F32)],
        compiler_params=_cparams(("arbitrary",)),
        name="merge",
    )(x, ya_p, ya_s, yb_p, yb_s, zb, zb, w_a, w_b, w_out, ln_g, ln_b, w_rg, w_re, b_rg, b_re)


SEG_ALIGN = 8
LOC_ROWS = 2 * TD + N_EXPERTS * SEG_ALIGN
LOC_BITS = (LOC_ROWS // SEG_ALIGN).bit_length()


def _route_tables(route, valid, nblk):
    m = route.shape[1]
    nt = m // TD
    per = BLK // SEG_ALIGN
    experts = jnp.arange(N_EXPERTS, dtype=I32)
    eid = jnp.where(valid[None, :], route[0:2].astype(I32), -1).reshape(2, nt, TD)
    e3 = jnp.concatenate([eid[0], eid[1]], axis=1)
    eflat = e3.reshape(2 * m)
    hit = eflat[:, None] == experts
    rank_all = jnp.sum(jnp.where(hit, jnp.cumsum(hit.astype(I32), axis=0), 0), axis=1) - 1
    own = e3[:, :, None] == experts
    raw = jnp.sum(own.astype(I32), axis=1)
    cnt = (raw + SEG_ALIGN - 1) // SEG_ALIGN
    lofs = jnp.cumsum(cnt, axis=1) - cnt
    ltot = jnp.sum(cnt, axis=1)
    before = jnp.cumsum(raw, axis=0) - raw
    pick = lambda tbl: jnp.sum(jnp.where(own, tbl[:, None, :], 0), axis=2)
    lslot = pick(lofs) * SEG_ALIGN + rank_all.reshape(nt, 2 * TD) - pick(before)
    lslot = jnp.where(e3 >= 0, lslot, -1).astype(F32)
    gates = route[2:4].reshape(2, nt, TD)
    table = jnp.stack([lslot[:, :TD], lslot[:, TD:], gates[0], gates[1]] + [jnp.zeros((nt, TD), F32)] * 4, axis=1)
    gcnt = jnp.sum(cnt, axis=0)
    gpad = (gcnt + per - 1) // per * per
    gend = jnp.cumsum(gpad)
    gofs = (gend - gpad)[None, :] + jnp.cumsum(cnt, axis=0) - cnt
    nused = gend[-1] // per
    unused = nused + jnp.arange(N_EXPERTS, dtype=I32)
    zstart = jnp.concatenate([(gend - gpad + gcnt) * SEG_ALIGN, jnp.minimum(unused, nblk - 1) * BLK]).astype(I32)
    zlen = jnp.concatenate([(gpad - gcnt) * SEG_ALIGN, jnp.where(unused < nblk, BLK, 0)]).astype(I32)
    zstart = jnp.concatenate([zstart, zlen])
    bstart = jnp.minimum(jnp.arange(nblk, dtype=I32), nused - 1) * per
    blk_e = jnp.minimum(jnp.sum((gend[None, :] <= bstart[:, None]).astype(I32), axis=1), N_EXPERTS - 1)
    flat = lambda t: t.reshape(-1).astype(I32)
    return (flat(cnt), flat(lofs), flat(gofs), flat(ltot), table, zstart, flat(blk_e),
            nused.astype(I32).reshape(1))


def _segment_copies(i, cnt_ref, lofs_ref, gofs_ref, make_copy):
    for e in range(N_EXPERTS):
        c = cnt_ref[i * N_EXPERTS + e]
        lo = lofs_ref[i * N_EXPERTS + e]
        go = gofs_ref[i * N_EXPERTS + e]

        @pl.when(c > 0)
        def _():
            make_copy(pl.multiple_of(lo * SEG_ALIGN, SEG_ALIGN), pl.multiple_of(go * SEG_ALIGN, SEG_ALIGN),
                      pl.multiple_of(c * SEG_ALIGN, SEG_ALIGN)).start()


def _wait_rows(tot, make_copy):
    for k in range(LOC_BITS):
        @pl.when(((tot >> k) & 1) == 1)
        def _():
            make_copy(0, 0, SEG_ALIGN << k).wait()


def _dispatch_kernel(cnt_ref, lofs_ref, gofs_ref, ltot_ref, zstart_ref, x_ref, ls_ref, xs_hbm, xloc, zbuf, sem):
    i = pl.program_id(0)
    last = pl.num_programs(0) - 1
    slot = i % 2

    def copy_from(s):
        return lambda lo, go, n: pltpu.make_async_copy(xloc.at[s, pl.ds(lo, n)], xs_hbm.at[pl.ds(go, n)], sem.at[s])

    @pl.when(i == 0)
    def _():
        zbuf[...] = jnp.zeros_like(zbuf)
        nz = zstart_ref.shape[0] // 2
        for phase in ("start", "wait"):
            for e in range(nz):
                n = pl.multiple_of(zstart_ref[nz + e], SEG_ALIGN)

                @pl.when(n > 0)
                def _():
                    start = pl.multiple_of(zstart_ref[e], SEG_ALIGN)
                    cp = pltpu.make_async_copy(zbuf.at[pl.ds(0, n)], xs_hbm.at[pl.ds(start, n)], sem.at[2])
                    getattr(cp, phase)()

    @pl.when(i >= 2)
    def _():
        _wait_rows(ltot_ref[jnp.maximum(i - 2, 0)], copy_from(slot))

    ls = ls_ref[0, 0:2, :].astype(I32)
    j = lax.broadcasted_iota(I32, (LOC_ROWS, TD), 0)
    onehot = jnp.where((j == ls[0:1, :]) | (j == ls[1:2, :]), 1.0, 0.0).astype(BF16)
    xloc[slot] = _dot(onehot, x_ref[...].astype(BF16))
    _segment_copies(i, cnt_ref, lofs_ref, gofs_ref, copy_from(slot))

    @pl.when(i == last)
    def _():
        _wait_rows(ltot_ref[i], copy_from(slot))

        @pl.when(i >= 1)
        def _():
            _wait_rows(ltot_ref[jnp.maximum(i - 1, 0)], copy_from(1 - slot))


def _dispatch(x1, cnt, lofs, gofs, ltot, zstart, table, nblk):
    m = x1.shape[0]
    nt = m // TD
    grid_spec = pltpu.PrefetchScalarGridSpec(
        num_scalar_prefetch=5,
        grid=(nt,),
        in_specs=[pl.BlockSpec((TD, D_MODEL), lambda i, *_: (i, 0)),
                  pl.BlockSpec((1, 8, TD), lambda i, *_: (i, 0, 0))],
        out_specs=pl.BlockSpec(memory_space=pl.ANY),
        scratch_shapes=[pltpu.VMEM((2, LOC_ROWS, D_MODEL), F32), pltpu.VMEM((BLK, D_MODEL), F32),
                        pltpu.SemaphoreType.DMA((3,))],
    )
    return pl.pallas_call(
        _dispatch_kernel,
        grid_spec=grid_spec,
        out_shape=jax.ShapeDtypeStruct((nblk * BLK, D_MODEL), F32),
        compiler_params=_cparams(("arbitrary",)),
        name="dispatch",
    )(cnt, lofs, gofs, ltot, zstart, x1, table)


def _experts_kernel(blk_e_ref, nused_ref, xs_ref, wg_ref, wu_ref, wd_ref, ys_ref, wgb, wub, wdb):
    b = pl.program_id(0)
    used = b < nused_ref[0]
    new_expert = (b == 0) | (blk_e_ref[b] != blk_e_ref[jnp.maximum(b - 1, 0)])

    @pl.when(used & new_expert)
    def _():
        _cast_into(wgb, wg_ref)
        _cast_into(wub, wu_ref)
        _cast_into(wdb, wd_ref)

    @pl.when(used)
    def _():
        xb = xs_ref[...].astype(BF16)
        g = _dot(xb, wgb[...])
        u = _dot(xb, wub[...])
        ys_ref[...] = _dot((g * jax.nn.sigmoid(g) * u).astype(BF16), wdb[...])

    @pl.when(jnp.logical_not(used))
    def _():
        ys_ref[...] = jnp.zeros_like(ys_ref)


def _experts(xs, blk_e, nused, w_eg, w_eu, w_ed, layer):
    nb = xs.shape[0] // BLK
    wspec = lambda s: pl.BlockSpec((None, None) + s, lambda b, be, nu: (layer, be[b], 0, 0))
    grid_spec = pltpu.PrefetchScalarGridSpec(
        num_scalar_prefetch=2,
        grid=(nb,),
        in_specs=[pl.BlockSpec((BLK, D_MODEL), lambda b, be, nu: (jnp.minimum(b, nu[0] - 1), 0)),
                  wspec((D_MODEL, D_EXPERT)), wspec((D_MODEL, D_EXPERT)), wspec((D_EXPERT, D_MODEL))],
        out_specs=pl.BlockSpec((BLK, D_MODEL), lambda b, be, nu: (b, 0)),
        scratch_shapes=[pltpu.VMEM((D_MODEL, D_EXPERT), BF16), pltpu.VMEM((D_MODEL, D_EXPERT), BF16),
                        pltpu.VMEM((D_EXPERT, D_MODEL), BF16)],
    )
    return pl.pallas_call(
        _experts_kernel,
        grid_spec=grid_spec,
        out_shape=jax.ShapeDtypeStruct(xs.shape, F32),
        compiler_params=_cparams(("arbitrary",)),
        name="experts",
    )(blk_e, nused, xs, w_eg, w_eu, w_ed)


def _post_kernel(cnt_ref, lofs_ref, gofs_ref, ltot_ref, x1_ref, tab_ref, pp_ref, ps_ref, wpg_ref, wpp_ref,
                 lng_ref, lnb_ref, ys_hbm, o_ref, yloc, wpg_b, wpp_b, sem, *, prompt_tiles):
    i = pl.program_id(0)
    last = pl.num_programs(0) - 1
    slot = i % 2

    def copy_to(s):
        return lambda lo, go, n: pltpu.make_async_copy(ys_hbm.at[pl.ds(go, n)], yloc.at[s, pl.ds(lo, n)], sem.at[s])

    @pl.when(i == 0)
    def _():
        _cast_into(wpg_b, wpg_ref)
        _cast_into(wpp_b, wpp_ref)
        yloc[...] = jnp.zeros_like(yloc)
        _segment_copies(0, cnt_ref, lofs_ref, gofs_ref, copy_to(0))

    _wait_rows(ltot_ref[i], copy_to(slot))

    @pl.when(i < last)
    def _():
        _segment_copies(i + 1, cnt_ref, lofs_ref, gofs_ref, copy_to(1 - slot))

    tab = jnp.concatenate([tab_ref[0], jnp.zeros((LANES - 8, tab_ref.shape[2]), F32)], axis=0)
    tcol = jnp.concatenate([tab[:, c:c + LANES].T for c in range(0, tab.shape[1], LANES)], axis=0)
    j = lax.broadcasted_iota(I32, (tcol.shape[0], LOC_ROWS), 1).astype(F32)
    w = jnp.where(j == tcol[:, 0:1], tcol[:, 2:3], 0.0) + jnp.where(j == tcol[:, 1:2], tcol[:, 3:4], 0.0)
    moe = _dot(w.astype(BF16), yloc[slot].astype(BF16))
    x2 = _layer_norm(DN_ALPHA * x1_ref[...] + moe, lng_ref[1:2, :], lnb_ref[1:2, :])
    p = jnp.where(i < prompt_tiles, pp_ref[...], ps_ref[...])
    ple = jax.nn.sigmoid(_dot(x2.astype(BF16), wpg_b[...])) * _dot(p.astype(BF16), wpp_b[...])
    o_ref[...] = _layer_norm(DN_ALPHA * x2 + ple, lng_ref[2:3, :], lnb_ref[2:3, :])


def _post(x1, ys, cnt, lofs, gofs, ltot, table, p_prompt, p_sample, w_pg, w_pp, ln_g, ln_b, layer):
    m = x1.shape[0]
    pt = p_prompt.shape[1] // TD
    rowblk = lambda w: pl.BlockSpec((TD, w), lambda i, *_: (i, 0))
    lspec = lambda s: pl.BlockSpec((None,) + s, lambda i, *_: (layer,) + (0,) * len(s), pipeline_mode=pl.Buffered(1))
    grid_spec = pltpu.PrefetchScalarGridSpec(
        num_scalar_prefetch=4,
        grid=(m // TD,),
        in_specs=[rowblk(D_MODEL), pl.BlockSpec((1, 8, TD), lambda i, *_: (i, 0, 0)),
                  pl.BlockSpec((None, TD, D_PLE), lambda i, *_: (layer, jnp.minimum(i, pt - 1), 0)),
                  pl.BlockSpec((None, TD, D_PLE), lambda i, *_: (layer, jnp.maximum(i - pt, 0), 0)),
                  lspec((D_MODEL, D_MODEL)), lspec((D_PLE, D_MODEL)), lspec((3, D_MODEL)), lspec((3, D_MODEL)),
                  pl.BlockSpec(memory_space=pl.ANY)],
        out_specs=rowblk(D_MODEL),
        scratch_shapes=[pltpu.VMEM((2, LOC_ROWS, D_MODEL), F32), pltpu.VMEM((D_MODEL, D_MODEL), BF16),
                        pltpu.VMEM((D_PLE, D_MODEL), BF16), pltpu.SemaphoreType.DMA((2,))],
    )
    return pl.pallas_call(
        functools.partial(_post_kernel, prompt_tiles=pt),
        grid_spec=grid_spec,
        out_shape=jax.ShapeDtypeStruct((m, D_MODEL), F32),
        compiler_params=_cparams(("arbitrary",)),
        name="post",
    )(cnt, lofs, gofs, ltot, x1, table, p_prompt, p_sample, w_pg, w_pp, ln_g, ln_b, ys)


def kernel(x_prompt, x_sample, state_mlstm_C, state_mlstm_n, state_mlstm_m, state_swa_k, state_swa_v, p_prompt, p_sample, w_in, b_in, mh_gain, w_a, w_b, w_out, rel_table, w_sink, ln_g, ln_b, w_rg, b_rg, w_re, b_re, w_eg, w_eu, w_ed, w_pg, w_pp):
    bp_, seq = x_prompt.shape[:2]
    bs_, tdec = x_sample.shape[:2]
    mp = bp_ * seq
    msp = bs_ * T_PAD
    mall = mp + msp
    nblk = -(-(2 * (mp + bs_ * tdec) + (mall // TD) * N_EXPERTS * (SEG_ALIGN - 1)) // BLK) + N_EXPERTS

    pad_t = lambda t: jnp.pad(t, [(0, 0)] * (t.ndim - 2) + [(0, T_PAD - tdec), (0, 0)])
    x = lax.dynamic_update_slice(jnp.pad(x_prompt.reshape(mp, D_MODEL), ((0, msp), (0, 0))),
                                 pad_t(x_sample).reshape(msp, D_MODEL), (mp, 0))
    valid = jnp.concatenate([jnp.ones((mp,), bool), jnp.tile(jnp.arange(T_PAD) < tdec, bs_)])
    p_prompt = p_prompt.reshape(DEPTH, mp, D_PLE)
    p_sample = pad_t(p_sample).reshape(DEPTH, msp, D_PLE)

    b_in3 = b_in.reshape(DEPTH, 1, -1)
    b_b = jnp.concatenate([b_in3[:, :, IN_GATES[0]:IN_GATES[1]], b_in3[:, :, IN_QKV2[0]:IN_QKV2[1]]], axis=-1)
    b_rg, b_re = b_rg.reshape(DEPTH, 1, N_GROUPS), b_re.reshape(DEPTH, 1, N_EXPERTS)
    gain = mh_gain.reshape(DEPTH, 1, D_MODEL)

    bias_p = _swa_bias(rel_table, WINDOW)
    bias_s = _swa_bias(rel_table, T_PAD)
    sink_p = sink_s = _sink_lanes(w_sink)
    buf_k = state_swa_k.reshape(DEPTH, bs_, WINDOW, D_KV)
    buf_v = state_swa_v.reshape(DEPTH, bs_, WINDOW, D_KV)
    m0_s = jnp.broadcast_to(state_mlstm_m[..., None], (DEPTH, bs_, M_HEADS, LANES))
    zc = jnp.zeros((1, bp_, M_HEADS, M_DQK, M_DV), F32)
    zn = jnp.zeros((1, bp_, M_HEADS, LANES), F32)
    st_s = kv_s = None

    st_p = []
    for i in range(DEPTH):
        za, zg = _inproj(x, [(w_in, IN_A)], [(b_in3, IN_A)], [(ZG, BF16), (LANES, F32)], i, "inproj_a")
        zb = _inproj_b(x, w_in, b_b, i)

        ya_p, c_p, n_p, m_p = _mlstm(za, zg, gain, i, zc, zn, zn, 0, None, 1, 0, row0=0, nseq=bp_,
                                     steps=seq // LANES, seg=LANES, t_valid=LANES)
        ya_s, *st_s = _mlstm(za, zg, gain, i, state_mlstm_C, state_mlstm_n, m0_s, i, st_s, DEPTH, i, row0=mp,
                             nseq=bs_ // SEQ_BLK, steps=1, seg=T_PAD, t_valid=tdec)
        yb_p = _swa_prompt(zb, bias_p[0], bias_p[1], sink_p, i, batch=bp_, seq=seq)
        yb_s, *kv_s = _swa_sample(zb, buf_k, buf_v, bias_s[0], bias_s[1], sink_s, i, kv_s, row0=mp, batch=bs_,
                                  t_new=tdec)

        x1, route = _merge(x, ya_p, ya_s, yb_p, yb_s, zb, w_a, w_b, w_out, ln_g, ln_b, w_rg, w_re, b_rg, b_re, i)
        cnt, lofs, gofs, ltot, table, zstart, blk_e, nused = _route_tables(route, valid, nblk)
        xs = _dispatch(x1, cnt, lofs, gofs, ltot, zstart, table, nblk)
        ys = _experts(xs, blk_e, nused, w_eg, w_eu, w_ed, i)
        x = _post(x1, ys, cnt, lofs, gofs, ltot, table, p_prompt, p_sample, w_pg, w_pp, ln_g, ln_b, i)

        kv_p = jnp.stack([lax.slice(zb, ((b + 1) * seq - WINDOW, ZKS), ((b + 1) * seq, IN_B)) for b in range(bp_)])
        kv_p = kv_p.astype(F32)
        heads = lambda t: t.reshape(t.shape[:-1] + (S_KV_HEADS, S_HEAD_DIM))
        st_p.append((c_p[0], n_p[0], m_p[0, :, :, 0], heads(kv_p[..., :D_KV]), heads(kv_p[..., D_KV:])))

    stk = lambda j: jnp.stack([s[j] for s in st_p], axis=0)
    y_p = x[:mp].reshape(bp_, seq, D_MODEL)
    y_s = x[mp:].reshape(bs_, T_PAD, D_MODEL)[:, :tdec]
    return (y_p, y_s, stk(0), stk(1), stk(2), stk(3), stk(4),
            st_s[0], st_s[1], st_s[2][..., 0], heads(kv_s[0]), heads(kv_s[1]))
```

```python
import functools

import numpy as np
import jax
import jax.numpy as jnp
from jax import lax
from jax.experimental import pallas as pl
from jax.experimental.pallas import tpu as pltpu

F32 = jnp.float32
BF16 = jnp.bfloat16
I32 = jnp.int32
HIGHEST = lax.Precision.HIGHEST

D_MODEL = 1024
DEPTH = 4
D_PLE = 256
M_HEADS = 4
M_DQK = 128
M_DV = 256
S_HEADS = 16
S_KV_HEADS = 4
S_HEAD_DIM = 64
S_GROUP = 4
D_KV = S_KV_HEADS * S_HEAD_DIM
WINDOW = 128
REL_BUCKETS = 32
N_GROUPS = 4
EXPERTS_PER_GROUP = 8
N_EXPERTS = 32
D_EXPERT = 512
DN_ALPHA = (2 * DEPTH) ** 0.25
LN_EPS = 1e-5
NEG = -1e30

IN_A = 3200
IN_QKV2 = (3080, 4616)
IN_GATES = (4616, 6664)
ZQ, ZK, ZV, ZOG, ZG = 0, 512, 1024, 2048, 3072
IN_B = 3584
ZGA, ZGB, ZQS, ZKS, ZVS = 0, 1024, 2048, 3072, 3328

LANES = 128
T_PAD = 8
SEQ_BLK = LANES // T_PAD
TM = 512
TD = 256
BLK = 512
VMEM_LIMIT = 56 * 1024 * 1024


def _cparams(sem):
    return pltpu.CompilerParams(dimension_semantics=sem, vmem_limit_bytes=VMEM_LIMIT)


def _layer_spec(shape, layer):
    nd = len(shape)
    return pl.BlockSpec((None,) + shape, lambda *_: (layer,) + (0,) * nd, pipeline_mode=pl.Buffered(1))


def _layer_norm(y, g, b):
    mu = jnp.mean(y, axis=-1, keepdims=True)
    d = y - mu
    var = jnp.mean(d * d, axis=-1, keepdims=True)
    return d * lax.rsqrt(var + LN_EPS) * g + b


def _dot(a, b):
    return jnp.dot(a, b, preferred_element_type=F32)


def _dot_nt(a, b):
    return lax.dot_general(a, b, (((1,), (1,)), ((), ())), preferred_element_type=F32)


def _dot_tn(a, b):
    return lax.dot_general(a, b, (((0,), (0,)), ((), ())), preferred_element_type=F32)


def _dot_exact(a, b):
    return jnp.dot(a, b, precision=HIGHEST, preferred_element_type=F32)


def _cast_into(dst_ref, src_ref, col0=0, step=512):
    n = src_ref.shape[-1]
    for j in range(0, n, step):
        w = min(step, n - j)
        dst_ref[:, col0 + j:col0 + j + w] = src_ref[:, j:j + w].astype(BF16)


def _inproj_kernel(*refs, widths, out_widths):
    ng, no = len(widths), len(out_widths)
    x_ref, w_refs, b_refs = refs[0], refs[1:1 + ng], refs[1 + ng:1 + 2 * ng]
    o_refs, wbf = refs[1 + 2 * ng:1 + 2 * ng + no], refs[1 + 2 * ng + no]

    @pl.when(pl.program_id(0) == 0)
    def _():
        c = 0
        for w_ref, n in zip(w_refs, widths):
            _cast_into(wbf, w_ref, c)
            c += n

    out_start = np.cumsum((0,) + tuple(out_widths))

    def store(col, val):
        k = int(np.searchsorted(out_start, col, side='right')) - 1
        lo = col - int(out_start[k])
        o_refs[k][:, lo:lo + val.shape[1]] = val.astype(o_refs[k].dtype)

    xb = x_ref[...].astype(BF16)
    c = 0
    for b_ref, n in zip(b_refs, widths):
        for j in range(0, n, 512):
            w = min(512, n - j)
            store(c + j, _dot(xb, wbf[:, c + j:c + j + w]) + b_ref[:, j:j + w])
        c += n


def _inproj(x, ws, bs, outs, layer, name):
    m = x.shape[0]
    widths = tuple(n for _, n in ws)
    out_widths = tuple(n for n, _ in outs)
    assert sum(widths) == sum(out_widths)
    return pl.pallas_call(
        functools.partial(_inproj_kernel, widths=widths, out_widths=out_widths),
        grid=(m // TM,),
        in_specs=([pl.BlockSpec((TM, D_MODEL), lambda i: (i, 0))]
                  + [_layer_spec((D_MODEL, n), layer) for _, n in ws]
                  + [_layer_spec((1, n), layer) for _, n in bs]),
        out_specs=[pl.BlockSpec((TM, n), lambda i: (i, 0)) for n, _ in outs],
        out_shape=[jax.ShapeDtypeStruct((m, n), dt) for n, dt in outs],
        scratch_shapes=[pltpu.VMEM((D_MODEL, sum(widths)), BF16)],
        compiler_params=_cparams(("arbitrary",)),
        name=name,
    )(x, *[w for w, _ in ws], *[b for b, _ in bs])


def _inproj_b_kernel(x_ref, b_ref, w_hbm, z_ref, wwin, wbf, sem, *, layer):
    lead = IN_QKV2[0] % LANES
    n_qkv = IN_QKV2[1] - IN_QKV2[0]

    @pl.when(pl.program_id(0) == 0)
    def _():
        cp = pltpu.make_async_copy(w_hbm.at[layer, :, pl.ds(IN_QKV2[0] - lead, wwin.shape[1])], wwin, sem.at[0])
        cp.start()
        cp.wait()
        for dst, src, n in ((0, n_qkv, IN_B - n_qkv), (IN_B - n_qkv, 0, n_qkv)):
            for j in range(0, n, 512):
                w = min(512, n - j)
                wbf[:, dst + j:dst + j + w] = wwin[:, lead + src + j:lead + src + j + w].astype(BF16)

    xb = x_ref[...].astype(BF16)
    for j in range(0, IN_B, 512):
        w = min(512, IN_B - j)
        z_ref[:, j:j + w] = (_dot(xb, wbf[:, j:j + w]) + b_ref[:, j:j + w]).astype(z_ref.dtype)


def _inproj_b(x, w_in, b_b, layer):
    m = x.shape[0]
    win = w_in.shape[2] - (IN_QKV2[0] - IN_QKV2[0] % LANES)
    return pl.pallas_call(
        functools.partial(_inproj_b_kernel, layer=layer),
        grid=(m // TM,),
        in_specs=[pl.BlockSpec((TM, D_MODEL), lambda i: (i, 0)), _layer_spec((1, IN_B), layer),
                  pl.BlockSpec(memory_space=pl.ANY)],
        out_specs=pl.BlockSpec((TM, IN_B), lambda i: (i, 0)),
        out_shape=jax.ShapeDtypeStruct((m, IN_B), BF16),
        scratch_shapes=[pltpu.VMEM((D_MODEL, win), F32), pltpu.VMEM((D_MODEL, IN_B), BF16),
                        pltpu.SemaphoreType.DMA((1,))],
        compiler_params=_cparams(("arbitrary",)),
        name="inproj_b",
    )(x, b_b, w_in)


def _log_sigmoid(x):
    return jnp.minimum(x, 0.0) - jnp.log(1.0 + jnp.exp(-jnp.abs(x)))


def _mlstm_kernel(zq, zk, zv, zog, zg, gain, c0, n0, m0, *rest, seg, t_valid, aliased):
    ya, co, no, mo, qc_s = rest[aliased:]

    @pl.when(pl.program_id(1) == 0)
    def _():
        co[...] = c0[...]
        no[...] = n0[...]
        mo[...] = m0[...]

    for r0 in range(0, zq.shape[0], LANES):
        rows = pl.ds(r0, LANES)
        _mlstm_block(zq.at[rows], zk.at[rows], zv.at[rows], zog.at[rows], zg.at[rows], gain, ya.at[rows],
                     co, no, mo, qc_s, seg=seg, t_valid=t_valid)


def _mlstm_block(zq, zk, zv, zog, zg, gain, ya, co, no, mo, qc_s, *, seg, t_valid):
    L = zq.shape[0]
    nseg = L // seg
    g_c = zg[...]
    g_r = g_c.T
    ig_c, lf_c = g_c, _log_sigmoid(g_c)
    ig_r, lf_r = g_r, _log_sigmoid(g_r)
    ri = lax.broadcasted_iota(I32, (L, L), 0)
    ci = lax.broadcasted_iota(I32, (L, L), 1)
    shift = seg.bit_length() - 1
    if t_valid < seg:
        rv = (lax.broadcasted_iota(I32, (L, 1), 0) & (seg - 1)) < t_valid
        cv = (lax.broadcasted_iota(I32, (1, L), 1) & (seg - 1)) < t_valid
        ig_c, lf_c = jnp.where(rv, ig_c, NEG), jnp.where(rv, lf_c, 0.0)
        ig_r, lf_r = jnp.where(cv, ig_r, NEG), jnp.where(cv, lf_r, 0.0)
    if nseg > 1:
        same = (ri >> shift) == (ci >> shift)
        causal, causal_t = same & (ri >= ci), same & (ri <= ci)
        same_f = same.astype(F32)
        be_c = _dot_exact(same_f, lf_c)
        be_r = _dot_exact(lf_r, same_f)
        expand = ((lax.broadcasted_iota(I32, (L, nseg), 0) >> shift)
                  == lax.broadcasted_iota(I32, (L, nseg), 1)).astype(F32)
    else:
        causal, causal_t = ri >= ci, ri <= ci
    b_c = _dot_exact(causal.astype(F32), lf_c)
    b_r = _dot_exact(lf_r, causal_t.astype(F32))

    for h in range(M_HEADS):
        icol, irow = ig_c[:, h:h + 1], ig_r[h:h + 1, :]
        bcol, brow = b_c[:, 4 + h:5 + h], b_r[4 + h:5 + h, :]
        q = zq[:, h * M_DQK:(h + 1) * M_DQK].astype(F32) * (M_DQK ** -0.5)
        k = zk[:, h * M_DQK:(h + 1) * M_DQK].astype(F32)
        qb, kb = q.astype(BF16), zk[:, h * M_DQK:(h + 1) * M_DQK]
        vb = zv[:, h * M_DV:(h + 1) * M_DV]
        if nseg > 1:
            mcol = _dot_exact(expand, mo[:, h, :])[:, 0:1]
            nrow = _dot_exact(expand, no[:, h, :])
            becol, berow = be_c[:, 4 + h:5 + h], be_r[4 + h:5 + h, :]
            for j in range(nseg):
                rows = slice(j * seg, (j + 1) * seg)
                qc_s[rows, :] = _dot(q[rows].astype(BF16), co[j, h].astype(BF16))
            q_c = qc_s[...]
        else:
            mcol, nrow = mo[0, h:h + 1, 0:1], no[0, h:h + 1, :]
            becol = bcol[L - 1:L, :]
            q_c = _dot(qb, co[0, h].astype(BF16))

        dmat = jnp.where(causal, bcol - brow + irow, NEG)
        inter = bcol + mcol
        mhat = jnp.maximum(inter, jnp.max(dmat, axis=-1, keepdims=True))
        w_intra = jnp.exp(dmat - mhat)
        w_inter = jnp.exp(inter - mhat)
        s = _dot_nt(qb, kb) * w_intra
        num = _dot(s.astype(BF16), vb) + w_inter * q_c
        den = jnp.sum(s, axis=-1, keepdims=True) + w_inter * jnp.sum(q * nrow, axis=-1, keepdims=True)
        hh = num / jnp.maximum(jnp.abs(den), jnp.exp(-mhat))

        gcol = icol + becol - bcol
        if nseg > 1:
            gmax = jnp.max(jnp.where(same, irow + berow - brow, NEG), axis=-1, keepdims=True)
        else:
            gmax = jnp.max(gcol, axis=0, keepdims=True)
        m_new = jnp.maximum(becol + mcol, gmax)
        decay = jnp.exp(becol + mcol - m_new)
        kw = k * jnp.exp(gcol - m_new)
        vf = vb.astype(F32) if nseg > 1 else vb
        for j in range(nseg):
            rows = slice(j * seg, (j + 1) * seg)
            dj = decay[j * seg:j * seg + 1, :] if nseg > 1 else decay
            mj = m_new[j * seg:j * seg + 1, :] if nseg > 1 else m_new
            co[j, h] = dj * co[j, h] + _dot_tn(kw[rows].astype(BF16), vf[rows].astype(BF16))
            no[j, h:h + 1, :] = dj * no[j, h:h + 1, :] + jnp.sum(kw[rows], axis=0, keepdims=True)
            mo[j, h:h + 1, :] = jnp.broadcast_to(mj, (1, LANES))

        mu = jnp.mean(hh, axis=-1, keepdims=True)
        d = hh - mu
        var = jnp.mean(d * d, axis=-1, keepdims=True)
        hn = d * lax.rsqrt(var + LN_EPS) * gain[:, h * M_DV:(h + 1) * M_DV]
        og = zog[:, h * M_DV:(h + 1) * M_DV].astype(F32)
        ya[:, h * M_DV:(h + 1) * M_DV] = (hn * jax.nn.sigmoid(og)).astype(ya.dtype)


def _mlstm(za, zg, gain, layer, c0, n0, m0, st_layer, outs_prev, out_layers, out_layer, *, row0, nseq, steps,
           seg, t_valid, blocks_per_step=1):
    nseg = LANES // seg
    rows = LANES * blocks_per_step
    r0 = row0 // rows
    row = lambda b, c: r0 + b * steps + c
    nstate = nseq * nseg
    aliased = 0 if outs_prev is None else 3
    kern = functools.partial(_mlstm_kernel, seg=seg, t_valid=min(t_valid, seg), aliased=aliased)
    in4 = pl.BlockSpec((None, nseg, M_HEADS, M_DQK, M_DV), lambda b, c: (st_layer, b, 0, 0, 0))
    in3 = pl.BlockSpec((None, nseg, M_HEADS, LANES), lambda b, c: (st_layer, b, 0, 0))
    out4 = pl.BlockSpec((None, nseg, M_HEADS, M_DQK, M_DV), lambda b, c: (out_layer, b, 0, 0, 0))
    out3 = pl.BlockSpec((None, nseg, M_HEADS, LANES), lambda b, c: (out_layer, b, 0, 0))
    anyspec = pl.BlockSpec(memory_space=pl.ANY)
    n_in = 9
    return pl.pallas_call(
        kern,
        grid=(nseq, steps),
        in_specs=[pl.BlockSpec((rows, 512), lambda b, c: (row(b, c), ZQ // 512)),
                  pl.BlockSpec((rows, 512), lambda b, c: (row(b, c), ZK // 512)),
                  pl.BlockSpec((rows, 1024), lambda b, c: (row(b, c), ZV // 1024)),
                  pl.BlockSpec((rows, 1024), lambda b, c: (row(b, c), ZOG // 1024)),
                  pl.BlockSpec((rows, LANES), lambda b, c: (row(b, c), 0)),
                  pl.BlockSpec((None, 1, D_MODEL), lambda b, c: (layer, 0, 0)),
                  in4, in3, in3] + [anyspec] * aliased,
        out_specs=[pl.BlockSpec((rows, D_MODEL), lambda b, c: (b * steps + c, 0)), out4, out3, out3],
        out_shape=[jax.ShapeDtypeStruct((nseq * steps * rows, D_MODEL), BF16),
                   jax.ShapeDtypeStruct((out_layers, nstate, M_HEADS, M_DQK, M_DV), F32),
                   jax.ShapeDtypeStruct((out_layers, nstate, M_HEADS, LANES), F32),
                   jax.ShapeDtypeStruct((out_layers, nstate, M_HEADS, LANES), F32)],
        scratch_shapes=[pltpu.VMEM((LANES, M_DV), F32)],
        input_output_aliases={n_in + j: 1 + j for j in range(aliased)},
        compiler_params=_cparams(("parallel", "arbitrary")),
        name="mlstm",
    )(za, za, za, za, zg, gain, c0, n0, m0, *(outs_prev or ()))


def _swa_kernel(q_ref, kc_ref, vc_ref, kp_ref, vp_ref, bp_ref, bc_ref, sink_ref, *rest, nseq, t_new,
                first_block_has_no_past, aliased):
    o_ref = rest[aliased]
    rows_n = q_ref.shape[0]
    tq = rows_n // nseq
    have_past = pl.program_id(1) > 0
    lane = lax.broadcasted_iota(I32, (1, LANES), 1)
    in_lo = lane < S_HEAD_DIM
    if nseq > 1:
        kp_all = kp_ref[...].reshape(nseq * WINDOW, D_KV).astype(BF16)
        vp_all = vp_ref[...].reshape(nseq * WINDOW, D_KV).astype(BF16)
        row_seq = lax.broadcasted_iota(I32, (rows_n, LANES), 0) >> (tq.bit_length() - 1)
    else:
        kp_all, vp_all = kp_ref[...], vp_ref[...]
    kc_all, vc_all = kc_ref[...], vc_ref[...]

    def placed(x, src_half, ones):
        swapped = jnp.concatenate([x[:, S_HEAD_DIM:], x[:, :S_HEAD_DIM]], axis=1)
        zero = jnp.zeros_like(x)
        lo = jnp.where(in_lo, x if src_half == 0 else swapped, zero)
        hi = jnp.where(in_lo, zero, x if src_half == 1 else swapped)
        if ones:
            lo = jnp.where(lane == S_HEAD_DIM, jnp.ones_like(x), lo)
            hi = jnp.where(lane == 0, jnp.ones_like(x), hi)
        return lo, hi

    kv_per_pass = S_KV_HEADS if nseq == 1 else 1
    for kv0 in range(0, S_KV_HEADS, kv_per_pass):
        kp_v, kc_v, vp_v, vc_v = {}, {}, {}, {}
        for kvh in range(kv0, kv0 + kv_per_pass):
            tile = slice((kvh // 2) * LANES, (kvh // 2 + 1) * LANES)
            kp_v[kvh] = placed(kp_all[:, tile], kvh % 2, False)
            kc_v[kvh] = placed(kc_all[:, tile], kvh % 2, False)
            vp_v[kvh] = placed(vp_all[:, tile], kvh % 2, True)
            vc_v[kvh] = placed(vc_all[:, tile], kvh % 2, True)
        heads = list(range(kv0 * S_GROUP, (kv0 + kv_per_pass) * S_GROUP))
        scores = []
        for h in heads:
            q2 = q_ref[:, (h // 2) * LANES:(h // 2 + 1) * LANES]
            scores.append((_dot_nt(q2, kp_v[h // S_GROUP][h % 2]), _dot_nt(q2, kc_v[h // S_GROUP][h % 2])))
        probs = []
        for h, (sp, sc) in zip(heads, scores):
            if nseq > 1:
                sp = jnp.concatenate([sp[b * tq:(b + 1) * tq, b * WINDOW:(b + 1) * WINDOW] for b in range(nseq)],
                                     axis=0)
            sp = sp * (S_HEAD_DIM ** -0.5) + bp_ref[h]
            sc = sc * (S_HEAD_DIM ** -0.5) + bc_ref[h]
            if first_block_has_no_past:
                sp = jnp.where(have_past, sp, NEG)
            sk = sink_ref[h][:, 0:1]
            mx = jnp.maximum(jnp.max(jnp.maximum(sp, sc), axis=-1, keepdims=True), sk)
            pp, pc = jnp.exp(sp - mx), jnp.exp(sc - mx)
            if nseq > 1:
                pp = jnp.concatenate([jnp.where(row_seq == b, pp, 0.0) for b in range(nseq)], axis=1)
            probs.append((pp.astype(BF16), pc.astype(BF16), jnp.exp(sk - mx)))
        outs = [_dot(pp, vp_v[h // S_GROUP][h % 2]) + _dot(pc, vc_v[h // S_GROUP][h % 2])
                for h, (pp, pc, _) in zip(heads, probs)]
        for pair in range(len(heads) // 2):
            halves = []
            for half in range(2):
                o, sink_term = outs[2 * pair + half], probs[2 * pair + half][2]
                ones_lane = S_HEAD_DIM if half == 0 else 0
                halves.append(o / (o[:, ones_lane:ones_lane + 1] + sink_term))
            qt = heads[2 * pair] // 2
            o_ref[:, qt * LANES:(qt + 1) * LANES] = jnp.where(in_lo, halves[0], halves[1]).astype(o_ref.dtype)

    if t_new:
        kc_f, vc_f = kc_all.astype(F32), vc_all.astype(F32)
        for b in range(nseq):
            for new, prev, out in ((kc_f, kp_ref, rest[aliased + 1]), (vc_f, vp_ref, rest[aliased + 2])):
                out[b, 0:WINDOW - t_new, :] = prev[b, t_new:, :]
                out[b, WINDOW - t_new:WINDOW, :] = new[b * tq:b * tq + t_new, :]


def _rel_bucket(dist):
    nn = np.maximum(dist, 0)
    max_exact = REL_BUCKETS // 2
    large = max_exact + (np.log(np.maximum(nn, 1) / max_exact) / np.log(WINDOW / max_exact)
                         * (REL_BUCKETS - max_exact)).astype(np.int32)
    large = np.minimum(large, REL_BUCKETS - 1)
    return np.where(nn < max_exact, nn, large).astype(np.int32)


def _swa_bias(rel_table, tq):
    nseq = LANES // tq
    t = np.tile(np.arange(tq), nseq)[:, None]
    seq = np.repeat(np.arange(nseq), tq)
    d_prev = t + WINDOW - np.arange(WINDOW)[None, :]
    d_cur = t - t.T
    own = seq[:, None] == seq[None, :]

    def build(dist, ok):
        onehot = np.eye(REL_BUCKETS, dtype=np.float32)[_rel_bucket(dist).reshape(-1)]
        bias = jnp.dot(onehot, rel_table.astype(F32), precision=HIGHEST).reshape(dist.shape + (S_HEADS,))
        return jnp.where(ok[None], jnp.transpose(bias, (2, 0, 1)), NEG)

    return build(d_prev, d_prev <= WINDOW), build(d_cur, own & (d_cur >= 0))


def _sink_lanes(w_sink):
    return jnp.broadcast_to(w_sink.astype(F32)[:, :, None, None], (DEPTH, S_HEADS, 1, LANES))


def _swa_bias_specs(layer):
    return [pl.BlockSpec((S_HEADS, LANES, WINDOW), lambda b, n: (0, 0, 0)),
            pl.BlockSpec((S_HEADS, LANES, LANES), lambda b, n: (0, 0, 0)),
            pl.BlockSpec((None, S_HEADS, 1, LANES), lambda b, n: (layer, 0, 0, 0))]


def _swa_prompt(zb, bp, bc, sk, layer, *, batch, seq):
    nb = seq // WINDOW
    row = lambda b, n: b * nb + n
    prev = lambda b, n: jnp.maximum(b * nb + n - 1, 0)
    in_specs = [pl.BlockSpec((WINDOW, 1024), lambda b, n: (row(b, n), ZQS // 1024)),
                pl.BlockSpec((WINDOW, D_KV), lambda b, n: (row(b, n), ZKS // D_KV)),
                pl.BlockSpec((WINDOW, D_KV), lambda b, n: (row(b, n), ZVS // D_KV)),
                pl.BlockSpec((WINDOW, D_KV), lambda b, n: (prev(b, n), ZKS // D_KV)),
                pl.BlockSpec((WINDOW, D_KV), lambda b, n: (prev(b, n), ZVS // D_KV))]
    kern = functools.partial(_swa_kernel, nseq=1, t_new=0, first_block_has_no_past=True, aliased=0)
    return pl.pallas_call(
        kern,
        grid=(batch, nb),
        in_specs=in_specs + _swa_bias_specs(layer),
        out_specs=pl.BlockSpec((WINDOW, D_MODEL), lambda b, n: (row(b, n), 0)),
        out_shape=jax.ShapeDtypeStruct((batch * seq, D_MODEL), BF16),
        compiler_params=_cparams(("parallel", "arbitrary")),
        name="swa_prompt",
    )(zb, zb, zb, zb, zb, bp, bc, sk)


def _swa_sample(zb, buf_k, buf_v, bp, bc, sk, layer, kv_prev, *, row0, batch, t_new):
    r0 = row0 // LANES
    steps = batch // SEQ_BLK
    state = pl.BlockSpec((None, SEQ_BLK, WINDOW, D_KV), lambda b, n: (layer, b, 0, 0))
    anyspec = pl.BlockSpec(memory_space=pl.ANY)
    in_specs = [pl.BlockSpec((LANES, 1024), lambda b, n: (r0 + b, ZQS // 1024)),
                pl.BlockSpec((LANES, D_KV), lambda b, n: (r0 + b, ZKS // D_KV)),
                pl.BlockSpec((LANES, D_KV), lambda b, n: (r0 + b, ZVS // D_KV)),
                state, state]
    aliased = 0 if kv_prev is None else 2
    kern = functools.partial(_swa_kernel, nseq=SEQ_BLK, t_new=t_new, first_block_has_no_past=False, aliased=aliased)
    return pl.pallas_call(
        kern,
        grid=(steps, 1),
        in_specs=in_specs + _swa_bias_specs(layer) + [anyspec] * aliased,
        out_specs=[pl.BlockSpec((LANES, D_MODEL), lambda b, n: (b, 0)), state, state],
        out_shape=[jax.ShapeDtypeStruct((batch * T_PAD, D_MODEL), BF16),
                   jax.ShapeDtypeStruct(buf_k.shape, F32), jax.ShapeDtypeStruct(buf_v.shape, F32)],
        input_output_aliases={8 + j: 1 + j for j in range(aliased)},
        compiler_params=_cparams(("parallel", "arbitrary")),
        name="swa_sample",
    )(zb, zb, zb, buf_k, buf_v, bp, bc, sk, *(kv_prev or ()))


def _merge_kernel(x_ref, yap_ref, yas_ref, ybp_ref, ybs_ref, ga_ref, gb_ref, wa_ref, wb_ref, wo_ref, lng_ref, lnb_ref,
                  wrg_ref, wre_ref, brg_ref, bre_ref, x1_ref, route_ref, wbf, wr_b, br_s, *, prompt_tiles):
    i = pl.program_id(0)

    @pl.when(i == 0)
    def _():
        for j, w_ref in enumerate((wa_ref, wb_ref, wo_ref)):
            _cast_into(wbf.at[j], w_ref)
        wr_b[...] = jnp.zeros_like(wr_b)
        wr_b[:, 0:N_GROUPS] = wrg_ref[...].astype(BF16)
        wr_b[:, N_GROUPS:N_GROUPS + N_EXPERTS] = wre_ref[...].astype(BF16)
        br_s[...] = jnp.zeros_like(br_s)
        br_s[:, 0:N_GROUPS] = brg_ref[...]
        br_s[:, N_GROUPS:N_GROUPS + N_EXPERTS] = bre_ref[...]

    is_prompt = i < prompt_tiles
    half = x_ref.shape[0] // 2
    for r0 in (0, half):
        rs = slice(r0, r0 + half)
        ya = jnp.where(is_prompt, yap_ref[rs, :], yas_ref[rs, :])
        yb = jnp.where(is_prompt, ybp_ref[rs, :], ybs_ref[rs, :])
        a = _dot(ya, wbf[0])
        b = _dot(yb, wbf[1])
        u = jax.nn.sigmoid(ga_ref[rs, :].astype(F32)) * a + jax.nn.sigmoid(gb_ref[rs, :].astype(F32)) * b
        mix = _dot(u.astype(BF16), wbf[2])
        x1_ref[rs, :] = _layer_norm(DN_ALPHA * x_ref[rs, :] + mix, lng_ref[0:1, :], lnb_ref[0:1, :])
    x1 = x1_ref[...]

    logits = _dot(x1.astype(BF16), wr_b[...]) + br_s[...]
    lane = lax.broadcasted_iota(I32, logits.shape, 1)
    lanef = lane.astype(F32)
    gmask = lane < N_GROUPS
    gl = jnp.where(gmask, logits, NEG)
    gmax = jnp.max(gl, axis=-1, keepdims=True)
    gidx = jnp.min(jnp.where(gmask & (gl == gmax), lanef, 999.0), axis=-1, keepdims=True)
    g_w = 1.0 / jnp.sum(jnp.where(gmask, jnp.exp(gl - gmax), 0.0), axis=-1, keepdims=True)
    lo = N_GROUPS + EXPERTS_PER_GROUP * gidx
    emask = (lanef >= lo) & (lanef < lo + EXPERTS_PER_GROUP)
    el = jnp.where(emask, logits, NEG)
    emax = jnp.max(el, axis=-1, keepdims=True)
    ep = jnp.where(emask, jnp.exp(el - emax), 0.0)
    prob = ep / jnp.sum(ep, axis=-1, keepdims=True)
    prob = jnp.where(emask, prob, -1.0)
    p1 = jnp.max(prob, axis=-1, keepdims=True)
    i1 = jnp.min(jnp.where(prob == p1, lanef, 999.0), axis=-1, keepdims=True)
    prob2 = jnp.where(lanef == i1, -1.0, prob)
    p2 = jnp.max(prob2, axis=-1, keepdims=True)
    i2 = jnp.min(jnp.where(prob2 == p2, lanef, 999.0), axis=-1, keepdims=True)
    tot = p1 + p2
    route = jnp.where(lane == 0, i1 - N_GROUPS,
                      jnp.where(lane == 1, i2 - N_GROUPS,
                                jnp.where(lane == 2, g_w * p1 / tot,
                                          jnp.where(lane == 3, g_w * p2 / tot, 0.0))))
    route_ref[...] = route.T[0:8, :]


def _merge(x, ya_p, ya_s, yb_p, yb_s, zb, w_a, w_b, w_out, ln_g, ln_b, w_rg, w_re, b_rg, b_re, layer):
    m = x.shape[0]
    pt = ya_p.shape[0] // TM
    rowblk = lambda w, j: pl.BlockSpec((TM, w), lambda i: (i, j))
    pblk = pl.BlockSpec((TM, D_MODEL), lambda i: (jnp.minimum(i, pt - 1), 0))
    sblk = pl.BlockSpec((TM, D_MODEL), lambda i: (jnp.maximum(i - pt, 0), 0))
    return pl.pallas_call(
        functools.partial(_merge_kernel, prompt_tiles=pt),
        grid=(m // TM,),
        in_specs=[rowblk(D_MODEL, 0), pblk, sblk, pblk, sblk,
                  rowblk(D_MODEL, ZGA // D_MODEL), rowblk(D_MODEL, ZGB // D_MODEL),
                  _layer_spec((D_MODEL, D_MODEL), layer), _layer_spec((D_MODEL, D_MODEL), layer),
                  _layer_spec((D_MODEL, D_MODEL), layer),
                  _layer_spec((3, D_MODEL), layer), _layer_spec((3, D_MODEL), layer),
                  _layer_spec((D_MODEL, N_GROUPS), layer), _layer_spec((D_MODEL, N_EXPERTS), layer),
                  _layer_spec((1, N_GROUPS), layer), _layer_spec((1, N_EXPERTS), layer)],
        out_specs=[rowblk(D_MODEL, 0), pl.BlockSpec((8, TM), lambda i: (0, i))],
        out_shape=[jax.ShapeDtypeStruct((m, D_MODEL), F32), jax.ShapeDtypeStruct((8, m), F32)],
        scratch_shapes=[pltpu.VMEM((3, D_MODEL, D_MODEL), BF16), pltpu.VMEM((D_MODEL, LANES), BF16),
                        pltpu.VMEM((1, LANES), F32)],
        compiler_params=_cparams(("arbitrary",)),
        name="merge",
    )(x, ya_p, ya_s, yb_p, yb_s, zb, zb, w_a, w_b, w_out, ln_g, ln_b, w_rg, w_re, b_rg, b_re)


SEG_ALIGN = 8
LOC_ROWS = 2 * TD + N_EXPERTS * SEG_ALIGN
LOC_BITS = (LOC_ROWS // SEG_ALIGN).bit_length()


def _route_tables(route, valid, nblk, min_used):
    m = route.shape[1]
    nt = m // TD
    per = BLK // SEG_ALIGN
    experts = jnp.arange(N_EXPERTS, dtype=I32)
    eid = jnp.where(valid[None, :], route[0:2].astype(I32), -1).reshape(2, nt, TD)
    e3 = jnp.concatenate([eid[0], eid[1]], axis=1)
    eflat = e3.reshape(2 * m)
    hit = eflat[:, None] == experts
    rank_all = jnp.sum(jnp.where(hit, jnp.cumsum(hit.astype(I32), axis=0), 0), axis=1) - 1
    own = e3[:, :, None] == experts
    raw = jnp.sum(own.astype(I32), axis=1)
    cnt = (raw + SEG_ALIGN - 1) // SEG_ALIGN
    lofs = jnp.cumsum(cnt, axis=1) - cnt
    ltot = jnp.sum(cnt, axis=1)
    before = jnp.cumsum(raw, axis=0) - raw
    pick = lambda tbl: jnp.sum(jnp.where(own, tbl[:, None, :], 0), axis=2)
    lslot = pick(lofs) * SEG_ALIGN + rank_all.reshape(nt, 2 * TD) - pick(before)
    lslot = jnp.where(e3 >= 0, lslot, -1).astype(F32)
    gates = route[2:4].reshape(2, nt, TD)
    table = jnp.stack([lslot[:, :TD], lslot[:, TD:], gates[0], gates[1]] + [jnp.zeros((nt, TD), F32)] * 4, axis=1)
    gcnt = jnp.sum(cnt, axis=0)
    gpad = (gcnt + per - 1) // per * per
    gend = jnp.cumsum(gpad)
    gofs = (gend - gpad)[None, :] + jnp.cumsum(cnt, axis=0) - cnt
    nused = gend[-1] // per
    unused = nused + jnp.arange(nblk - min_used, dtype=I32)
    zstart = jnp.concatenate([(gend - gpad + gcnt) * SEG_ALIGN, jnp.minimum(unused, nblk - 1) * BLK]).astype(I32)
    zlen = jnp.concatenate([(gpad - gcnt) * SEG_ALIGN, jnp.where(unused < nblk, BLK, 0)]).astype(I32)
    zstart = jnp.concatenate([zstart, zlen])
    bstart = jnp.minimum(jnp.arange(nblk, dtype=I32), nused - 1) * per
    blk_e = jnp.minimum(jnp.sum((gend[None, :] <= bstart[:, None]).astype(I32), axis=1), N_EXPERTS - 1)
    flat = lambda t: t.reshape(-1).astype(I32)
    return (flat(cnt), flat(lofs), flat(gofs), flat(ltot), table, zstart, flat(blk_e),
            nused.astype(I32).reshape(1))


def _segment_copies(i, cnt_ref, lofs_ref, gofs_ref, make_copy):
    for e in range(N_EXPERTS):
        c = cnt_ref[i * N_EXPERTS + e]
        lo = lofs_ref[i * N_EXPERTS + e]
        go = gofs_ref[i * N_EXPERTS + e]

        @pl.when(c > 0)
        def _():
            make_copy(pl.multiple_of(lo * SEG_ALIGN, SEG_ALIGN), pl.multiple_of(go * SEG_ALIGN, SEG_ALIGN),
                      pl.multiple_of(c * SEG_ALIGN, SEG_ALIGN)).start()


def _wait_rows(tot, make_copy):
    for k in range(LOC_BITS):
        @pl.when(((tot >> k) & 1) == 1)
        def _():
            make_copy(0, 0, SEG_ALIGN << k).wait()


def _dispatch_kernel(cnt_ref, lofs_ref, gofs_ref, ltot_ref, zstart_ref, x_ref, ls_ref, xs_hbm, xloc, zbuf, sem):
    i = pl.program_id(0)
    last = pl.num_programs(0) - 1
    slot = i % 2

    def copy_from(s):
        return lambda lo, go, n: pltpu.make_async_copy(xloc.at[s, pl.ds(lo, n)], xs_hbm.at[pl.ds(go, n)], sem.at[s])

    @pl.when(i == 0)
    def _():
        zbuf[...] = jnp.zeros_like(zbuf)
        nz = zstart_ref.shape[0] // 2
        for phase in ("start", "wait"):
            for e in range(nz):
                n = pl.multiple_of(zstart_ref[nz + e], SEG_ALIGN)

                @pl.when(n > 0)
                def _():
                    start = pl.multiple_of(zstart_ref[e], SEG_ALIGN)
                    cp = pltpu.make_async_copy(zbuf.at[pl.ds(0, n)], xs_hbm.at[pl.ds(start, n)], sem.at[2])
                    getattr(cp, phase)()

    @pl.when(i >= 2)
    def _():
        _wait_rows(ltot_ref[jnp.maximum(i - 2, 0)], copy_from(slot))

    ls = ls_ref[0, 0:2, :].astype(I32)
    j = lax.broadcasted_iota(I32, (LOC_ROWS, TD), 0)
    onehot = jnp.where((j == ls[0:1, :]) | (j == ls[1:2, :]), 1.0, 0.0).astype(BF16)
    xloc[slot] = _dot(onehot, x_ref[...].astype(BF16))
    _segment_copies(i, cnt_ref, lofs_ref, gofs_ref, copy_from(slot))

    @pl.when(i == last)
    def _():
        _wait_rows(ltot_ref[i], copy_from(slot))

        @pl.when(i >= 1)
        def _():
            _wait_rows(ltot_ref[jnp.maximum(i - 1, 0)], copy_from(1 - slot))


def _dispatch(x1, cnt, lofs, gofs, ltot, zstart, table, nblk):
    m = x1.shape[0]
    nt = m // TD
    grid_spec = pltpu.PrefetchScalarGridSpec(
        num_scalar_prefetch=5,
        grid=(nt,),
        in_specs=[pl.BlockSpec((TD, D_MODEL), lambda i, *_: (i, 0)),
                  pl.BlockSpec((1, 8, TD), lambda i, *_: (i, 0, 0))],
        out_specs=pl.BlockSpec(memory_space=pl.ANY),
        scratch_shapes=[pltpu.VMEM((2, LOC_ROWS, D_MODEL), F32), pltpu.VMEM((BLK, D_MODEL), F32),
                        pltpu.SemaphoreType.DMA((3,))],
    )
    return pl.pallas_call(
        _dispatch_kernel,
        grid_spec=grid_spec,
        out_shape=jax.ShapeDtypeStruct((nblk * BLK, D_MODEL), F32),
        compiler_params=_cparams(("arbitrary",)),
        name="dispatch",
    )(cnt, lofs, gofs, ltot, zstart, x1, table)


def _experts_kernel(blk_e_ref, nused_ref, xs_ref, wg_ref, wu_ref, wd_ref, ys_ref, wgb, wub, wdb):
    b = pl.program_id(0)
    used = b < nused_ref[0]
    new_expert = (b == 0) | (blk_e_ref[b] != blk_e_ref[jnp.maximum(b - 1, 0)])

    @pl.when(used & new_expert)
    def _():
        _cast_into(wgb, wg_ref)
        _cast_into(wub, wu_ref)
        _cast_into(wdb, wd_ref)

    @pl.when(used)
    def _():
        xb = xs_ref[...].astype(BF16)
        g = _dot(xb, wgb[...])
        u = _dot(xb, wub[...])
        ys_ref[...] = _dot((g * jax.nn.sigmoid(g) * u).astype(BF16), wdb[...])

    @pl.when(jnp.logical_not(used))
    def _():
        ys_ref[...] = jnp.zeros_like(ys_ref)


def _experts(xs, blk_e, nused, w_eg, w_eu, w_ed, layer):
    nb = xs.shape[0] // BLK
    wspec = lambda s: pl.BlockSpec((None, None) + s, lambda b, be, nu: (layer, be[b], 0, 0))
    grid_spec = pltpu.PrefetchScalarGridSpec(
        num_scalar_prefetch=2,
        grid=(nb,),
        in_specs=[pl.BlockSpec((BLK, D_MODEL), lambda b, be, nu: (jnp.minimum(b, nu[0] - 1), 0)),
                  wspec((D_MODEL, D_EXPERT)), wspec((D_MODEL, D_EXPERT)), wspec((D_EXPERT, D_MODEL))],
        out_specs=pl.BlockSpec((BLK, D_MODEL), lambda b, be, nu: (b, 0)),
        scratch_shapes=[pltpu.VMEM((D_MODEL, D_EXPERT), BF16), pltpu.VMEM((D_MODEL, D_EXPERT), BF16),
                        pltpu.VMEM((D_EXPERT, D_MODEL), BF16)],
    )
    return pl.pallas_call(
        _experts_kernel,
        grid_spec=grid_spec,
        out_shape=jax.ShapeDtypeStruct(xs.shape, F32),
        compiler_params=_cparams(("arbitrary",)),
        name="experts",
    )(blk_e, nused, xs, w_eg, w_eu, w_ed)


def _post_kernel(cnt_ref, lofs_ref, gofs_ref, ltot_ref, x1_ref, tab_ref, pp_ref, ps_ref, wpg_ref, wpp_ref,
                 lng_ref, lnb_ref, ys_hbm, o_ref, yloc, wpg_b, wpp_b, sem, *, prompt_tiles):
    i = pl.program_id(0)
    last = pl.num_programs(0) - 1
    slot = i % 2

    def copy_to(s):
        return lambda lo, go, n: pltpu.make_async_copy(ys_hbm.at[pl.ds(go, n)], yloc.at[s, pl.ds(lo, n)], sem.at[s])

    @pl.when(i == 0)
    def _():
        _cast_into(wpg_b, wpg_ref)
        _cast_into(wpp_b, wpp_ref)
        yloc[...] = jnp.zeros_like(yloc)
        _segment_copies(0, cnt_ref, lofs_ref, gofs_ref, copy_to(0))

    _wait_rows(ltot_ref[i], copy_to(slot))

    @pl.when(i < last)
    def _():
        _segment_copies(i + 1, cnt_ref, lofs_ref, gofs_ref, copy_to(1 - slot))

    tab = jnp.concatenate([tab_ref[0], jnp.zeros((LANES - 8, tab_ref.shape[2]), F32)], axis=0)
    tcol = jnp.concatenate([tab[:, c:c + LANES].T for c in range(0, tab.shape[1], LANES)], axis=0)
    j = lax.broadcasted_iota(I32, (tcol.shape[0], LOC_ROWS), 1).astype(F32)
    w = jnp.where(j == tcol[:, 0:1], tcol[:, 2:3], 0.0) + jnp.where(j == tcol[:, 1:2], tcol[:, 3:4], 0.0)
    moe = _dot(w.astype(BF16), yloc[slot].astype(BF16))
    x2 = _layer_norm(DN_ALPHA * x1_ref[...] + moe, lng_ref[1:2, :], lnb_ref[1:2, :])
    p = jnp.where(i < prompt_tiles, pp_ref[...], ps_ref[...])
    ple = jax.nn.sigmoid(_dot(x2.astype(BF16), wpg_b[...])) * _dot(p.astype(BF16), wpp_b[...])
    o_ref[...] = _layer_norm(DN_ALPHA * x2 + ple, lng_ref[2:3, :], lnb_ref[2:3, :])


def _post(x1, ys, cnt, lofs, gofs, ltot, table, p_prompt, p_sample, w_pg, w_pp, ln_g, ln_b, layer):
    m = x1.shape[0]
    pt = p_prompt.shape[1] // TD
    rowblk = lambda w: pl.BlockSpec((TD, w), lambda i, *_: (i, 0))
    lspec = lambda s: pl.BlockSpec((None,) + s, lambda i, *_: (layer,) + (0,) * len(s), pipeline_mode=pl.Buffered(1))
    grid_spec = pltpu.PrefetchScalarGridSpec(
        num_scalar_prefetch=4,
        grid=(m // TD,),
        in_specs=[rowblk(D_MODEL), pl.BlockSpec((1, 8, TD), lambda i, *_: (i, 0, 0)),
                  pl.BlockSpec((None, TD, D_PLE), lambda i, *_: (layer, jnp.minimum(i, pt - 1), 0)),
                  pl.BlockSpec((None, TD, D_PLE), lambda i, *_: (layer, jnp.maximum(i - pt, 0), 0)),
                  lspec((D_MODEL, D_MODEL)), lspec((D_PLE, D_MODEL)), lspec((3, D_MODEL)), lspec((3, D_MODEL)),
                  pl.BlockSpec(memory_space=pl.ANY)],
        out_specs=rowblk(D_MODEL),
        scratch_shapes=[pltpu.VMEM((2, LOC_ROWS, D_MODEL), F32), pltpu.VMEM((D_MODEL, D_MODEL), BF16),
                        pltpu.VMEM((D_PLE, D_MODEL), BF16), pltpu.SemaphoreType.DMA((2,))],
    )
    return pl.pallas_call(
        functools.partial(_post_kernel, prompt_tiles=pt),
        grid_spec=grid_spec,
        out_shape=jax.ShapeDtypeStruct((m, D_MODEL), F32),
        compiler_params=_cparams(("arbitrary",)),
        name="post",
    )(cnt, lofs, gofs, ltot, x1, table, p_prompt, p_sample, w_pg, w_pp, ln_g, ln_b, ys)


def kernel(x_prompt, x_sample, state_mlstm_C, state_mlstm_n, state_mlstm_m, state_swa_k, state_swa_v, p_prompt, p_sample, w_in, b_in, mh_gain, w_a, w_b, w_out, rel_table, w_sink, ln_g, ln_b, w_rg, b_rg, w_re, b_re, w_eg, w_eu, w_ed, w_pg, w_pp):
    bp_, seq = x_prompt.shape[:2]
    bs_, tdec = x_sample.shape[:2]
    mp = bp_ * seq
    msp = bs_ * T_PAD
    mall = mp + msp
    nblk = -(-(2 * (mp + bs_ * tdec) + (mall // TD) * N_EXPERTS * (SEG_ALIGN - 1)) // BLK) + N_EXPERTS

    pad_t = lambda t: jnp.pad(t, [(0, 0)] * (t.ndim - 2) + [(0, T_PAD - tdec), (0, 0)])
    x = lax.dynamic_update_slice(jnp.pad(x_prompt.reshape(mp, D_MODEL), ((0, msp), (0, 0))),
                                 pad_t(x_sample).reshape(msp, D_MODEL), (mp, 0))
    valid = jnp.concatenate([jnp.ones((mp,), bool), jnp.tile(jnp.arange(T_PAD) < tdec, bs_)])
    p_prompt = p_prompt.reshape(DEPTH, mp, D_PLE)
    p_sample = pad_t(p_sample).reshape(DEPTH, msp, D_PLE)

    b_in3 = b_in.reshape(DEPTH, 1, -1)
    b_b = jnp.concatenate([b_in3[:, :, IN_GATES[0]:IN_GATES[1]], b_in3[:, :, IN_QKV2[0]:IN_QKV2[1]]], axis=-1)
    b_rg, b_re = b_rg.reshape(DEPTH, 1, N_GROUPS), b_re.reshape(DEPTH, 1, N_EXPERTS)
    gain = mh_gain.reshape(DEPTH, 1, D_MODEL)

    bias_p = _swa_bias(rel_table, WINDOW)
    bias_s = _swa_bias(rel_table, T_PAD)
    sink_p = sink_s = _sink_lanes(w_sink)
    buf_k = state_swa_k.reshape(DEPTH, bs_, WINDOW, D_KV)
    buf_v = state_swa_v.reshape(DEPTH, bs_, WINDOW, D_KV)
    m0_s = jnp.broadcast_to(state_mlstm_m[..., None], (DEPTH, bs_, M_HEADS, LANES))
    zc = jnp.zeros((1, bp_, M_HEADS, M_DQK, M_DV), F32)
    zn = jnp.zeros((1, bp_, M_HEADS, LANES), F32)
    st_s = kv_s = None

    st_p = []
    for i in range(DEPTH):
        za, zg = _inproj(x, [(w_in, IN_A)], [(b_in3, IN_A)], [(ZG, BF16), (LANES, F32)], i, "inproj_a")
        zb = _inproj_b(x, w_in, b_b, i)

        ya_p, c_p, n_p, m_p = _mlstm(za, zg, gain, i, zc, zn, zn, 0, None, 1, 0, row0=0, nseq=bp_,
                                     steps=seq // LANES, seg=LANES, t_valid=LANES)
        ya_s, *st_s = _mlstm(za, zg, gain, i, state_mlstm_C, state_mlstm_n, m0_s, i, st_s, DEPTH, i, row0=mp,
                             nseq=bs_ // SEQ_BLK, steps=1, seg=T_PAD, t_valid=tdec)
        yb_p = _swa_prompt(zb, bias_p[0], bias_p[1], sink_p, i, batch=bp_, seq=seq)
        yb_s, *kv_s = _swa_sample(zb, buf_k, buf_v, bias_s[0], bias_s[1], sink_s, i, kv_s, row0=mp, batch=bs_,
                                  t_new=tdec)

        x1, route = _merge(x, ya_p, ya_s, yb_p, yb_s, zb, w_a, w_b, w_out, ln_g, ln_b, w_rg, w_re, b_rg, b_re, i)
        cnt, lofs, gofs, ltot, table, zstart, blk_e, nused = _route_tables(route, valid, nblk, 2 * (mp + bs_ * tdec) // BLK)
        xs = _dispatch(x1, cnt, lofs, gofs, ltot, zstart, table, nblk)
        ys = _experts(xs, blk_e, nused, w_eg, w_eu, w_ed, i)
        x = _post(x1, ys, cnt, lofs, gofs, ltot, table, p_prompt, p_sample, w_pg, w_pp, ln_g, ln_b, i)

        kv_p = jnp.stack([lax.slice(zb, ((b + 1) * seq - WINDOW, ZKS), ((b + 1) * seq, IN_B)) for b in range(bp_)])
        kv_p = kv_p.astype(F32)
        heads = lambda t: t.reshape(t.shape[:-1] + (S_KV_HEADS, S_HEAD_DIM))
        st_p.append((c_p[0], n_p[0], m_p[0, :, :, 0], heads(kv_p[..., :D_KV]), heads(kv_p[..., D_KV:])))

    stk = lambda j: jnp.stack([s[j] for s in st_p], axis=0)
    y_p = x[:mp].reshape(bp_, seq, D_MODEL)
    y_s = x[mp:].reshape(bs_, T_PAD, D_MODEL)[:, :tdec]
    return (y_p, y_s, stk(0), stk(1), stk(2), stk(3), stk(4),
            st_s[0], st_s[1], st_s[2][..., 0], heads(kv_s[0]), heads(kv_s[1]))
```

```python
import functools

import numpy as np
import jax
import jax.numpy as jnp
from jax import lax
from jax.experimental import pallas as pl
from jax.experimental.pallas import tpu as pltpu

F32 = jnp.float32
BF16 = jnp.bfloat16
I32 = jnp.int32
HIGHEST = lax.Precision.HIGHEST

D_MODEL = 1024
DEPTH = 4
D_PLE = 256
M_HEADS = 4
M_DQK = 128
M_DV = 256
S_HEADS = 16
S_KV_HEADS = 4
S_HEAD_DIM = 64
S_GROUP = 4
D_KV = S_KV_HEADS * S_HEAD_DIM
WINDOW = 128
REL_BUCKETS = 32
N_GROUPS = 4
EXPERTS_PER_GROUP = 8
N_EXPERTS = 32
D_EXPERT = 512
DN_ALPHA = (2 * DEPTH) ** 0.25
LN_EPS = 1e-5
NEG = -1e30

IN_A = 3200
IN_QKV2 = (3080, 4616)
IN_GATES = (4616, 6664)
ZQ, ZK, ZV, ZOG, ZG = 0, 512, 1024, 2048, 3072
IN_B = 3584
ZGA, ZGB, ZQS, ZKS, ZVS = 0, 1024, 2048, 3072, 3328

LANES = 128
T_PAD = 8
SEQ_BLK = LANES // T_PAD
TM = 512
TD = 256
BLK = 512
VMEM_LIMIT = 56 * 1024 * 1024


def _cparams(sem):
    return pltpu.CompilerParams(dimension_semantics=sem, vmem_limit_bytes=VMEM_LIMIT)


def _layer_spec(shape, layer):
    nd = len(shape)
    return pl.BlockSpec((None,) + shape, lambda *_: (layer,) + (0,) * nd, pipeline_mode=pl.Buffered(1))


def _layer_norm(y, g, b):
    mu = jnp.mean(y, axis=-1, keepdims=True)
    d = y - mu
    var = jnp.mean(d * d, axis=-1, keepdims=True)
    return d * lax.rsqrt(var + LN_EPS) * g + b


def _dot(a, b):
    return jnp.dot(a, b, preferred_element_type=F32)


def _dot_nt(a, b):
    return lax.dot_general(a, b, (((1,), (1,)), ((), ())), preferred_element_type=F32)


def _dot_tn(a, b):
    return lax.dot_general(a, b, (((0,), (0,)), ((), ())), preferred_element_type=F32)


def _dot_exact(a, b):
    return jnp.dot(a, b, precision=HIGHEST, preferred_element_type=F32)


def _cast_into(dst_ref, src_ref, col0=0, step=512):
    n = src_ref.shape[-1]
    for j in range(0, n, step):
        w = min(step, n - j)
        dst_ref[:, col0 + j:col0 + j + w] = src_ref[:, j:j + w].astype(BF16)


def _inproj_kernel(*refs, widths, out_widths):
    ng, no = len(widths), len(out_widths)
    x_ref, w_refs, b_refs = refs[0], refs[1:1 + ng], refs[1 + ng:1 + 2 * ng]
    o_refs, wbf = refs[1 + 2 * ng:1 + 2 * ng + no], refs[1 + 2 * ng + no]

    @pl.when(pl.program_id(0) == 0)
    def _():
        c = 0
        for w_ref, n in zip(w_refs, widths):
            _cast_into(wbf, w_ref, c)
            c += n

    out_start = np.cumsum((0,) + tuple(out_widths))

    def store(col, val):
        k = int(np.searchsorted(out_start, col, side='right')) - 1
        lo = col - int(out_start[k])
        o_refs[k][:, lo:lo + val.shape[1]] = val.astype(o_refs[k].dtype)

    xb = x_ref[...].astype(BF16)
    c = 0
    for b_ref, n in zip(b_refs, widths):
        for j in range(0, n, 512):
            w = min(512, n - j)
            store(c + j, _dot(xb, wbf[:, c + j:c + j + w]) + b_ref[:, j:j + w])
        c += n


def _inproj(x, ws, bs, outs, layer, name):
    m = x.shape[0]
    widths = tuple(n for _, n in ws)
    out_widths = tuple(n for n, _ in outs)
    assert sum(widths) == sum(out_widths)
    return pl.pallas_call(
        functools.partial(_inproj_kernel, widths=widths, out_widths=out_widths),
        grid=(m // TM,),
        in_specs=([pl.BlockSpec((TM, D_MODEL), lambda i: (i, 0))]
                  + [_layer_spec((D_MODEL, n), layer) for _, n in ws]
                  + [_layer_spec((1, n), layer) for _, n in bs]),
        out_specs=[pl.BlockSpec((TM, n), lambda i: (i, 0)) for n, _ in outs],
        out_shape=[jax.ShapeDtypeStruct((m, n), dt) for n, dt in outs],
        scratch_shapes=[pltpu.VMEM((D_MODEL, sum(widths)), BF16)],
        compiler_params=_cparams(("arbitrary",)),
        name=name,
    )(x, *[w for w, _ in ws], *[b for b, _ in bs])


def _inproj_b_kernel(x_ref, b_ref, w_hbm, z_ref, wwin, wbf, sem, *, layer):
    lead = IN_QKV2[0] % LANES
    n_qkv = IN_QKV2[1] - IN_QKV2[0]

    @pl.when(pl.program_id(0) == 0)
    def _():
        cp = pltpu.make_async_copy(w_hbm.at[layer, :, pl.ds(IN_QKV2[0] - lead, wwin.shape[1])], wwin, sem.at[0])
        cp.start()
        cp.wait()
        for dst, src, n in ((0, n_qkv, IN_B - n_qkv), (IN_B - n_qkv, 0, n_qkv)):
            for j in range(0, n, 512):
                w = min(512, n - j)
                wbf[:, dst + j:dst + j + w] = wwin[:, lead + src + j:lead + src + j + w].astype(BF16)

    xb = x_ref[...].astype(BF16)
    for j in range(0, IN_B, 512):
        w = min(512, IN_B - j)
        z_ref[:, j:j + w] = (_dot(xb, wbf[:, j:j + w]) + b_ref[:, j:j + w]).astype(z_ref.dtype)


def _inproj_b(x, w_in, b_b, layer):
    m = x.shape[0]
    win = w_in.shape[2] - (IN_QKV2[0] - IN_QKV2[0] % LANES)
    return pl.pallas_call(
        functools.partial(_inproj_b_kernel, layer=layer),
        grid=(m // TM,),
        in_specs=[pl.BlockSpec((TM, D_MODEL), lambda i: (i, 0)), _layer_spec((1, IN_B), layer),
                  pl.BlockSpec(memory_space=pl.ANY)],
        out_specs=pl.BlockSpec((TM, IN_B), lambda i: (i, 0)),
        out_shape=jax.ShapeDtypeStruct((m, IN_B), BF16),
        scratch_shapes=[pltpu.VMEM((D_MODEL, win), F32), pltpu.VMEM((D_MODEL, IN_B), BF16),
                        pltpu.SemaphoreType.DMA((1,))],
        compiler_params=_cparams(("arbitrary",)),
        name="inproj_b",
    )(x, b_b, w_in)


def _log_sigmoid(x):
    return jnp.minimum(x, 0.0) - jnp.log(1.0 + jnp.exp(-jnp.abs(x)))


def _mlstm_kernel(zq, zk, zv, zog, zg, gain, c0, n0, m0, *rest, seg, t_valid, aliased):
    ya, co, no, mo, qc_s = rest[aliased:]

    @pl.when(pl.program_id(1) == 0)
    def _():
        co[...] = c0[...]
        no[...] = n0[...]
        mo[...] = m0[...]

    for r0 in range(0, zq.shape[0], LANES):
        rows = pl.ds(r0, LANES)
        _mlstm_block(zq.at[rows], zk.at[rows], zv.at[rows], zog.at[rows], zg.at[rows], gain, ya.at[rows],
                     co, no, mo, qc_s, seg=seg, t_valid=t_valid)


def _mlstm_block(zq, zk, zv, zog, zg, gain, ya, co, no, mo, qc_s, *, seg, t_valid):
    L = zq.shape[0]
    nseg = L // seg
    g_c = zg[...]
    g_r = g_c.T
    ig_c, lf_c = g_c, _log_sigmoid(g_c)
    ig_r, lf_r = g_r, _log_sigmoid(g_r)
    ri = lax.broadcasted_iota(I32, (L, L), 0)
    ci = lax.broadcasted_iota(I32, (L, L), 1)
    shift = seg.bit_length() - 1
    if t_valid < seg:
        rv = (lax.broadcasted_iota(I32, (L, 1), 0) & (seg - 1)) < t_valid
        cv = (lax.broadcasted_iota(I32, (1, L), 1) & (seg - 1)) < t_valid
        ig_c, lf_c = jnp.where(rv, ig_c, NEG), jnp.where(rv, lf_c, 0.0)
        ig_r, lf_r = jnp.where(cv, ig_r, NEG), jnp.where(cv, lf_r, 0.0)
    if nseg > 1:
        same = (ri >> shift) == (ci >> shift)
        causal, causal_t = same & (ri >= ci), same & (ri <= ci)
        same_f = same.astype(F32)
        be_c = _dot_exact(same_f, lf_c)
        be_r = _dot_exact(lf_r, same_f)
        expand = ((lax.broadcasted_iota(I32, (L, nseg), 0) >> shift)
                  == lax.broadcasted_iota(I32, (L, nseg), 1)).astype(F32)
    else:
        causal, causal_t = ri >= ci, ri <= ci
    b_c = _dot_exact(causal.astype(F32), lf_c)
    b_r = _dot_exact(lf_r, causal_t.astype(F32))

    for h in range(M_HEADS):
        icol, irow = ig_c[:, h:h + 1], ig_r[h:h + 1, :]
        bcol, brow = b_c[:, 4 + h:5 + h], b_r[4 + h:5 + h, :]
        q = zq[:, h * M_DQK:(h + 1) * M_DQK].astype(F32) * (M_DQK ** -0.5)
        k = zk[:, h * M_DQK:(h + 1) * M_DQK].astype(F32)
        qb, kb = q.astype(BF16), zk[:, h * M_DQK:(h + 1) * M_DQK]
        vb = zv[:, h * M_DV:(h + 1) * M_DV]
        if nseg > 1:
            mcol = _dot_exact(expand, mo[:, h, :])[:, 0:1]
            nrow = _dot_exact(expand, no[:, h, :])
            becol, berow = be_c[:, 4 + h:5 + h], be_r[4 + h:5 + h, :]
            for j in range(nseg):
                rows = slice(j * seg, (j + 1) * seg)
                qc_s[rows, :] = _dot(q[rows].astype(BF16), co[j, h].astype(BF16))
            q_c = qc_s[...]
        else:
            mcol, nrow = mo[0, h:h + 1, 0:1], no[0, h:h + 1, :]
            becol = bcol[L - 1:L, :]
            q_c = _dot(qb, co[0, h].astype(BF16))

        dmat = jnp.where(causal, bcol - brow + irow, NEG)
        inter = bcol + mcol
        mhat = jnp.maximum(inter, jnp.max(dmat, axis=-1, keepdims=True))
        w_intra = jnp.exp(dmat - mhat)
        w_inter = jnp.exp(inter - mhat)
        s = _dot_nt(qb, kb) * w_intra
        num = _dot(s.astype(BF16), vb) + w_inter * q_c
        den = jnp.sum(s, axis=-1, keepdims=True) + w_inter * jnp.sum(q * nrow, axis=-1, keepdims=True)
        hh = num / jnp.maximum(jnp.abs(den), jnp.exp(-mhat))

        gcol = icol + becol - bcol
        if nseg > 1:
            gmax = jnp.max(jnp.where(same, irow + berow - brow, NEG), axis=-1, keepdims=True)
        else:
            gmax = jnp.max(gcol, axis=0, keepdims=True)
        m_new = jnp.maximum(becol + mcol, gmax)
        decay = jnp.exp(becol + mcol - m_new)
        kw = k * jnp.exp(gcol - m_new)
        vf = vb.astype(F32) if nseg > 1 else vb
        for j in range(nseg):
            rows = slice(j * seg, (j + 1) * seg)
            dj = decay[j * seg:j * seg + 1, :] if nseg > 1 else decay
            mj = m_new[j * seg:j * seg + 1, :] if nseg > 1 else m_new
            co[j, h] = dj * co[j, h] + _dot_tn(kw[rows].astype(BF16), vf[rows].astype(BF16))
            no[j, h:h + 1, :] = dj * no[j, h:h + 1, :] + jnp.sum(kw[rows], axis=0, keepdims=True)
            mo[j, h:h + 1, :] = jnp.broadcast_to(mj, (1, LANES))

        mu = jnp.mean(hh, axis=-1, keepdims=True)
        d = hh - mu
        var = jnp.mean(d * d, axis=-1, keepdims=True)
        hn = d * lax.rsqrt(var + LN_EPS) * gain[:, h * M_DV:(h + 1) * M_DV]
        og = zog[:, h * M_DV:(h + 1) * M_DV].astype(F32)
        ya[:, h * M_DV:(h + 1) * M_DV] = (hn * jax.nn.sigmoid(og)).astype(ya.dtype)


def _mlstm(za, zg, gain, layer, c0, n0, m0, st_layer, outs_prev, out_layers, out_layer, *, row0, nseq, steps,
           seg, t_valid, blocks_per_step=1):
    nseg = LANES // seg
    rows = LANES * blocks_per_step
    r0 = row0 // rows
    row = lambda b, c: r0 + b * steps + c
    nstate = nseq * nseg
    aliased = 0 if outs_prev is None else 3
    kern = functools.partial(_mlstm_kernel, seg=seg, t_valid=min(t_valid, seg), aliased=aliased)
    in4 = pl.BlockSpec((None, nseg, M_HEADS, M_DQK, M_DV), lambda b, c: (st_layer, b, 0, 0, 0))
    in3 = pl.BlockSpec((None, nseg, M_HEADS, LANES), lambda b, c: (st_layer, b, 0, 0))
    out4 = pl.BlockSpec((None, nseg, M_HEADS, M_DQK, M_DV), lambda b, c: (out_layer, b, 0, 0, 0))
    out3 = pl.BlockSpec((None, nseg, M_HEADS, LANES), lambda b, c: (out_layer, b, 0, 0))
    anyspec = pl.BlockSpec(memory_space=pl.ANY)
    n_in = 9
    return pl.pallas_call(
        kern,
        grid=(nseq, steps),
        in_specs=[pl.BlockSpec((rows, 512), lambda b, c: (row(b, c), ZQ // 512)),
                  pl.BlockSpec((rows, 512), lambda b, c: (row(b, c), ZK // 512)),
                  pl.BlockSpec((rows, 1024), lambda b, c: (row(b, c), ZV // 1024)),
                  pl.BlockSpec((rows, 1024), lambda b, c: (row(b, c), ZOG // 1024)),
                  pl.BlockSpec((rows, LANES), lambda b, c: (row(b, c), 0)),
                  pl.BlockSpec((None, 1, D_MODEL), lambda b, c: (layer, 0, 0)),
                  in4, in3, in3] + [anyspec] * aliased,
        out_specs=[pl.BlockSpec((rows, D_MODEL), lambda b, c: (b * steps + c, 0)), out4, out3, out3],
        out_shape=[jax.ShapeDtypeStruct((nseq * steps * rows, D_MODEL), BF16),
                   jax.ShapeDtypeStruct((out_layers, nstate, M_HEADS, M_DQK, M_DV), F32),
                   jax.ShapeDtypeStruct((out_layers, nstate, M_HEADS, LANES), F32),
                   jax.ShapeDtypeStruct((out_layers, nstate, M_HEADS, LANES), F32)],
        scratch_shapes=[pltpu.VMEM((LANES, M_DV), F32)],
        input_output_aliases={n_in + j: 1 + j for j in range(aliased)},
        compiler_params=_cparams(("parallel", "arbitrary")),
        name="mlstm",
    )(za, za, za, za, zg, gain, c0, n0, m0, *(outs_prev or ()))


def _swa_kernel(q_ref, kc_ref, vc_ref, kp_ref, vp_ref, bp_ref, bc_ref, sink_ref, *rest, nseq, t_new,
                first_block_has_no_past, aliased):
    o_ref = rest[aliased]
    rows_n = q_ref.shape[0]
    tq = rows_n // nseq
    have_past = pl.program_id(1) > 0
    lane = lax.broadcasted_iota(I32, (1, LANES), 1)
    in_lo = lane < S_HEAD_DIM
    if nseq > 1:
        kp_all = kp_ref[...].reshape(nseq * WINDOW, D_KV).astype(BF16)
        vp_all = vp_ref[...].reshape(nseq * WINDOW, D_KV).astype(BF16)
        row_seq = lax.broadcasted_iota(I32, (rows_n, LANES), 0) >> (tq.bit_length() - 1)
    else:
        kp_all, vp_all = kp_ref[...], vp_ref[...]
    kc_all, vc_all = kc_ref[...], vc_ref[...]

    def placed(x, src_half, ones):
        swapped = jnp.concatenate([x[:, S_HEAD_DIM:], x[:, :S_HEAD_DIM]], axis=1)
        zero = jnp.zeros_like(x)
        lo = jnp.where(in_lo, x if src_half == 0 else swapped, zero)
        hi = jnp.where(in_lo, zero, x if src_half == 1 else swapped)
        if ones:
            lo = jnp.where(lane == S_HEAD_DIM, jnp.ones_like(x), lo)
            hi = jnp.where(lane == 0, jnp.ones_like(x), hi)
        return lo, hi

    kv_per_pass = S_KV_HEADS if nseq == 1 else 1
    for kv0 in range(0, S_KV_HEADS, kv_per_pass):
        kp_v, kc_v, vp_v, vc_v = {}, {}, {}, {}
        for kvh in range(kv0, kv0 + kv_per_pass):
            tile = slice((kvh // 2) * LANES, (kvh // 2 + 1) * LANES)
            kp_v[kvh] = placed(kp_all[:, tile], kvh % 2, False)
            kc_v[kvh] = placed(kc_all[:, tile], kvh % 2, False)
            vp_v[kvh] = placed(vp_all[:, tile], kvh % 2, True)
            vc_v[kvh] = placed(vc_all[:, tile], kvh % 2, True)
        heads = list(range(kv0 * S_GROUP, (kv0 + kv_per_pass) * S_GROUP))
        scores = []
        for h in heads:
            q2 = q_ref[:, (h // 2) * LANES:(h // 2 + 1) * LANES]
            scores.append((_dot_nt(q2, kp_v[h // S_GROUP][h % 2]), _dot_nt(q2, kc_v[h // S_GROUP][h % 2])))
        probs = []
        for h, (sp, sc) in zip(heads, scores):
            if nseq > 1:
                sp = jnp.concatenate([sp[b * tq:(b + 1) * tq, b * WINDOW:(b + 1) * WINDOW] for b in range(nseq)],
                                     axis=0)
            sp = sp * (S_HEAD_DIM ** -0.5) + bp_ref[h]
            sc = sc * (S_HEAD_DIM ** -0.5) + bc_ref[h]
            if first_block_has_no_past:
                sp = jnp.where(have_past, sp, NEG)
            sk = sink_ref[h][:, 0:1]
            mx = jnp.maximum(jnp.max(jnp.maximum(sp, sc), axis=-1, keepdims=True), sk)
            pp, pc = jnp.exp(sp - mx), jnp.exp(sc - mx)
            if nseq > 1:
                pp = jnp.concatenate([jnp.where(row_seq == b, pp, 0.0) for b in range(nseq)], axis=1)
            probs.append((pp.astype(BF16), pc.astype(BF16), jnp.exp(sk - mx)))
        outs = [_dot(pp, vp_v[h // S_GROUP][h % 2]) + _dot(pc, vc_v[h // S_GROUP][h % 2])
                for h, (pp, pc, _) in zip(heads, probs)]
        for pair in range(len(heads) // 2):
            halves = []
            for half in range(2):
                o, sink_term = outs[2 * pair + half], probs[2 * pair + half][2]
                ones_lane = S_HEAD_DIM if half == 0 else 0
                halves.append(o / (o[:, ones_lane:ones_lane + 1] + sink_term))
            qt = heads[2 * pair] // 2
            o_ref[:, qt * LANES:(qt + 1) * LANES] = jnp.where(in_lo, halves[0], halves[1]).astype(o_ref.dtype)

    if t_new:
        kc_f, vc_f = kc_all.astype(F32), vc_all.astype(F32)
        for b in range(nseq):
            for new, prev, out in ((kc_f, kp_ref, rest[aliased + 1]), (vc_f, vp_ref, rest[aliased + 2])):
                out[b, 0:WINDOW - t_new, :] = prev[b, t_new:, :]
                out[b, WINDOW - t_new:WINDOW, :] = new[b * tq:b * tq + t_new, :]


def _rel_bucket(dist):
    nn = np.maximum(dist, 0)
    max_exact = REL_BUCKETS // 2
    large = max_exact + (np.log(np.maximum(nn, 1) / max_exact) / np.log(WINDOW / max_exact)
                         * (REL_BUCKETS - max_exact)).astype(np.int32)
    large = np.minimum(large, REL_BUCKETS - 1)
    return np.where(nn < max_exact, nn, large).astype(np.int32)


def _swa_bias(rel_table, tq):
    nseq = LANES // tq
    t = np.tile(np.arange(tq), nseq)[:, None]
    seq = np.repeat(np.arange(nseq), tq)
    d_prev = t + WINDOW - np.arange(WINDOW)[None, :]
    d_cur = t - t.T
    own = seq[:, None] == seq[None, :]

    def build(dist, ok):
        onehot = np.eye(REL_BUCKETS, dtype=np.float32)[_rel_bucket(dist).reshape(-1)]
        bias = jnp.dot(onehot, rel_table.astype(F32), precision=HIGHEST).reshape(dist.shape + (S_HEADS,))
        return jnp.where(ok[None], jnp.transpose(bias, (2, 0, 1)), NEG)

    return build(d_prev, d_prev <= WINDOW), build(d_cur, own & (d_cur >= 0))


def _sink_lanes(w_sink):
    return jnp.broadcast_to(w_sink.astype(F32)[:, :, None, None], (DEPTH, S_HEADS, 1, LANES))


def _swa_bias_specs(layer):
    return [pl.BlockSpec((S_HEADS, LANES, WINDOW), lambda b, n: (0, 0, 0)),
            pl.BlockSpec((S_HEADS, LANES, LANES), lambda b, n: (0, 0, 0)),
            pl.BlockSpec((None, S_HEADS, 1, LANES), lambda b, n: (layer, 0, 0, 0))]


def _swa_prompt(zb, bp, bc, sk, layer, *, batch, seq):
    nb = seq // WINDOW
    row = lambda b, n: b * nb + n
    prev = lambda b, n: jnp.maximum(b * nb + n - 1, 0)
    in_specs = [pl.BlockSpec((WINDOW, 1024), lambda b, n: (row(b, n), ZQS // 1024)),
                pl.BlockSpec((WINDOW, D_KV), lambda b, n: (row(b, n), ZKS // D_KV)),
                pl.BlockSpec((WINDOW, D_KV), lambda b, n: (row(b, n), ZVS // D_KV)),
                pl.BlockSpec((WINDOW, D_KV), lambda b, n: (prev(b, n), ZKS // D_KV)),
                pl.BlockSpec((WINDOW, D_KV), lambda b, n: (prev(b, n), ZVS // D_KV))]
    kern = functools.partial(_swa_kernel, nseq=1, t_new=0, first_block_has_no_past=True, aliased=0)
    return pl.pallas_call(
        kern,
        grid=(batch, nb),
        in_specs=in_specs + _swa_bias_specs(layer),
        out_specs=pl.BlockSpec((WINDOW, D_MODEL), lambda b, n: (row(b, n), 0)),
        out_shape=jax.ShapeDtypeStruct((batch * seq, D_MODEL), BF16),
        compiler_params=_cparams(("parallel", "arbitrary")),
        name="swa_prompt",
    )(zb, zb, zb, zb, zb, bp, bc, sk)


def _swa_sample(zb, buf_k, buf_v, bp, bc, sk, layer, kv_prev, *, row0, batch, t_new):
    r0 = row0 // LANES
    steps = batch // SEQ_BLK
    state = pl.BlockSpec((None, SEQ_BLK, WINDOW, D_KV), lambda b, n: (layer, b, 0, 0))
    anyspec = pl.BlockSpec(memory_space=pl.ANY)
    in_specs = [pl.BlockSpec((LANES, 1024), lambda b, n: (r0 + b, ZQS // 1024)),
                pl.BlockSpec((LANES, D_KV), lambda b, n: (r0 + b, ZKS // D_KV)),
                pl.BlockSpec((LANES, D_KV), lambda b, n: (r0 + b, ZVS // D_KV)),
                state, state]
    aliased = 0 if kv_prev is None else 2
    kern = functools.partial(_swa_kernel, nseq=SEQ_BLK, t_new=t_new, first_block_has_no_past=False, aliased=aliased)
    return pl.pallas_call(
        kern,
        grid=(steps, 1),
        in_specs=in_specs + _swa_bias_specs(layer) + [anyspec] * aliased,
        out_specs=[pl.BlockSpec((LANES, D_MODEL), lambda b, n: (b, 0)), state, state],
        out_shape=[jax.ShapeDtypeStruct((batch * T_PAD, D_MODEL), BF16),
                   jax.ShapeDtypeStruct(buf_k.shape, F32), jax.ShapeDtypeStruct(buf_v.shape, F32)],
        input_output_aliases={8 + j: 1 + j for j in range(aliased)},
        compiler_params=_cparams(("parallel", "arbitrary")),
        name="swa_sample",
    )(zb, zb, zb, buf_k, buf_v, bp, bc, sk, *(kv_prev or ()))


def _merge_kernel(x_ref, yap_ref, yas_ref, ybp_ref, ybs_ref, ga_ref, gb_ref, wa_ref, wb_ref, wo_ref, lng_ref, lnb_ref,
                  wrg_ref, wre_ref, brg_ref, bre_ref, x1_ref, route_ref, wbf, wr_b, br_s, *, prompt_tiles):
    i = pl.program_id(0)

    @pl.when(i == 0)
    def _():
        for j, w_ref in enumerate((wa_ref, wb_ref, wo_ref)):
            _cast_into(wbf.at[j], w_ref)
        wr_b[...] = jnp.zeros_like(wr_b)
        wr_b[:, 0:N_GROUPS] = wrg_ref[...].astype(BF16)
        wr_b[:, N_GROUPS:N_GROUPS + N_EXPERTS] = wre_ref[...].astype(BF16)
        br_s[...] = jnp.zeros_like(br_s)
        br_s[:, 0:N_GROUPS] = brg_ref[...]
        br_s[:, N_GROUPS:N_GROUPS + N_EXPERTS] = bre_ref[...]

    is_prompt = i < prompt_tiles
    half = x_ref.shape[0] // 2
    for r0 in (0, half):
        rs = slice(r0, r0 + half)
        ya = jnp.where(is_prompt, yap_ref[rs, :], yas_ref[rs, :])
        yb = jnp.where(is_prompt, ybp_ref[rs, :], ybs_ref[rs, :])
        a = _dot(ya, wbf[0])
        b = _dot(yb, wbf[1])
        u = jax.nn.sigmoid(ga_ref[rs, :].astype(F32)) * a + jax.nn.sigmoid(gb_ref[rs, :].astype(F32)) * b
        mix = _dot(u.astype(BF16), wbf[2])
        x1_ref[rs, :] = _layer_norm(DN_ALPHA * x_ref[rs, :] + mix, lng_ref[0:1, :], lnb_ref[0:1, :])
    x1 = x1_ref[...]

    logits = _dot(x1.astype(BF16), wr_b[...]) + br_s[...]
    lane = lax.broadcasted_iota(I32, logits.shape, 1)
    lanef = lane.astype(F32)
    gmask = lane < N_GROUPS
    gl = jnp.where(gmask, logits, NEG)
    gmax = jnp.max(gl, axis=-1, keepdims=True)
    gidx = jnp.min(jnp.where(gmask & (gl == gmax), lanef, 999.0), axis=-1, keepdims=True)
    g_w = 1.0 / jnp.sum(jnp.where(gmask, jnp.exp(gl - gmax), 0.0), axis=-1, keepdims=True)
    lo = N_GROUPS + EXPERTS_PER_GROUP * gidx
    emask = (lanef >= lo) & (lanef < lo + EXPERTS_PER_GROUP)
    el = jnp.where(emask, logits, NEG)
    emax = jnp.max(el, axis=-1, keepdims=True)
    ep = jnp.where(emask, jnp.exp(el - emax), 0.0)
    prob = ep / jnp.sum(ep, axis=-1, keepdims=True)
    prob = jnp.where(emask, prob, -1.0)
    p1 = jnp.max(prob, axis=-1, keepdims=True)
    i1 = jnp.min(jnp.where(prob == p1, lanef, 999.0), axis=-1, keepdims=True)
    prob2 = jnp.where(lanef == i1, -1.0, prob)
    p2 = jnp.max(prob2, axis=-1, keepdims=True)
    i2 = jnp.min(jnp.where(prob2 == p2, lanef, 999.0), axis=-1, keepdims=True)
    tot = p1 + p2
    route = jnp.where(lane == 0, i1 - N_GROUPS,
                      jnp.where(lane == 1, i2 - N_GROUPS,
                                jnp.where(lane == 2, g_w * p1 / tot,
                                          jnp.where(lane == 3, g_w * p2 / tot, 0.0))))
    route_ref[...] = route.T[0:8, :]


def _merge(x, ya_p, ya_s, yb_p, yb_s, zb, w_a, w_b, w_out, ln_g, ln_b, w_rg, w_re, b_rg, b_re, layer):
    m = x.shape[0]
    pt = ya_p.shape[0] // TM
    rowblk = lambda w, j: pl.BlockSpec((TM, w), lambda i: (i, j))
    pblk = pl.BlockSpec((TM, D_MODEL), lambda i: (jnp.minimum(i, pt - 1), 0))
    sblk = pl.BlockSpec((TM, D_MODEL), lambda i: (jnp.maximum(i - pt, 0), 0))
    return pl.pallas_call(
        functools.partial(_merge_kernel, prompt_tiles=pt),
        grid=(m // TM,),
        in_specs=[rowblk(D_MODEL, 0), pblk, sblk, pblk, sblk,
                  rowblk(D_MODEL, ZGA // D_MODEL), rowblk(D_MODEL, ZGB // D_MODEL),
                  _layer_spec((D_MODEL, D_MODEL), layer), _layer_spec((D_MODEL, D_MODEL), layer),
                  _layer_spec((D_MODEL, D_MODEL), layer),
                  _layer_spec((3, D_MODEL), layer), _layer_spec((3, D_MODEL), layer),
                  _layer_spec((D_MODEL, N_GROUPS), layer), _layer_spec((D_MODEL, N_EXPERTS), layer),
                  _layer_spec((1, N_GROUPS), layer), _layer_spec((1, N_EXPERTS), layer)],
        out_specs=[rowblk(D_MODEL, 0), pl.BlockSpec((8, TM), lambda i: (0, i))],
        out_shape=[jax.ShapeDtypeStruct((m, D_MODEL), F32), jax.ShapeDtypeStruct((8, m), F32)],
        scratch_shapes=[pltpu.VMEM((3, D_MODEL, D_MODEL), BF16), pltpu.VMEM((D_MODEL, LANES), BF16),
                        pltpu.VMEM((1, LANES), F32)],
        compiler_params=_cparams(("arbitrary",)),
        name="merge",
    )(x, ya_p, ya_s, yb_p, yb_s, zb, zb, w_a, w_b, w_out, ln_g, ln_b, w_rg, w_re, b_rg, b_re)


SEG_ALIGN = 8
LOC_ROWS = 2 * TD + N_EXPERTS * SEG_ALIGN
LOC_BITS = (LOC_ROWS // SEG_ALIGN).bit_length()


def _route_tables(route, valid, nblk, min_used):
    m = route.shape[1]
    nt = m // TD
    per = BLK // SEG_ALIGN
    experts = jnp.arange(N_EXPERTS, dtype=I32)
    eid = jnp.where(valid[None, :], route[0:2].astype(I32), -1).reshape(2, nt, TD)
    e3 = jnp.concatenate([eid[0], eid[1]], axis=1)
    eflat = e3.reshape(2 * m)
    hit = eflat[:, None] == experts
    rank_all = jnp.sum(jnp.where(hit, jnp.cumsum(hit.astype(I32), axis=0), 0), axis=1) - 1
    own = e3[:, :, None] == experts
    raw = jnp.sum(own.astype(I32), axis=1)
    cnt = (raw + SEG_ALIGN - 1) // SEG_ALIGN
    lofs = jnp.cumsum(cnt, axis=1) - cnt
    ltot = jnp.sum(cnt, axis=1)
    before = jnp.cumsum(raw, axis=0) - raw
    pick = lambda tbl: jnp.sum(jnp.where(own, tbl[:, None, :], 0), axis=2)
    lslot = pick(lofs) * SEG_ALIGN + rank_all.reshape(nt, 2 * TD) - pick(before)
    lslot = jnp.where(e3 >= 0, lslot, -1).astype(F32)
    gates = route[2:4].reshape(2, nt, TD)
    table = jnp.stack([lslot[:, :TD], lslot[:, TD:], gates[0], gates[1]] + [jnp.zeros((nt, TD), F32)] * 4, axis=1)
    gcnt = jnp.sum(cnt, axis=0)
    gpad = (gcnt + per - 1) // per * per
    gend = jnp.cumsum(gpad)
    gofs = (gend - gpad)[None, :] + jnp.cumsum(cnt, axis=0) - cnt
    nused = gend[-1] // per
    unused = nused + jnp.arange(nblk - min_used, dtype=I32)
    zstart = jnp.concatenate([(gend - gpad + gcnt) * SEG_ALIGN, jnp.minimum(unused, nblk - 1) * BLK]).astype(I32)
    zlen = jnp.concatenate([(gpad - gcnt) * SEG_ALIGN, jnp.where(unused < nblk, BLK, 0)]).astype(I32)
    zstart = jnp.concatenate([zstart, zlen])
    bstart = jnp.minimum(jnp.arange(nblk, dtype=I32), nused - 1) * per
    blk_e = jnp.minimum(jnp.sum((gend[None, :] <= bstart[:, None]).astype(I32), axis=1), N_EXPERTS - 1)
    flat = lambda t: t.reshape(-1).astype(I32)
    return (flat(cnt), flat(lofs), flat(gofs), flat(ltot), table, zstart, flat(blk_e),
            nused.astype(I32).reshape(1))


def _segment_copies(i, cnt_ref, lofs_ref, gofs_ref, make_copy):
    for e in range(N_EXPERTS):
        c = cnt_ref[i * N_EXPERTS + e]
        lo = lofs_ref[i * N_EXPERTS + e]
        go = gofs_ref[i * N_EXPERTS + e]

        @pl.when(c > 0)
        def _():
            make_copy(pl.multiple_of(lo * SEG_ALIGN, SEG_ALIGN), pl.multiple_of(go * SEG_ALIGN, SEG_ALIGN),
                      pl.multiple_of(c * SEG_ALIGN, SEG_ALIGN)).start()


def _wait_rows(tot, make_copy):
    for k in range(LOC_BITS):
        @pl.when(((tot >> k) & 1) == 1)
        def _():
            make_copy(0, 0, SEG_ALIGN << k).wait()


def _dispatch_kernel(cnt_ref, lofs_ref, gofs_ref, ltot_ref, zstart_ref, x_ref, ls_ref, xs_hbm, xloc, zbuf, sem):
    i = pl.program_id(0)
    last = pl.num_programs(0) - 1
    slot = i % 2

    def copy_from(s):
        return lambda lo, go, n: pltpu.make_async_copy(xloc.at[s, pl.ds(lo, n)], xs_hbm.at[pl.ds(go, n)], sem.at[s])

    @pl.when(i == 0)
    def _():
        zbuf[...] = jnp.zeros_like(zbuf)
        nz = zstart_ref.shape[0] // 2
        for phase in ("start", "wait"):
            for e in range(nz):
                n = pl.multiple_of(zstart_ref[nz + e], SEG_ALIGN)

                @pl.when(n > 0)
                def _():
                    start = pl.multiple_of(zstart_ref[e], SEG_ALIGN)
                    cp = pltpu.make_async_copy(zbuf.at[pl.ds(0, n)], xs_hbm.at[pl.ds(start, n)], sem.at[2])
                    getattr(cp, phase)()

    @pl.when(i >= 2)
    def _():
        _wait_rows(ltot_ref[jnp.maximum(i - 2, 0)], copy_from(slot))

    ls = ls_ref[0, 0:2, :].astype(I32)
    j = lax.broadcasted_iota(I32, (LOC_ROWS, TD), 0)
    onehot = jnp.where((j == ls[0:1, :]) | (j == ls[1:2, :]), 1.0, 0.0).astype(BF16)
    xloc[slot] = _dot(onehot, x_ref[...].astype(BF16))
    _segment_copies(i, cnt_ref, lofs_ref, gofs_ref, copy_from(slot))

    @pl.when(i == last)
    def _():
        _wait_rows(ltot_ref[i], copy_from(slot))

        @pl.when(i >= 1)
        def _():
            _wait_rows(ltot_ref[jnp.maximum(i - 1, 0)], copy_from(1 - slot))


def _dispatch(x1, cnt, lofs, gofs, ltot, zstart, table, nblk):
    m = x1.shape[0]
    nt = m // TD
    grid_spec = pltpu.PrefetchScalarGridSpec(
        num_scalar_prefetch=5,
        grid=(nt,),
        in_specs=[pl.BlockSpec((TD, D_MODEL), lambda i, *_: (i, 0)),
                  pl.BlockSpec((1, 8, TD), lambda i, *_: (i, 0, 0))],
        out_specs=pl.BlockSpec(memory_space=pl.ANY),
        scratch_shapes=[pltpu.VMEM((2, LOC_ROWS, D_MODEL), F32), pltpu.VMEM((BLK, D_MODEL), F32),
                        pltpu.SemaphoreType.DMA((3,))],
    )
    return pl.pallas_call(
        _dispatch_kernel,
        grid_spec=grid_spec,
        out_shape=jax.ShapeDtypeStruct((nblk * BLK, D_MODEL), F32),
        compiler_params=_cparams(("arbitrary",)),
        name="dispatch",
    )(cnt, lofs, gofs, ltot, zstart, x1, table)


def _experts_kernel(blk_e_ref, nused_ref, xs_ref, wg_ref, wu_ref, wd_ref, ys_ref, wgb, wub, wdb):
    b = pl.program_id(0)
    used = b < nused_ref[0]
    new_expert = (b == 0) | (blk_e_ref[b] != blk_e_ref[jnp.maximum(b - 1, 0)])

    @pl.when(used & new_expert)
    def _():
        _cast_into(wgb, wg_ref)
        _cast_into(wub, wu_ref)
        _cast_into(wdb, wd_ref)

    @pl.when(used)
    def _():
        xb = xs_ref[...].astype(BF16)
        g = _dot(xb, wgb[...])
        u = _dot(xb, wub[...])
        ys_ref[...] = _dot((g * jax.nn.sigmoid(g) * u).astype(BF16), wdb[...])

    @pl.when(jnp.logical_not(used))
    def _():
        ys_ref[...] = jnp.zeros_like(ys_ref)


def _experts(xs, blk_e, nused, w_eg, w_eu, w_ed, layer):
    nb = xs.shape[0] // BLK
    wspec = lambda s: pl.BlockSpec((None, None) + s, lambda b, be, nu: (layer, be[b], 0, 0))
    grid_spec = pltpu.PrefetchScalarGridSpec(
        num_scalar_prefetch=2,
        grid=(nb,),
        in_specs=[pl.BlockSpec((BLK, D_MODEL), lambda b, be, nu: (jnp.minimum(b, nu[0] - 1), 0)),
                  wspec((D_MODEL, D_EXPERT)), wspec((D_MODEL, D_EXPERT)), wspec((D_EXPERT, D_MODEL))],
        out_specs=pl.BlockSpec((BLK, D_MODEL), lambda b, be, nu: (b, 0)),
        scratch_shapes=[pltpu.VMEM((D_MODEL, D_EXPERT), BF16), pltpu.VMEM((D_MODEL, D_EXPERT), BF16),
                        pltpu.VMEM((D_EXPERT, D_MODEL), BF16)],
    )
    return pl.pallas_call(
        _experts_kernel,
        grid_spec=grid_spec,
        out_shape=jax.ShapeDtypeStruct(xs.shape, F32),
        compiler_params=_cparams(("arbitrary",)),
        name="experts",
    )(blk_e, nused, xs, w_eg, w_eu, w_ed)


def _post_kernel(cnt_ref, lofs_ref, gofs_ref, ltot_ref, x1_ref, tab_ref, pp_ref, ps_ref, wpg_ref, wpp_ref,
                 lng_ref, lnb_ref, ys_hbm, o_ref, yloc, wpg_b, wpp_b, sem, *, prompt_tiles):
    i = pl.program_id(0)
    last = pl.num_programs(0) - 1
    slot = i % 2

    def copy_to(s):
        return lambda lo, go, n: pltpu.make_async_copy(ys_hbm.at[pl.ds(go, n)], yloc.at[s, pl.ds(lo, n)], sem.at[s])

    @pl.when(i == 0)
    def _():
        _cast_into(wpg_b, wpg_ref)
        _cast_into(wpp_b, wpp_ref)
        yloc[...] = jnp.zeros_like(yloc)
        _segment_copies(0, cnt_ref, lofs_ref, gofs_ref, copy_to(0))

    _wait_rows(ltot_ref[i], copy_to(slot))

    @pl.when(i < last)
    def _():
        _segment_copies(i + 1, cnt_ref, lofs_ref, gofs_ref, copy_to(1 - slot))

    tab = jnp.concatenate([tab_ref[0], jnp.zeros((LANES - 8, tab_ref.shape[2]), F32)], axis=0)
    tcol = jnp.concatenate([tab[:, c:c + LANES].T for c in range(0, tab.shape[1], LANES)], axis=0)
    j = lax.broadcasted_iota(I32, (tcol.shape[0], LOC_ROWS), 1).astype(F32)
    w = jnp.where(j == tcol[:, 0:1], tcol[:, 2:3], 0.0) + jnp.where(j == tcol[:, 1:2], tcol[:, 3:4], 0.0)
    moe = _dot(w.astype(BF16), yloc[slot].astype(BF16))
    x2 = _layer_norm(DN_ALPHA * x1_ref[...] + moe, lng_ref[1:2, :], lnb_ref[1:2, :])
    p = jnp.where(i < prompt_tiles, pp_ref[...], ps_ref[...])
    ple = jax.nn.sigmoid(_dot(x2.astype(BF16), wpg_b[...])) * _dot(p.astype(BF16), wpp_b[...])
    o_ref[...] = _layer_norm(DN_ALPHA * x2 + ple, lng_ref[2:3, :], lnb_ref[2:3, :])


def _post(x1, ys, cnt, lofs, gofs, ltot, table, p_prompt, p_sample, w_pg, w_pp, ln_g, ln_b, layer):
    m = x1.shape[0]
    pt = p_prompt.shape[1] // TD
    rowblk = lambda w: pl.BlockSpec((TD, w), lambda i, *_: (i, 0))
    lspec = lambda s: pl.BlockSpec((None,) + s, lambda i, *_: (layer,) + (0,) * len(s), pipeline_mode=pl.Buffered(1))
    grid_spec = pltpu.PrefetchScalarGridSpec(
        num_scalar_prefetch=4,
        grid=(m // TD,),
        in_specs=[rowblk(D_MODEL), pl.BlockSpec((1, 8, TD), lambda i, *_: (i, 0, 0)),
                  pl.BlockSpec((None, TD, D_PLE), lambda i, *_: (layer, jnp.minimum(i, pt - 1), 0)),
                  pl.BlockSpec((None, TD, D_PLE), lambda i, *_: (layer, jnp.maximum(i - pt, 0), 0)),
                  lspec((D_MODEL, D_MODEL)), lspec((D_PLE, D_MODEL)), lspec((3, D_MODEL)), lspec((3, D_MODEL)),
                  pl.BlockSpec(memory_space=pl.ANY)],
        out_specs=rowblk(D_MODEL),
        scratch_shapes=[pltpu.VMEM((2, LOC_ROWS, D_MODEL), F32), pltpu.VMEM((D_MODEL, D_MODEL), BF16),
                        pltpu.VMEM((D_PLE, D_MODEL), BF16), pltpu.SemaphoreType.DMA((2,))],
    )
    return pl.pallas_call(
        functools.partial(_post_kernel, prompt_tiles=pt),
        grid_spec=grid_spec,
        out_shape=jax.ShapeDtypeStruct((m, D_MODEL), F32),
        compiler_params=_cparams(("arbitrary",)),
        name="post",
    )(cnt, lofs, gofs, ltot, x1, table, p_prompt, p_sample, w_pg, w_pp, ln_g, ln_b, ys)


def kernel(x_prompt, x_sample, state_mlstm_C, state_mlstm_n, state_mlstm_m, state_swa_k, state_swa_v, p_prompt, p_sample, w_in, b_in, mh_gain, w_a, w_b, w_out, rel_table, w_sink, ln_g, ln_b, w_rg, b_rg, w_re, b_re, w_eg, w_eu, w_ed, w_pg, w_pp):
    bp_, seq = x_prompt.shape[:2]
    bs_, tdec = x_sample.shape[:2]
    mp = bp_ * seq
    msp = bs_ * T_PAD
    mall = mp + msp
    nblk = -(-(2 * (mp + bs_ * tdec) + (mall // TD) * N_EXPERTS * (SEG_ALIGN - 1)) // BLK) + N_EXPERTS

    pad_t = lambda t: jnp.pad(t, [(0, 0)] * (t.ndim - 2) + [(0, T_PAD - tdec), (0, 0)])
    x = lax.dynamic_update_slice(jnp.pad(x_prompt.reshape(mp, D_MODEL), ((0, msp), (0, 0))),
                                 pad_t(x_sample).reshape(msp, D_MODEL), (mp, 0))
    valid = jnp.concatenate([jnp.ones((mp,), bool), jnp.tile(jnp.arange(T_PAD) < tdec, bs_)])
    p_prompt = p_prompt.reshape(DEPTH, mp, D_PLE)
    p_sample = pad_t(p_sample).reshape(DEPTH, msp, D_PLE)

    b_in3 = b_in.reshape(DEPTH, 1, -1)
    b_b = jnp.concatenate([b_in3[:, :, IN_GATES[0]:IN_GATES[1]], b_in3[:, :, IN_QKV2[0]:IN_QKV2[1]]], axis=-1)
    b_rg, b_re = b_rg.reshape(DEPTH, 1, N_GROUPS), b_re.reshape(DEPTH, 1, N_EXPERTS)
    gain = mh_gain.reshape(DEPTH, 1, D_MODEL)

    bias_p = _swa_bias(rel_table, WINDOW)
    bias_s = _swa_bias(rel_table, T_PAD)
    sink_p = sink_s = _sink_lanes(w_sink)
    buf_k = state_swa_k.reshape(DEPTH, bs_, WINDOW, D_KV)
    buf_v = state_swa_v.reshape(DEPTH, bs_, WINDOW, D_KV)
    m0_s = jnp.broadcast_to(state_mlstm_m[..., None], (DEPTH, bs_, M_HEADS, LANES))
    zc = jnp.zeros((1, bp_, M_HEADS, M_DQK, M_DV), F32)
    zn = jnp.zeros((1, bp_, M_HEADS, LANES), F32)
    st_s = (jnp.zeros_like(state_mlstm_C), jnp.zeros_like(state_mlstm_n), jnp.zeros_like(m0_s))
    kv_s = (jnp.zeros_like(buf_k), jnp.zeros_like(buf_v))

    st_p = []
    for i in range(DEPTH):
        za, zg = _inproj(x, [(w_in, IN_A)], [(b_in3, IN_A)], [(ZG, BF16), (LANES, F32)], i, "inproj_a")
        zb = _inproj_b(x, w_in, b_b, i)

        ya_p, c_p, n_p, m_p = _mlstm(za, zg, gain, i, zc, zn, zn, 0, None, 1, 0, row0=0, nseq=bp_,
                                     steps=seq // LANES, seg=LANES, t_valid=LANES)
        ya_s, *st_s = _mlstm(za, zg, gain, i, state_mlstm_C, state_mlstm_n, m0_s, i, st_s, DEPTH, i, row0=mp,
                             nseq=bs_ // SEQ_BLK, steps=1, seg=T_PAD, t_valid=tdec)
        yb_p = _swa_prompt(zb, bias_p[0], bias_p[1], sink_p, i, batch=bp_, seq=seq)
        yb_s, *kv_s = _swa_sample(zb, buf_k, buf_v, bias_s[0], bias_s[1], sink_s, i, kv_s, row0=mp, batch=bs_,
                                  t_new=tdec)

        x1, route = _merge(x, ya_p, ya_s, yb_p, yb_s, zb, w_a, w_b, w_out, ln_g, ln_b, w_rg, w_re, b_rg, b_re, i)
        cnt, lofs, gofs, ltot, table, zstart, blk_e, nused = _route_tables(route, valid, nblk, 2 * (mp + bs_ * tdec) // BLK)
        xs = _dispatch(x1, cnt, lofs, gofs, ltot, zstart, table, nblk)
        ys = _experts(xs, blk_e, nused, w_eg, w_eu, w_ed, i)
        x = _post(x1, ys, cnt, lofs, gofs, ltot, table, p_prompt, p_sample, w_pg, w_pp, ln_g, ln_b, i)

        kv_p = jnp.stack([lax.slice(zb, ((b + 1) * seq - WINDOW, ZKS), ((b + 1) * seq, IN_B)) for b in range(bp_)])
        kv_p = kv_p.astype(F32)
        heads = lambda t: t.reshape(t.shape[:-1] + (S_KV_HEADS, S_HEAD_DIM))
        st_p.append((c_p[0], n_p[0], m_p[0, :, :, 0], heads(kv_p[..., :D_KV]), heads(kv_p[..., D_KV:])))

    stk = lambda j: jnp.stack([s[j] for s in st_p], axis=0)
    y_p = x[:mp].reshape(bp_, seq, D_MODEL)
    y_s = x[mp:].reshape(bs_, T_PAD, D_MODEL)[:, :tdec]
    return (y_p, y_s, stk(0), stk(1), stk(2), stk(3), stk(4),
            st_s[0], st_s[1], st_s[2][..., 0], heads(kv_s[0]), heads(kv_s[1]))
```
